```python
import jax, jax.numpy as jnp
from jax import lax
import numpy as np

D_MODEL = 1024
BATCH = 4
SEQ = 4096
DEPTH = 2
DEC_BATCH = 8
DEC_SEQ = 16
PAST_LEN = 1024

CHUNK = 64
D_A = 512
G_A = 4
A_CHUNK = 128
D_B = 512
H_B = 8
DH_B = D_B // H_B
CONV_W = 4
LRU_C = 8.0
H_C = 8
D_NOPE = 64
D_ROPE = 32
D_V = 64
Q_LORA = 256
KV_LORA = 128
D_C = H_C * D_V
ROPE_BASE = 10000.0
Q_BLOCK = 128
N_MEM = 256
H_M = 4
DH_M = 64
D_M = H_M * DH_M
N_BRANCH = 4
ALPHA = (2.0 * DEPTH) ** 0.25
BETA = (8.0 * DEPTH) ** -0.25
EPS = 1e-6
SPLITS = (D_A, D_A, D_A, D_B, D_B, Q_LORA, KV_LORA, D_ROPE, D_C, D_M, N_BRANCH * D_MODEL)
D_IN = 3 * D_A + 2 * D_B + Q_LORA + KV_LORA + D_ROPE + D_C + D_M + N_BRANCH * D_MODEL
D_BR = D_A + D_B + D_C + D_M

kernel_name = 'hybrid_gmlp_rglru_mla_stream_step'


def layer_norm(x, g, b):
    xf = x.astype(jnp.float32)
    mu = jnp.mean(xf, -1, keepdims=True)
    var = jnp.mean(jnp.square(xf - mu), -1, keepdims=True)
    return ((xf - mu) * lax.rsqrt(var + EPS) * g + b).astype(x.dtype)


def rms_norm(x, g):
    xf = x.astype(jnp.float32)
    return (xf * lax.rsqrt(jnp.mean(jnp.square(xf), -1, keepdims=True) + EPS) * g).astype(x.dtype)


def split_cols(z):
    parts, start = [], 0
    for n in SPLITS:
        parts.append(z[..., start:start + n])
        start += n
    return parts


def rope(x, pos):
    half = D_ROPE // 2
    freq = ROPE_BASE ** (-jnp.arange(half, dtype=jnp.float32) / half)
    ang = pos.astype(jnp.float32)[:, None] * freq[None, :]
    ang = ang.reshape((1, pos.shape[0]) + (1,) * (x.ndim - 3) + (half,))
    cos, sin = jnp.cos(ang), jnp.sin(ang)
    xf = x.astype(jnp.float32)
    x1, x2 = xf[..., :half], xf[..., half:]
    return jnp.concatenate([x1 * cos - x2 * sin, x1 * sin + x2 * cos], -1).astype(x.dtype)


def gmlp_branch(u, v, ln_g, ln_b, w_s, b_s):
    B, T, _ = v.shape
    u = jax.nn.gelu(u)
    v = layer_norm(jax.nn.gelu(v), ln_g, ln_b)
    L = min(T, A_CHUNK)
    vc = v.reshape(B, T // L, L, G_A, D_A // G_A)
    w = jnp.tril(w_s[:, :L, :L])
    s = jnp.einsum('gpq,bnqgc->bnpgc', w, vc) + jnp.transpose(b_s[:, :L])[None, None, :, :, None]
    return u * s.reshape(B, T, D_A), v


def _lin_combine(left, right):
    a_l, b_l = left
    a_r, b_r = right
    return a_l * a_r, a_r * b_l + b_r


def rglru_branch(xb, conv0, h0, conv_w, conv_b, w_r, b_r, w_i, b_i, lam):
    B, T, _ = xb.shape
    xp = jnp.concatenate([conv0.astype(xb.dtype), xb], axis=1)
    xc = conv_b
    for k in range(CONV_W):
        xc = xc + xp[:, k:k + T] * conv_w[k]
    xh = xc.reshape(B, T, H_B, DH_B)
    r = jax.nn.sigmoid((jnp.einsum('bthi,hij->bthj', xh, w_r).reshape(B, T, D_B) + b_r).astype(jnp.float32))
    i = jax.nn.sigmoid((jnp.einsum('bthi,hij->bthj', xh, w_i).reshape(B, T, D_B) + b_i).astype(jnp.float32))
    log_a = -LRU_C * r * jax.nn.softplus(-lam.astype(jnp.float32))
    a = jnp.exp(log_a)
    bval = jnp.sqrt(-jnp.expm1(2.0 * log_a)) * (i * xc.astype(jnp.float32))
    acc_a, acc_b = lax.associative_scan(_lin_combine, (a, bval), axis=1)
    h = acc_a * h0.astype(jnp.float32)[:, None, :] + acc_b
    return h.astype(xb.dtype), xp[:, -(CONV_W - 1):], h[:, -1].astype(xb.dtype)


def _mla_block(q_nope, q_rope, k_nope, k_rope, v, q_pos, k_pos):
    scale = (D_NOPE + D_ROPE) ** -0.5
    s = (jnp.einsum('bqhd,bkhd->bhqk', q_nope, k_nope)
         + jnp.einsum('bqhd,bkd->bhqk', q_rope, k_rope)).astype(jnp.float32) * scale
    mask = (k_pos[None, :] // CHUNK) <= (q_pos[:, None] // CHUNK)
    s = jnp.where(mask[None, None], s, -1e30)
    p = jax.nn.softmax(s, axis=-1).astype(v.dtype)
    return jnp.einsum('bhqk,bkhd->bqhd', p, v)


def chunk_causal_mla(q_nope, q_rope, k_nope, k_rope, v, q_pos, k_pos):
    B, T = q_nope.shape[:2]
    if T <= Q_BLOCK:
        return _mla_block(q_nope, q_rope, k_nope, k_rope, v, q_pos, k_pos)
    nb = T // Q_BLOCK
    qn = jnp.swapaxes(q_nope.reshape(B, nb, Q_BLOCK, H_C, D_NOPE), 0, 1)
    qr = jnp.swapaxes(q_rope.reshape(B, nb, Q_BLOCK, H_C, D_ROPE), 0, 1)
    qp = q_pos.reshape(nb, Q_BLOCK)
    out = lax.map(lambda a: _mla_block(a[0], a[1], k_nope, k_rope, v, a[2], k_pos), (qn, qr, qp))
    return jnp.swapaxes(out, 0, 1).reshape(B, T, H_C, D_V)


def mla_branch(c_q, c_kv, kr_raw, q_norm, w_uq, kv_norm, w_ukv, q_pos, past_ckv, past_kr):
    B, T, _ = c_q.shape
    q = (rms_norm(c_q, q_norm) @ w_uq).reshape(B, T, H_C, D_NOPE + D_ROPE)
    q_nope, q_rope = q[..., :D_NOPE], rope(q[..., D_NOPE:], q_pos)
    ckv_new = rms_norm(c_kv, kv_norm)
    kr_new = rope(kr_raw, q_pos)
    if past_ckv is None:
        ckv_all, kr_all, k_pos = ckv_new, kr_new, q_pos
    else:
        ckv_all = jnp.concatenate([past_ckv, ckv_new], axis=1)
        kr_all = jnp.concatenate([past_kr, kr_new], axis=1)
        k_pos = jnp.arange(ckv_all.shape[1])
    Tk = ckv_all.shape[1]
    kv = (ckv_all @ w_ukv).reshape(B, Tk, H_C, D_NOPE + D_V)
    o = chunk_causal_mla(q_nope, q_rope, kv[..., :D_NOPE], kr_all, kv[..., D_NOPE:], q_pos, k_pos)
    return o.reshape(B, T, D_C), ckv_new, kr_new


def mem_attend(q, mem_k, mem_v):
    s = jnp.einsum('bqhd,bkhd->bhqk', q, mem_k).astype(jnp.float32) * (DH_M ** -0.5)
    p = jax.nn.softmax(s, axis=-1).astype(mem_v.dtype)
    return jnp.einsum('bhqk,bkhd->bqhd', p, mem_v)


def trunk_layer(x, pos, mem_k, mem_v, past_ckv, past_kr, conv0, h0,
                w_in, gmlp_ln_g, gmlp_ln_b, gmlp_ws, gmlp_bs,
                lru_conv_w, lru_conv_b, lru_w_r, lru_b_r, lru_w_i, lru_b_i, lru_lambda,
                mla_q_norm, mla_w_uq, mla_kv_norm, mla_w_ukv, w_br, w_out, ln_g, ln_b):
    B, T, _ = x.shape
    (a_u, a_v, a_g, b_x, b_g, c_q, c_kv, c_kr, c_g, m_q, gate_logits) = split_cols(x @ w_in)
    oa, v_rows = gmlp_branch(a_u, a_v, gmlp_ln_g, gmlp_ln_b, gmlp_ws, gmlp_bs)
    oa = oa * jax.nn.silu(a_g)
    ob, conv_new, h_new = rglru_branch(b_x, conv0, h0, lru_conv_w, lru_conv_b,
                                       lru_w_r, lru_b_r, lru_w_i, lru_b_i, lru_lambda)
    ob = ob * jax.nn.silu(b_g)
    oc, ckv_new, kr_new = mla_branch(c_q, c_kv, c_kr, mla_q_norm, mla_w_uq, mla_kv_norm, mla_w_ukv,
                                     pos, past_ckv, past_kr)
    oc = oc * jax.nn.silu(c_g)
    om = mem_attend(m_q.reshape(B, T, H_M, DH_M), mem_k, mem_v).reshape(B, T, D_M)
    g = jax.nn.sigmoid(gate_logits.astype(jnp.float32)).astype(x.dtype).reshape(B, T, N_BRANCH, D_MODEL)
    ya = oa @ w_br[:D_A]
    yb = ob @ w_br[D_A:D_A + D_B]
    yc = oc @ w_br[D_A + D_B:D_A + D_B + D_C]
    ym = om @ w_br[D_A + D_B + D_C:]
    merged = g[:, :, 0] * ya + g[:, :, 1] * yb + g[:, :, 2] * yc + g[:, :, 3] * ym
    y = merged @ w_out
    x_new = layer_norm(ALPHA * x + y, ln_g, ln_b)
    return x_new, v_rows, conv_new, h_new, ckv_new, kr_new


def setup_inputs(seed: int = 0) -> dict:
    key = jax.random.key(seed)
    ks = iter(jax.random.split(key, 48))
    def nrm(shape, scale):
        return jax.random.normal(next(ks), shape, jnp.float32) * scale
    a0 = jax.random.uniform(next(ks), (DEPTH, D_B), jnp.float32, minval=0.9, maxval=0.999)
    p0 = a0 ** (1.0 / LRU_C)
    lru_lambda = jnp.log(p0) - jnp.log1p(-p0)
    w_br = jnp.concatenate([nrm((DEPTH, D_A, D_MODEL), BETA * D_A ** -0.5),
                            nrm((DEPTH, D_B, D_MODEL), BETA * D_B ** -0.5),
                            nrm((DEPTH, D_C, D_MODEL), BETA * D_C ** -0.5),
                            nrm((DEPTH, D_M, D_MODEL), BETA * D_M ** -0.5)], axis=1)
    return {
        'x_prompt': nrm((BATCH, SEQ, D_MODEL), 1.0),
        'x_sample': nrm((DEC_BATCH, DEC_SEQ, D_MODEL), 1.0),
        'mem_prompt': nrm((BATCH, N_MEM, D_MODEL), 1.0),
        'cache_mla_ckv': nrm((DEPTH, DEC_BATCH, PAST_LEN, KV_LORA), 1.0),
        'cache_mla_krope': nrm((DEPTH, DEC_BATCH, PAST_LEN, D_ROPE), 1.0),
        'cache_mem_k': nrm((DEPTH, DEC_BATCH, N_MEM, H_M, DH_M), 1.0),
        'cache_mem_v': nrm((DEPTH, DEC_BATCH, N_MEM, H_M, DH_M), 1.0),
        'state_lru_h': nrm((DEPTH, DEC_BATCH, D_B), 0.5),
        'state_lru_conv': nrm((DEPTH, DEC_BATCH, CONV_W - 1, D_B), 1.0),
        'w_in': nrm((DEPTH, D_MODEL, D_IN), D_MODEL ** -0.5),
        'gmlp_ln_g': 1.0 + nrm((DEPTH, D_A), 0.02),
        'gmlp_ln_b': nrm((DEPTH, D_A), 0.02),
        'gmlp_ws': nrm((DEPTH, G_A, A_CHUNK, A_CHUNK), 0.5 * A_CHUNK ** -0.5),
        'gmlp_bs': 1.0 + nrm((DEPTH, G_A, A_CHUNK), 0.1),
        'lru_conv_w': nrm((DEPTH, CONV_W, D_B), CONV_W ** -0.5),
        'lru_conv_b': nrm((DEPTH, D_B), 0.02),
        'lru_w_r': nrm((DEPTH, H_B, DH_B, DH_B), DH_B ** -0.5),
        'lru_b_r': nrm((DEPTH, D_B), 0.02),
        'lru_w_i': nrm((DEPTH, H_B, DH_B, DH_B), DH_B ** -0.5),
        'lru_b_i': nrm((DEPTH, D_B), 0.02),
        'lru_lambda': lru_lambda,
        'mla_q_norm': 1.0 + nrm((DEPTH, Q_LORA), 0.02),
        'mla_w_uq': nrm((DEPTH, Q_LORA, H_C * (D_NOPE + D_ROPE)), Q_LORA ** -0.5),
        'mla_kv_norm': 1.0 + nrm((DEPTH, KV_LORA), 0.02),
        'mla_w_ukv': nrm((DEPTH, KV_LORA, H_C * (D_NOPE + D_V)), KV_LORA ** -0.5),
        'mem_w_k': nrm((DEPTH, D_MODEL, D_M), D_MODEL ** -0.5),
        'mem_w_v': nrm((DEPTH, D_MODEL, D_M), D_MODEL ** -0.5),
        'w_br': w_br,
        'w_out': nrm((DEPTH, D_MODEL, D_MODEL), BETA * D_MODEL ** -0.5),
        'ln_g': 1.0 + nrm((DEPTH, D_MODEL), 0.02),
        'ln_b': nrm((DEPTH, D_MODEL), 0.02),
    }


def reference(x_prompt, x_sample, mem_prompt, cache_mla_ckv, cache_mla_krope, cache_mem_k, cache_mem_v,
              state_lru_h, state_lru_conv, w_in, gmlp_ln_g, gmlp_ln_b, gmlp_ws, gmlp_bs,
              lru_conv_w, lru_conv_b, lru_w_r, lru_b_r, lru_w_i, lru_b_i, lru_lambda,
              mla_q_norm, mla_w_uq, mla_kv_norm, mla_w_ukv, mem_w_k, mem_w_v, w_br, w_out, ln_g, ln_b):
    Bp, Tp, _ = x_prompt.shape
    Bs, Ts, _ = x_sample.shape
    pos_p = jnp.arange(Tp)
    pos_s = PAST_LEN + jnp.arange(Ts)
    xp, xs = x_prompt, x_sample
    p_ckv, p_kr, p_mk, p_mv, p_h, p_conv = [], [], [], [], [], []
    s_ckv, s_kr, s_h, s_conv, s_v = [], [], [], [], []
    for l in range(DEPTH):
        lw = (w_in[l], gmlp_ln_g[l], gmlp_ln_b[l], gmlp_ws[l], gmlp_bs[l],
              lru_conv_w[l], lru_conv_b[l], lru_w_r[l], lru_b_r[l], lru_w_i[l], lru_b_i[l], lru_lambda[l],
              mla_q_norm[l], mla_w_uq[l], mla_kv_norm[l], mla_w_ukv[l], w_br[l], w_out[l], ln_g[l], ln_b[l])
        mk = (mem_prompt @ mem_w_k[l]).reshape(Bp, N_MEM, H_M, DH_M)
        mv = (mem_prompt @ mem_w_v[l]).reshape(Bp, N_MEM, H_M, DH_M)
        conv0 = jnp.zeros((Bp, CONV_W - 1, D_B), xp.dtype)
        h0 = jnp.zeros((Bp, D_B), xp.dtype)
        xp, _, conv_n, h_n, ckv_n, kr_n = trunk_layer(xp, pos_p, mk, mv, None, None, conv0, h0, *lw)
        p_ckv.append(ckv_n); p_kr.append(kr_n); p_mk.append(mk); p_mv.append(mv)
        p_h.append(h_n); p_conv.append(conv_n)
        xs, v_n, conv_n, h_n, ckv_n, kr_n = trunk_layer(
            xs, pos_s, cache_mem_k[l], cache_mem_v[l], cache_mla_ckv[l], cache_mla_krope[l],
            state_lru_conv[l], state_lru_h[l], *lw)
        s_ckv.append(ckv_n); s_kr.append(kr_n); s_h.append(h_n); s_conv.append(conv_n); s_v.append(v_n)
    return (xp, xs,
            jnp.stack(p_ckv), jnp.stack(p_kr), jnp.stack(p_mk), jnp.stack(p_mv), jnp.stack(p_h), jnp.stack(p_conv),
            jnp.stack(s_ckv), jnp.stack(s_kr), jnp.stack(s_h), jnp.stack(s_conv), jnp.stack(s_v))
```

```python
import functools

import jax
import jax.numpy as jnp
from jax import lax
from jax.experimental import pallas as pl
from jax.experimental.pallas import tpu as pltpu

CHUNK = 64
G_A = 4
A_CHUNK = 128
H_B = 8
CONV_W = 4
LRU_C = 8.0
H_C = 8
D_NOPE = 64
D_ROPE = 32
D_V = 64
ROPE_BASE = 10000.0
H_M = 4
DH_M = 64
N_BRANCH = 4
EPS = 1e-6

LANES = 128
SUBLANES = 8
VMEM_LIMIT = 56 * 1024 * 1024

F32 = jnp.float32
BF16 = jnp.bfloat16


def _dot(a, b):
    return jnp.dot(a, b, preferred_element_type=F32)


def _dot_nt(a, b):
    return lax.dot_general(a, b, (((1,), (1,)), ((), ())), preferred_element_type=F32)


def _sigmoid(x):
    return 1.0 / (1.0 + jnp.exp(-x))


def _silu(x):
    return x * _sigmoid(x)


def _gelu(x):
    return jax.nn.gelu(x)


def _expm1_nonpos(x):
    u = jnp.exp(x)
    near = (u - 1.0) * x / jnp.log(jnp.where(u == 1.0, 2.0, jnp.maximum(u, 0.5)))
    return jnp.where(u == 1.0, x, jnp.where(u > 0.5, near, u - 1.0))


def _layer_norm(x, g, b):
    mu = jnp.mean(x, -1, keepdims=True)
    var = jnp.mean(jnp.square(x - mu), -1, keepdims=True)
    return (x - mu) * lax.rsqrt(var + EPS) * g + b


def _rms_norm(x, g):
    return x * lax.rsqrt(jnp.mean(jnp.square(x), -1, keepdims=True) + EPS) * g


def _const_spec(shape):
    nd = len(shape)
    return pl.BlockSpec(shape, lambda *_: (0,) * nd, pipeline_mode=pl.Buffered(1))


def _branch_kernel(x_ref, w1_ref, glng_ref, glnb_ref, wsp_ref, bsp_ref, convw_ref, convb_ref,
                   wgate_ref, bgate_ref, lam_ref, conv0_ref, h0_ref, qnorm_ref, wuq_ref, pq_ref,
                   kvnorm_ref, cq_ref, sq_ref, ck_ref, sk_ref, mk_ref, mv_ref, wbr_ref,
                   partial_ref, g2_ref, cgs_ref, qcat_ref, ckv_ref, kr_ref, conv_ref, h_ref,
                   *maybe_v_ref, tm, sp_len, d_a, d_b, d_c, d_m, q_lora, kv_lora, d_model, cols):
    t = pl.program_id(1)
    xb = x_ref[0].astype(BF16)

    def zin(name):
        lo, hi = cols[name]
        return _dot(xb, w1_ref[:, lo:hi])

    u = _gelu(zin("a_u"))
    v = _layer_norm(_gelu(zin("a_v")), glng_ref[...], glnb_ref[...])
    if maybe_v_ref:
        maybe_v_ref[0][0] = v
    vb = v.astype(BF16)
    n_groups = d_a // LANES
    row_blocks = []
    for c in range(tm // sp_len):
        col_blocks = [_dot(wsp_ref[g], vb[c * sp_len:(c + 1) * sp_len, g * LANES:(g + 1) * LANES])
                      for g in range(n_groups)]
        row_blocks.append(jnp.concatenate(col_blocks, axis=1) + bsp_ref[...])
    s = row_blocks[0] if len(row_blocks) == 1 else jnp.concatenate(row_blocks, axis=0)
    oa = (u * s) * _silu(zin("a_g"))
    partial = _sigmoid(zin("g0")) * _dot(oa.astype(BF16), wbr_ref[0:d_a, :])

    @pl.when(t == 0)
    def _():
        conv_ref[...] = conv0_ref[...]
        h_ref[...] = h0_ref[...]

    bx = zin("b_x")
    hist = conv_ref[0]
    row8 = lax.broadcasted_iota(jnp.int32, (SUBLANES, 1), 0)
    xc = convb_ref[...]
    for k in range(CONV_W):
        shift = CONV_W - 1 - k
        if shift == 0:
            sh = bx
        else:
            sh = pltpu.roll(bx, shift, axis=0)
            head = jnp.where(row8 < shift, pltpu.roll(hist, shift, axis=0), sh[0:SUBLANES])
            sh = head if tm == SUBLANES else jnp.concatenate([head, sh[SUBLANES:]], axis=0)
        xc = xc + sh * convw_ref[k:k + 1, :]
    conv_ref[0] = bx[tm - SUBLANES:tm]

    xcb = xc.astype(BF16)
    r = _sigmoid(_dot(xcb, wgate_ref[:, 0:d_b]) + bgate_ref[0:1, :])
    i_gate = _sigmoid(_dot(xcb, wgate_ref[:, d_b:2 * d_b]) + bgate_ref[1:2, :])
    neg_lam = -lam_ref[...]
    softplus = jnp.maximum(neg_lam, 0.0) + jnp.log1p(jnp.exp(-jnp.abs(neg_lam)))
    log_a = (-LRU_C * r) * softplus
    a = jnp.exp(log_a)
    bval = jnp.sqrt(-_expm1_nonpos(2.0 * log_a)) * (i_gate * xc)
    rows = lax.broadcasted_iota(jnp.int32, (tm, 1), 0)
    d = 1
    while d < tm:
        keep = rows >= d
        a_sh = pltpu.roll(a, d, axis=0)
        b_sh = pltpu.roll(bval, d, axis=0)
        bval = jnp.where(keep, a * b_sh + bval, bval)
        a = jnp.where(keep, a * a_sh, a)
        d *= 2
    h = a * h_ref[0] + bval
    h_ref[0] = h[tm - 1:tm]
    ob = h * _silu(zin("b_g"))
    partial = partial + _sigmoid(zin("g1")) * _dot(ob.astype(BF16), wbr_ref[d_a:d_a + d_b, :])

    mq = zin("m_q").astype(BF16)
    mkb = mk_ref[0].astype(BF16)
    mvb = mv_ref[0].astype(BF16)
    lane = lax.broadcasted_iota(jnp.int32, (1, LANES), 1)
    slabs = []
    for p in range(d_m // LANES):
        mq_p = mq[:, p * LANES:(p + 1) * LANES]
        mk_p = mkb[:, p * LANES:(p + 1) * LANES]
        mv_p = mvb[:, p * LANES:(p + 1) * LANES]
        acc = None
        for half in range(LANES // DH_M):
            sel = (lane >= half * DH_M) & (lane < (half + 1) * DH_M)
            sc = _dot_nt(mq_p, jnp.where(sel, mk_p, jnp.zeros_like(mk_p))) * (DH_M ** -0.5)
            e = jnp.exp(sc - jnp.max(sc, -1, keepdims=True))
            prob = (e / jnp.sum(e, -1, keepdims=True)).astype(BF16)
            o = _dot(prob, jnp.where(sel, mv_p, jnp.zeros_like(mv_p)))
            acc = o if acc is None else acc + o
        slabs.append(acc)
    om = jnp.concatenate(slabs, axis=1)
    m_lo = d_a + d_b + d_c
    partial = partial + _sigmoid(zin("g3")) * _dot(om.astype(BF16), wbr_ref[m_lo:m_lo + d_m, :])
    partial_ref[0] = partial
    g2_ref[0] = _sigmoid(zin("g2")).astype(BF16)
    cgs_ref[0] = _silu(zin("c_g"))

    zc = zin("c_all")
    cqn = _rms_norm(zc[:, 0:q_lora], qnorm_ref[...]).astype(BF16)
    qall = _dot(cqn, wuq_ref[...])
    n_nope = H_C * D_NOPE
    x1 = qall[:, n_nope:n_nope + LANES]
    x2 = qall[:, n_nope + LANES:n_nope + 2 * LANES]
    yr = jnp.concatenate([x1, x2], axis=1) * cq_ref[...] + jnp.concatenate([x2, x1], axis=1) * sq_ref[...]
    q_in = jnp.concatenate([qall[:, 0:n_nope], yr], axis=1).astype(BF16)
    qcat_ref[0] = _dot(q_in, pq_ref[...]).astype(BF16)
    ckv_ref[0] = _rms_norm(zc[:, q_lora:q_lora + kv_lora], kvnorm_ref[...])
    r_lo = q_lora + kv_lora
    kr_ref[0] = (zc[:, r_lo:r_lo + D_ROPE] * ck_ref[...]
                 + zc[:, r_lo + D_ROPE:r_lo + 2 * D_ROPE] * sk_ref[...])


def _branch_call(x, lw, conv0_pad, h0, mk, mv, rope, *, tm, emit_v):
    bsz, seq, d_model = x.shape
    d_a, d_b, d_c, d_m = lw["dims"]
    q_lora, kv_lora = lw["q_lora"], lw["kv_lora"]
    sp_len = lw["wsp"].shape[-1]
    n_mem = mk.shape[1]
    cq_t, sq_t, ck_t, sk_t = rope
    grid = (bsz, seq // tm)

    def row_spec(width):
        return pl.BlockSpec((1, tm, width), lambda b, t: (b, t, 0))

    def tab_spec(width):
        return pl.BlockSpec((tm, width), lambda b, t: (t, 0))

    def batch_spec(rows, width):
        return pl.BlockSpec((1, rows, width), lambda b, t: (b, 0, 0))

    in_specs = [
        row_spec(d_model),
        _const_spec(lw["w1"].shape),
        _const_spec(lw["gln_g"].shape), _const_spec(lw["gln_b"].shape),
        _const_spec(lw["wsp"].shape), _const_spec(lw["bsp"].shape),
        _const_spec(lw["conv_w"].shape), _const_spec(lw["conv_b"].shape),
        _const_spec(lw["wgate"].shape), _const_spec(lw["bgate"].shape),
        _const_spec(lw["lam"].shape),
        batch_spec(SUBLANES, d_b), batch_spec(1, d_b),
        _const_spec(lw["q_norm"].shape), _const_spec(lw["wuq"].shape), _const_spec(lw["pq"].shape),
        _const_spec(lw["kv_norm"].shape),
        tab_spec(2 * LANES), tab_spec(2 * LANES), tab_spec(D_ROPE), tab_spec(D_ROPE),
        batch_spec(n_mem, d_m), batch_spec(n_mem, d_m),
        _const_spec(lw["wbr"].shape),
    ]
    out_shape = [
        jax.ShapeDtypeStruct((bsz, seq, d_model), F32),
        jax.ShapeDtypeStruct((bsz, seq, d_model), BF16),
        jax.ShapeDtypeStruct((bsz, seq, d_c), F32),
        jax.ShapeDtypeStruct((bsz, seq, H_C * LANES), BF16),
        jax.ShapeDtypeStruct((bsz, seq, kv_lora), F32),
        jax.ShapeDtypeStruct((bsz, seq, D_ROPE), F32),
        jax.ShapeDtypeStruct((bsz, SUBLANES, d_b), F32),
        jax.ShapeDtypeStruct((bsz, 1, d_b), F32),
    ]
    out_specs = [
        row_spec(d_model), row_spec(d_model), row_spec(d_c), row_spec(H_C * LANES),
        row_spec(kv_lora), row_spec(D_ROPE), batch_spec(SUBLANES, d_b), batch_spec(1, d_b),
    ]
    if emit_v:
        out_shape.append(jax.ShapeDtypeStruct((bsz, seq, d_a), F32))
        out_specs.append(row_spec(d_a))
    kern = functools.partial(
        _branch_kernel, tm=tm, sp_len=sp_len, d_a=d_a, d_b=d_b, d_c=d_c, d_m=d_m,
        q_lora=q_lora, kv_lora=kv_lora, d_model=d_model, cols=lw["cols"])
    return pl.pallas_call(
        kern, grid=grid, in_specs=in_specs, out_specs=out_specs, out_shape=out_shape,
        name="branch",
        compiler_params=pltpu.CompilerParams(
            dimension_semantics=("arbitrary", "arbitrary"), vmem_limit_bytes=VMEM_LIMIT),
    )(x, lw["w1"], lw["gln_g"], lw["gln_b"], lw["wsp"], lw["bsp"], lw["conv_w"], lw["conv_b"],
      lw["wgate"], lw["bgate"], lw["lam"], conv0_pad, h0, lw["q_norm"], lw["wuq"], lw["pq"],
      lw["kv_norm"], cq_t, sq_t, ck_t, sk_t, mk, mv, lw["wbr"])


def _kv_kernel(ckv_ref, kr_ref, wkc_ref, wkr_ref, wva_ref, wvb_ref, kcat_ref, va_ref, vb_ref):
    c = ckv_ref[...].astype(BF16)
    kr = kr_ref[...].astype(BF16)
    kcat_ref[...] = (_dot(c, wkc_ref[...]) + _dot(kr, wkr_ref[...])).astype(BF16)
    va_ref[...] = _dot(c, wva_ref[...]).astype(BF16)
    vb_ref[...] = _dot(c, wvb_ref[...]).astype(BF16)


def _kv_call(ckv_all, kr_all, lw, *, tr):
    rows, kv_lora = ckv_all.shape
    n_k = H_C * LANES
    n_v = H_C * D_V
    return pl.pallas_call(
        _kv_kernel, grid=(rows // tr,),
        in_specs=[pl.BlockSpec((tr, kv_lora), lambda i: (i, 0)),
                  pl.BlockSpec((tr, D_ROPE), lambda i: (i, 0)),
                  _const_spec(lw["wkc"].shape), _const_spec(lw["wkr"].shape),
                  _const_spec(lw["wva"].shape), _const_spec(lw["wvb"].shape)],
        out_specs=[pl.BlockSpec((tr, n_k), lambda i: (i, 0)),
                   pl.BlockSpec((tr, n_v), lambda i: (i, 0)),
                   pl.BlockSpec((tr, n_v), lambda i: (i, 0))],
        out_shape=[jax.ShapeDtypeStruct((rows, n_k), BF16),
                   jax.ShapeDtypeStruct((rows, n_v), BF16),
                   jax.ShapeDtypeStruct((rows, n_v), BF16)],
        name="kv",
        compiler_params=pltpu.CompilerParams(
            dimension_semantics=("arbitrary",), vmem_limit_bytes=VMEM_LIMIT),
    )(ckv_all, kr_all, lw["wkc"], lw["wkr"], lw["wva"], lw["wvb"])


def _attn_kernel(q_ref, k_ref, va_ref, vb_ref, cgs_ref, o_ref, *, tq, tk, causal, scale):
    i = pl.program_id(2)
    q_pair = (q_ref[0, :, 0:LANES], q_ref[0, :, LANES:2 * LANES])
    lane_lo = lax.broadcasted_iota(jnp.int32, (1, LANES), 1) < D_V

    def step(j, carry, masked):
        (m_a, l_a, m_b, l_b, acc) = carry
        start = pl.multiple_of(j * tk, tk)
        ks = k_ref[0, pl.ds(start, tk), :]
        new = []
        probs = []
        for q_h, k_h, m_h, l_h in ((q_pair[0], ks[:, 0:LANES], m_a, l_a),
                                   (q_pair[1], ks[:, LANES:2 * LANES], m_b, l_b)):
            sc = _dot_nt(q_h, k_h) * scale
            if masked:
                q_chunk = (i * tq + lax.broadcasted_iota(jnp.int32, (tq, 1), 0)) // CHUNK
                k_chunk = (j * tk + lax.broadcasted_iota(jnp.int32, (1, tk), 1)) // CHUNK
                sc = jnp.where(k_chunk <= q_chunk, sc, -1e30)
            m_new = jnp.maximum(m_h, jnp.max(sc, -1, keepdims=True))
            alpha = jnp.exp(m_h - m_new)
            e = jnp.exp(sc - m_new)
            new.append((m_new, alpha * l_h + jnp.sum(e, -1, keepdims=True), alpha))
            probs.append(e.astype(BF16))
        alpha_lanes = jnp.where(lane_lo, new[0][2], new[1][2])
        acc = (acc * alpha_lanes + _dot(probs[0], va_ref[0, pl.ds(start, tk), :])
               + _dot(probs[1], vb_ref[0, pl.ds(start, tk), :]))
        return (new[0][0], new[0][1], new[1][0], new[1][1], acc)

    neg = jnp.full((tq, 1), -jnp.inf, F32)
    zero = jnp.zeros((tq, 1), F32)
    carry = (neg, zero, neg, zero, jnp.zeros((tq, LANES), F32))
    if causal:
        carry = lax.fori_loop(0, i, lambda j, c: step(j, c, False), carry)
        carry = step(i, carry, True)
    else:
        carry = step(0, carry, False)
    (_, l_a, _, l_b, acc) = carry
    o_ref[0] = (acc / jnp.where(lane_lo, l_a, l_b) * cgs_ref[0]).astype(BF16)


def _attn_call(qcat, kcat, va, vb, cgs, *, tq, tk, causal):
    bsz, seq, _ = qcat.shape
    t_k = kcat.shape[1]
    n_pairs = H_C // 2
    kern = functools.partial(_attn_kernel, tq=tq, tk=tk, causal=causal,
                             scale=(D_NOPE + D_ROPE) ** -0.5)
    return pl.pallas_call(
        kern, grid=(bsz, n_pairs, seq // tq),
        in_specs=[pl.BlockSpec((1, tq, 2 * LANES), lambda b, p, i: (b, i, p)),
                  pl.BlockSpec((1, t_k, 2 * LANES), lambda b, p, i: (b, 0, p)),
                  pl.BlockSpec((1, t_k, LANES), lambda b, p, i: (b, 0, p)),
                  pl.BlockSpec((1, t_k, LANES), lambda b, p, i: (b, 0, p)),
                  pl.BlockSpec((1, tq, LANES), lambda b, p, i: (b, i, p))],
        out_specs=pl.BlockSpec((1, tq, LANES), lambda b, p, i: (b, i, p)),
        out_shape=jax.ShapeDtypeStruct((bsz, seq, H_C * D_V), BF16),
        name="attn",
        compiler_params=pltpu.CompilerParams(
            dimension_semantics=("arbitrary", "arbitrary", "arbitrary"),
            vmem_limit_bytes=VMEM_LIMIT),
    )(qcat, kcat, va, vb, cgs)


def _merge_kernel(x_ref, partial_ref, g2_ref, oc_ref, wbr_ref, wout_ref, lng_ref, lnb_ref, o_ref,
                  *, c_lo, d_c, alpha):
    yc = _dot(oc_ref[...], wbr_ref[c_lo:c_lo + d_c, :])
    merged = partial_ref[...] + g2_ref[...].astype(F32) * yc
    y = _dot(merged.astype(BF16), wout_ref[...])
    o_ref[...] = _layer_norm(alpha * x_ref[...] + y, lng_ref[...], lnb_ref[...])


def _merge_call(x, partial, g2, oc, lw, *, tr, alpha):
    rows, d_model = x.shape
    d_a, d_b, d_c, _ = lw["dims"]
    kern = functools.partial(_merge_kernel, c_lo=d_a + d_b, d_c=d_c, alpha=alpha)

    def row_spec(width):
        return pl.BlockSpec((tr, width), lambda i: (i, 0))

    return pl.pallas_call(
        kern, grid=(rows // tr,),
        in_specs=[row_spec(d_model), row_spec(d_model), row_spec(d_model), row_spec(d_c),
                  _const_spec(lw["wbr"].shape), _const_spec(lw["wout"].shape),
                  _const_spec(lw["ln_g"].shape), _const_spec(lw["ln_b"].shape)],
        out_specs=row_spec(d_model),
        out_shape=jax.ShapeDtypeStruct((rows, d_model), F32),
        name="merge",
        compiler_params=pltpu.CompilerParams(
            dimension_semantics=("arbitrary",), vmem_limit_bytes=VMEM_LIMIT),
    )(x, partial, g2, oc, lw["wbr"], lw["wout"], lw["ln_g"], lw["ln_b"])


def _mem_kernel(m_ref, w_ref, k_ref, v_ref, *, d_m):
    kv = _dot(m_ref[...].astype(BF16), w_ref[...])
    k_ref[...] = kv[:, 0:d_m]
    v_ref[...] = kv[:, d_m:2 * d_m]


def _mem_call(mem2d, wmem, *, tr):
    rows, d_model = mem2d.shape
    d_m = wmem.shape[1] // 2
    return pl.pallas_call(
        functools.partial(_mem_kernel, d_m=d_m), grid=(rows // tr,),
        in_specs=[pl.BlockSpec((tr, d_model), lambda i: (i, 0)), _const_spec(wmem.shape)],
        out_specs=[pl.BlockSpec((tr, d_m), lambda i: (i, 0)), pl.BlockSpec((tr, d_m), lambda i: (i, 0))],
        out_shape=[jax.ShapeDtypeStruct((rows, d_m), F32), jax.ShapeDtypeStruct((rows, d_m), F32)],
        name="mem",
        compiler_params=pltpu.CompilerParams(
            dimension_semantics=("arbitrary",), vmem_limit_bytes=VMEM_LIMIT),
    )(mem2d, wmem)


def _block_diag(w):
    h, n, _ = w.shape
    eye = jnp.eye(h, dtype=w.dtype)
    return (eye[:, None, :, None] * w[:, :, None, :]).reshape(h * n, h * n)


def _prep_layer(l, sp_len, n_seq_rep, w_in, gmlp_ln_g, gmlp_ln_b, gmlp_ws, gmlp_bs, lru_conv_w,
                lru_conv_b, lru_w_r, lru_b_r, lru_w_i, lru_b_i, lru_lambda, mla_q_norm, mla_w_uq,
                mla_kv_norm, mla_w_ukv, w_br, w_out, ln_g, ln_b):
    d_model = w_in.shape[1]
    d_a = gmlp_ln_g.shape[1]
    d_b = lru_lambda.shape[1]
    q_lora = mla_q_norm.shape[1]
    kv_lora = mla_kv_norm.shape[1]
    d_c = H_C * D_V
    d_m = H_M * DH_M
    w = w_in[l]
    o_cq = 3 * d_a + 2 * d_b
    o_kr = o_cq + q_lora + kv_lora
    o_cg = o_kr + D_ROPE
    o_mq = o_cg + d_c
    o_g = o_mq + d_m
    half = D_ROPE // 2
    c_all = jnp.concatenate(
        [w[:, o_cq:o_cg], w[:, o_kr + half:o_cg], w[:, o_kr:o_kr + half],
         jnp.zeros((d_model, 4 * LANES - (q_lora + kv_lora + 2 * D_ROPE)), w.dtype)], axis=1)
    pieces = [("a_u", w[:, 0:d_a]), ("a_v", w[:, d_a:2 * d_a]), ("a_g", w[:, 2 * d_a:3 * d_a]),
              ("b_x", w[:, 3 * d_a:3 * d_a + d_b]), ("b_g", w[:, 3 * d_a + d_b:o_cq]),
              ("c_all", c_all), ("c_g", w[:, o_cg:o_mq]), ("m_q", w[:, o_mq:o_g])]
    for k in range(N_BRANCH):
        pieces.append((f"g{k}", w[:, o_g + k * d_model:o_g + (k + 1) * d_model]))
    cols, off = {}, 0
    for name, p in pieces:
        cols[name] = (off, off + p.shape[1])
        off += p.shape[1]
    w1 = jnp.concatenate([p for _, p in pieces], axis=1).astype(BF16)

    ws = jnp.tril(gmlp_ws[l][:, :sp_len, :sp_len])
    bs = gmlp_bs[l][:, :sp_len]
    bsp = jnp.repeat(jnp.transpose(bs), d_a // G_A, axis=1)

    wq = mla_w_uq[l].reshape(q_lora, H_C, D_NOPE + D_ROPE)
    wuq = jnp.concatenate([wq[:, :, :D_NOPE].reshape(q_lora, -1),
                           wq[:, :, D_NOPE:D_NOPE + half].reshape(q_lora, -1),
                           wq[:, :, D_NOPE + half:].reshape(q_lora, -1)], axis=1).astype(BF16)
    n_nope = H_C * D_NOPE
    src = []
    for h in range(H_C):
        src += [h * D_NOPE + j for j in range(D_NOPE)]
        src += [n_nope + h * half + j for j in range(half)]
        src += [n_nope + H_C * half + h * half + j for j in range(half)]
        src += [-1] * (LANES - D_NOPE - D_ROPE)
    src = jnp.asarray(src)
    pq = (jnp.arange(n_nope + H_C * D_ROPE)[:, None] == src[None, :]).astype(BF16)

    wkv = mla_w_ukv[l].reshape(kv_lora, H_C, D_NOPE + D_V)
    wk = wkv[:, :, :D_NOPE]
    wv = wkv[:, :, D_NOPE:]
    wkc = jnp.concatenate([wk, jnp.zeros((kv_lora, H_C, LANES - D_NOPE), wk.dtype)], axis=2)
    wkc = wkc.reshape(kv_lora, H_C * LANES).astype(BF16)
    lane = jnp.arange(H_C * LANES) % LANES
    wkr = ((lane[None, :] - D_NOPE) == jnp.arange(D_ROPE)[:, None]).astype(BF16)
    even = (jnp.arange(H_C) % 2 == 0)[None, :, None]
    wva = jnp.where(even, wv, 0.0).reshape(kv_lora, H_C * D_V).astype(BF16)
    wvb = jnp.where(even, 0.0, wv).reshape(kv_lora, H_C * D_V).astype(BF16)

    return dict(
        dims=(d_a, d_b, d_c, d_m), q_lora=q_lora, kv_lora=kv_lora, cols=cols, w1=w1,
        gln_g=gmlp_ln_g[l][None, :], gln_b=gmlp_ln_b[l][None, :],
        wsp=ws.astype(BF16), bsp=bsp,
        conv_w=lru_conv_w[l], conv_b=lru_conv_b[l][None, :],
        wgate=jnp.concatenate([_block_diag(lru_w_r[l]), _block_diag(lru_w_i[l])], axis=1).astype(BF16),
        bgate=jnp.stack([lru_b_r[l], lru_b_i[l]]),
        lam=lru_lambda[l][None, :],
        q_norm=mla_q_norm[l][None, :], wuq=wuq, pq=pq, kv_norm=mla_kv_norm[l][None, :],
        wkc=wkc, wkr=wkr, wva=wva, wvb=wvb,
        wbr=w_br[l].astype(BF16), wout=w_out[l].astype(BF16),
        ln_g=ln_g[l][None, :], ln_b=ln_b[l][None, :],
    )


def _rope_tables(pos):
    half = D_ROPE // 2
    freq = ROPE_BASE ** (-jnp.arange(half, dtype=F32) / half)
    ang = pos.astype(F32)[:, None] * freq[None, :]
    cos, sin = jnp.cos(ang), jnp.sin(ang)
    cos_h, sin_h = jnp.tile(cos, (1, H_C)), jnp.tile(sin, (1, H_C))
    return (jnp.concatenate([cos_h, cos_h], axis=1), jnp.concatenate([-sin_h, sin_h], axis=1),
            jnp.concatenate([cos, cos], axis=1), jnp.concatenate([-sin, sin], axis=1))


def _pad_conv_state(conv):
    return jnp.pad(conv, ((0, 0), (SUBLANES - (CONV_W - 1), 0), (0, 0)))


def _trunk_layer(x, lw, rope, conv0, h0, mk, mv, past_ckv, past_kr, *, tm, tq, alpha, emit_v):
    bsz, seq, d_model = x.shape
    outs = _branch_call(x, lw, _pad_conv_state(conv0), h0[:, None, :], mk, mv, rope, tm=tm, emit_v=emit_v)
    partial, g2, cgs, qcat, ckv_new, kr_new, conv_pad, h_new = outs[:8]
    v_rows = outs[8] if emit_v else None
    if past_ckv is None:
        ckv_all, kr_all = ckv_new, kr_new
    else:
        ckv_all = jnp.concatenate([past_ckv, ckv_new], axis=1)
        kr_all = jnp.concatenate([past_kr, kr_new], axis=1)
    t_k = ckv_all.shape[1]
    tr_kv = 512 if t_k % 512 == 0 else t_k
    kcat, va, vb = _kv_call(ckv_all.reshape(bsz * t_k, -1), kr_all.reshape(bsz * t_k, -1), lw, tr=tr_kv)
    causal = past_ckv is None
    oc = _attn_call(qcat, kcat.reshape(bsz, t_k, -1), va.reshape(bsz, t_k, -1), vb.reshape(bsz, t_k, -1),
                    cgs, tq=tq, tk=tq if causal else t_k, causal=causal)
    rows = bsz * seq
    x_new = _merge_call(x.reshape(rows, d_model), partial.reshape(rows, d_model), g2.reshape(rows, d_model),
                        oc.reshape(rows, -1), lw, tr=min(rows, 256), alpha=alpha)
    return (x_new.reshape(bsz, seq, d_model), v_rows, conv_pad[:, SUBLANES - (CONV_W - 1):], h_new[:, 0],
            ckv_new, kr_new)


def kernel(x_prompt, x_sample, mem_prompt, cache_mla_ckv, cache_mla_krope, cache_mem_k, cache_mem_v,
           state_lru_h, state_lru_conv, w_in, gmlp_ln_g, gmlp_ln_b, gmlp_ws, gmlp_bs,
           lru_conv_w, lru_conv_b, lru_w_r, lru_b_r, lru_w_i, lru_b_i, lru_lambda,
           mla_q_norm, mla_w_uq, mla_kv_norm, mla_w_ukv, mem_w_k, mem_w_v, w_br, w_out, ln_g, ln_b):
    bp, tp, d_model = x_prompt.shape
    bs, ts, _ = x_sample.shape
    depth = w_in.shape[0]
    past_len = cache_mla_ckv.shape[2]
    n_mem = mem_prompt.shape[1]
    d_b = lru_lambda.shape[1]
    alpha = (2.0 * depth) ** 0.25
    weights = (w_in, gmlp_ln_g, gmlp_ln_b, gmlp_ws, gmlp_bs, lru_conv_w, lru_conv_b, lru_w_r, lru_b_r,
               lru_w_i, lru_b_i, lru_lambda, mla_q_norm, mla_w_uq, mla_kv_norm, mla_w_ukv, w_br, w_out,
               ln_g, ln_b)
    rope_p = _rope_tables(jnp.arange(tp))
    rope_s = _rope_tables(past_len + jnp.arange(ts))
    tm_p = min(tp, 256)
    assert tp % tm_p == 0 and tm_p % CHUNK == 0 and ts % SUBLANES == 0

    xp, xs = x_prompt, x_sample
    acc = [[] for _ in range(11)]
    for l in range(depth):
        lw_p = _prep_layer(l, min(tp, A_CHUNK), 1, *weights)
        lw_s = lw_p if min(ts, A_CHUNK) == min(tp, A_CHUNK) else _prep_layer(l, min(ts, A_CHUNK), 1, *weights)
        wmem = jnp.concatenate([mem_w_k[l], mem_w_v[l]], axis=1).astype(BF16)
        mk, mv = _mem_call(mem_prompt.reshape(bp * n_mem, d_model), wmem, tr=n_mem)
        mk = mk.reshape(bp, n_mem, -1)
        mv = mv.reshape(bp, n_mem, -1)
        xp, _, conv_n, h_n, ckv_n, kr_n = _trunk_layer(
            xp, lw_p, rope_p, jnp.zeros((bp, CONV_W - 1, d_b), F32), jnp.zeros((bp, d_b), F32),
            mk, mv, None, None, tm=tm_p, tq=tm_p, alpha=alpha, emit_v=False)
        for k, val in zip(range(6), (ckv_n, kr_n, mk.reshape(bp, n_mem, H_M, DH_M),
                                     mv.reshape(bp, n_mem, H_M, DH_M), h_n, conv_n)):
            acc[k].append(val)
        xs, v_n, conv_n, h_n, ckv_n, kr_n = _trunk_layer(
            xs, lw_s, rope_s, state_lru_conv[l], state_lru_h[l],
            cache_mem_k[l].reshape(bs, n_mem, -1), cache_mem_v[l].reshape(bs, n_mem, -1),
            cache_mla_ckv[l], cache_mla_krope[l], tm=ts, tq=ts, alpha=alpha, emit_v=True)
        for k, val in zip(range(6, 11), (ckv_n, kr_n, h_n, conv_n, v_n)):
            acc[k].append(val)
    return (xp, xs) + tuple(jnp.stack(a) for a in acc)
```

```python
import functools
import math

import jax
import jax.numpy as jnp
from jax import lax
from jax.experimental import pallas as pl
from jax.experimental.pallas import tpu as pltpu

CHUNK = 64
G_A = 4
A_CHUNK = 128
H_B = 8
CONV_W = 4
LRU_C = 8.0
H_C = 8
D_NOPE = 64
D_ROPE = 32
D_V = 64
ROPE_BASE = 10000.0
H_M = 4
DH_M = 64
N_BRANCH = 4
EPS = 1e-6

LANES = 128
SUBLANES = 8
VMEM_LIMIT = 56 * 1024 * 1024
ROW_TILE = 256

F32 = jnp.float32
BF16 = jnp.bfloat16


def _dot(a, b):
    return jnp.dot(a, b, preferred_element_type=F32)


def _dot_nt(a, b):
    return lax.dot_general(a, b, (((1,), (1,)), ((), ())), preferred_element_type=F32)


def _dot_tn(a, b):
    return lax.dot_general(a, b, (((0,), (0,)), ((), ())), preferred_element_type=F32)


def _sigmoid(x):
    return 1.0 / (1.0 + jnp.exp(-x))


def _silu(x):
    return x * _sigmoid(x)


def _gelu(x):
    return jax.nn.gelu(x)


def _expm1_nonpos(x):
    u = jnp.exp(x)
    near = (u - 1.0) * x / jnp.log(jnp.where(u == 1.0, 2.0, jnp.maximum(u, 0.5)))
    return jnp.where(u == 1.0, x, jnp.where(u > 0.5, near, u - 1.0))


def _layer_norm(x, g, b):
    mu = jnp.mean(x, -1, keepdims=True)
    var = jnp.mean(jnp.square(x - mu), -1, keepdims=True)
    return (x - mu) * lax.rsqrt(var + EPS) * g + b


def _rms_norm(x, g):
    return x * lax.rsqrt(jnp.mean(jnp.square(x), -1, keepdims=True) + EPS) * g


def _const_spec(shape):
    nd = len(shape)
    return pl.BlockSpec(shape, lambda *_: (0,) * nd, pipeline_mode=pl.Buffered(1))


def _params(n_axes):
    return pltpu.CompilerParams(dimension_semantics=("arbitrary",) * n_axes,
                                vmem_limit_bytes=VMEM_LIMIT)


def _branch_kernel(x_ref, w1_ref, wcgt_ref, glng_ref, glnb_ref, wsp_ref, bsp_ref, convw_ref, convb_ref,
                   wgate_ref, bgate_ref, lam_ref, conv0_ref, h0_ref, qnorm_ref, wuq_ref, pqt_ref,
                   kvnorm_ref, cq_ref, sq_ref, ck_ref, sk_ref, mk_ref, mv_ref, wbr_ref,
                   partial_ref, g2_ref, cgst_ref, qt_ref, ckv_ref, kr_ref, conv_ref, h_ref,
                   *maybe_v_ref, tm, sp_len, d_a, d_b, d_c, d_m, q_lora, kv_lora, cols):
    t = pl.program_id(1)
    xb = x_ref[0].astype(BF16)

    def zin(name):
        lo, hi = cols[name]
        return _dot(xb, w1_ref[:, lo:hi])

    u = _gelu(zin("a_u"))
    v = _layer_norm(_gelu(zin("a_v")), glng_ref[...], glnb_ref[...])
    if maybe_v_ref:
        maybe_v_ref[0][0] = v
    vb = v.astype(BF16)
    n_groups = d_a // LANES
    row_blocks = []
    for c in range(tm // sp_len):
        col_blocks = [_dot(wsp_ref[g], vb[c * sp_len:(c + 1) * sp_len, g * LANES:(g + 1) * LANES])
                      for g in range(n_groups)]
        row_blocks.append(jnp.concatenate(col_blocks, axis=1) + bsp_ref[...])
    s = row_blocks[0] if len(row_blocks) == 1 else jnp.concatenate(row_blocks, axis=0)
    oa = (u * s) * _silu(zin("a_g"))
    partial = _sigmoid(zin("g0")) * _dot(oa.astype(BF16), wbr_ref[0:d_a, :])

    @pl.when(t == 0)
    def _():
        conv_ref[...] = conv0_ref[...]
        h_ref[...] = h0_ref[...]

    bx = zin("b_x")
    hist = conv_ref[0]
    row8 = lax.broadcasted_iota(jnp.int32, (SUBLANES, 1), 0)
    xc = convb_ref[...]
    for k in range(CONV_W):
        shift = CONV_W - 1 - k
        if shift == 0:
            sh = bx
        else:
            sh = pltpu.roll(bx, shift, axis=0)
            head = jnp.where(row8 < shift, pltpu.roll(hist, shift, axis=0), sh[0:SUBLANES])
            sh = head if tm == SUBLANES else jnp.concatenate([head, sh[SUBLANES:]], axis=0)
        xc = xc + sh * convw_ref[k:k + 1, :]
    conv_ref[0] = bx[tm - SUBLANES:tm]

    xcb = xc.astype(BF16)
    r = _sigmoid(_dot(xcb, wgate_ref[:, 0:d_b]) + bgate_ref[0:1, :])
    i_gate = _sigmoid(_dot(xcb, wgate_ref[:, d_b:2 * d_b]) + bgate_ref[1:2, :])
    neg_lam = -lam_ref[...]
    softplus = jnp.maximum(neg_lam, 0.0) + jnp.log1p(jnp.exp(-jnp.abs(neg_lam)))
    log_a = (-LRU_C * r) * softplus
    a = jnp.exp(log_a)
    bval = jnp.sqrt(-_expm1_nonpos(2.0 * log_a)) * (i_gate * xc)
    rows = lax.broadcasted_iota(jnp.int32, (tm, 1), 0)
    d = 1
    while d < tm:
        keep = rows >= d
        a_sh = pltpu.roll(a, d, axis=0)
        b_sh = pltpu.roll(bval, d, axis=0)
        bval = jnp.where(keep, a * b_sh + bval, bval)
        a = jnp.where(keep, a * a_sh, a)
        d *= 2
    h = a * h_ref[0] + bval
    h_ref[0] = h[tm - 1:tm]
    ob = h * _silu(zin("b_g"))
    partial = partial + _sigmoid(zin("g1")) * _dot(ob.astype(BF16), wbr_ref[d_a:d_a + d_b, :])

    mq = zin("m_q").astype(BF16)
    mkb = mk_ref[0].astype(BF16)
    mvb = mv_ref[0].astype(BF16)
    lane = lax.broadcasted_iota(jnp.int32, (1, LANES), 1)
    slabs = []
    for p in range(d_m // LANES):
        mq_p = mq[:, p * LANES:(p + 1) * LANES]
        mk_p = mkb[:, p * LANES:(p + 1) * LANES]
        mv_p = mvb[:, p * LANES:(p + 1) * LANES]
        acc = None
        for half in range(LANES // DH_M):
            sel = (lane >= half * DH_M) & (lane < (half + 1) * DH_M)
            sc = _dot_nt(mq_p, jnp.where(sel, mk_p, jnp.zeros_like(mk_p))) * (DH_M ** -0.5)
            e = jnp.exp(sc - jnp.max(sc, -1, keepdims=True))
            prob = (e / jnp.sum(e, -1, keepdims=True)).astype(BF16)
            o = _dot(prob, jnp.where(sel, mv_p, jnp.zeros_like(mv_p)))
            acc = o if acc is None else acc + o
        slabs.append(acc)
    om = jnp.concatenate(slabs, axis=1)
    m_lo = d_a + d_b + d_c
    partial = partial + _sigmoid(zin("g3")) * _dot(om.astype(BF16), wbr_ref[m_lo:m_lo + d_m, :])
    partial_ref[0] = partial
    g2_ref[0] = _sigmoid(zin("g2")).astype(BF16)
    cgst_ref[0] = _silu(_dot_nt(wcgt_ref[...], xb))

    zc = zin("c_all")
    cqn = _rms_norm(zc[:, 0:q_lora], qnorm_ref[...]).astype(BF16)
    qall = _dot(cqn, wuq_ref[...])
    n_nope = H_C * D_NOPE
    x1 = qall[:, n_nope:n_nope + LANES]
    x2 = qall[:, n_nope + LANES:n_nope + 2 * LANES]
    yr = jnp.concatenate([x1, x2], axis=1) * cq_ref[...] + jnp.concatenate([x2, x1], axis=1) * sq_ref[...]
    q_in = jnp.concatenate([qall[:, 0:n_nope], yr], axis=1).astype(BF16)
    qt_ref[0] = _dot_nt(pqt_ref[...], q_in).astype(BF16)
    ckv_ref[0] = _rms_norm(zc[:, q_lora:q_lora + kv_lora], kvnorm_ref[...])
    r_lo = q_lora + kv_lora
    kr_ref[0] = (zc[:, r_lo:r_lo + D_ROPE] * ck_ref[...]
                 + zc[:, r_lo + D_ROPE:r_lo + 2 * D_ROPE] * sk_ref[...])


def _branch_call(x, lw, conv0_pad, h0, mk, mv, rope, *, tm, emit_v):
    bsz, seq, d_model = x.shape
    d_a, d_b, d_c, d_m = lw["dims"]
    q_lora, kv_lora = lw["q_lora"], lw["kv_lora"]
    sp_len = lw["wsp"].shape[-1]
    n_mem = mk.shape[1]
    cq_t, sq_t, ck_t, sk_t = rope
    grid = (bsz, seq // tm)

    def row_spec(width):
        return pl.BlockSpec((1, tm, width), lambda b, t: (b, t, 0))

    def col_spec(height):
        return pl.BlockSpec((1, height, tm), lambda b, t: (b, 0, t))

    def tab_spec(width):
        return pl.BlockSpec((tm, width), lambda b, t: (t, 0))

    def batch_spec(rows, width):
        return pl.BlockSpec((1, rows, width), lambda b, t: (b, 0, 0))

    in_specs = [
        row_spec(d_model),
        _const_spec(lw["w1"].shape), _const_spec(lw["wcgt"].shape),
        _const_spec(lw["gln_g"].shape), _const_spec(lw["gln_b"].shape),
        _const_spec(lw["wsp"].shape), _const_spec(lw["bsp"].shape),
        _const_spec(lw["conv_w"].shape), _const_spec(lw["conv_b"].shape),
        _const_spec(lw["wgate"].shape), _const_spec(lw["bgate"].shape),
        _const_spec(lw["lam"].shape),
        batch_spec(SUBLANES, d_b), batch_spec(1, d_b),
        _const_spec(lw["q_norm"].shape), _const_spec(lw["wuq"].shape), _const_spec(lw["pqt"].shape),
        _const_spec(lw["kv_norm"].shape),
        tab_spec(2 * LANES), tab_spec(2 * LANES), tab_spec(D_ROPE), tab_spec(D_ROPE),
        batch_spec(n_mem, d_m), batch_spec(n_mem, d_m),
        _const_spec(lw["wbr"].shape),
    ]
    out_shape = [
        jax.ShapeDtypeStruct((bsz, seq, d_model), F32),
        jax.ShapeDtypeStruct((bsz, seq, d_model), BF16),
        jax.ShapeDtypeStruct((bsz, d_c, seq), F32),
        jax.ShapeDtypeStruct((bsz, H_C * LANES, seq), BF16),
        jax.ShapeDtypeStruct((bsz, seq, kv_lora), F32),
        jax.ShapeDtypeStruct((bsz, seq, D_ROPE), F32),
        jax.ShapeDtypeStruct((bsz, SUBLANES, d_b), F32),
        jax.ShapeDtypeStruct((bsz, 1, d_b), F32),
    ]
    out_specs = [
        row_spec(d_model), row_spec(d_model), col_spec(d_c), col_spec(H_C * LANES),
        row_spec(kv_lora), row_spec(D_ROPE), batch_spec(SUBLANES, d_b), batch_spec(1, d_b),
    ]
    if emit_v:
        out_shape.append(jax.ShapeDtypeStruct((bsz, seq, d_a), F32))
        out_specs.append(row_spec(d_a))
    kern = functools.partial(
        _branch_kernel, tm=tm, sp_len=sp_len, d_a=d_a, d_b=d_b, d_c=d_c, d_m=d_m,
        q_lora=q_lora, kv_lora=kv_lora, cols=lw["cols"])
    return pl.pallas_call(
        kern, grid=grid, in_specs=in_specs, out_specs=out_specs, out_shape=out_shape,
        name="branch", compiler_params=_params(2),
    )(x, lw["w1"], lw["wcgt"], lw["gln_g"], lw["gln_b"], lw["wsp"], lw["bsp"], lw["conv_w"],
      lw["conv_b"], lw["wgate"], lw["bgate"], lw["lam"], conv0_pad, h0, lw["q_norm"], lw["wuq"],
      lw["pqt"], lw["kv_norm"], cq_t, sq_t, ck_t, sk_t, mk, mv, lw["wbr"])


def _kv_kernel(ckv_ref, kr_ref, wkc_ref, wkr_ref, wvt_ref, kcat_ref, vt_ref):
    c = ckv_ref[0].astype(BF16)
    kr = kr_ref[0].astype(BF16)
    kcat_ref[0] = (_dot(c, wkc_ref[...]) + _dot(kr, wkr_ref[...])).astype(BF16)
    vt_ref[0, 0] = _dot_nt(wvt_ref[...], c).astype(BF16)


def _kv_call(ckv_all, kr_all, lw, *, tk):
    bsz, t_k, kv_lora = ckv_all.shape
    n_k = H_C * LANES
    n_v = H_C * D_V
    nkt = t_k // tk
    return pl.pallas_call(
        _kv_kernel, grid=(bsz, nkt),
        in_specs=[pl.BlockSpec((1, tk, kv_lora), lambda b, j: (b, j, 0)),
                  pl.BlockSpec((1, tk, D_ROPE), lambda b, j: (b, j, 0)),
                  _const_spec(lw["wkc"].shape), _const_spec(lw["wkr"].shape),
                  _const_spec(lw["wvt"].shape)],
        out_specs=[pl.BlockSpec((1, tk, n_k), lambda b, j: (b, j, 0)),
                   pl.BlockSpec((1, 1, n_v, tk), lambda b, j: (b, j, 0, 0))],
        out_shape=[jax.ShapeDtypeStruct((bsz, t_k, n_k), BF16),
                   jax.ShapeDtypeStruct((bsz, nkt, n_v, tk), BF16)],
        name="kv", compiler_params=_params(2),
    )(ckv_all, kr_all, lw["wkc"], lw["wkr"], lw["wvt"])


HEADS_PER_STEP = 2
_EXP2_SCALE = (D_NOPE + D_ROPE) ** -0.5 * math.log2(math.e)


def _scores(k_tile, qt_ref, out_refs=None):
    res = [_dot(k_tile[:, h * LANES:(h + 1) * LANES], qt_ref[0, h * LANES:(h + 1) * LANES, :])
           for h in range(HEADS_PER_STEP)]
    if out_refs is None:
        return res
    for h in range(HEADS_PER_STEP):
        out_refs[h] = res[h]
    return None


def _softmax_tile(s_t, m_old, l_old, mask):
    if mask is not None:
        s_t = jnp.where(mask, s_t, -1e30)
    m_new = jnp.maximum(m_old, jnp.max(s_t, axis=0, keepdims=True))
    alpha = jnp.exp2((m_old - m_new) * _EXP2_SCALE)
    e = jnp.exp2((s_t - m_new) * _EXP2_SCALE)
    return e.astype(BF16), m_new, alpha * l_old + jnp.sum(e, axis=0, keepdims=True), alpha


def _attn_causal_kernel(qt_ref, k_ref, vt_ref, cgst_ref, o_ref, s0, s1, e0, e1, acc_scr, *, tq, tk):
    i = pl.program_id(2)
    nh = HEADS_PER_STEP

    def qk(j, s_out):
        _scores(k_ref[0, pl.ds(pl.multiple_of(j * tk, tk), tk), :], qt_ref, s_out)

    def pv(j, e_in):
        vt = vt_ref[0, j]
        return [_dot(vt[h * D_V:(h + 1) * D_V, :], e_in[h]) for h in range(nh)]

    def softmax(j, s_in, e_out, stats, masked):
        mask = None
        if masked:
            q_chunk = (i * tq + lax.broadcasted_iota(jnp.int32, (1, tq), 1)) // CHUNK
            k_chunk = (j * tk + lax.broadcasted_iota(jnp.int32, (tk, 1), 0)) // CHUNK
            mask = k_chunk <= q_chunk
        new = []
        for h in range(nh):
            e, m_new, l_new, alpha = _softmax_tile(s_in[h], stats[h][0], stats[h][1], mask)
            e_out[h] = e
            new.append((m_new, l_new, alpha))
        return tuple(new)

    def acc_update(pvs, stats):
        for h in range(nh):
            acc_scr[h] = acc_scr[h] * stats[h][2] + pvs[h]

    def stage(j, s_cur, s_nxt, e_cur, e_prev, stats):
        pvs = pv(jnp.maximum(j - 1, 0), e_prev)
        qk(j + 1, s_nxt)
        new = softmax(j, s_cur, e_cur, stats, False)
        acc_update(pvs, stats)
        return new

    def final(j, s_cur, e_cur, e_prev, stats):
        pvs = pv(jnp.maximum(j - 1, 0), e_prev)
        new = softmax(j, s_cur, e_cur, stats, True)
        acc_update(pvs, stats)
        acc_update(pv(j, e_cur), new)
        for h in range(nh):
            o_ref[0, h * D_V:(h + 1) * D_V, :] = (
                acc_scr[h] / new[h][1] * cgst_ref[0, h * D_V:(h + 1) * D_V, :]).astype(BF16)

    e1[...] = jnp.zeros(e1.shape, BF16)
    acc_scr[...] = jnp.zeros(acc_scr.shape, F32)
    qk(0, s0)
    stats = tuple((jnp.full((1, tq), -jnp.inf, F32), jnp.zeros((1, tq), F32), jnp.ones((1, tq), F32))
                  for _ in range(nh))

    def body(t, stats):
        stats = stage(2 * t, s0, s1, e0, e1, stats)
        return stage(2 * t + 1, s1, s0, e1, e0, stats)

    stats = lax.fori_loop(0, i // 2, body, stats)

    @pl.when(i % 2 == 1)
    def _():
        final(i, s1, e1, e0, stage(i - 1, s0, s1, e0, e1, stats))

    @pl.when(i % 2 == 0)
    def _():
        final(i, s0, e0, e1, stats)


def _attn_full_kernel(qt_ref, k_ref, vt_ref, cgst_ref, o_ref):
    tq = qt_ref.shape[-1]
    scores = _scores(k_ref[0], qt_ref)
    vt = vt_ref[0, 0]
    for h in range(HEADS_PER_STEP):
        e, _, l, _ = _softmax_tile(scores[h], jnp.full((1, tq), -jnp.inf, F32), jnp.zeros((1, tq), F32), None)
        acc = _dot(vt[h * D_V:(h + 1) * D_V, :], e)
        o_ref[0, h * D_V:(h + 1) * D_V, :] = (acc / l * cgst_ref[0, h * D_V:(h + 1) * D_V, :]).astype(BF16)


def _attn_call(qt, kcat, vt, cgst, *, tq, causal):
    bsz, _, seq = qt.shape
    t_k = kcat.shape[1]
    nkt, _, tk = vt.shape[1:]
    nh = HEADS_PER_STEP
    in_specs = [pl.BlockSpec((1, nh * LANES, tq), lambda b, p, i: (b, p, i)),
                pl.BlockSpec((1, t_k, nh * LANES), lambda b, p, i: (b, 0, p)),
                pl.BlockSpec((1, nkt, nh * D_V, tk), lambda b, p, i: (b, 0, p, 0)),
                pl.BlockSpec((1, nh * D_V, tq), lambda b, p, i: (b, p, i))]
    if causal:
        assert tk == tq and seq == t_k
        kern = functools.partial(_attn_causal_kernel, tq=tq, tk=tk)
        scratch = [pltpu.VMEM((nh, tk, tq), F32), pltpu.VMEM((nh, tk, tq), F32),
                   pltpu.VMEM((nh, tk, tq), BF16), pltpu.VMEM((nh, tk, tq), BF16),
                   pltpu.VMEM((nh, D_V, tq), F32)]
    else:
        assert nkt == 1 and seq == tq
        kern = _attn_full_kernel
        scratch = []
    return pl.pallas_call(
        kern, grid=(bsz, H_C // nh, seq // tq),
        in_specs=in_specs,
        out_specs=pl.BlockSpec((1, nh * D_V, tq), lambda b, p, i: (b, p, i)),
        out_shape=jax.ShapeDtypeStruct((bsz, H_C * D_V, seq), BF16),
        scratch_shapes=scratch,
        name="attn", compiler_params=_params(3),
    )(qt, kcat, vt, cgst)


def _merge_kernel(x_ref, partial_ref, g2_ref, oct_ref, wbr_ref, wout_ref, lng_ref, lnb_ref, o_ref,
                  *, c_lo, d_c, alpha):
    yc = _dot_tn(oct_ref[0], wbr_ref[c_lo:c_lo + d_c, :])
    merged = partial_ref[0] + g2_ref[0].astype(F32) * yc
    y = _dot(merged.astype(BF16), wout_ref[...])
    o_ref[0] = _layer_norm(alpha * x_ref[0] + y, lng_ref[...], lnb_ref[...])


def _merge_call(x, partial, g2, oct, lw, *, tr, alpha):
    bsz, seq, d_model = x.shape
    d_a, d_b, d_c, _ = lw["dims"]
    kern = functools.partial(_merge_kernel, c_lo=d_a + d_b, d_c=d_c, alpha=alpha)

    def row_spec(width):
        return pl.BlockSpec((1, tr, width), lambda b, i: (b, i, 0))

    return pl.pallas_call(
        kern, grid=(bsz, seq // tr),
        in_specs=[row_spec(d_model), row_spec(d_model), row_spec(d_model),
                  pl.BlockSpec((1, d_c, tr), lambda b, i: (b, 0, i)),
                  _const_spec(lw["wbr"].shape), _const_spec(lw["wout"].shape),
                  _const_spec(lw["ln_g"].shape), _const_spec(lw["ln_b"].shape)],
        out_specs=row_spec(d_model),
        out_shape=jax.ShapeDtypeStruct((bsz, seq, d_model), F32),
        name="merge", compiler_params=_params(2),
    )(x, partial, g2, oct, lw["wbr"], lw["wout"], lw["ln_g"], lw["ln_b"])


def _mem_kernel(m_ref, w_ref, k_ref, v_ref, *, d_m):
    kv = _dot(m_ref[...].astype(BF16), w_ref[...])
    k_ref[...] = kv[:, 0:d_m]
    v_ref[...] = kv[:, d_m:2 * d_m]


def _mem_call(mem2d, wmem, *, tr):
    rows, d_model = mem2d.shape
    d_m = wmem.shape[1] // 2
    return pl.pallas_call(
        functools.partial(_mem_kernel, d_m=d_m), grid=(rows // tr,),
        in_specs=[pl.BlockSpec((tr, d_model), lambda i: (i, 0)), _const_spec(wmem.shape)],
        out_specs=[pl.BlockSpec((tr, d_m), lambda i: (i, 0)), pl.BlockSpec((tr, d_m), lambda i: (i, 0))],
        out_shape=[jax.ShapeDtypeStruct((rows, d_m), F32), jax.ShapeDtypeStruct((rows, d_m), F32)],
        name="mem", compiler_params=_params(1),
    )(mem2d, wmem)


def _block_diag(w):
    h, n, _ = w.shape
    eye = jnp.eye(h, dtype=w.dtype)
    return (eye[:, None, :, None] * w[:, :, None, :]).reshape(h * n, h * n)


def _prep_layer(l, sp_len, w_in, gmlp_ln_g, gmlp_ln_b, gmlp_ws, gmlp_bs, lru_conv_w,
                lru_conv_b, lru_w_r, lru_b_r, lru_w_i, lru_b_i, lru_lambda, mla_q_norm, mla_w_uq,
                mla_kv_norm, mla_w_ukv, w_br, w_out, ln_g, ln_b):
    d_model = w_in.shape[1]
    d_a = gmlp_ln_g.shape[1]
    d_b = lru_lambda.shape[1]
    q_lora = mla_q_norm.shape[1]
    kv_lora = mla_kv_norm.shape[1]
    d_c = H_C * D_V
    d_m = H_M * DH_M
    w = w_in[l]
    o_cq = 3 * d_a + 2 * d_b
    o_kr = o_cq + q_lora + kv_lora
    o_cg = o_kr + D_ROPE
    o_mq = o_cg + d_c
    o_g = o_mq + d_m
    half = D_ROPE // 2
    c_all = jnp.concatenate(
        [w[:, o_cq:o_cg], w[:, o_kr + half:o_cg], w[:, o_kr:o_kr + half],
         jnp.zeros((d_model, 4 * LANES - (q_lora + kv_lora + 2 * D_ROPE)), w.dtype)], axis=1)
    pieces = [("a_u", w[:, 0:d_a]), ("a_v", w[:, d_a:2 * d_a]), ("a_g", w[:, 2 * d_a:3 * d_a]),
              ("b_x", w[:, 3 * d_a:3 * d_a + d_b]), ("b_g", w[:, 3 * d_a + d_b:o_cq]),
              ("c_all", c_all), ("m_q", w[:, o_mq:o_g])]
    for k in range(N_BRANCH):
        pieces.append((f"g{k}", w[:, o_g + k * d_model:o_g + (k + 1) * d_model]))
    cols, off = {}, 0
    for name, p in pieces:
        cols[name] = (off, off + p.shape[1])
        off += p.shape[1]
    w1 = jnp.concatenate([p for _, p in pieces], axis=1).astype(BF16)
    wcgt = jnp.transpose(w[:, o_cg:o_mq]).astype(BF16)

    ws = jnp.tril(gmlp_ws[l][:, :sp_len, :sp_len])
    bs = gmlp_bs[l][:, :sp_len]
    bsp = jnp.repeat(jnp.transpose(bs), d_a // G_A, axis=1)

    wq = mla_w_uq[l].reshape(q_lora, H_C, D_NOPE + D_ROPE)
    wuq = jnp.concatenate([wq[:, :, :D_NOPE].reshape(q_lora, -1),
                           wq[:, :, D_NOPE:D_NOPE + half].reshape(q_lora, -1),
                           wq[:, :, D_NOPE + half:].reshape(q_lora, -1)], axis=1).astype(BF16)
    n_nope = H_C * D_NOPE
    src = []
    for h in range(H_C):
        src += [h * D_NOPE + j for j in range(D_NOPE)]
        src += [n_nope + h * half + j for j in range(half)]
        src += [n_nope + H_C * half + h * half + j for j in range(half)]
        src += [-1] * (LANES - D_NOPE - D_ROPE)
    src = jnp.asarray(src)
    pqt = (src[:, None] == jnp.arange(n_nope + H_C * D_ROPE)[None, :]).astype(BF16)

    wkv = mla_w_ukv[l].reshape(kv_lora, H_C, D_NOPE + D_V)
    wk = wkv[:, :, :D_NOPE]
    wv = wkv[:, :, D_NOPE:]
    wkc = jnp.concatenate([wk, jnp.zeros((kv_lora, H_C, LANES - D_NOPE), wk.dtype)], axis=2)
    wkc = wkc.reshape(kv_lora, H_C * LANES).astype(BF16)
    lane = jnp.arange(H_C * LANES) % LANES
    wkr = ((lane[None, :] - D_NOPE) == jnp.arange(D_ROPE)[:, None]).astype(BF16)
    wvt = jnp.transpose(wv.reshape(kv_lora, H_C * D_V)).astype(BF16)

    return dict(
        dims=(d_a, d_b, d_c, d_m), q_lora=q_lora, kv_lora=kv_lora, cols=cols, w1=w1, wcgt=wcgt,
        gln_g=gmlp_ln_g[l][None, :], gln_b=gmlp_ln_b[l][None, :],
        wsp=ws.astype(BF16), bsp=bsp,
        conv_w=lru_conv_w[l], conv_b=lru_conv_b[l][None, :],
        wgate=jnp.concatenate([_block_diag(lru_w_r[l]), _block_diag(lru_w_i[l])], axis=1).astype(BF16),
        bgate=jnp.stack([lru_b_r[l], lru_b_i[l]]),
        lam=lru_lambda[l][None, :],
        q_norm=mla_q_norm[l][None, :], wuq=wuq, pqt=pqt, kv_norm=mla_kv_norm[l][None, :],
        wkc=wkc, wkr=wkr, wvt=wvt,
        wbr=w_br[l].astype(BF16), wout=w_out[l].astype(BF16),
        ln_g=ln_g[l][None, :], ln_b=ln_b[l][None, :],
    )


def _rope_tables(pos):
    half = D_ROPE // 2
    freq = ROPE_BASE ** (-jnp.arange(half, dtype=F32) / half)
    ang = pos.astype(F32)[:, None] * freq[None, :]
    cos, sin = jnp.cos(ang), jnp.sin(ang)
    cos_h, sin_h = jnp.tile(cos, (1, H_C)), jnp.tile(sin, (1, H_C))
    return (jnp.concatenate([cos_h, cos_h], axis=1), jnp.concatenate([-sin_h, sin_h], axis=1),
            jnp.concatenate([cos, cos], axis=1), jnp.concatenate([-sin, sin], axis=1))


def _pad_conv_state(conv):
    return jnp.pad(conv, ((0, 0), (SUBLANES - (CONV_W - 1), 0), (0, 0)))


def _trunk_layer(x, lw, rope, conv0, h0, mk, mv, past_ckv, past_kr, *, tm, alpha, emit_v):
    outs = _branch_call(x, lw, _pad_conv_state(conv0), h0[:, None, :], mk, mv, rope, tm=tm, emit_v=emit_v)
    partial, g2, cgst, qt, ckv_new, kr_new, conv_pad, h_new = outs[:8]
    v_rows = outs[8] if emit_v else None
    causal = past_ckv is None
    if causal:
        ckv_all, kr_all = ckv_new, kr_new
    else:
        ckv_all = jnp.concatenate([past_ckv, ckv_new], axis=1)
        kr_all = jnp.concatenate([past_kr, kr_new], axis=1)
    kcat, vt = _kv_call(ckv_all, kr_all, lw, tk=tm if causal else ckv_all.shape[1])
    oct = _attn_call(qt, kcat, vt, cgst, tq=tm, causal=causal)
    x_new = _merge_call(x, partial, g2, oct, lw, tr=tm, alpha=alpha)
    return x_new, v_rows, conv_pad[:, SUBLANES - (CONV_W - 1):], h_new[:, 0], ckv_new, kr_new


def kernel(x_prompt, x_sample, mem_prompt, cache_mla_ckv, cache_mla_krope, cache_mem_k, cache_mem_v,
           state_lru_h, state_lru_conv, w_in, gmlp_ln_g, gmlp_ln_b, gmlp_ws, gmlp_bs,
           lru_conv_w, lru_conv_b, lru_w_r, lru_b_r, lru_w_i, lru_b_i, lru_lambda,
           mla_q_norm, mla_w_uq, mla_kv_norm, mla_w_ukv, mem_w_k, mem_w_v, w_br, w_out, ln_g, ln_b):
    bp, tp, d_model = x_prompt.shape
    bs, ts, _ = x_sample.shape
    depth = w_in.shape[0]
    past_len = cache_mla_ckv.shape[2]
    n_mem = mem_prompt.shape[1]
    d_b = lru_lambda.shape[1]
    alpha = (2.0 * depth) ** 0.25
    weights = (w_in, gmlp_ln_g, gmlp_ln_b, gmlp_ws, gmlp_bs, lru_conv_w, lru_conv_b, lru_w_r, lru_b_r,
               lru_w_i, lru_b_i, lru_lambda, mla_q_norm, mla_w_uq, mla_kv_norm, mla_w_ukv, w_br, w_out,
               ln_g, ln_b)
    rope_p = _rope_tables(jnp.arange(tp))
    rope_s = _rope_tables(past_len + jnp.arange(ts))
    tm_p = min(tp, ROW_TILE)
    assert tp % tm_p == 0 and tm_p % CHUNK == 0 and ts % SUBLANES == 0

    xp, xs = x_prompt, x_sample
    acc = [[] for _ in range(11)]
    for l in range(depth):
        lw_p = _prep_layer(l, min(tp, A_CHUNK), *weights)
        lw_s = lw_p if min(ts, A_CHUNK) == min(tp, A_CHUNK) else _prep_layer(l, min(ts, A_CHUNK), *weights)
        wmem = jnp.concatenate([mem_w_k[l], mem_w_v[l]], axis=1).astype(BF16)
        mk, mv = _mem_call(mem_prompt.reshape(bp * n_mem, d_model), wmem, tr=n_mem)
        mk = mk.reshape(bp, n_mem, -1)
        mv = mv.reshape(bp, n_mem, -1)
        xp, _, conv_n, h_n, ckv_n, kr_n = _trunk_layer(
            xp, lw_p, rope_p, jnp.zeros((bp, CONV_W - 1, d_b), F32), jnp.zeros((bp, d_b), F32),
            mk, mv, None, None, tm=tm_p, alpha=alpha, emit_v=False)
        for k, val in zip(range(6), (ckv_n, kr_n, mk.reshape(bp, n_mem, H_M, DH_M),
                                     mv.reshape(bp, n_mem, H_M, DH_M), h_n, conv_n)):
            acc[k].append(val)
        xs, v_n, conv_n, h_n, ckv_n, kr_n = _trunk_layer(
            xs, lw_s, rope_s, state_lru_conv[l], state_lru_h[l],
            cache_mem_k[l].reshape(bs, n_mem, -1), cache_mem_v[l].reshape(bs, n_mem, -1),
            cache_mla_ckv[l], cache_mla_krope[l], tm=ts, alpha=alpha, emit_v=True)
        for k, val in zip(range(6, 11), (ckv_n, kr_n, h_n, conv_n, v_n)):
            acc[k].append(val)
    return (xp, xs) + tuple(jnp.stack(a) for a in acc)
```

```python
import functools
import math

import jax
import jax.numpy as jnp
from jax import lax
from jax.experimental import pallas as pl
from jax.experimental.pallas import tpu as pltpu

CHUNK = 64
G_A = 4
A_CHUNK = 128
H_B = 8
CONV_W = 4
LRU_C = 8.0
H_C = 8
D_NOPE = 64
D_ROPE = 32
D_V = 64
ROPE_BASE = 10000.0
H_M = 4
DH_M = 64
N_BRANCH = 4
EPS = 1e-6

LANES = 128
SUBLANES = 8
VMEM_LIMIT = 56 * 1024 * 1024
ROW_TILE = 256
MXU_TILE = 256

F32 = jnp.float32
BF16 = jnp.bfloat16


def _dot(a, b):
    return jnp.dot(a, b, preferred_element_type=F32)


def _dot_nt(a, b):
    return lax.dot_general(a, b, (((1,), (1,)), ((), ())), preferred_element_type=F32)


def _dot_tn(a, b):
    return lax.dot_general(a, b, (((0,), (0,)), ((), ())), preferred_element_type=F32)


def _sigmoid(x):
    return 1.0 / (1.0 + jnp.exp(-x))


def _silu(x):
    return x * _sigmoid(x)


def _gelu(x):
    return jax.nn.gelu(x)


def _expm1_nonpos(x):
    u = jnp.exp(x)
    near = (u - 1.0) * x / jnp.log(jnp.where(u == 1.0, 2.0, jnp.maximum(u, 0.5)))
    return jnp.where(u == 1.0, x, jnp.where(u > 0.5, near, u - 1.0))


def _layer_norm(x, g, b):
    mu = jnp.mean(x, -1, keepdims=True)
    var = jnp.mean(jnp.square(x - mu), -1, keepdims=True)
    return (x - mu) * lax.rsqrt(var + EPS) * g + b


def _rms_norm(x, g):
    return x * lax.rsqrt(jnp.mean(jnp.square(x), -1, keepdims=True) + EPS) * g


def _const_spec(shape):
    nd = len(shape)
    return pl.BlockSpec(shape, lambda *_: (0,) * nd, pipeline_mode=pl.Buffered(1))


def _params(n_axes):
    return pltpu.CompilerParams(dimension_semantics=("arbitrary",) * n_axes,
                                vmem_limit_bytes=VMEM_LIMIT)


def _branch_kernel(x_ref, w1_ref, wcgt_ref, glng_ref, glnb_ref, wsp_ref, bsp_ref, convw_ref, convb_ref,
                   wgate_ref, bgate_ref, lam_ref, conv0_ref, h0_ref, qnorm_ref, wuqt_ref,
                   kvnorm_ref, cosq_ref, sinq_ref, ck_ref, sk_ref, mk_ref, mv_ref, wbr_ref,
                   partial_ref, g2_ref, cgst_ref, qt_ref, ckv_ref, kr_ref, conv_ref, h_ref,
                   *maybe_v_ref, tm, sp_len, d_a, d_b, d_c, d_m, q_lora, kv_lora, cols):
    t = pl.program_id(1)
    xb = x_ref[0].astype(BF16)

    def zin(name):
        lo, hi = cols[name]
        return _dot(xb, w1_ref[:, lo:hi])

    z_u, z_v, z_ag, z_g0 = zin("a_u"), zin("a_v"), zin("a_g"), zin("g0")
    u = _gelu(z_u)
    v = _layer_norm(_gelu(z_v), glng_ref[...], glnb_ref[...])
    if maybe_v_ref:
        maybe_v_ref[0][0] = v
    vb = v.astype(BF16)
    n_groups = d_a // LANES
    row_blocks = []
    for c in range(tm // sp_len):
        col_blocks = [_dot(wsp_ref[g], vb[c * sp_len:(c + 1) * sp_len, g * LANES:(g + 1) * LANES])
                      for g in range(n_groups)]
        row_blocks.append(jnp.concatenate(col_blocks, axis=1) + bsp_ref[...])
    s = row_blocks[0] if len(row_blocks) == 1 else jnp.concatenate(row_blocks, axis=0)
    oa = (u * s) * _silu(z_ag)
    partial_ref[0] = _sigmoid(z_g0) * _dot(oa.astype(BF16), wbr_ref[0:d_a, :])

    @pl.when(t == 0)
    def _():
        conv_ref[...] = conv0_ref[...]
        h_ref[...] = h0_ref[...]

    bx = zin("b_x")
    hist = conv_ref[0]
    row8 = lax.broadcasted_iota(jnp.int32, (SUBLANES, 1), 0)
    xc = convb_ref[...]
    for k in range(CONV_W):
        shift = CONV_W - 1 - k
        if shift == 0:
            sh = bx
        else:
            sh = pltpu.roll(bx, shift, axis=0)
            head = jnp.where(row8 < shift, pltpu.roll(hist, shift, axis=0), sh[0:SUBLANES])
            sh = head if tm == SUBLANES else jnp.concatenate([head, sh[SUBLANES:]], axis=0)
        xc = xc + sh * convw_ref[k:k + 1, :]
    conv_ref[0] = bx[tm - SUBLANES:tm]

    xcb = xc.astype(BF16)
    n_blk = d_b // MXU_TILE
    ri = [_dot(xcb[:, k * MXU_TILE:(k + 1) * MXU_TILE], wgate_ref[k]) for k in range(n_blk)]
    z_bg, z_g1, z_g2 = zin("b_g"), zin("g1"), zin("g2")
    r = _sigmoid(jnp.concatenate([p[:, 0:MXU_TILE] for p in ri], axis=1) + bgate_ref[0:1, :])
    i_gate = _sigmoid(jnp.concatenate([p[:, MXU_TILE:2 * MXU_TILE] for p in ri], axis=1) + bgate_ref[1:2, :])
    neg_lam = -lam_ref[...]
    softplus = jnp.maximum(neg_lam, 0.0) + jnp.log1p(jnp.exp(-jnp.abs(neg_lam)))
    log_a = (-LRU_C * r) * softplus
    a = jnp.exp(log_a)
    bval = jnp.sqrt(-_expm1_nonpos(2.0 * log_a)) * (i_gate * xc)
    rows = lax.broadcasted_iota(jnp.int32, (tm, 1), 0)
    d = 1
    while d < tm:
        keep = rows >= d
        a_sh = pltpu.roll(a, d, axis=0)
        b_sh = pltpu.roll(bval, d, axis=0)
        bval = jnp.where(keep, a * b_sh + bval, bval)
        a = jnp.where(keep, a * a_sh, a)
        d *= 2
    h = a * h_ref[0] + bval
    h_ref[0] = h[tm - 1:tm]
    ob = h * _silu(z_bg)
    partial_ref[0] += _sigmoid(z_g1) * _dot(ob.astype(BF16), wbr_ref[d_a:d_a + d_b, :])
    g2_ref[0] = _sigmoid(z_g2).astype(BF16)

    mq = zin("m_q").astype(BF16)
    z_g3, z_c = zin("g3"), zin("c_all")
    cg_t = _dot_nt(wcgt_ref[...], xb)
    mkb = mk_ref[0].astype(BF16)
    mvb = mv_ref[0].astype(BF16)
    lane = lax.broadcasted_iota(jnp.int32, (1, LANES), 1)
    slabs = []
    for p in range(d_m // LANES):
        mq_p = mq[:, p * LANES:(p + 1) * LANES]
        mk_p = mkb[:, p * LANES:(p + 1) * LANES]
        mv_p = mvb[:, p * LANES:(p + 1) * LANES]
        acc = None
        for half in range(LANES // DH_M):
            sel = (lane >= half * DH_M) & (lane < (half + 1) * DH_M)
            sc = _dot_nt(mq_p, jnp.where(sel, mk_p, jnp.zeros_like(mk_p))) * (DH_M ** -0.5)
            e = jnp.exp(sc - jnp.max(sc, -1, keepdims=True))
            prob = (e / jnp.sum(e, -1, keepdims=True)).astype(BF16)
            o = _dot(prob, jnp.where(sel, mv_p, jnp.zeros_like(mv_p)))
            acc = o if acc is None else acc + o
        slabs.append(acc)
    om = jnp.concatenate(slabs, axis=1)
    m_lo = d_a + d_b + d_c
    partial_ref[0] += _sigmoid(z_g3) * _dot(om.astype(BF16), wbr_ref[m_lo:m_lo + d_m, :])
    cgst_ref[0] = _silu(cg_t)

    ckv_ref[0] = _rms_norm(z_c[:, q_lora:q_lora + kv_lora], kvnorm_ref[...])
    r_lo = q_lora + kv_lora
    kr_ref[0] = (z_c[:, r_lo:r_lo + D_ROPE] * ck_ref[...]
                 + z_c[:, r_lo + D_ROPE:r_lo + 2 * D_ROPE] * sk_ref[...])
    cqn = _rms_norm(z_c[:, 0:q_lora], qnorm_ref[...]).astype(BF16)
    q_t = _dot_nt(wuqt_ref[...], cqn)
    cos_t, sin_t = cosq_ref[...], sinq_ref[...]
    half = D_ROPE // 2
    for hd in range(H_C):
        lo = hd * LANES
        x1 = q_t[lo + D_NOPE:lo + D_NOPE + half]
        x2 = q_t[lo + D_NOPE + half:lo + D_NOPE + D_ROPE]
        qt_ref[0, lo:lo + D_NOPE, :] = q_t[lo:lo + D_NOPE].astype(BF16)
        qt_ref[0, lo + D_NOPE:lo + D_NOPE + half, :] = (x1 * cos_t - x2 * sin_t).astype(BF16)
        qt_ref[0, lo + D_NOPE + half:lo + D_NOPE + D_ROPE, :] = (x1 * sin_t + x2 * cos_t).astype(BF16)
        qt_ref[0, lo + D_NOPE + D_ROPE:lo + LANES, :] = jnp.zeros((LANES - D_NOPE - D_ROPE, tm), BF16)


def _branch_call(x, lw, conv0_pad, h0, mk, mv, rope, *, tm, emit_v):
    bsz, seq, d_model = x.shape
    d_a, d_b, d_c, d_m = lw["dims"]
    q_lora, kv_lora = lw["q_lora"], lw["kv_lora"]
    sp_len = lw["wsp"].shape[-1]
    n_mem = mk.shape[1]
    cos_t, sin_t, ck_t, sk_t = rope
    grid = (bsz, seq // tm)

    def row_spec(width):
        return pl.BlockSpec((1, tm, width), lambda b, t: (b, t, 0))

    def col_spec(height):
        return pl.BlockSpec((1, height, tm), lambda b, t: (b, 0, t))

    def tab_spec(width):
        return pl.BlockSpec((tm, width), lambda b, t: (t, 0))

    def batch_spec(rows, width):
        return pl.BlockSpec((1, rows, width), lambda b, t: (b, 0, 0))

    in_specs = [
        row_spec(d_model),
        _const_spec(lw["w1"].shape), _const_spec(lw["wcgt"].shape),
        _const_spec(lw["gln_g"].shape), _const_spec(lw["gln_b"].shape),
        _const_spec(lw["wsp"].shape), _const_spec(lw["bsp"].shape),
        _const_spec(lw["conv_w"].shape), _const_spec(lw["conv_b"].shape),
        _const_spec(lw["wgate"].shape), _const_spec(lw["bgate"].shape),
        _const_spec(lw["lam"].shape),
        batch_spec(SUBLANES, d_b), batch_spec(1, d_b),
        _const_spec(lw["q_norm"].shape), _const_spec(lw["wuqt"].shape),
        _const_spec(lw["kv_norm"].shape),
        pl.BlockSpec((D_ROPE // 2, tm), lambda b, t: (0, t)), pl.BlockSpec((D_ROPE // 2, tm), lambda b, t: (0, t)),
        tab_spec(D_ROPE), tab_spec(D_ROPE),
        batch_spec(n_mem, d_m), batch_spec(n_mem, d_m),
        _const_spec(lw["wbr"].shape),
    ]
    out_shape = [
        jax.ShapeDtypeStruct((bsz, seq, d_model), F32),
        jax.ShapeDtypeStruct((bsz, seq, d_model), BF16),
        jax.ShapeDtypeStruct((bsz, d_c, seq), F32),
        jax.ShapeDtypeStruct((bsz, H_C * LANES, seq), BF16),
        jax.ShapeDtypeStruct((bsz, seq, kv_lora), F32),
        jax.ShapeDtypeStruct((bsz, seq, D_ROPE), F32),
        jax.ShapeDtypeStruct((bsz, SUBLANES, d_b), F32),
        jax.ShapeDtypeStruct((bsz, 1, d_b), F32),
    ]
    out_specs = [
        row_spec(d_model), row_spec(d_model), col_spec(d_c), col_spec(H_C * LANES),
        row_spec(kv_lora), row_spec(D_ROPE), batch_spec(SUBLANES, d_b), batch_spec(1, d_b),
    ]
    if emit_v:
        out_shape.append(jax.ShapeDtypeStruct((bsz, seq, d_a), F32))
        out_specs.append(row_spec(d_a))
    kern = functools.partial(
        _branch_kernel, tm=tm, sp_len=sp_len, d_a=d_a, d_b=d_b, d_c=d_c, d_m=d_m,
        q_lora=q_lora, kv_lora=kv_lora, cols=lw["cols"])
    return pl.pallas_call(
        kern, grid=grid, in_specs=in_specs, out_specs=out_specs, out_shape=out_shape,
        name="branch", compiler_params=_params(2),
    )(x, lw["w1"], lw["wcgt"], lw["gln_g"], lw["gln_b"], lw["wsp"], lw["bsp"], lw["conv_w"],
      lw["conv_b"], lw["wgate"], lw["bgate"], lw["lam"], conv0_pad, h0, lw["q_norm"], lw["wuqt"],
      lw["kv_norm"], cos_t, sin_t, ck_t, sk_t, mk, mv, lw["wbr"])


V_ROWS = 80


def _kv_kernel(ckv_ref, kr_ref, wkc_ref, wkr_ref, wvt_ref, kcat_ref, vt_ref):
    c = ckv_ref[0].astype(BF16)
    kr = kr_ref[0].astype(BF16)
    kcat_ref[0] = (_dot(c, wkc_ref[...]) + _dot(kr, wkr_ref[...])).astype(BF16)
    v_t = _dot_nt(wvt_ref[...], c)
    row = lax.broadcasted_iota(jnp.int32, (v_t.shape[0], 1), 0)
    vt_ref[0, 0] = jnp.where(row % V_ROWS == D_V, 1.0, v_t).astype(BF16)


def _kv_call(ckv_all, kr_all, lw, *, tk):
    bsz, t_k, kv_lora = ckv_all.shape
    n_k = H_C * LANES
    n_v = H_C * V_ROWS
    nkt = t_k // tk
    return pl.pallas_call(
        _kv_kernel, grid=(bsz, nkt),
        in_specs=[pl.BlockSpec((1, tk, kv_lora), lambda b, j: (b, j, 0)),
                  pl.BlockSpec((1, tk, D_ROPE), lambda b, j: (b, j, 0)),
                  _const_spec(lw["wkc"].shape), _const_spec(lw["wkr"].shape),
                  _const_spec(lw["wvt"].shape)],
        out_specs=[pl.BlockSpec((1, tk, n_k), lambda b, j: (b, j, 0)),
                   pl.BlockSpec((1, 1, n_v, tk), lambda b, j: (b, j, 0, 0))],
        out_shape=[jax.ShapeDtypeStruct((bsz, t_k, n_k), BF16),
                   jax.ShapeDtypeStruct((bsz, nkt, n_v, tk), BF16)],
        name="kv", compiler_params=_params(2),
    )(ckv_all, kr_all, lw["wkc"], lw["wkr"], lw["wvt"])


HEADS_PER_STEP = 2
_EXP2_SCALE = (D_NOPE + D_ROPE) ** -0.5 * math.log2(math.e)


def _scores(k_tile, qt_ref, out_refs=None):
    res = [_dot(k_tile[:, h * LANES:(h + 1) * LANES], qt_ref[0, h * LANES:(h + 1) * LANES, :])
           for h in range(HEADS_PER_STEP)]
    if out_refs is None:
        return res
    for h in range(HEADS_PER_STEP):
        out_refs[h] = res[h]
    return None


def _softmax_tile(s_t, m_old, mask):
    if mask is not None:
        s_t = jnp.where(mask, s_t, -1e30)
    m_new = jnp.maximum(m_old, jnp.max(s_t, axis=0, keepdims=True))
    return jnp.exp2(s_t - m_new).astype(BF16), m_new, jnp.exp2(m_old - m_new)


def _attn_output(acc, cgs_t):
    return (acc[0:D_V] / acc[D_V:D_V + 1] * cgs_t).astype(BF16)


def _attn_causal_kernel(qt_ref, k_ref, vt_ref, cgst_ref, o_ref, s0, s1, e0, e1, acc_scr, *, tq, tk):
    i = pl.program_id(2)
    nh = HEADS_PER_STEP

    def qk(j, s_out):
        _scores(k_ref[0, pl.ds(pl.multiple_of(j * tk, tk), tk), :], qt_ref, s_out)

    def pv(j, e_in):
        vt = vt_ref[0, j]
        return [_dot(vt[h * V_ROWS:(h + 1) * V_ROWS, :], e_in[h]) for h in range(nh)]

    def softmax(j, s_in, e_out, stats, masked):
        mask = None
        if masked:
            q_chunk = (i * tq + lax.broadcasted_iota(jnp.int32, (1, tq), 1)) // CHUNK
            k_chunk = (j * tk + lax.broadcasted_iota(jnp.int32, (tk, 1), 0)) // CHUNK
            mask = k_chunk <= q_chunk
        new = []
        for h in range(nh):
            e, m_new, alpha = _softmax_tile(s_in[h], stats[h][0], mask)
            e_out[h] = e
            new.append((m_new, alpha))
        return tuple(new)

    def acc_update(pvs, stats):
        for h in range(nh):
            acc_scr[h] = acc_scr[h] * stats[h][1] + pvs[h]

    def stage(j, s_cur, s_nxt, e_cur, e_prev, stats, masked=False, prefetch=True):
        pvs = pv(jnp.maximum(j - 1, 0), e_prev)
        if prefetch:
            qk(j + 1, s_nxt)
        new = softmax(j, s_cur, e_cur, stats, masked)
        acc_update(pvs, stats)
        return new

    e1[...] = jnp.zeros(e1.shape, BF16)
    acc_scr[...] = jnp.zeros(acc_scr.shape, F32)
    qk(0, s0)
    stats = tuple((jnp.full((1, tq), -jnp.inf, F32), jnp.ones((1, tq), F32)) for _ in range(nh))

    def body(t, stats):
        stats = stage(2 * t, s0, s1, e0, e1, stats)
        return stage(2 * t + 1, s1, s0, e1, e0, stats)

    stats = lax.fori_loop(0, i, body, stats)
    j0 = 2 * i
    stats = stage(j0, s0, s1, e0, e1, stats, masked=True)
    stats = stage(j0 + 1, s1, s0, e1, e0, stats, masked=True, prefetch=False)
    acc_update(pv(j0 + 1, e1), stats)
    for h in range(nh):
        o_ref[0, h * D_V:(h + 1) * D_V, :] = _attn_output(acc_scr[h], cgst_ref[0, h * D_V:(h + 1) * D_V, :])


def _attn_full_kernel(qt_ref, k_ref, vt_ref, cgst_ref, o_ref):
    tq = qt_ref.shape[-1]
    scores = _scores(k_ref[0], qt_ref)
    vt = vt_ref[0, 0]
    for h in range(HEADS_PER_STEP):
        e, _, _ = _softmax_tile(scores[h], jnp.full((1, tq), -jnp.inf, F32), None)
        acc = _dot(vt[h * V_ROWS:(h + 1) * V_ROWS, :], e)
        o_ref[0, h * D_V:(h + 1) * D_V, :] = _attn_output(acc, cgst_ref[0, h * D_V:(h + 1) * D_V, :])


def _attn_call(qt, kcat, vt, cgst, *, tq, causal):
    bsz, _, seq = qt.shape
    t_k = kcat.shape[1]
    nkt, _, tk = vt.shape[1:]
    nh = HEADS_PER_STEP
    in_specs = [pl.BlockSpec((1, nh * LANES, tq), lambda b, p, i: (b, p, i)),
                pl.BlockSpec((1, t_k, nh * LANES), lambda b, p, i: (b, 0, p)),
                pl.BlockSpec((1, nkt, nh * V_ROWS, tk), lambda b, p, i: (b, 0, p, 0)),
                pl.BlockSpec((1, nh * D_V, tq), lambda b, p, i: (b, p, i))]
    if causal:
        assert tq == 2 * tk and seq == t_k and tk % CHUNK == 0
        kern = functools.partial(_attn_causal_kernel, tq=tq, tk=tk)
        scratch = [pltpu.VMEM((nh, tk, tq), F32), pltpu.VMEM((nh, tk, tq), F32),
                   pltpu.VMEM((nh, tk, tq), BF16), pltpu.VMEM((nh, tk, tq), BF16),
                   pltpu.VMEM((nh, V_ROWS, tq), F32)]
    else:
        assert nkt == 1 and seq == tq
        kern = _attn_full_kernel
        scratch = []
    return pl.pallas_call(
        kern, grid=(bsz, H_C // nh, seq // tq),
        in_specs=in_specs,
        out_specs=pl.BlockSpec((1, nh * D_V, tq), lambda b, p, i: (b, p, i)),
        out_shape=jax.ShapeDtypeStruct((bsz, H_C * D_V, seq), BF16),
        scratch_shapes=scratch,
        name="attn", compiler_params=_params(3),
    )(qt, kcat, vt, cgst)


def _merge_kernel(x_ref, partial_ref, g2_ref, oct_ref, wbr_ref, wout_ref, lng_ref, lnb_ref, o_ref,
                  *, c_lo, d_c, alpha):
    yc = _dot_tn(oct_ref[0], wbr_ref[c_lo:c_lo + d_c, :])
    merged = partial_ref[0] + g2_ref[0].astype(F32) * yc
    y = _dot(merged.astype(BF16), wout_ref[...])
    o_ref[0] = _layer_norm(alpha * x_ref[0] + y, lng_ref[...], lnb_ref[...])


def _merge_call(x, partial, g2, oct, lw, *, tr, alpha):
    bsz, seq, d_model = x.shape
    d_a, d_b, d_c, _ = lw["dims"]
    kern = functools.partial(_merge_kernel, c_lo=d_a + d_b, d_c=d_c, alpha=alpha)

    def row_spec(width):
        return pl.BlockSpec((1, tr, width), lambda b, i: (b, i, 0))

    return pl.pallas_call(
        kern, grid=(bsz, seq // tr),
        in_specs=[row_spec(d_model), row_spec(d_model), row_spec(d_model),
                  pl.BlockSpec((1, d_c, tr), lambda b, i: (b, 0, i)),
                  _const_spec(lw["wbr"].shape), _const_spec(lw["wout"].shape),
                  _const_spec(lw["ln_g"].shape), _const_spec(lw["ln_b"].shape)],
        out_specs=row_spec(d_model),
        out_shape=jax.ShapeDtypeStruct((bsz, seq, d_model), F32),
        name="merge", compiler_params=_params(2),
    )(x, partial, g2, oct, lw["wbr"], lw["wout"], lw["ln_g"], lw["ln_b"])


def _mem_kernel(m_ref, w_ref, k_ref, v_ref, *, d_m):
    kv = _dot(m_ref[...].astype(BF16), w_ref[...])
    k_ref[...] = kv[:, 0:d_m]
    v_ref[...] = kv[:, d_m:2 * d_m]


def _mem_call(mem2d, wmem, *, tr):
    rows, d_model = mem2d.shape
    d_m = wmem.shape[1] // 2
    return pl.pallas_call(
        functools.partial(_mem_kernel, d_m=d_m), grid=(rows // tr,),
        in_specs=[pl.BlockSpec((tr, d_model), lambda i: (i, 0)), _const_spec(wmem.shape)],
        out_specs=[pl.BlockSpec((tr, d_m), lambda i: (i, 0)), pl.BlockSpec((tr, d_m), lambda i: (i, 0))],
        out_shape=[jax.ShapeDtypeStruct((rows, d_m), F32), jax.ShapeDtypeStruct((rows, d_m), F32)],
        name="mem", compiler_params=_params(1),
    )(mem2d, wmem)


def _block_diag(w):
    h, n, _ = w.shape
    eye = jnp.eye(h, dtype=w.dtype)
    return (eye[:, None, :, None] * w[:, :, None, :]).reshape(h * n, h * n)


def _gate_blocks(w_r, w_i):
    per = MXU_TILE // w_r.shape[1]
    blocks = []
    for k in range(w_r.shape[0] // per):
        blocks.append(jnp.concatenate([_block_diag(w_r[k * per:(k + 1) * per]),
                                       _block_diag(w_i[k * per:(k + 1) * per])], axis=1))
    return jnp.stack(blocks)


def _prep_layer(l, sp_len, w_in, gmlp_ln_g, gmlp_ln_b, gmlp_ws, gmlp_bs, lru_conv_w,
                lru_conv_b, lru_w_r, lru_b_r, lru_w_i, lru_b_i, lru_lambda, mla_q_norm, mla_w_uq,
                mla_kv_norm, mla_w_ukv, w_br, w_out, ln_g, ln_b):
    d_model = w_in.shape[1]
    d_a = gmlp_ln_g.shape[1]
    d_b = lru_lambda.shape[1]
    q_lora = mla_q_norm.shape[1]
    kv_lora = mla_kv_norm.shape[1]
    d_c = H_C * D_V
    d_m = H_M * DH_M
    w = w_in[l]
    o_cq = 3 * d_a + 2 * d_b
    o_kr = o_cq + q_lora + kv_lora
    o_cg = o_kr + D_ROPE
    o_mq = o_cg + d_c
    o_g = o_mq + d_m
    half = D_ROPE // 2
    c_all = jnp.concatenate(
        [w[:, o_cq:o_cg], w[:, o_kr + half:o_cg], w[:, o_kr:o_kr + half],
         jnp.zeros((d_model, 4 * LANES - (q_lora + kv_lora + 2 * D_ROPE)), w.dtype)], axis=1)
    pieces = [("a_u", w[:, 0:d_a]), ("a_v", w[:, d_a:2 * d_a]), ("a_g", w[:, 2 * d_a:3 * d_a]),
              ("b_x", w[:, 3 * d_a:3 * d_a + d_b]), ("b_g", w[:, 3 * d_a + d_b:o_cq]),
              ("c_all", c_all), ("m_q", w[:, o_mq:o_g])]
    for k in range(N_BRANCH):
        pieces.append((f"g{k}", w[:, o_g + k * d_model:o_g + (k + 1) * d_model]))
    cols, off = {}, 0
    for name, p in pieces:
        cols[name] = (off, off + p.shape[1])
        off += p.shape[1]
    w1 = jnp.concatenate([p for _, p in pieces], axis=1).astype(BF16)
    wcgt = jnp.transpose(w[:, o_cg:o_mq]).astype(BF16)

    ws = jnp.tril(gmlp_ws[l][:, :sp_len, :sp_len])
    bs = gmlp_bs[l][:, :sp_len]
    bsp = jnp.repeat(jnp.transpose(bs), d_a // G_A, axis=1)

    wq = mla_w_uq[l].reshape(q_lora, H_C, D_NOPE + D_ROPE) * _EXP2_SCALE
    wq = jnp.concatenate([wq, jnp.zeros((q_lora, H_C, LANES - D_NOPE - D_ROPE), wq.dtype)], axis=2)
    wuqt = jnp.transpose(wq.reshape(q_lora, H_C * LANES)).astype(BF16)

    wkv = mla_w_ukv[l].reshape(kv_lora, H_C, D_NOPE + D_V)
    wk = wkv[:, :, :D_NOPE]
    wv = wkv[:, :, D_NOPE:]
    wkc = jnp.concatenate([wk, jnp.zeros((kv_lora, H_C, LANES - D_NOPE), wk.dtype)], axis=2)
    wkc = wkc.reshape(kv_lora, H_C * LANES).astype(BF16)
    lane = jnp.arange(H_C * LANES) % LANES
    wkr = ((lane[None, :] - D_NOPE) == jnp.arange(D_ROPE)[:, None]).astype(BF16)
    wv = jnp.concatenate([wv, jnp.zeros((kv_lora, H_C, V_ROWS - D_V), wv.dtype)], axis=2)
    wvt = jnp.transpose(wv.reshape(kv_lora, H_C * V_ROWS)).astype(BF16)

    return dict(
        dims=(d_a, d_b, d_c, d_m), q_lora=q_lora, kv_lora=kv_lora, cols=cols, w1=w1, wcgt=wcgt,
        gln_g=gmlp_ln_g[l][None, :], gln_b=gmlp_ln_b[l][None, :],
        wsp=ws.astype(BF16), bsp=bsp,
        conv_w=lru_conv_w[l], conv_b=lru_conv_b[l][None, :],
        wgate=_gate_blocks(lru_w_r[l], lru_w_i[l]).astype(BF16),
        bgate=jnp.stack([lru_b_r[l], lru_b_i[l]]),
        lam=lru_lambda[l][None, :],
        q_norm=mla_q_norm[l][None, :], wuqt=wuqt, kv_norm=mla_kv_norm[l][None, :],
        wkc=wkc, wkr=wkr, wvt=wvt,
        wbr=w_br[l].astype(BF16), wout=w_out[l].astype(BF16),
        ln_g=ln_g[l][None, :], ln_b=ln_b[l][None, :],
    )


def _rope_tables(pos):
    half = D_ROPE // 2
    freq = ROPE_BASE ** (-jnp.arange(half, dtype=F32) / half)
    ang = pos.astype(F32)[:, None] * freq[None, :]
    cos, sin = jnp.cos(ang), jnp.sin(ang)
    return (jnp.transpose(cos), jnp.transpose(sin),
            jnp.concatenate([cos, cos], axis=1), jnp.concatenate([-sin, sin], axis=1))


def _pad_conv_state(conv):
    return jnp.pad(conv, ((0, 0), (SUBLANES - (CONV_W - 1), 0), (0, 0)))


def _trunk_layer(x, lw, rope, conv0, h0, mk, mv, past_ckv, past_kr, *, tm, alpha, emit_v):
    outs = _branch_call(x, lw, _pad_conv_state(conv0), h0[:, None, :], mk, mv, rope, tm=tm, emit_v=emit_v)
    partial, g2, cgst, qt, ckv_new, kr_new, conv_pad, h_new = outs[:8]
    v_rows = outs[8] if emit_v else None
    causal = past_ckv is None
    if causal:
        ckv_all, kr_all = ckv_new, kr_new
    else:
        ckv_all = jnp.concatenate([past_ckv, ckv_new], axis=1)
        kr_all = jnp.concatenate([past_kr, kr_new], axis=1)
    kcat, vt = _kv_call(ckv_all, kr_all, lw, tk=tm if causal else ckv_all.shape[1])
    oct = _attn_call(qt, kcat, vt, cgst, tq=2 * tm if causal else tm, causal=causal)
    x_new = _merge_call(x, partial, g2, oct, lw, tr=tm, alpha=alpha)
    return x_new, v_rows, conv_pad[:, SUBLANES - (CONV_W - 1):], h_new[:, 0], ckv_new, kr_new


def kernel(x_prompt, x_sample, mem_prompt, cache_mla_ckv, cache_mla_krope, cache_mem_k, cache_mem_v,
           state_lru_h, state_lru_conv, w_in, gmlp_ln_g, gmlp_ln_b, gmlp_ws, gmlp_bs,
           lru_conv_w, lru_conv_b, lru_w_r, lru_b_r, lru_w_i, lru_b_i, lru_lambda,
           mla_q_norm, mla_w_uq, mla_kv_norm, mla_w_ukv, mem_w_k, mem_w_v, w_br, w_out, ln_g, ln_b):
    bp, tp, d_model = x_prompt.shape
    bs, ts, _ = x_sample.shape
    depth = w_in.shape[0]
    past_len = cache_mla_ckv.shape[2]
    n_mem = mem_prompt.shape[1]
    d_b = lru_lambda.shape[1]
    alpha = (2.0 * depth) ** 0.25
    weights = (w_in, gmlp_ln_g, gmlp_ln_b, gmlp_ws, gmlp_bs, lru_conv_w, lru_conv_b, lru_w_r, lru_b_r,
               lru_w_i, lru_b_i, lru_lambda, mla_q_norm, mla_w_uq, mla_kv_norm, mla_w_ukv, w_br, w_out,
               ln_g, ln_b)
    rope_p = _rope_tables(jnp.arange(tp))
    rope_s = _rope_tables(past_len + jnp.arange(ts))
    tm_p = min(tp, ROW_TILE)
    assert tp % (2 * tm_p) == 0 and tm_p % CHUNK == 0 and ts % SUBLANES == 0

    xp, xs = x_prompt, x_sample
    acc = [[] for _ in range(11)]
    for l in range(depth):
        lw_p = _prep_layer(l, min(tp, A_CHUNK), *weights)
        lw_s = lw_p if min(ts, A_CHUNK) == min(tp, A_CHUNK) else _prep_layer(l, min(ts, A_CHUNK), *weights)
        wmem = jnp.concatenate([mem_w_k[l], mem_w_v[l]], axis=1).astype(BF16)
        mk, mv = _mem_call(mem_prompt.reshape(bp * n_mem, d_model), wmem, tr=n_mem)
        mk = mk.reshape(bp, n_mem, -1)
        mv = mv.reshape(bp, n_mem, -1)
        xp, _, conv_n, h_n, ckv_n, kr_n = _trunk_layer(
            xp, lw_p, rope_p, jnp.zeros((bp, CONV_W - 1, d_b), F32), jnp.zeros((bp, d_b), F32),
            mk, mv, None, None, tm=tm_p, alpha=alpha, emit_v=False)
        for k, val in zip(range(6), (ckv_n, kr_n, mk.reshape(bp, n_mem, H_M, DH_M),
                                     mv.reshape(bp, n_mem, H_M, DH_M), h_n, conv_n)):
            acc[k].append(val)
        xs, v_n, conv_n, h_n, ckv_n, kr_n = _trunk_layer(
            xs, lw_s, rope_s, state_lru_conv[l], state_lru_h[l],
            cache_mem_k[l].reshape(bs, n_mem, -1), cache_mem_v[l].reshape(bs, n_mem, -1),
            cache_mla_ckv[l], cache_mla_krope[l], tm=ts, alpha=alpha, emit_v=True)
        for k, val in zip(range(6, 11), (ckv_n, kr_n, h_n, conv_n, v_n)):
            acc[k].append(val)
    return (xp, xs) + tuple(jnp.stack(a) for a in acc)
```

```python
import functools
import math

import jax
import jax.numpy as jnp
from jax import lax
from jax.experimental import pallas as pl
from jax.experimental.pallas import tpu as pltpu

CHUNK = 64
G_A = 4
A_CHUNK = 128
H_B = 8
CONV_W = 4
LRU_C = 8.0
H_C = 8
D_NOPE = 64
D_ROPE = 32
D_V = 64
ROPE_BASE = 10000.0
H_M = 4
DH_M = 64
N_BRANCH = 4
EPS = 1e-6

LANES = 128
SUBLANES = 8
VMEM_LIMIT = 56 * 1024 * 1024
ROW_TILE = 256
MXU_TILE = 256

F32 = jnp.float32
BF16 = jnp.bfloat16


def _dot(a, b):
    return jnp.dot(a, b, preferred_element_type=F32)


def _dot_nt(a, b):
    return lax.dot_general(a, b, (((1,), (1,)), ((), ())), preferred_element_type=F32)


def _dot_tn(a, b):
    return lax.dot_general(a, b, (((0,), (0,)), ((), ())), preferred_element_type=F32)


def _sigmoid(x):
    return 1.0 / (1.0 + jnp.exp(-x))


def _silu(x):
    return x * _sigmoid(x)


def _gelu(x):
    return jax.nn.gelu(x)


def _expm1_nonpos(x):
    u = jnp.exp(x)
    near = (u - 1.0) * x / jnp.log(jnp.where(u == 1.0, 2.0, jnp.maximum(u, 0.5)))
    return jnp.where(u == 1.0, x, jnp.where(u > 0.5, near, u - 1.0))


def _layer_norm(x, g, b):
    mu = jnp.mean(x, -1, keepdims=True)
    var = jnp.mean(jnp.square(x - mu), -1, keepdims=True)
    return (x - mu) * lax.rsqrt(var + EPS) * g + b


def _rms_norm(x, g):
    return x * lax.rsqrt(jnp.mean(jnp.square(x), -1, keepdims=True) + EPS) * g


def _const_spec(shape):
    nd = len(shape)
    return pl.BlockSpec(shape, lambda *_: (0,) * nd, pipeline_mode=pl.Buffered(1))


def _params(n_axes):
    return pltpu.CompilerParams(dimension_semantics=("arbitrary",) * n_axes,
                                vmem_limit_bytes=VMEM_LIMIT)


def _branch_kernel(x_ref, w1_ref, wcgt_ref, glng_ref, glnb_ref, wsp_ref, bsp_ref, convw_ref, convb_ref,
                   wgate_ref, bgate_ref, lam_ref, conv0_ref, h0_ref, qnorm_ref, wuqt_ref,
                   kvnorm_ref, cosq_ref, sinq_ref, ck_ref, sk_ref, mk_ref, mv_ref, wbr_ref,
                   partial_ref, g2_ref, cgst_ref, qt_ref, ckv_ref, kr_ref, conv_ref, h_ref,
                   *v_and_scratch, tm, sp_len, d_a, d_b, d_c, d_m, q_lora, kv_lora, cols):
    maybe_v_ref, merged_scr = v_and_scratch[:-1], v_and_scratch[-1]
    t = pl.program_id(1)
    xb = x_ref[0].astype(BF16)

    def zin(name):
        lo, hi = cols[name]
        return _dot(xb, w1_ref[:, lo:hi])

    z_u, z_v, z_ag, z_g0 = zin("a_u"), zin("a_v"), zin("a_g"), zin("g0")
    u = _gelu(z_u)
    v = _layer_norm(_gelu(z_v), glng_ref[...], glnb_ref[...])
    if maybe_v_ref:
        maybe_v_ref[0][0] = v
    vb = v.astype(BF16)
    n_groups = d_a // LANES
    row_blocks = []
    for c in range(tm // sp_len):
        col_blocks = [_dot(wsp_ref[g], vb[c * sp_len:(c + 1) * sp_len, g * LANES:(g + 1) * LANES])
                      for g in range(n_groups)]
        row_blocks.append(jnp.concatenate(col_blocks, axis=1) + bsp_ref[...])
    s = row_blocks[0] if len(row_blocks) == 1 else jnp.concatenate(row_blocks, axis=0)
    oa = (u * s) * _silu(z_ag)
    merged_scr[...] = _sigmoid(z_g0) * _dot(oa.astype(BF16), wbr_ref[0:d_a, :])

    @pl.when(t == 0)
    def _():
        conv_ref[...] = conv0_ref[...]
        h_ref[...] = h0_ref[...]

    bx = zin("b_x")
    hist = conv_ref[0]
    row8 = lax.broadcasted_iota(jnp.int32, (SUBLANES, 1), 0)
    xc = convb_ref[...]
    for k in range(CONV_W):
        shift = CONV_W - 1 - k
        if shift == 0:
            sh = bx
        else:
            sh = pltpu.roll(bx, shift, axis=0)
            head = jnp.where(row8 < shift, pltpu.roll(hist, shift, axis=0), sh[0:SUBLANES])
            sh = head if tm == SUBLANES else jnp.concatenate([head, sh[SUBLANES:]], axis=0)
        xc = xc + sh * convw_ref[k:k + 1, :]
    conv_ref[0] = bx[tm - SUBLANES:tm]

    xcb = xc.astype(BF16)
    n_blk = d_b // MXU_TILE
    ri = [_dot(xcb[:, k * MXU_TILE:(k + 1) * MXU_TILE], wgate_ref[k]) for k in range(n_blk)]
    z_bg, z_g1, z_g2 = zin("b_g"), zin("g1"), zin("g2")
    z_mq, z_g3, z_c = zin("m_q"), zin("g3"), zin("c_all")
    cg_t = _dot_nt(wcgt_ref[...], xb)
    r = _sigmoid(jnp.concatenate([p[:, 0:MXU_TILE] for p in ri], axis=1) + bgate_ref[0:1, :])
    i_gate = _sigmoid(jnp.concatenate([p[:, MXU_TILE:2 * MXU_TILE] for p in ri], axis=1) + bgate_ref[1:2, :])
    neg_lam = -lam_ref[...]
    softplus = jnp.maximum(neg_lam, 0.0) + jnp.log1p(jnp.exp(-jnp.abs(neg_lam)))
    log_a = (-LRU_C * r) * softplus
    a = jnp.exp(log_a)
    bval = jnp.sqrt(-_expm1_nonpos(2.0 * log_a)) * (i_gate * xc)
    rows = lax.broadcasted_iota(jnp.int32, (tm, 1), 0)
    d = 1
    while d < tm:
        keep = rows >= d
        a_sh = pltpu.roll(a, d, axis=0)
        b_sh = pltpu.roll(bval, d, axis=0)
        bval = jnp.where(keep, a * b_sh + bval, bval)
        a = jnp.where(keep, a * a_sh, a)
        d *= 2
    h = a * h_ref[0] + bval
    h_ref[0] = h[tm - 1:tm]
    ob = h * _silu(z_bg)
    merged_scr[...] += _sigmoid(z_g1) * _dot(ob.astype(BF16), wbr_ref[d_a:d_a + d_b, :])
    g2_ref[0] = _sigmoid(z_g2).astype(BF16)

    mq = z_mq.astype(BF16)
    mkb =mk_ref[0].astype(BF16)
    mvb = mv_ref[0].astype(BF16)
    lane = lax.broadcasted_iota(jnp.int32, (1, LANES), 1)
    slabs = []
    for p in range(d_m // LANES):
        mq_p = mq[:, p * LANES:(p + 1) * LANES]
        mk_p = mkb[:, p * LANES:(p + 1) * LANES]
        mv_p = mvb[:, p * LANES:(p + 1) * LANES]
        acc = None
        for half in range(LANES // DH_M):
            sel = (lane >= half * DH_M) & (lane < (half + 1) * DH_M)
            sc = _dot_nt(mq_p, jnp.where(sel, mk_p, jnp.zeros_like(mk_p))) * (DH_M ** -0.5)
            e = jnp.exp(sc - jnp.max(sc, -1, keepdims=True))
            prob = (e / jnp.sum(e, -1, keepdims=True)).astype(BF16)
            o = _dot(prob, jnp.where(sel, mv_p, jnp.zeros_like(mv_p)))
            acc = o if acc is None else acc + o
        slabs.append(acc)
    om = jnp.concatenate(slabs, axis=1)
    m_lo = d_a + d_b + d_c
    partial_ref[0] = (merged_scr[...]
                      + _sigmoid(z_g3) * _dot(om.astype(BF16), wbr_ref[m_lo:m_lo + d_m, :])).astype(BF16)
    cgst_ref[0] = _silu(cg_t)

    ckv_ref[0] = _rms_norm(z_c[:, q_lora:q_lora + kv_lora], kvnorm_ref[...])
    r_lo = q_lora + kv_lora
    kr_ref[0] = (z_c[:, r_lo:r_lo + D_ROPE] * ck_ref[...]
                 + z_c[:, r_lo + D_ROPE:r_lo + 2 * D_ROPE] * sk_ref[...])
    cqn = _rms_norm(z_c[:, 0:q_lora], qnorm_ref[...]).astype(BF16)
    q_t = _dot_nt(wuqt_ref[...], cqn)
    cos_t, sin_t = cosq_ref[...], sinq_ref[...]
    half = D_ROPE // 2
    for hd in range(H_C):
        lo = hd * LANES
        x1 = q_t[lo + D_NOPE:lo + D_NOPE + half]
        x2 = q_t[lo + D_NOPE + half:lo + D_NOPE + D_ROPE]
        qt_ref[0, lo:lo + D_NOPE, :] = q_t[lo:lo + D_NOPE].astype(BF16)
        qt_ref[0, lo + D_NOPE:lo + D_NOPE + half, :] = (x1 * cos_t - x2 * sin_t).astype(BF16)
        qt_ref[0, lo + D_NOPE + half:lo + D_NOPE + D_ROPE, :] = (x1 * sin_t + x2 * cos_t).astype(BF16)
        qt_ref[0, lo + D_NOPE + D_ROPE:lo + LANES, :] = jnp.zeros((LANES - D_NOPE - D_ROPE, tm), BF16)


def _branch_call(x, lw, conv0_pad, h0, mk, mv, rope, *, tm, emit_v):
    bsz, seq, d_model = x.shape
    d_a, d_b, d_c, d_m = lw["dims"]
    q_lora, kv_lora = lw["q_lora"], lw["kv_lora"]
    sp_len = lw["wsp"].shape[-1]
    n_mem = mk.shape[1]
    cos_t, sin_t, ck_t, sk_t = rope
    grid = (bsz, seq // tm)

    def row_spec(width):
        return pl.BlockSpec((1, tm, width), lambda b, t: (b, t, 0))

    def col_spec(height):
        return pl.BlockSpec((1, height, tm), lambda b, t: (b, 0, t))

    def tab_spec(width):
        return pl.BlockSpec((tm, width), lambda b, t: (t, 0))

    def batch_spec(rows, width):
        return pl.BlockSpec((1, rows, width), lambda b, t: (b, 0, 0))

    in_specs = [
        row_spec(d_model),
        _const_spec(lw["w1"].shape), _const_spec(lw["wcgt"].shape),
        _const_spec(lw["gln_g"].shape), _const_spec(lw["gln_b"].shape),
        _const_spec(lw["wsp"].shape), _const_spec(lw["bsp"].shape),
        _const_spec(lw["conv_w"].shape), _const_spec(lw["conv_b"].shape),
        _const_spec(lw["wgate"].shape), _const_spec(lw["bgate"].shape),
        _const_spec(lw["lam"].shape),
        batch_spec(SUBLANES, d_b), batch_spec(1, d_b),
        _const_spec(lw["q_norm"].shape), _const_spec(lw["wuqt"].shape),
        _const_spec(lw["kv_norm"].shape),
        pl.BlockSpec((D_ROPE // 2, tm), lambda b, t: (0, t)), pl.BlockSpec((D_ROPE // 2, tm), lambda b, t: (0, t)),
        tab_spec(D_ROPE), tab_spec(D_ROPE),
        batch_spec(n_mem, d_m), batch_spec(n_mem, d_m),
        _const_spec(lw["wbr"].shape),
    ]
    out_shape = [
        jax.ShapeDtypeStruct((bsz, seq, d_model), BF16),
        jax.ShapeDtypeStruct((bsz, seq, d_model), BF16),
        jax.ShapeDtypeStruct((bsz, d_c, seq), F32),
        jax.ShapeDtypeStruct((bsz, H_C * LANES, seq), BF16),
        jax.ShapeDtypeStruct((bsz, seq, kv_lora), F32),
        jax.ShapeDtypeStruct((bsz, seq, D_ROPE), F32),
        jax.ShapeDtypeStruct((bsz, SUBLANES, d_b), F32),
        jax.ShapeDtypeStruct((bsz, 1, d_b), F32),
    ]
    out_specs = [
        row_spec(d_model), row_spec(d_model), col_spec(d_c), col_spec(H_C * LANES),
        row_spec(kv_lora), row_spec(D_ROPE), batch_spec(SUBLANES, d_b), batch_spec(1, d_b),
    ]
    if emit_v:
        out_shape.append(jax.ShapeDtypeStruct((bsz, seq, d_a), F32))
        out_specs.append(row_spec(d_a))
    kern = functools.partial(
        _branch_kernel, tm=tm, sp_len=sp_len, d_a=d_a, d_b=d_b, d_c=d_c, d_m=d_m,
        q_lora=q_lora, kv_lora=kv_lora, cols=lw["cols"])
    return pl.pallas_call(
        kern, grid=grid, in_specs=in_specs, out_specs=out_specs, out_shape=out_shape,
        scratch_shapes=[pltpu.VMEM((tm, d_model), F32)],
        name="branch", compiler_params=_params(2),
    )(x, lw["w1"], lw["wcgt"], lw["gln_g"], lw["gln_b"], lw["wsp"], lw["bsp"], lw["conv_w"],
      lw["conv_b"], lw["wgate"], lw["bgate"], lw["lam"], conv0_pad, h0, lw["q_norm"], lw["wuqt"],
      lw["kv_norm"], cos_t, sin_t, ck_t, sk_t, mk, mv, lw["wbr"])


V_ROWS = 80


def _kv_kernel(ckv_ref, kr_ref, wkc_ref, wkr_ref, wvt_ref, kcat_ref, vt_ref):
    c = ckv_ref[0].astype(BF16)
    kr = kr_ref[0].astype(BF16)
    kcat_ref[0] = (_dot(c, wkc_ref[...]) + _dot(kr, wkr_ref[...])).astype(BF16)
    v_t = _dot_nt(wvt_ref[...], c)
    row = lax.broadcasted_iota(jnp.int32, (v_t.shape[0], 1), 0)
    vt_ref[0, 0] = jnp.where(row % V_ROWS == D_V, 1.0, v_t).astype(BF16)


def _kv_call(ckv_all, kr_all, lw, *, tk):
    bsz, t_k, kv_lora = ckv_all.shape
    n_k = H_C * LANES
    n_v = H_C * V_ROWS
    nkt = t_k // tk
    return pl.pallas_call(
        _kv_kernel, grid=(bsz, nkt),
        in_specs=[pl.BlockSpec((1, tk, kv_lora), lambda b, j: (b, j, 0)),
                  pl.BlockSpec((1, tk, D_ROPE), lambda b, j: (b, j, 0)),
                  _const_spec(lw["wkc"].shape), _const_spec(lw["wkr"].shape),
                  _const_spec(lw["wvt"].shape)],
        out_specs=[pl.BlockSpec((1, tk, n_k), lambda b, j: (b, j, 0)),
                   pl.BlockSpec((1, 1, n_v, tk), lambda b, j: (b, j, 0, 0))],
        out_shape=[jax.ShapeDtypeStruct((bsz, t_k, n_k), BF16),
                   jax.ShapeDtypeStruct((bsz, nkt, n_v, tk), BF16)],
        name="kv", compiler_params=_params(2),
    )(ckv_all, kr_all, lw["wkc"], lw["wkr"], lw["wvt"])


HEADS_PER_STEP = 2
_EXP2_SCALE = (D_NOPE + D_ROPE) ** -0.5 * math.log2(math.e)


def _scores(k_tile, qt_ref, out_refs=None):
    res = [_dot(k_tile[:, h * LANES:(h + 1) * LANES], qt_ref[0, h * LANES:(h + 1) * LANES, :])
           for h in range(HEADS_PER_STEP)]
    if out_refs is None:
        return res
    for h in range(HEADS_PER_STEP):
        out_refs[h] = res[h]
    return None


def _softmax_tile(s_t, m_old):
    m_new = jnp.maximum(m_old, jnp.max(s_t, axis=0, keepdims=True))
    return jnp.exp2(s_t - m_new).astype(BF16), m_new, jnp.exp2(m_old - m_new)


def _attn_output(acc, cgs_t):
    return (acc[0:D_V] / acc[D_V:D_V + 1] * cgs_t).astype(BF16)


def _attn_causal_kernel(qt_ref, k_ref, vt_ref, cgst_ref, bias_ref, o_ref, s0, s1, e0, e1, acc_scr, *, tq, tk):
    i = pl.program_id(2)
    nh = HEADS_PER_STEP

    def qk(j, s_out):
        _scores(k_ref[0, pl.ds(pl.multiple_of(j * tk, tk), tk), :], qt_ref, s_out)

    def pv(j, e_in):
        vt = vt_ref[0, j]
        return [_dot(vt[h * V_ROWS:(h + 1) * V_ROWS, :], e_in[h]) for h in range(nh)]

    def acc_update(pvs, stats):
        for h in range(nh):
            acc_scr[h] = acc_scr[h] * stats[h][1] + pvs[h]

    def stage(j, s_cur, s_nxt, e_cur, e_prev, stats):
        pvs = pv(jnp.maximum(j - 1, 0), e_prev)
        qk(j + 1, s_nxt)
        new = []
        for h in range(nh):
            e, m_new, alpha = _softmax_tile(s_cur[h], stats[h][0])
            e_cur[h] = e
            new.append((m_new, alpha))
        acc_update(pvs, stats)
        return tuple(new)

    e1[...] = jnp.zeros(e1.shape, BF16)
    acc_scr[...] = jnp.zeros(acc_scr.shape, F32)
    qk(0, s0)
    stats = tuple((jnp.full((1, tq), -jnp.inf, F32), jnp.ones((1, tq), F32)) for _ in range(nh))

    def body(t, stats):
        stats = stage(2 * t, s0, s1, e0, e1, stats)
        return stage(2 * t + 1, s1, s0, e1, e0, stats)

    stats = lax.fori_loop(0, i, body, stats)
    j0 = 2 * i
    bias = bias_ref[...]

    pvs = pv(jnp.maximum(j0 - 1, 0), e1)
    k_last = k_ref[0, pl.ds(pl.multiple_of((j0 + 1) * tk, tk), tk), :]
    for h in range(nh):
        s1[h, :, 0:tk] = _dot(k_last[:, h * LANES:(h + 1) * LANES],
                              qt_ref[0, h * LANES:(h + 1) * LANES, tk:tq]) + bias
    new = []
    for h in range(nh):
        s_t = jnp.concatenate([s0[h, :, 0:tk] + bias, s0[h, :, tk:tq]], axis=1)
        e, m_new, alpha = _softmax_tile(s_t, stats[h][0])
        e0[h] = e
        new.append((m_new, alpha))
    acc_update(pvs, stats)
    stats = tuple(new)

    pvs = pv(j0, e0)
    alpha_last = []
    for h in range(nh):
        e, _, alpha = _softmax_tile(s1[h, :, 0:tk], stats[h][0][:, tk:tq])
        e1[h, :, 0:tk] = e
        alpha_last.append(alpha)
    acc_update(pvs, stats)
    vt = vt_ref[0, j0 + 1]
    for h in range(nh):
        acc_scr[h, :, tk:tq] = (acc_scr[h, :, tk:tq] * alpha_last[h]
                                + _dot(vt[h * V_ROWS:(h + 1) * V_ROWS, :], e1[h, :, 0:tk]))
        o_ref[0, h * D_V:(h + 1) * D_V, :] = _attn_output(acc_scr[h], cgst_ref[0, h * D_V:(h + 1) * D_V, :])


def _attn_full_kernel(qt_ref, k_ref, vt_ref, cgst_ref, o_ref):
    tq = qt_ref.shape[-1]
    scores = _scores(k_ref[0], qt_ref)
    vt = vt_ref[0, 0]
    for h in range(HEADS_PER_STEP):
        e, _, _ = _softmax_tile(scores[h], jnp.full((1, tq), -jnp.inf, F32))
        acc = _dot(vt[h * V_ROWS:(h + 1) * V_ROWS, :], e)
        o_ref[0, h * D_V:(h + 1) * D_V, :] = _attn_output(acc, cgst_ref[0, h * D_V:(h + 1) * D_V, :])


def _attn_call(qt, kcat, vt, cgst, *, tq, causal):
    bsz, _, seq = qt.shape
    t_k = kcat.shape[1]
    nkt, _, tk = vt.shape[1:]
    nh = HEADS_PER_STEP
    in_specs = [pl.BlockSpec((1, nh * LANES, tq), lambda b, p, i: (b, p, i)),
                pl.BlockSpec((1, t_k, nh * LANES), lambda b, p, i: (b, 0, p)),
                pl.BlockSpec((1, nkt, nh * V_ROWS, tk), lambda b, p, i: (b, 0, p, 0)),
                pl.BlockSpec((1, nh * D_V, tq), lambda b, p, i: (b, p, i))]
    operands = [qt, kcat, vt, cgst]
    if causal:
        assert tq == 2 * tk and seq == t_k and tk % CHUNK == 0
        chunk = jnp.arange(tk) // CHUNK
        operands.append(jnp.where(chunk[:, None] <= chunk[None, :], 0.0, -1e30).astype(F32))
        in_specs.append(_const_spec((tk, tk)))
        kern = functools.partial(_attn_causal_kernel, tq=tq, tk=tk)
        scratch = [pltpu.VMEM((nh, tk, tq), F32), pltpu.VMEM((nh, tk, tq), F32),
                   pltpu.VMEM((nh, tk, tq), BF16), pltpu.VMEM((nh, tk, tq), BF16),
                   pltpu.VMEM((nh, V_ROWS, tq), F32)]
    else:
        assert nkt == 1 and seq == tq
        kern = _attn_full_kernel
        scratch = []
    return pl.pallas_call(
        kern, grid=(bsz, H_C // nh, seq // tq),
        in_specs=in_specs,
        out_specs=pl.BlockSpec((1, nh * D_V, tq), lambda b, p, i: (b, p, i)),
        out_shape=jax.ShapeDtypeStruct((bsz, H_C * D_V, seq), BF16),
        scratch_shapes=scratch,
        name="attn", compiler_params=_params(3),
    )(*operands)


def _merge_kernel(x_ref, partial_ref, g2_ref, oct_ref, wbr_ref, wout_ref, lng_ref, lnb_ref, o_ref,
                  *, c_lo, d_c, alpha):
    yc = _dot_tn(oct_ref[0], wbr_ref[c_lo:c_lo + d_c, :])
    merged = partial_ref[0].astype(F32) + g2_ref[0].astype(F32) * yc
    y = _dot(merged.astype(BF16), wout_ref[...])
    o_ref[0] = _layer_norm(alpha * x_ref[0] + y, lng_ref[...], lnb_ref[...])


def _merge_call(x, partial, g2, oct, lw, *, tr, alpha):
    bsz, seq, d_model = x.shape
    d_a, d_b, d_c, _ = lw["dims"]
    kern = functools.partial(_merge_kernel, c_lo=d_a + d_b, d_c=d_c, alpha=alpha)

    def row_spec(width):
        return pl.BlockSpec((1, tr, width), lambda b, i: (b, i, 0))

    return pl.pallas_call(
        kern, grid=(bsz, seq // tr),
        in_specs=[row_spec(d_model), row_spec(d_model), row_spec(d_model),
                  pl.BlockSpec((1, d_c, tr), lambda b, i: (b, 0, i)),
                  _const_spec(lw["wbr"].shape), _const_spec(lw["wout"].shape),
                  _const_spec(lw["ln_g"].shape), _const_spec(lw["ln_b"].shape)],
        out_specs=row_spec(d_model),
        out_shape=jax.ShapeDtypeStruct((bsz, seq, d_model), F32),
        name="merge", compiler_params=_params(2),
    )(x, partial, g2, oct, lw["wbr"], lw["wout"], lw["ln_g"], lw["ln_b"])


def _mem_kernel(m_ref, w_ref, k_ref, v_ref, *, d_m):
    kv = _dot(m_ref[...].astype(BF16), w_ref[...])
    k_ref[...] = kv[:, 0:d_m]
    v_ref[...] = kv[:, d_m:2 * d_m]


def _mem_call(mem2d, wmem, *, tr):
    rows, d_model = mem2d.shape
    d_m = wmem.shape[1] // 2
    return pl.pallas_call(
        functools.partial(_mem_kernel, d_m=d_m), grid=(rows // tr,),
        in_specs=[pl.BlockSpec((tr, d_model), lambda i: (i, 0)), _const_spec(wmem.shape)],
        out_specs=[pl.BlockSpec((tr, d_m), lambda i: (i, 0)), pl.BlockSpec((tr, d_m), lambda i: (i, 0))],
        out_shape=[jax.ShapeDtypeStruct((rows, d_m), F32), jax.ShapeDtypeStruct((rows, d_m), F32)],
        name="mem", compiler_params=_params(1),
    )(mem2d, wmem)


def _block_diag(w):
    h, n, _ = w.shape
    eye = jnp.eye(h, dtype=w.dtype)
    return (eye[:, None, :, None] * w[:, :, None, :]).reshape(h * n, h * n)


def _gate_blocks(w_r, w_i):
    per = MXU_TILE // w_r.shape[1]
    blocks = []
    for k in range(w_r.shape[0] // per):
        blocks.append(jnp.concatenate([_block_diag(w_r[k * per:(k + 1) * per]),
                                       _block_diag(w_i[k * per:(k + 1) * per])], axis=1))
    return jnp.stack(blocks)


def _prep_layer(l, sp_len, w_in, gmlp_ln_g, gmlp_ln_b, gmlp_ws, gmlp_bs, lru_conv_w,
                lru_conv_b, lru_w_r, lru_b_r, lru_w_i, lru_b_i, lru_lambda, mla_q_norm, mla_w_uq,
                mla_kv_norm, mla_w_ukv, w_br, w_out, ln_g, ln_b):
    d_model = w_in.shape[1]
    d_a = gmlp_ln_g.shape[1]
    d_b = lru_lambda.shape[1]
    q_lora = mla_q_norm.shape[1]
    kv_lora = mla_kv_norm.shape[1]
    d_c = H_C * D_V
    d_m = H_M * DH_M
    w = w_in[l]
    o_cq = 3 * d_a + 2 * d_b
    o_kr = o_cq + q_lora + kv_lora
    o_cg = o_kr + D_ROPE
    o_mq = o_cg + d_c
    o_g = o_mq + d_m
    half = D_ROPE // 2
    c_all = jnp.concatenate(
        [w[:, o_cq:o_cg], w[:, o_kr + half:o_cg], w[:, o_kr:o_kr + half],
         jnp.zeros((d_model, 4 * LANES - (q_lora + kv_lora + 2 * D_ROPE)), w.dtype)], axis=1)
    pieces = [("a_u", w[:, 0:d_a]), ("a_v", w[:, d_a:2 * d_a]), ("a_g", w[:, 2 * d_a:3 * d_a]),
              ("b_x", w[:, 3 * d_a:3 * d_a + d_b]), ("b_g", w[:, 3 * d_a + d_b:o_cq]),
              ("c_all", c_all), ("m_q", w[:, o_mq:o_g])]
    for k in range(N_BRANCH):
        pieces.append((f"g{k}", w[:, o_g + k * d_model:o_g + (k + 1) * d_model]))
    cols, off = {}, 0
    for name, p in pieces:
        cols[name] = (off, off + p.shape[1])
        off += p.shape[1]
    w1 = jnp.concatenate([p for _, p in pieces], axis=1).astype(BF16)
    wcgt = jnp.transpose(w[:, o_cg:o_mq]).astype(BF16)

    ws = jnp.tril(gmlp_ws[l][:, :sp_len, :sp_len])
    bs = gmlp_bs[l][:, :sp_len]
    bsp = jnp.repeat(jnp.transpose(bs), d_a // G_A, axis=1)

    wq = mla_w_uq[l].reshape(q_lora, H_C, D_NOPE + D_ROPE) * _EXP2_SCALE
    wq = jnp.concatenate([wq, jnp.zeros((q_lora, H_C, LANES - D_NOPE - D_ROPE), wq.dtype)], axis=2)
    wuqt = jnp.transpose(wq.reshape(q_lora, H_C * LANES)).astype(BF16)

    wkv = mla_w_ukv[l].reshape(kv_lora, H_C, D_NOPE + D_V)
    wk = wkv[:, :, :D_NOPE]
    wv = wkv[:, :, D_NOPE:]
    wkc = jnp.concatenate([wk, jnp.zeros((kv_lora, H_C, LANES - D_NOPE), wk.dtype)], axis=2)
    wkc = wkc.reshape(kv_lora, H_C * LANES).astype(BF16)
    lane = jnp.arange(H_C * LANES) % LANES
    wkr = ((lane[None, :] - D_NOPE) == jnp.arange(D_ROPE)[:, None]).astype(BF16)
    wv = jnp.concatenate([wv, jnp.zeros((kv_lora, H_C, V_ROWS - D_V), wv.dtype)], axis=2)
    wvt = jnp.transpose(wv.reshape(kv_lora, H_C * V_ROWS)).astype(BF16)

    return dict(
        dims=(d_a, d_b, d_c, d_m), q_lora=q_lora, kv_lora=kv_lora, cols=cols, w1=w1, wcgt=wcgt,
        gln_g=gmlp_ln_g[l][None, :], gln_b=gmlp_ln_b[l][None, :],
        wsp=ws.astype(BF16), bsp=bsp,
        conv_w=lru_conv_w[l], conv_b=lru_conv_b[l][None, :],
        wgate=_gate_blocks(lru_w_r[l], lru_w_i[l]).astype(BF16),
        bgate=jnp.stack([lru_b_r[l], lru_b_i[l]]),
        lam=lru_lambda[l][None, :],
        q_norm=mla_q_norm[l][None, :], wuqt=wuqt, kv_norm=mla_kv_norm[l][None, :],
        wkc=wkc, wkr=wkr, wvt=wvt,
        wbr=w_br[l].astype(BF16), wout=w_out[l].astype(BF16),
        ln_g=ln_g[l][None, :], ln_b=ln_b[l][None, :],
    )


def _rope_tables(pos):
    half = D_ROPE // 2
    freq = ROPE_BASE ** (-jnp.arange(half, dtype=F32) / half)
    ang = pos.astype(F32)[:, None] * freq[None, :]
    cos, sin = jnp.cos(ang), jnp.sin(ang)
    return (jnp.transpose(cos), jnp.transpose(sin),
            jnp.concatenate([cos, cos], axis=1), jnp.concatenate([-sin, sin], axis=1))


def _pad_conv_state(conv):
    return jnp.pad(conv, ((0, 0), (SUBLANES - (CONV_W - 1), 0), (0, 0)))


def _trunk_layer(x, lw, rope, conv0, h0, mk, mv, past_ckv, past_kr, *, tm, alpha, emit_v):
    outs = _branch_call(x, lw, _pad_conv_state(conv0), h0[:, None, :], mk, mv, rope, tm=tm, emit_v=emit_v)
    partial, g2, cgst, qt, ckv_new, kr_new, conv_pad, h_new = outs[:8]
    v_rows = outs[8] if emit_v else None
    causal = past_ckv is None
    if causal:
        ckv_all, kr_all = ckv_new, kr_new
    else:
        ckv_all = jnp.concatenate([past_ckv, ckv_new], axis=1)
        kr_all = jnp.concatenate([past_kr, kr_new], axis=1)
    kcat, vt = _kv_call(ckv_all, kr_all, lw, tk=tm if causal else ckv_all.shape[1])
    oct = _attn_call(qt, kcat, vt, cgst, tq=2 * tm if causal else tm, causal=causal)
    x_new = _merge_call(x, partial, g2, oct, lw, tr=tm, alpha=alpha)
    return x_new, v_rows, conv_pad[:, SUBLANES - (CONV_W - 1):], h_new[:, 0], ckv_new, kr_new


def kernel(x_prompt, x_sample, mem_prompt, cache_mla_ckv, cache_mla_krope, cache_mem_k, cache_mem_v,
           state_lru_h, state_lru_conv, w_in, gmlp_ln_g, gmlp_ln_b, gmlp_ws, gmlp_bs,
           lru_conv_w, lru_conv_b, lru_w_r, lru_b_r, lru_w_i, lru_b_i, lru_lambda,
           mla_q_norm, mla_w_uq, mla_kv_norm, mla_w_ukv, mem_w_k, mem_w_v, w_br, w_out, ln_g, ln_b):
    bp, tp, d_model = x_prompt.shape
    bs, ts, _ = x_sample.shape
    depth = w_in.shape[0]
    past_len = cache_mla_ckv.shape[2]
    n_mem = mem_prompt.shape[1]
    d_b = lru_lambda.shape[1]
    alpha = (2.0 * depth) ** 0.25
    weights = (w_in, gmlp_ln_g, gmlp_ln_b, gmlp_ws, gmlp_bs, lru_conv_w, lru_conv_b, lru_w_r, lru_b_r,
               lru_w_i, lru_b_i, lru_lambda, mla_q_norm, mla_w_uq, mla_kv_norm, mla_w_ukv, w_br, w_out,
               ln_g, ln_b)
    rope_p = _rope_tables(jnp.arange(tp))
    rope_s = _rope_tables(past_len + jnp.arange(ts))
    tm_p = min(tp, ROW_TILE)
    assert tp % (2 * tm_p) == 0 and tm_p % CHUNK == 0 and ts % SUBLANES == 0

    xp, xs = x_prompt, x_sample
    acc = [[] for _ in range(11)]
    for l in range(depth):
        lw_p = _prep_layer(l, min(tp, A_CHUNK), *weights)
        lw_s = lw_p if min(ts, A_CHUNK) == min(tp, A_CHUNK) else _prep_layer(l, min(ts, A_CHUNK), *weights)
        wmem = jnp.concatenate([mem_w_k[l], mem_w_v[l]], axis=1).astype(BF16)
        mk, mv = _mem_call(mem_prompt.reshape(bp * n_mem, d_model), wmem, tr=n_mem)
        mk = mk.reshape(bp, n_mem, -1)
        mv = mv.reshape(bp, n_mem, -1)
        xp, _, conv_n, h_n, ckv_n, kr_n = _trunk_layer(
            xp, lw_p, rope_p, jnp.zeros((bp, CONV_W - 1, d_b), F32), jnp.zeros((bp, d_b), F32),
            mk, mv, None, None, tm=tm_p, alpha=alpha, emit_v=False)
        for k, val in zip(range(6), (ckv_n, kr_n, mk.reshape(bp, n_mem, H_M, DH_M),
                                     mv.reshape(bp, n_mem, H_M, DH_M), h_n, conv_n)):
            acc[k].append(val)
        xs, v_n, conv_n, h_n, ckv_n, kr_n = _trunk_layer(
            xs, lw_s, rope_s, state_lru_conv[l], state_lru_h[l],
            cache_mem_k[l].reshape(bs, n_mem, -1), cache_mem_v[l].reshape(bs, n_mem, -1),
            cache_mla_ckv[l], cache_mla_krope[l], tm=ts, alpha=alpha, emit_v=True)
        for k, val in zip(range(6, 11), (ckv_n, kr_n, h_n, conv_n, v_n)):
            acc[k].append(val)
    return (xp, xs) + tuple(jnp.stack(a) for a in acc)
```

```python
import functools
import math

import jax
import jax.numpy as jnp
from jax import lax
from jax.experimental import pallas as pl
from jax.experimental.pallas import tpu as pltpu

CHUNK = 64
G_A = 4
A_CHUNK = 128
H_B = 8
CONV_W = 4
LRU_C = 8.0
H_C = 8
D_NOPE = 64
D_ROPE = 32
D_V = 64
ROPE_BASE = 10000.0
H_M = 4
DH_M = 64
N_BRANCH = 4
EPS = 1e-6

LANES = 128
SUBLANES = 8
VMEM_LIMIT = 56 * 1024 * 1024
ROW_TILE = 256
MXU_TILE = 256

F32 = jnp.float32
BF16 = jnp.bfloat16


def _dot(a, b):
    return jnp.dot(a, b, preferred_element_type=F32)


def _dot_nt(a, b):
    return lax.dot_general(a, b, (((1,), (1,)), ((), ())), preferred_element_type=F32)


def _dot_tn(a, b):
    return lax.dot_general(a, b, (((0,), (0,)), ((), ())), preferred_element_type=F32)


def _sigmoid(x):
    return 1.0 / (1.0 + jnp.exp(-x))


def _silu(x):
    return x * _sigmoid(x)


def _gelu(x):
    return jax.nn.gelu(x)


def _expm1_nonpos(x):
    u = jnp.exp(x)
    near = (u - 1.0) * x / jnp.log(jnp.where(u == 1.0, 2.0, jnp.maximum(u, 0.5)))
    return jnp.where(u == 1.0, x, jnp.where(u > 0.5, near, u - 1.0))


def _layer_norm(x, g, b):
    mu = jnp.mean(x, -1, keepdims=True)
    var = jnp.mean(jnp.square(x - mu), -1, keepdims=True)
    return (x - mu) * lax.rsqrt(var + EPS) * g + b


def _rms_norm(x, g):
    return x * lax.rsqrt(jnp.mean(jnp.square(x), -1, keepdims=True) + EPS) * g


def _const_spec(shape):
    nd = len(shape)
    return pl.BlockSpec(shape, lambda *_: (0,) * nd, pipeline_mode=pl.Buffered(1))


def _params(n_axes):
    return pltpu.CompilerParams(dimension_semantics=("arbitrary",) * n_axes,
                                vmem_limit_bytes=VMEM_LIMIT)


V_ROWS = 80


def _keys_values(ckv, kr, wkc_ref, wkr_ref, wvt_ref):
    c = ckv.astype(BF16)
    kcat = (_dot(c, wkc_ref[...]) + _dot(kr.astype(BF16), wkr_ref[...])).astype(BF16)
    v_t = _dot_nt(wvt_ref[...], c)
    row = lax.broadcasted_iota(jnp.int32, (v_t.shape[0], 1), 0)
    return kcat, jnp.where(row % V_ROWS == D_V, 1.0, v_t).astype(BF16)


def _branch_kernel(x_ref, w1_ref, wcgt_ref, glng_ref, glnb_ref, wsp_ref, bsp_ref, convw_ref, convb_ref,
                   wgate_ref, bgate_ref, lam_ref, conv0_ref, h0_ref, qnorm_ref, wuqt_ref,
                   kvnorm_ref, cosq_ref, sinq_ref, ck_ref, sk_ref, mk_ref, mv_ref, wbr_ref,
                   *rest, first_chunk, tm, sp_len, d_a, d_b, d_c, d_m, q_lora, kv_lora, cols):
    if first_chunk:
        (wkc_ref, wkr_ref, wvt_ref, partial_ref, g2_ref, cgst_ref, qt_ref, ckv_ref, kr_ref, conv_ref, h_ref,
         kcat_ref, vt_ref, merged_scr) = rest
    else:
        partial_ref, g2_ref, cgst_ref, qt_ref, ckv_ref, kr_ref, conv_ref, h_ref, v_ref, merged_scr = rest
    t = pl.program_id(1)
    xb = x_ref[0].astype(BF16)

    def zin(name):
        lo, hi = cols[name]
        return _dot(xb, w1_ref[:, lo:hi])

    z_u, z_v, z_ag, z_g0 = zin("a_u"), zin("a_v"), zin("a_g"), zin("g0")
    u = _gelu(z_u)
    v = _layer_norm(_gelu(z_v), glng_ref[...], glnb_ref[...])
    if not first_chunk:
        v_ref[0] = v
    vb = v.astype(BF16)
    n_groups = d_a // LANES
    row_blocks = []
    for c in range(tm // sp_len):
        col_blocks = [_dot(wsp_ref[g], vb[c * sp_len:(c + 1) * sp_len, g * LANES:(g + 1) * LANES])
                      for g in range(n_groups)]
        row_blocks.append(jnp.concatenate(col_blocks, axis=1) + bsp_ref[...])
    s = row_blocks[0] if len(row_blocks) == 1 else jnp.concatenate(row_blocks, axis=0)
    oa = (u * s) * _silu(z_ag)
    merged_scr[...] = _sigmoid(z_g0) * _dot(oa.astype(BF16), wbr_ref[0:d_a, :])

    @pl.when(t == 0)
    def _():
        conv_ref[...] = conv0_ref[...]
        h_ref[...] = h0_ref[...]

    bx = zin("b_x")
    hist = conv_ref[0]
    row8 = lax.broadcasted_iota(jnp.int32, (SUBLANES, 1), 0)
    xc = convb_ref[...]
    for k in range(CONV_W):
        shift = CONV_W - 1 - k
        if shift == 0:
            sh = bx
        else:
            sh = pltpu.roll(bx, shift, axis=0)
            head = jnp.where(row8 < shift, pltpu.roll(hist, shift, axis=0), sh[0:SUBLANES])
            sh = head if tm == SUBLANES else jnp.concatenate([head, sh[SUBLANES:]], axis=0)
        xc = xc + sh * convw_ref[k:k + 1, :]
    conv_ref[0] = bx[tm - SUBLANES:tm]

    xcb = xc.astype(BF16)
    n_blk = d_b // MXU_TILE
    ri = [_dot(xcb[:, k * MXU_TILE:(k + 1) * MXU_TILE], wgate_ref[k]) for k in range(n_blk)]
    z_bg, z_g1, z_g2 = zin("b_g"), zin("g1"), zin("g2")
    z_mq, z_g3, z_c = zin("m_q"), zin("g3"), zin("c_all")
    cg_t = _dot_nt(wcgt_ref[...], xb)
    r = _sigmoid(jnp.concatenate([p[:, 0:MXU_TILE] for p in ri], axis=1) + bgate_ref[0:1, :])
    i_gate = _sigmoid(jnp.concatenate([p[:, MXU_TILE:2 * MXU_TILE] for p in ri], axis=1) + bgate_ref[1:2, :])
    neg_lam = -lam_ref[...]
    softplus = jnp.maximum(neg_lam, 0.0) + jnp.log1p(jnp.exp(-jnp.abs(neg_lam)))
    log_a = (-LRU_C * r) * softplus
    a = jnp.exp(log_a)
    bval = jnp.sqrt(-_expm1_nonpos(2.0 * log_a)) * (i_gate * xc)
    rows = lax.broadcasted_iota(jnp.int32, (tm, 1), 0)
    d = 1
    while d < tm:
        keep = rows >= d
        a_sh = pltpu.roll(a, d, axis=0)
        b_sh = pltpu.roll(bval, d, axis=0)
        bval = jnp.where(keep, a * b_sh + bval, bval)
        a = jnp.where(keep, a * a_sh, a)
        d *= 2
    h = a * h_ref[0] + bval
    h_ref[0] = h[tm - 1:tm]
    ob = h * _silu(z_bg)
    merged_scr[...] += _sigmoid(z_g1) * _dot(ob.astype(BF16), wbr_ref[d_a:d_a + d_b, :])
    g2_ref[0] = _sigmoid(z_g2).astype(BF16)

    mq = z_mq.astype(BF16)
    mkb = mk_ref[0].astype(BF16)
    mvb = mv_ref[0].astype(BF16)
    lane = lax.broadcasted_iota(jnp.int32, (1, LANES), 1)
    slabs = []
    for p in range(d_m // LANES):
        mq_p = mq[:, p * LANES:(p + 1) * LANES]
        mk_p = mkb[:, p * LANES:(p + 1) * LANES]
        mv_p = mvb[:, p * LANES:(p + 1) * LANES]
        acc = None
        for half in range(LANES // DH_M):
            sel = (lane >= half * DH_M) & (lane < (half + 1) * DH_M)
            sc = _dot_nt(mq_p, jnp.where(sel, mk_p, jnp.zeros_like(mk_p))) * (DH_M ** -0.5)
            e = jnp.exp(sc - jnp.max(sc, -1, keepdims=True))
            prob = (e / jnp.sum(e, -1, keepdims=True)).astype(BF16)
            o = _dot(prob, jnp.where(sel, mv_p, jnp.zeros_like(mv_p)))
            acc = o if acc is None else acc + o
        slabs.append(acc)
    om = jnp.concatenate(slabs, axis=1)
    m_lo = d_a + d_b + d_c
    partial_ref[0] = (merged_scr[...]
                      + _sigmoid(z_g3) * _dot(om.astype(BF16), wbr_ref[m_lo:m_lo + d_m, :])).astype(BF16)
    cgst_ref[0] = _silu(cg_t)

    ckv = _rms_norm(z_c[:, q_lora:q_lora + kv_lora], kvnorm_ref[...])
    r_lo = q_lora + kv_lora
    kr = (z_c[:, r_lo:r_lo + D_ROPE] * ck_ref[...]
          + z_c[:, r_lo + D_ROPE:r_lo + 2 * D_ROPE] * sk_ref[...])
    ckv_ref[0] = ckv
    kr_ref[0] = kr
    if first_chunk:
        kcat_ref[0], vt_ref[0, 0] = _keys_values(ckv, kr, wkc_ref, wkr_ref, wvt_ref)
    cqn = _rms_norm(z_c[:, 0:q_lora], qnorm_ref[...]).astype(BF16)
    q_t = _dot_nt(wuqt_ref[...], cqn)
    cos_t, sin_t = cosq_ref[...], sinq_ref[...]
    half = D_ROPE // 2
    for hd in range(H_C):
        lo = hd * LANES
        x1 = q_t[lo + D_NOPE:lo + D_NOPE + half]
        x2 = q_t[lo + D_NOPE + half:lo + D_NOPE + D_ROPE]
        qt_ref[0, 0, lo:lo + D_NOPE, :] = q_t[lo:lo + D_NOPE].astype(BF16)
        qt_ref[0, 0, lo + D_NOPE:lo + D_NOPE + half, :] = (x1 * cos_t - x2 * sin_t).astype(BF16)
        qt_ref[0, 0, lo + D_NOPE + half:lo + D_NOPE + D_ROPE, :] = (x1 * sin_t + x2 * cos_t).astype(BF16)
        qt_ref[0, 0, lo + D_NOPE + D_ROPE:lo + LANES, :] = jnp.zeros((LANES - D_NOPE - D_ROPE, tm), BF16)


def _branch_call(x, lw, conv0_pad, h0, mk, mv, rope, *, tm, tq, first_chunk):
    bsz, seq, d_model = x.shape
    d_a, d_b, d_c, d_m = lw["dims"]
    q_lora, kv_lora = lw["q_lora"], lw["kv_lora"]
    sp_len = lw["wsp"].shape[-1]
    n_mem = mk.shape[1]
    cos_t, sin_t, ck_t, sk_t = rope
    grid = (bsz, seq // tm)

    def row_spec(width):
        return pl.BlockSpec((1, tm, width), lambda b, t: (b, t, 0))

    def col_spec(height):
        return pl.BlockSpec((1, height, tm), lambda b, t: (b, 0, t))

    def tab_spec(width):
        return pl.BlockSpec((tm, width), lambda b, t: (t, 0))

    def batch_spec(rows, width):
        return pl.BlockSpec((1, rows, width), lambda b, t: (b, 0, 0))

    in_specs = [
        row_spec(d_model),
        _const_spec(lw["w1"].shape), _const_spec(lw["wcgt"].shape),
        _const_spec(lw["gln_g"].shape), _const_spec(lw["gln_b"].shape),
        _const_spec(lw["wsp"].shape), _const_spec(lw["bsp"].shape),
        _const_spec(lw["conv_w"].shape), _const_spec(lw["conv_b"].shape),
        _const_spec(lw["wgate"].shape), _const_spec(lw["bgate"].shape),
        _const_spec(lw["lam"].shape),
        batch_spec(SUBLANES, d_b), batch_spec(1, d_b),
        _const_spec(lw["q_norm"].shape), _const_spec(lw["wuqt"].shape),
        _const_spec(lw["kv_norm"].shape),
        pl.BlockSpec((D_ROPE // 2, tm), lambda b, t: (0, t)), pl.BlockSpec((D_ROPE // 2, tm), lambda b, t: (0, t)),
        tab_spec(D_ROPE), tab_spec(D_ROPE),
        batch_spec(n_mem, d_m), batch_spec(n_mem, d_m),
        _const_spec(lw["wbr"].shape),
    ]
    operands = [x, lw["w1"], lw["wcgt"], lw["gln_g"], lw["gln_b"], lw["wsp"], lw["bsp"], lw["conv_w"],
                lw["conv_b"], lw["wgate"], lw["bgate"], lw["lam"], conv0_pad, h0, lw["q_norm"], lw["wuqt"],
                lw["kv_norm"], cos_t, sin_t, ck_t, sk_t, mk, mv, lw["wbr"]]
    out_shape = [
        jax.ShapeDtypeStruct((bsz, seq, d_model), BF16),
        jax.ShapeDtypeStruct((bsz, seq, d_model), BF16),
        jax.ShapeDtypeStruct((bsz, d_c, seq), F32),
        jax.ShapeDtypeStruct((bsz, seq // tq, H_C * LANES, tq), BF16),
        jax.ShapeDtypeStruct((bsz, seq, kv_lora), F32),
        jax.ShapeDtypeStruct((bsz, seq, D_ROPE), F32),
        jax.ShapeDtypeStruct((bsz, SUBLANES, d_b), F32),
        jax.ShapeDtypeStruct((bsz, 1, d_b), F32),
    ]
    out_specs = [
        row_spec(d_model), row_spec(d_model), col_spec(d_c),
        pl.BlockSpec((1, 1, H_C * LANES, tm), lambda b, t: (b, t // (tq // tm), 0, t % (tq // tm))),
        row_spec(kv_lora), row_spec(D_ROPE), batch_spec(SUBLANES, d_b), batch_spec(1, d_b),
    ]
    if first_chunk:
        for name in ("wkc", "wkr", "wvt"):
            in_specs.append(_const_spec(lw[name].shape))
            operands.append(lw[name])
        out_shape += [jax.ShapeDtypeStruct((bsz, seq, H_C * LANES), BF16),
                      jax.ShapeDtypeStruct((bsz, seq // tm, H_C * V_ROWS, tm), BF16)]
        out_specs += [row_spec(H_C * LANES),
                      pl.BlockSpec((1, 1, H_C * V_ROWS, tm), lambda b, t: (b, t, 0, 0))]
    else:
        out_shape.append(jax.ShapeDtypeStruct((bsz, seq, d_a), F32))
        out_specs.append(row_spec(d_a))
    kern = functools.partial(
        _branch_kernel, first_chunk=first_chunk, tm=tm, sp_len=sp_len, d_a=d_a, d_b=d_b, d_c=d_c, d_m=d_m,
        q_lora=q_lora, kv_lora=kv_lora, cols=lw["cols"])
    return pl.pallas_call(
        kern, grid=grid, in_specs=in_specs, out_specs=out_specs, out_shape=out_shape,
        scratch_shapes=[pltpu.VMEM((tm, d_model), F32)],
        name="branch", compiler_params=_params(2),
    )(*operands)


def _kv_kernel(ckv_ref, kr_ref, wkc_ref, wkr_ref, wvt_ref, kcat_ref, vt_ref):
    kcat_ref[0], vt_ref[0, 0] = _keys_values(ckv_ref[0], kr_ref[0], wkc_ref, wkr_ref, wvt_ref)


def _kv_call(ckv_all, kr_all, lw, *, tk):
    bsz, t_k, kv_lora = ckv_all.shape
    n_k = H_C * LANES
    n_v = H_C * V_ROWS
    nkt = t_k // tk
    return pl.pallas_call(
        _kv_kernel, grid=(bsz, nkt),
        in_specs=[pl.BlockSpec((1, tk, kv_lora), lambda b, j: (b, j, 0)),
                  pl.BlockSpec((1, tk, D_ROPE), lambda b, j: (b, j, 0)),
                  _const_spec(lw["wkc"].shape), _const_spec(lw["wkr"].shape),
                  _const_spec(lw["wvt"].shape)],
        out_specs=[pl.BlockSpec((1, tk, n_k), lambda b, j: (b, j, 0)),
                   pl.BlockSpec((1, 1, n_v, tk), lambda b, j: (b, j, 0, 0))],
        out_shape=[jax.ShapeDtypeStruct((bsz, t_k, n_k), BF16),
                   jax.ShapeDtypeStruct((bsz, nkt, n_v, tk), BF16)],
        name="kv", compiler_params=_params(2),
    )(ckv_all, kr_all, lw["wkc"], lw["wkr"], lw["wvt"])


HEADS_PER_STEP = 2
_EXP2_SCALE = (D_NOPE + D_ROPE) ** -0.5 * math.log2(math.e)


def _scores(k_tile, q_tile, out_refs=None):
    res = [_dot(k_tile[:, h * LANES:(h + 1) * LANES], q_tile[h * LANES:(h + 1) * LANES, :])
           for h in range(HEADS_PER_STEP)]
    if out_refs is None:
        return res
    for h in range(HEADS_PER_STEP):
        out_refs[h] = res[h]
    return None


def _softmax_tile(s_t, m_old):
    m_new = jnp.maximum(m_old, jnp.max(s_t, axis=0, keepdims=True))
    return jnp.exp2(s_t - m_new).astype(BF16), m_new, jnp.exp2(m_old - m_new)


def _attn_output(acc, cgs_t):
    return (acc[0:D_V] / acc[D_V:D_V + 1] * cgs_t).astype(BF16)


N_BUF = 4


def _attn_causal_kernel(qt_ref, k_ref, vt_ref, cgst_ref, bias_ref, o_ref, *scratch, tq, tk):
    s_buf, e_buf, acc_scr = scratch[0:N_BUF], scratch[N_BUF:2 * N_BUF], scratch[2 * N_BUF]
    i = pl.program_id(2)
    nh = HEADS_PER_STEP
    q_cur = qt_ref.at[0, i]
    q_nxt = qt_ref.at[0, jnp.minimum(i + 1, pl.num_programs(2) - 1)]

    def k_tile(j):
        return k_ref[0, pl.ds(pl.multiple_of(j * tk, tk), tk), :]

    def pv(j, e_in, c0):
        vt = vt_ref[0, j]
        return [_dot(vt[h * V_ROWS:(h + 1) * V_ROWS, :], e_in[h, :, c0:tq]) for h in range(nh)]

    def acc_update(pvs, alphas, c0):
        for h in range(nh):
            acc_scr[h, :, c0:tq] = acc_scr[h, :, c0:tq] * alphas[h][:, c0:tq] + pvs[h]

    def softmax(slot, carry, c0, add_bias):
        new = []
        for h in range(nh):
            m_old, a_prev, _ = carry[h]
            s_t = s_buf[slot][h, :, c0:tq]
            if add_bias:
                lead = s_t[:, 0:tk] + bias_ref[...]
                s_t = lead if c0 + tk == tq else jnp.concatenate([lead, s_t[:, tk:]], axis=1)
            e, m_new, alpha = _softmax_tile(s_t, m_old[:, c0:tq])
            e_buf[slot][h, :, c0:tq] = e
            if c0:
                m_new = jnp.concatenate([m_old[:, 0:c0], m_new], axis=1)
                alpha = jnp.concatenate([jnp.ones((1, c0), F32), alpha], axis=1)
            new.append((m_new, alpha, a_prev))
        return tuple(new)

    def stage(n, slot, carry):
        pvs = pv(jnp.maximum(n - 2, 0), e_buf[(slot + 2) % N_BUF], 0)
        _scores(k_tile(n + 3), q_cur, s_buf[(slot + 3) % N_BUF])
        new = softmax(slot, carry, 0, False)
        acc_update(pvs, [c[2] for c in carry], 0)
        return new

    for slot in (N_BUF - 2, N_BUF - 1):
        e_buf[slot][...] = jnp.zeros(e_buf[slot].shape, BF16)
    acc_scr[...] = jnp.zeros(acc_scr.shape, F32)

    @pl.when(i == 0)
    def _():
        for slot in range(N_BUF - 1):
            _scores(k_tile(slot), q_cur, s_buf[slot])

    ones = jnp.ones((1, tq), F32)
    carry = tuple((jnp.full((1, tq), -jnp.inf, F32), ones, ones) for _ in range(nh))

    def body(t, carry):
        for slot in range(N_BUF):
            carry = stage(N_BUF * t + slot, slot, carry)
        return carry

    carry = lax.fori_loop(0, i, body, carry)

    j0 = N_BUF * i
    last = N_BUF - 1
    for d in range(N_BUF):
        c_prev = max(d - 2, 0) * tk
        pvs = pv(jnp.maximum(j0 + d - 2, 0), e_buf[(d + 2) % N_BUF], c_prev)
        if d == 0:
            k_last = k_tile(j0 + last)
            for h in range(nh):
                s_buf[last][h, :, tq - tk:tq] = _dot(
                    k_last[:, h * LANES:(h + 1) * LANES],
                    q_cur[h * LANES:(h + 1) * LANES, tq - tk:tq]) + bias_ref[...]
        else:
            _scores(k_tile(d - 1), q_nxt, s_buf[d - 1])
        alphas = [c[2] for c in carry]
        carry = softmax(d, carry, d * tk, d < last)
        acc_update(pvs, alphas, c_prev)
    acc_update(pv(j0 + last - 1, e_buf[last - 1], (last - 1) * tk), [c[2] for c in carry], (last - 1) * tk)
    acc_update(pv(j0 + last, e_buf[last], last * tk), [c[1] for c in carry], last * tk)
    for h in range(nh):
        o_ref[0, h * D_V:(h + 1) * D_V, :] = _attn_output(acc_scr[h], cgst_ref[0, h * D_V:(h + 1) * D_V, :])


def _attn_full_kernel(qt_ref, k_ref, vt_ref, cgst_ref, o_ref):
    tq = qt_ref.shape[-1]
    scores = _scores(k_ref[0], qt_ref.at[0, 0])
    vt = vt_ref[0, 0]
    for h in range(HEADS_PER_STEP):
        e, _, _ = _softmax_tile(scores[h], jnp.full((1, tq), -jnp.inf, F32))
        acc = _dot(vt[h * V_ROWS:(h + 1) * V_ROWS, :], e)
        o_ref[0, h * D_V:(h + 1) * D_V, :] = _attn_output(acc, cgst_ref[0, h * D_V:(h + 1) * D_V, :])


def _attn_call(qt, kcat, vt, cgst, *, causal):
    bsz, n_qt, _, tq = qt.shape
    seq = n_qt * tq
    t_k = kcat.shape[1]
    nkt, _, tk = vt.shape[1:]
    nh = HEADS_PER_STEP
    in_specs = [pl.BlockSpec((1, n_qt, nh * LANES, tq), lambda b, p, i: (b, 0, p, 0)),
                pl.BlockSpec((1, t_k, nh * LANES), lambda b, p, i: (b, 0, p)),
                pl.BlockSpec((1, nkt, nh * V_ROWS, tk), lambda b, p, i: (b, 0, p, 0)),
                pl.BlockSpec((1, nh * D_V, tq), lambda b, p, i: (b, p, i))]
    operands = [qt, kcat, vt, cgst]
    if causal:
        assert tq == N_BUF * tk and seq == t_k and tk % CHUNK == 0
        chunk = jnp.arange(tk) // CHUNK
        operands.append(jnp.where(chunk[:, None] <= chunk[None, :], 0.0, -1e30).astype(F32))
        in_specs.append(_const_spec((tk, tk)))
        kern = functools.partial(_attn_causal_kernel, tq=tq, tk=tk)
        scratch = ([pltpu.VMEM((nh, tk, tq), F32)] * N_BUF + [pltpu.VMEM((nh, tk, tq), BF16)] * N_BUF
                   + [pltpu.VMEM((nh, V_ROWS, tq), F32)])
    else:
        assert nkt == 1 and seq == tq
        kern = _attn_full_kernel
        scratch = []
    return pl.pallas_call(
        kern, grid=(bsz, H_C // nh, seq // tq),
        in_specs=in_specs,
        out_specs=pl.BlockSpec((1, nh * D_V, tq), lambda b, p, i: (b, p, i)),
        out_shape=jax.ShapeDtypeStruct((bsz, H_C * D_V, seq), BF16),
        scratch_shapes=scratch,
        name="attn", compiler_params=_params(3),
    )(*operands)


def _merge_kernel(x_ref, partial_ref, g2_ref, oct_ref, wbr_ref, wout_ref, lng_ref, lnb_ref, o_ref,
                  *, c_lo, d_c, alpha):
    yc = _dot_tn(oct_ref[0], wbr_ref[c_lo:c_lo + d_c, :])
    merged = partial_ref[0].astype(F32) + g2_ref[0].astype(F32) * yc
    y = _dot(merged.astype(BF16), wout_ref[...])
    o_ref[0] = _layer_norm(alpha * x_ref[0] + y, lng_ref[...], lnb_ref[...])


def _merge_call(x, partial, g2, oct, lw, *, tr, alpha):
    bsz, seq, d_model = x.shape
    d_a, d_b, d_c, _ = lw["dims"]
    kern = functools.partial(_merge_kernel, c_lo=d_a + d_b, d_c=d_c, alpha=alpha)

    def row_spec(width):
        return pl.BlockSpec((1, tr, width), lambda b, i: (b, i, 0))

    return pl.pallas_call(
        kern, grid=(bsz, seq // tr),
        in_specs=[row_spec(d_model), row_spec(d_model), row_spec(d_model),
                  pl.BlockSpec((1, d_c, tr), lambda b, i: (b, 0, i)),
                  _const_spec(lw["wbr"].shape), _const_spec(lw["wout"].shape),
                  _const_spec(lw["ln_g"].shape), _const_spec(lw["ln_b"].shape)],
        out_specs=row_spec(d_model),
        out_shape=jax.ShapeDtypeStruct((bsz, seq, d_model), F32),
        name="merge", compiler_params=_params(2),
    )(x, partial, g2, oct, lw["wbr"], lw["wout"], lw["ln_g"], lw["ln_b"])


def _mem_kernel(m_ref, w_ref, k_ref, v_ref, *, d_m):
    kv = _dot(m_ref[...].astype(BF16), w_ref[...])
    k_ref[...] = kv[:, 0:d_m]
    v_ref[...] = kv[:, d_m:2 * d_m]


def _mem_call(mem2d, wmem, *, tr):
    rows, d_model = mem2d.shape
    d_m = wmem.shape[1] // 2
    return pl.pallas_call(
        functools.partial(_mem_kernel, d_m=d_m), grid=(rows // tr,),
        in_specs=[pl.BlockSpec((tr, d_model), lambda i: (i, 0)), _const_spec(wmem.shape)],
        out_specs=[pl.BlockSpec((tr, d_m), lambda i: (i, 0)), pl.BlockSpec((tr, d_m), lambda i: (i, 0))],
        out_shape=[jax.ShapeDtypeStruct((rows, d_m), F32), jax.ShapeDtypeStruct((rows, d_m), F32)],
        name="mem", compiler_params=_params(1),
    )(mem2d, wmem)


def _block_diag(w):
    h, n, _ = w.shape
    eye = jnp.eye(h, dtype=w.dtype)
    return (eye[:, None, :, None] * w[:, :, None, :]).reshape(h * n, h * n)


def _gate_blocks(w_r, w_i):
    per = MXU_TILE // w_r.shape[1]
    blocks = []
    for k in range(w_r.shape[0] // per):
        blocks.append(jnp.concatenate([_block_diag(w_r[k * per:(k + 1) * per]),
                                       _block_diag(w_i[k * per:(k + 1) * per])], axis=1))
    return jnp.stack(blocks)


def _prep_layer(l, sp_len, w_in, gmlp_ln_g, gmlp_ln_b, gmlp_ws, gmlp_bs, lru_conv_w,
                lru_conv_b, lru_w_r, lru_b_r, lru_w_i, lru_b_i, lru_lambda, mla_q_norm, mla_w_uq,
                mla_kv_norm, mla_w_ukv, w_br, w_out, ln_g, ln_b):
    d_model = w_in.shape[1]
    d_a = gmlp_ln_g.shape[1]
    d_b = lru_lambda.shape[1]
    q_lora = mla_q_norm.shape[1]
    kv_lora = mla_kv_norm.shape[1]
    d_c = H_C * D_V
    d_m = H_M * DH_M
    w = w_in[l]
    o_cq = 3 * d_a + 2 * d_b
    o_kr = o_cq + q_lora + kv_lora
    o_cg = o_kr + D_ROPE
    o_mq = o_cg + d_c
    o_g = o_mq + d_m
    half = D_ROPE // 2
    c_all = jnp.concatenate(
        [w[:, o_cq:o_cg], w[:, o_kr + half:o_cg], w[:, o_kr:o_kr + half],
         jnp.zeros((d_model, 4 * LANES - (q_lora + kv_lora + 2 * D_ROPE)), w.dtype)], axis=1)
    pieces = [("a_u", w[:, 0:d_a]), ("a_v", w[:, d_a:2 * d_a]), ("a_g", w[:, 2 * d_a:3 * d_a]),
              ("b_x", w[:, 3 * d_a:3 * d_a + d_b]), ("b_g", w[:, 3 * d_a + d_b:o_cq]),
              ("c_all", c_all), ("m_q", w[:, o_mq:o_g])]
    for k in range(N_BRANCH):
        pieces.append((f"g{k}", w[:, o_g + k * d_model:o_g + (k + 1) * d_model]))
    cols, off = {}, 0
    for name, p in pieces:
        cols[name] = (off, off + p.shape[1])
        off += p.shape[1]
    w1 = jnp.concatenate([p for _, p in pieces], axis=1).astype(BF16)
    wcgt = jnp.transpose(w[:, o_cg:o_mq]).astype(BF16)

    ws = jnp.tril(gmlp_ws[l][:, :sp_len, :sp_len])
    bs = gmlp_bs[l][:, :sp_len]
    bsp = jnp.repeat(jnp.transpose(bs), d_a // G_A, axis=1)

    wq = mla_w_uq[l].reshape(q_lora, H_C, D_NOPE + D_ROPE) * _EXP2_SCALE
    wq = jnp.concatenate([wq, jnp.zeros((q_lora, H_C, LANES - D_NOPE - D_ROPE), wq.dtype)], axis=2)
    wuqt = jnp.transpose(wq.reshape(q_lora, H_C * LANES)).astype(BF16)

    wkv = mla_w_ukv[l].reshape(kv_lora, H_C, D_NOPE + D_V)
    wk = wkv[:, :, :D_NOPE]
    wv = wkv[:, :, D_NOPE:]
    wkc = jnp.concatenate([wk, jnp.zeros((kv_lora, H_C, LANES - D_NOPE), wk.dtype)], axis=2)
    wkc = wkc.reshape(kv_lora, H_C * LANES).astype(BF16)
    lane = jnp.arange(H_C * LANES) % LANES
    wkr = ((lane[None, :] - D_NOPE) == jnp.arange(D_ROPE)[:, None]).astype(BF16)
    wv = jnp.concatenate([wv, jnp.zeros((kv_lora, H_C, V_ROWS - D_V), wv.dtype)], axis=2)
    wvt = jnp.transpose(wv.reshape(kv_lora, H_C * V_ROWS)).astype(BF16)

    return dict(
        dims=(d_a, d_b, d_c, d_m), q_lora=q_lora, kv_lora=kv_lora, cols=cols, w1=w1, wcgt=wcgt,
        gln_g=gmlp_ln_g[l][None, :], gln_b=gmlp_ln_b[l][None, :],
        wsp=ws.astype(BF16), bsp=bsp,
        conv_w=lru_conv_w[l], conv_b=lru_conv_b[l][None, :],
        wgate=_gate_blocks(lru_w_r[l], lru_w_i[l]).astype(BF16),
        bgate=jnp.stack([lru_b_r[l], lru_b_i[l]]),
        lam=lru_lambda[l][None, :],
        q_norm=mla_q_norm[l][None, :], wuqt=wuqt, kv_norm=mla_kv_norm[l][None, :],
        wkc=wkc, wkr=wkr, wvt=wvt,
        wbr=w_br[l].astype(BF16), wout=w_out[l].astype(BF16),
        ln_g=ln_g[l][None, :], ln_b=ln_b[l][None, :],
    )


def _rope_tables(pos):
    half = D_ROPE // 2
    freq = ROPE_BASE ** (-jnp.arange(half, dtype=F32) / half)
    ang = pos.astype(F32)[:, None] * freq[None, :]
    cos, sin = jnp.cos(ang), jnp.sin(ang)
    return (jnp.transpose(cos), jnp.transpose(sin),
            jnp.concatenate([cos, cos], axis=1), jnp.concatenate([-sin, sin], axis=1))


def _pad_conv_state(conv):
    return jnp.pad(conv, ((0, 0), (SUBLANES - (CONV_W - 1), 0), (0, 0)))


def _trunk_layer(x, lw, rope, conv0, h0, mk, mv, past_ckv, past_kr, *, tm, alpha):
    first_chunk = past_ckv is None
    tq = N_BUF * tm if first_chunk else tm
    outs = _branch_call(x, lw, _pad_conv_state(conv0), h0[:, None, :], mk, mv, rope, tm=tm, tq=tq,
                        first_chunk=first_chunk)
    partial, g2, cgst, qt, ckv_new, kr_new, conv_pad, h_new = outs[:8]
    if first_chunk:
        kcat, vt = outs[8:]
        v_rows = None
    else:
        v_rows = outs[8]
        ckv_all = jnp.concatenate([past_ckv, ckv_new], axis=1)
        kr_all = jnp.concatenate([past_kr, kr_new], axis=1)
        kcat, vt = _kv_call(ckv_all, kr_all, lw, tk=ckv_all.shape[1])
    oct = _attn_call(qt, kcat, vt, cgst, causal=first_chunk)
    x_new = _merge_call(x, partial, g2, oct, lw, tr=tm, alpha=alpha)
    return x_new, v_rows, conv_pad[:, SUBLANES - (CONV_W - 1):], h_new[:, 0], ckv_new, kr_new


def kernel(x_prompt, x_sample, mem_prompt, cache_mla_ckv, cache_mla_krope, cache_mem_k, cache_mem_v,
           state_lru_h, state_lru_conv, w_in, gmlp_ln_g, gmlp_ln_b, gmlp_ws, gmlp_bs,
           lru_conv_w, lru_conv_b, lru_w_r, lru_b_r, lru_w_i, lru_b_i, lru_lambda,
           mla_q_norm, mla_w_uq, mla_kv_norm, mla_w_ukv, mem_w_k, mem_w_v, w_br, w_out, ln_g, ln_b):
    bp, tp, d_model = x_prompt.shape
    bs, ts, _ = x_sample.shape
    depth = w_in.shape[0]
    past_len = cache_mla_ckv.shape[2]
    n_mem = mem_prompt.shape[1]
    d_b = lru_lambda.shape[1]
    alpha = (2.0 * depth) ** 0.25
    weights = (w_in, gmlp_ln_g, gmlp_ln_b, gmlp_ws, gmlp_bs, lru_conv_w, lru_conv_b, lru_w_r, lru_b_r,
               lru_w_i, lru_b_i, lru_lambda, mla_q_norm, mla_w_uq, mla_kv_norm, mla_w_ukv, w_br, w_out,
               ln_g, ln_b)
    rope_p = _rope_tables(jnp.arange(tp))
    rope_s = _rope_tables(past_len + jnp.arange(ts))
    tm_p = min(tp, ROW_TILE)
    assert tp % (N_BUF * tm_p) == 0 and tm_p % CHUNK == 0 and ts % SUBLANES == 0

    xp, xs = x_prompt, x_sample
    acc = [[] for _ in range(11)]
    for l in range(depth):
        lw_p = _prep_layer(l, min(tp, A_CHUNK), *weights)
        lw_s = lw_p if min(ts, A_CHUNK) == min(tp, A_CHUNK) else _prep_layer(l, min(ts, A_CHUNK), *weights)
        wmem = jnp.concatenate([mem_w_k[l], mem_w_v[l]], axis=1).astype(BF16)
        mk, mv = _mem_call(mem_prompt.reshape(bp * n_mem, d_model), wmem, tr=n_mem)
        mk = mk.reshape(bp, n_mem, -1)
        mv = mv.reshape(bp, n_mem, -1)
        xp, _, conv_n, h_n, ckv_n, kr_n = _trunk_layer(
            xp, lw_p, rope_p, jnp.zeros((bp, CONV_W - 1, d_b), F32), jnp.zeros((bp, d_b), F32),
            mk, mv, None, None, tm=tm_p, alpha=alpha)
        for k, val in zip(range(6), (ckv_n, kr_n, mk.reshape(bp, n_mem, H_M, DH_M),
                                     mv.reshape(bp, n_mem, H_M, DH_M), h_n, conv_n)):
            acc[k].append(val)
        xs, v_n, conv_n, h_n, ckv_n, kr_n = _trunk_layer(
            xs, lw_s, rope_s, state_lru_conv[l], state_lru_h[l],
            cache_mem_k[l].reshape(bs, n_mem, -1), cache_mem_v[l].reshape(bs, n_mem, -1),
            cache_mla_ckv[l], cache_mla_krope[l], tm=ts, alpha=alpha)
        for k, val in zip(range(6, 11), (ckv_n, kr_n, h_n, conv_n, v_n)):
            acc[k].append(val)
    return (xp, xs) + tuple(jnp.stack(a) for a in acc)
```

```python
import functools
import math

import jax
import jax.numpy as jnp
from jax import lax
from jax.experimental import pallas as pl
from jax.experimental.pallas import tpu as pltpu

CHUNK = 64
G_A = 4
A_CHUNK = 128
H_B = 8
CONV_W = 4
LRU_C = 8.0
H_C = 8
D_NOPE = 64
D_ROPE = 32
D_V = 64
ROPE_BASE = 10000.0
H_M = 4
DH_M = 64
N_BRANCH = 4
EPS = 1e-6

LANES = 128
SUBLANES = 8
VMEM_LIMIT = 56 * 1024 * 1024
ROW_TILE = 256
MXU_TILE = 256

F32 = jnp.float32
BF16 = jnp.bfloat16


def _dot(a, b):
    return jnp.dot(a, b, preferred_element_type=F32)


def _dot_nt(a, b):
    return lax.dot_general(a, b, (((1,), (1,)), ((), ())), preferred_element_type=F32)


def _dot_tn(a, b):
    return lax.dot_general(a, b, (((0,), (0,)), ((), ())), preferred_element_type=F32)


def _sigmoid(x):
    return 1.0 / (1.0 + jnp.exp(-x))


def _silu(x):
    return x * _sigmoid(x)


def _gelu(x):
    return jax.nn.gelu(x)


def _expm1_nonpos(x):
    u = jnp.exp(x)
    near = (u - 1.0) * x / jnp.log(jnp.where(u == 1.0, 2.0, jnp.maximum(u, 0.5)))
    return jnp.where(u == 1.0, x, jnp.where(u > 0.5, near, u - 1.0))


def _layer_norm(x, g, b):
    mu = jnp.mean(x, -1, keepdims=True)
    var = jnp.mean(jnp.square(x - mu), -1, keepdims=True)
    return (x - mu) * lax.rsqrt(var + EPS) * g + b


def _rms_norm(x, g):
    return x * lax.rsqrt(jnp.mean(jnp.square(x), -1, keepdims=True) + EPS) * g


def _const_spec(shape):
    nd = len(shape)
    return pl.BlockSpec(shape, lambda *_: (0,) * nd, pipeline_mode=pl.Buffered(1))


def _layer_spec(arr, l):
    nd = arr.ndim - 1
    return pl.BlockSpec((None,) + arr.shape[1:], lambda *_: (l,) + (0,) * nd, pipeline_mode=pl.Buffered(1))


def _params(n_axes):
    return pltpu.CompilerParams(dimension_semantics=("arbitrary",) * n_axes,
                                vmem_limit_bytes=VMEM_LIMIT)


V_ROWS = 80


def _keys_values(ckv, kr, wkc_ref, wkr_ref, wvt_ref):
    c = ckv.astype(BF16)
    kcat = (_dot(c, wkc_ref[...]) + _dot(kr.astype(BF16), wkr_ref[...])).astype(BF16)
    v_t = _dot_nt(wvt_ref[...], c)
    row = lax.broadcasted_iota(jnp.int32, (v_t.shape[0], 1), 0)
    return kcat, jnp.where(row % V_ROWS == D_V, 1.0, v_t).astype(BF16)


def _branch_kernel(x_ref, whead_ref, wkrin_ref, wtail_ref, wcgt_ref, glng_ref, glnb_ref, wsp_ref, bsp_ref,
                   convw_ref, convb_ref, wgate_ref, br_ref, bi_ref, lam_ref, conv0_ref, h0_ref, qnorm_ref, wuqt_ref,
                   kvnorm_ref, cosq_ref, sinq_ref, ck_ref, sk_ref, mk_ref, mv_ref, wbr_ref,
                   *rest, first_chunk, tm, sp_len, d_a, d_b, d_c, d_m, q_lora, kv_lora, cols):
    if first_chunk:
        (wkc_ref, wkr_ref, wvt_ref, partial_ref, g2_ref, cgst_ref, qt_ref, ckv_ref, kr_ref, conv_ref, h_ref,
         kcat_ref, vt_ref, merged_scr) = rest
    else:
        partial_ref, g2_ref, cgst_ref, qt_ref, ckv_ref, kr_ref, conv_ref, h_ref, v_ref, merged_scr = rest
    t = pl.program_id(1)
    xb = x_ref[0].astype(BF16)

    w_in_refs = (whead_ref, wkrin_ref, wtail_ref)

    def zin(name):
        which, lo, hi = cols[name]
        return _dot(xb, w_in_refs[which][:, lo:hi])

    z_u, z_v, z_ag, z_g0 = zin("a_u"), zin("a_v"), zin("a_g"), zin("g0")
    u = _gelu(z_u)
    v = _layer_norm(_gelu(z_v), glng_ref[...], glnb_ref[...])
    if not first_chunk:
        v_ref[0] = v
    vb = v.astype(BF16)
    n_groups = d_a // LANES
    row_blocks = []
    for c in range(tm // sp_len):
        col_blocks = [_dot(wsp_ref[g], vb[c * sp_len:(c + 1) * sp_len, g * LANES:(g + 1) * LANES])
                      for g in range(n_groups)]
        row_blocks.append(jnp.concatenate(col_blocks, axis=1) + bsp_ref[...])
    s = row_blocks[0] if len(row_blocks) == 1 else jnp.concatenate(row_blocks, axis=0)
    oa = (u * s) * _silu(z_ag)
    merged_scr[...] = _sigmoid(z_g0) * _dot(oa.astype(BF16), wbr_ref[0:d_a, :])

    @pl.when(t == 0)
    def _():
        conv_ref[...] = conv0_ref[...]
        h_ref[...] = h0_ref[...]

    bx = zin("b_x")
    hist = conv_ref[0]
    row8 = lax.broadcasted_iota(jnp.int32, (SUBLANES, 1), 0)
    xc = convb_ref[...]
    for k in range(CONV_W):
        shift = CONV_W - 1 - k
        if shift == 0:
            sh = bx
        else:
            sh = pltpu.roll(bx, shift, axis=0)
            head = jnp.where(row8 < shift, pltpu.roll(hist, shift, axis=0), sh[0:SUBLANES])
            sh = head if tm == SUBLANES else jnp.concatenate([head, sh[SUBLANES:]], axis=0)
        xc = xc + sh * convw_ref[k:k + 1, :]
    conv_ref[0] = bx[tm - SUBLANES:tm]

    xcb = xc.astype(BF16)
    n_blk = d_b // MXU_TILE
    ri = [_dot(xcb[:, k * MXU_TILE:(k + 1) * MXU_TILE], wgate_ref[k]) for k in range(n_blk)]
    z_bg, z_g1, z_g2 = zin("b_g"), zin("g1"), zin("g2")
    z_mq, z_g3, z_c, z_kr = zin("m_q"), zin("g3"), zin("c_qkv"), zin("c_kr")
    cg_t = _dot_nt(wcgt_ref[...], xb)
    r = _sigmoid(jnp.concatenate([p[:, 0:MXU_TILE] for p in ri], axis=1) + br_ref[...])
    i_gate = _sigmoid(jnp.concatenate([p[:, MXU_TILE:2 * MXU_TILE] for p in ri], axis=1) + bi_ref[...])
    neg_lam = -lam_ref[...]
    softplus = jnp.maximum(neg_lam, 0.0) + jnp.log1p(jnp.exp(-jnp.abs(neg_lam)))
    log_a = (-LRU_C * r) * softplus
    a = jnp.exp(log_a)
    bval = jnp.sqrt(-_expm1_nonpos(2.0 * log_a)) * (i_gate * xc)
    rows = lax.broadcasted_iota(jnp.int32, (tm, 1), 0)
    d = 1
    while d < tm:
        keep = rows >= d
        a_sh = pltpu.roll(a, d, axis=0)
        b_sh = pltpu.roll(bval, d, axis=0)
        bval = jnp.where(keep, a * b_sh + bval, bval)
        a = jnp.where(keep, a * a_sh, a)
        d *= 2
    h = a * h_ref[0] + bval
    h_ref[0] = h[tm - 1:tm]
    ob = h * _silu(z_bg)
    merged_scr[...] += _sigmoid(z_g1) * _dot(ob.astype(BF16), wbr_ref[d_a:d_a + d_b, :])
    g2_ref[0] = _sigmoid(z_g2).astype(BF16)

    mq = z_mq.astype(BF16)
    mkb = mk_ref[0].astype(BF16)
    mvb = mv_ref[0].astype(BF16)
    lane = lax.broadcasted_iota(jnp.int32, (1, LANES), 1)
    slabs = []
    for p in range(d_m // LANES):
        mq_p = mq[:, p * LANES:(p + 1) * LANES]
        mk_p = mkb[:, p * LANES:(p + 1) * LANES]
        mv_p = mvb[:, p * LANES:(p + 1) * LANES]
        acc = None
        for half in range(LANES // DH_M):
            sel = (lane >= half * DH_M) & (lane < (half + 1) * DH_M)
            sc = _dot_nt(mq_p, jnp.where(sel, mk_p, jnp.zeros_like(mk_p))) * (DH_M ** -0.5)
            e = jnp.exp(sc - jnp.max(sc, -1, keepdims=True))
            prob = (e / jnp.sum(e, -1, keepdims=True)).astype(BF16)
            o = _dot(prob, jnp.where(sel, mv_p, jnp.zeros_like(mv_p)))
            acc = o if acc is None else acc + o
        slabs.append(acc)
    om = jnp.concatenate(slabs, axis=1)
    m_lo = d_a + d_b + d_c
    partial_ref[0] = (merged_scr[...]
                      + _sigmoid(z_g3) * _dot(om.astype(BF16), wbr_ref[m_lo:m_lo + d_m, :])).astype(BF16)
    cgst_ref[0] = _silu(cg_t)

    ckv = _rms_norm(z_c[:, q_lora:q_lora + kv_lora], kvnorm_ref[...])
    kr = z_kr[:, 0:D_ROPE] * ck_ref[...] + z_kr[:, D_ROPE:2 * D_ROPE] * sk_ref[...]
    ckv_ref[0] = ckv
    kr_ref[0] = kr
    if first_chunk:
        kcat_ref[0], vt_ref[0, 0] = _keys_values(ckv, kr, wkc_ref, wkr_ref, wvt_ref)
    cqn = _rms_norm(z_c[:, 0:q_lora], qnorm_ref[...]).astype(BF16)
    q_t = _dot_nt(wuqt_ref[...], cqn)
    cos_t, sin_t = cosq_ref[...], sinq_ref[...]
    half = D_ROPE // 2
    for hd in range(H_C):
        lo = hd * LANES
        x1 = q_t[lo + D_NOPE:lo + D_NOPE + half]
        x2 = q_t[lo + D_NOPE + half:lo + D_NOPE + D_ROPE]
        qt_ref[0, 0, lo:lo + D_NOPE, :] = q_t[lo:lo + D_NOPE].astype(BF16)
        qt_ref[0, 0, lo + D_NOPE:lo + D_NOPE + half, :] = (x1 * cos_t - x2 * sin_t).astype(BF16)
        qt_ref[0, 0, lo + D_NOPE + half:lo + D_NOPE + D_ROPE, :] = (x1 * sin_t + x2 * cos_t).astype(BF16)
        qt_ref[0, 0, lo + D_NOPE + D_ROPE:lo + LANES, :] = jnp.zeros((LANES - D_NOPE - D_ROPE, tm), BF16)


def _branch_call(x, pw, l, sp, conv0_pad, h0, mk, mv, rope, *, tm, tq, first_chunk):
    bsz, seq, d_model = x.shape
    d_a, d_b, d_c, d_m = pw["dims"]
    q_lora, kv_lora = pw["q_lora"], pw["kv_lora"]
    n_mem = mk.shape[1]
    cos_t, sin_t, ck_t, sk_t = rope
    grid = (bsz, seq // tm)

    def row_spec(width):
        return pl.BlockSpec((1, tm, width), lambda b, t: (b, t, 0))

    def col_spec(height):
        return pl.BlockSpec((1, height, tm), lambda b, t: (b, 0, t))

    def tab_spec(width):
        return pl.BlockSpec((tm, width), lambda b, t: (t, 0))

    def batch_spec(rows, width):
        return pl.BlockSpec((1, rows, width), lambda b, t: (b, 0, 0))

    names = ["w_head", "w_krin", "w_tail", "wcgt", "gln_g", "gln_b", "wsp%d" % sp, "bsp%d" % sp, "conv_w", "conv_b",
             "wgate", "b_r", "b_i", "lam"]
    names2 = ["q_norm", "wuqt", "kv_norm"]
    in_specs = ([row_spec(d_model)] + [_layer_spec(pw[n], l) for n in names]
                + [batch_spec(SUBLANES, d_b), batch_spec(1, d_b)] + [_layer_spec(pw[n], l) for n in names2]
                + [pl.BlockSpec((D_ROPE // 2, tm), lambda b, t: (0, t)), pl.BlockSpec((D_ROPE // 2, tm), lambda b, t: (0, t)),
                   tab_spec(D_ROPE), tab_spec(D_ROPE), batch_spec(n_mem, d_m), batch_spec(n_mem, d_m),
                   _layer_spec(pw["wbr"], l)])
    operands = ([x] + [pw[n] for n in names] + [conv0_pad, h0] + [pw[n] for n in names2]
                + [cos_t, sin_t, ck_t, sk_t, mk, mv, pw["wbr"]])
    out_shape = [
        jax.ShapeDtypeStruct((bsz, seq, d_model), BF16),
        jax.ShapeDtypeStruct((bsz, seq, d_model), BF16),
        jax.ShapeDtypeStruct((bsz, d_c, seq), F32),
        jax.ShapeDtypeStruct((bsz, seq // tq, H_C * LANES, tq), BF16),
        jax.ShapeDtypeStruct((bsz, seq, kv_lora), F32),
        jax.ShapeDtypeStruct((bsz, seq, D_ROPE), F32),
        jax.ShapeDtypeStruct((bsz, SUBLANES, d_b), F32),
        jax.ShapeDtypeStruct((bsz, 1, d_b), F32),
    ]
    out_specs = [
        row_spec(d_model), row_spec(d_model), col_spec(d_c),
        pl.BlockSpec((1, 1, H_C * LANES, tm), lambda b, t: (b, t // (tq // tm), 0, t % (tq // tm))),
        row_spec(kv_lora), row_spec(D_ROPE), batch_spec(SUBLANES, d_b), batch_spec(1, d_b),
    ]
    if first_chunk:
        for name in ("wkc", "wkr", "wvt"):
            in_specs.append(_layer_spec(pw[name], l))
            operands.append(pw[name])
        out_shape += [jax.ShapeDtypeStruct((bsz, seq, H_C * LANES), BF16),
                      jax.ShapeDtypeStruct((bsz, seq // tm, H_C * V_ROWS, tm), BF16)]
        out_specs += [row_spec(H_C * LANES),
                      pl.BlockSpec((1, 1, H_C * V_ROWS, tm), lambda b, t: (b, t, 0, 0))]
    else:
        out_shape.append(jax.ShapeDtypeStruct((bsz, seq, d_a), F32))
        out_specs.append(row_spec(d_a))
    kern = functools.partial(
        _branch_kernel, first_chunk=first_chunk, tm=tm, sp_len=sp, d_a=d_a, d_b=d_b, d_c=d_c, d_m=d_m,
        q_lora=q_lora, kv_lora=kv_lora, cols=pw["cols"])
    return pl.pallas_call(
        kern, grid=grid, in_specs=in_specs, out_specs=out_specs, out_shape=out_shape,
        scratch_shapes=[pltpu.VMEM((tm, d_model), F32)],
        name="branch", compiler_params=_params(2),
    )(*operands)


def _kv_kernel(ckv_ref, kr_ref, wkc_ref, wkr_ref, wvt_ref, kcat_ref, vt_ref):
    kcat_ref[0], vt_ref[0, 0] = _keys_values(ckv_ref[0], kr_ref[0], wkc_ref, wkr_ref, wvt_ref)


def _kv_call(ckv_all, kr_all, pw, l, *, tk):
    bsz, t_k, kv_lora = ckv_all.shape
    n_k = H_C * LANES
    n_v = H_C * V_ROWS
    nkt = t_k // tk
    return pl.pallas_call(
        _kv_kernel, grid=(bsz, nkt),
        in_specs=[pl.BlockSpec((1, tk, kv_lora), lambda b, j: (b, j, 0)),
                  pl.BlockSpec((1, tk, D_ROPE), lambda b, j: (b, j, 0)),
                  _layer_spec(pw["wkc"], l), _layer_spec(pw["wkr"], l), _layer_spec(pw["wvt"], l)],
        out_specs=[pl.BlockSpec((1, tk, n_k), lambda b, j: (b, j, 0)),
                   pl.BlockSpec((1, 1, n_v, tk), lambda b, j: (b, j, 0, 0))],
        out_shape=[jax.ShapeDtypeStruct((bsz, t_k, n_k), BF16),
                   jax.ShapeDtypeStruct((bsz, nkt, n_v, tk), BF16)],
        name="kv", compiler_params=_params(2),
    )(ckv_all, kr_all, pw["wkc"], pw["wkr"], pw["wvt"])


HEADS_PER_STEP = 2
_EXP2_SCALE = (D_NOPE + D_ROPE) ** -0.5 * math.log2(math.e)


def _scores(k_tile, q_tile, out_refs=None):
    res = [_dot(k_tile[:, h * LANES:(h + 1) * LANES], q_tile[h * LANES:(h + 1) * LANES, :])
           for h in range(HEADS_PER_STEP)]
    if out_refs is None:
        return res
    for h in range(HEADS_PER_STEP):
        out_refs[h] = res[h]
    return None


def _softmax_tile(s_t, m_old):
    m_new = jnp.maximum(m_old, jnp.max(s_t, axis=0, keepdims=True))
    return jnp.exp2(s_t - m_new).astype(BF16), m_new, jnp.exp2(m_old - m_new)


def _attn_output(acc, cgs_t):
    return (acc[0:D_V] / acc[D_V:D_V + 1] * cgs_t).astype(BF16)


N_BUF = 4


def _attn_causal_kernel(qt_ref, k_ref, vt_ref, cgst_ref, bias_ref, o_ref, *scratch, tq, tk):
    s_buf, e_buf, acc_scr = scratch[0:N_BUF], scratch[N_BUF:2 * N_BUF], scratch[2 * N_BUF]
    i = pl.program_id(2)
    nh = HEADS_PER_STEP
    q_cur = qt_ref.at[0, i]
    q_nxt = qt_ref.at[0, jnp.minimum(i + 1, pl.num_programs(2) - 1)]

    def k_tile(j):
        return k_ref[0, pl.ds(pl.multiple_of(j * tk, tk), tk), :]

    def pv(j, e_in, c0):
        vt = vt_ref[0, j]
        return [_dot(vt[h * V_ROWS:(h + 1) * V_ROWS, :], e_in[h, :, c0:tq]) for h in range(nh)]

    def acc_update(pvs, alphas, c0):
        for h in range(nh):
            acc_scr[h, :, c0:tq] = acc_scr[h, :, c0:tq] * alphas[h][:, c0:tq] + pvs[h]

    def softmax(slot, carry, c0, add_bias):
        new = []
        for h in range(nh):
            m_old, a_prev, _ = carry[h]
            s_t = s_buf[slot][h, :, c0:tq]
            if add_bias:
                lead = s_t[:, 0:tk] + bias_ref[...]
                s_t = lead if c0 + tk == tq else jnp.concatenate([lead, s_t[:, tk:]], axis=1)
            e, m_new, alpha = _softmax_tile(s_t, m_old[:, c0:tq])
            e_buf[slot][h, :, c0:tq] = e
            if c0:
                m_new = jnp.concatenate([m_old[:, 0:c0], m_new], axis=1)
                alpha = jnp.concatenate([jnp.ones((1, c0), F32), alpha], axis=1)
            new.append((m_new, alpha, a_prev))
        return tuple(new)

    def stage(n, slot, carry):
        pvs = pv(jnp.maximum(n - 2, 0), e_buf[(slot + 2) % N_BUF], 0)
        _scores(k_tile(n + 3), q_cur, s_buf[(slot + 3) % N_BUF])
        new = softmax(slot, carry, 0, False)
        acc_update(pvs, [c[2] for c in carry], 0)
        return new

    for slot in (N_BUF - 2, N_BUF - 1):
        e_buf[slot][...] = jnp.zeros(e_buf[slot].shape, BF16)
    acc_scr[...] = jnp.zeros(acc_scr.shape, F32)

    @pl.when(i == 0)
    def _():
        for slot in range(N_BUF - 1):
            _scores(k_tile(slot), q_cur, s_buf[slot])

    ones = jnp.ones((1, tq), F32)
    carry = tuple((jnp.full((1, tq), -jnp.inf, F32), ones, ones) for _ in range(nh))

    def body(t, carry):
        for slot in range(N_BUF):
            carry = stage(N_BUF * t + slot, slot, carry)
        return carry

    carry = lax.fori_loop(0, i, body, carry)

    j0 = N_BUF * i
    last = N_BUF - 1
    for d in range(N_BUF):
        c_prev = max(d - 2, 0) * tk
        pvs = pv(jnp.maximum(j0 + d - 2, 0), e_buf[(d + 2) % N_BUF], c_prev)
        if d == 0:
            k_last = k_tile(j0 + last)
            for h in range(nh):
                s_buf[last][h, :, tq - tk:tq] = _dot(
                    k_last[:, h * LANES:(h + 1) * LANES],
                    q_cur[h * LANES:(h + 1) * LANES, tq - tk:tq]) + bias_ref[...]
        else:
            _scores(k_tile(d - 1), q_nxt, s_buf[d - 1])
        alphas = [c[2] for c in carry]
        carry = softmax(d, carry, d * tk, d < last)
        acc_update(pvs, alphas, c_prev)
    acc_update(pv(j0 + last - 1, e_buf[last - 1], (last - 1) * tk), [c[2] for c in carry], (last - 1) * tk)
    acc_update(pv(j0 + last, e_buf[last], last * tk), [c[1] for c in carry], last * tk)
    for h in range(nh):
        o_ref[0, h * D_V:(h + 1) * D_V, :] = _attn_output(acc_scr[h], cgst_ref[0, h * D_V:(h + 1) * D_V, :])


def _attn_full_kernel(qt_ref, k_ref, vt_ref, cgst_ref, o_ref):
    tq = qt_ref.shape[-1]
    scores = _scores(k_ref[0], qt_ref.at[0, 0])
    vt = vt_ref[0, 0]
    for h in range(HEADS_PER_STEP):
        e, _, _ = _softmax_tile(scores[h], jnp.full((1, tq), -jnp.inf, F32))
        acc = _dot(vt[h * V_ROWS:(h + 1) * V_ROWS, :], e)
        o_ref[0, h * D_V:(h + 1) * D_V, :] = _attn_output(acc, cgst_ref[0, h * D_V:(h + 1) * D_V, :])


def _attn_call(qt, kcat, vt, cgst, *, causal):
    bsz, n_qt, _, tq = qt.shape
    seq = n_qt * tq
    t_k = kcat.shape[1]
    nkt, _, tk = vt.shape[1:]
    nh = HEADS_PER_STEP
    in_specs = [pl.BlockSpec((1, n_qt, nh * LANES, tq), lambda b, p, i: (b, 0, p, 0)),
                pl.BlockSpec((1, t_k, nh * LANES), lambda b, p, i: (b, 0, p)),
                pl.BlockSpec((1, nkt, nh * V_ROWS, tk), lambda b, p, i: (b, 0, p, 0)),
                pl.BlockSpec((1, nh * D_V, tq), lambda b, p, i: (b, p, i))]
    operands = [qt, kcat, vt, cgst]
    if causal:
        assert tq == N_BUF * tk and seq == t_k and tk % CHUNK == 0
        chunk = jnp.arange(tk) // CHUNK
        operands.append(jnp.where(chunk[:, None] <= chunk[None, :], 0.0, -1e30).astype(F32))
        in_specs.append(_const_spec((tk, tk)))
        kern = functools.partial(_attn_causal_kernel, tq=tq, tk=tk)
        scratch = ([pltpu.VMEM((nh, tk, tq), F32)] * N_BUF + [pltpu.VMEM((nh, tk, tq), BF16)] * N_BUF
                   + [pltpu.VMEM((nh, V_ROWS, tq), F32)])
    else:
        assert nkt == 1 and seq == tq
        kern = _attn_full_kernel
        scratch = []
    return pl.pallas_call(
        kern, grid=(bsz, H_C // nh, seq // tq),
        in_specs=in_specs,
        out_specs=pl.BlockSpec((1, nh * D_V, tq), lambda b, p, i: (b, p, i)),
        out_shape=jax.ShapeDtypeStruct((bsz, H_C * D_V, seq), BF16),
        scratch_shapes=scratch,
        name="attn", compiler_params=_params(3),
    )(*operands)


def _merge_kernel(x_ref, partial_ref, g2_ref, oct_ref, wbr_ref, wout_ref, lng_ref, lnb_ref, o_ref,
                  *, c_lo, d_c, alpha):
    yc = _dot_tn(oct_ref[0], wbr_ref[c_lo:c_lo + d_c, :])
    merged = partial_ref[0].astype(F32) + g2_ref[0].astype(F32) * yc
    y = _dot(merged.astype(BF16), wout_ref[...])
    o_ref[0] = _layer_norm(alpha * x_ref[0] + y, lng_ref[...], lnb_ref[...])


def _merge_call(x, partial, g2, oct, pw, l, *, tr, alpha):
    bsz, seq, d_model = x.shape
    d_a, d_b, d_c, _ = pw["dims"]
    kern = functools.partial(_merge_kernel, c_lo=d_a + d_b, d_c=d_c, alpha=alpha)

    def row_spec(width):
        return pl.BlockSpec((1, tr, width), lambda b, i: (b, i, 0))

    return pl.pallas_call(
        kern, grid=(bsz, seq // tr),
        in_specs=[row_spec(d_model), row_spec(d_model), row_spec(d_model),
                  pl.BlockSpec((1, d_c, tr), lambda b, i: (b, 0, i)),
                  _layer_spec(pw["wbr"], l), _layer_spec(pw["wout"], l),
                  _layer_spec(pw["ln_g"], l), _layer_spec(pw["ln_b"], l)],
        out_specs=row_spec(d_model),
        out_shape=jax.ShapeDtypeStruct((bsz, seq, d_model), F32),
        name="merge", compiler_params=_params(2),
    )(x, partial, g2, oct, pw["wbr"], pw["wout"], pw["ln_g"], pw["ln_b"])


def _mem_kernel(m_ref, w_ref, k_ref, v_ref, *, d_m):
    kv = _dot(m_ref[...].astype(BF16), w_ref[...])
    k_ref[...] = kv[:, 0:d_m]
    v_ref[...] = kv[:, d_m:2 * d_m]


def _mem_call(mem2d, wmem, l, *, tr):
    rows, d_model = mem2d.shape
    d_m = wmem.shape[2] // 2
    return pl.pallas_call(
        functools.partial(_mem_kernel, d_m=d_m), grid=(rows // tr,),
        in_specs=[pl.BlockSpec((tr, d_model), lambda i: (i, 0)), _layer_spec(wmem, l)],
        out_specs=[pl.BlockSpec((tr, d_m), lambda i: (i, 0)), pl.BlockSpec((tr, d_m), lambda i: (i, 0))],
        out_shape=[jax.ShapeDtypeStruct((rows, d_m), F32), jax.ShapeDtypeStruct((rows, d_m), F32)],
        name="mem", compiler_params=_params(1),
    )(mem2d, wmem)


def _block_diag(w):
    h, n, _ = w.shape
    eye = jnp.eye(h, dtype=w.dtype)
    return (eye[:, None, :, None] * w[:, :, None, :]).reshape(h * n, h * n)


def _gate_blocks(w_r, w_i):
    per = MXU_TILE // w_r.shape[1]
    blocks = []
    for k in range(w_r.shape[0] // per):
        blocks.append(jnp.concatenate([_block_diag(w_r[k * per:(k + 1) * per]),
                                       _block_diag(w_i[k * per:(k + 1) * per])], axis=1))
    return jnp.stack(blocks)


def _prep_weights(sp_lens, w_in, gmlp_ln_g, gmlp_ln_b, gmlp_ws, gmlp_bs, lru_conv_w,
                  lru_conv_b, lru_w_r, lru_b_r, lru_w_i, lru_b_i, lru_lambda, mla_q_norm, mla_w_uq,
                  mla_kv_norm, mla_w_ukv, mem_w_k, mem_w_v, w_br, w_out, ln_g, ln_b):
    depth, d_model, _ = w_in.shape
    d_a = gmlp_ln_g.shape[1]
    d_b = lru_lambda.shape[1]
    q_lora = mla_q_norm.shape[1]
    kv_lora = mla_kv_norm.shape[1]
    d_c = H_C * D_V
    d_m = H_M * DH_M
    o_cq = 3 * d_a + 2 * d_b
    o_kr = o_cq + q_lora + kv_lora
    o_cg = o_kr + D_ROPE
    o_mq = o_cg + d_c
    o_g = o_mq + d_m
    half = D_ROPE // 2
    assert o_kr % LANES == 0 and (o_g - o_mq) % LANES == 0
    cols = {"a_u": (0, 0, d_a), "a_v": (0, d_a, 2 * d_a), "a_g": (0, 2 * d_a, 3 * d_a),
            "b_x": (0, 3 * d_a, 3 * d_a + d_b), "b_g": (0, 3 * d_a + d_b, o_cq), "c_qkv": (0, o_cq, o_kr),
            "c_kr": (1, 0, LANES), "m_q": (2, 0, d_m)}
    for k in range(N_BRANCH):
        cols[f"g{k}"] = (2, d_m + k * d_model, d_m + (k + 1) * d_model)
    w_head = w_in[:, :, 0:o_kr].astype(BF16)
    w_tail = w_in[:, :, o_mq:].astype(BF16)
    w_krin = jnp.concatenate(
        [w_in[:, :, o_kr:o_cg], w_in[:, :, o_kr + half:o_cg], w_in[:, :, o_kr:o_kr + half],
         jnp.zeros((depth, d_model, LANES - 2 * D_ROPE), w_in.dtype)], axis=2).astype(BF16)
    wcgt = jnp.swapaxes(w_in[:, :, o_cg:o_mq], 1, 2).astype(BF16)

    wq = mla_w_uq.reshape(depth, q_lora, H_C, D_NOPE + D_ROPE) * _EXP2_SCALE
    wq = jnp.concatenate([wq, jnp.zeros((depth, q_lora, H_C, LANES - D_NOPE - D_ROPE), wq.dtype)], axis=3)
    wuqt = jnp.swapaxes(wq.reshape(depth, q_lora, H_C * LANES), 1, 2).astype(BF16)

    wkv = mla_w_ukv.reshape(depth, kv_lora, H_C, D_NOPE + D_V)
    wk = wkv[..., :D_NOPE]
    wv = wkv[..., D_NOPE:]
    wkc = jnp.concatenate([wk, jnp.zeros((depth, kv_lora, H_C, LANES - D_NOPE), wk.dtype)], axis=3)
    wkc = wkc.reshape(depth, kv_lora, H_C * LANES).astype(BF16)
    lane = jnp.arange(H_C * LANES) % LANES
    wkr = ((lane[None, :] - D_NOPE) == jnp.arange(D_ROPE)[:, None]).astype(BF16)
    wkr = jnp.broadcast_to(wkr, (depth,) + wkr.shape)
    wv = jnp.concatenate([wv, jnp.zeros((depth, kv_lora, H_C, V_ROWS - D_V), wv.dtype)], axis=3)
    wvt = jnp.swapaxes(wv.reshape(depth, kv_lora, H_C * V_ROWS), 1, 2).astype(BF16)

    def row(p):
        return p[:, None, :]

    pw = dict(
        dims=(d_a, d_b, d_c, d_m), q_lora=q_lora, kv_lora=kv_lora, cols=cols,
        w_head=w_head, w_krin=w_krin, w_tail=w_tail, wcgt=wcgt,
        gln_g=row(gmlp_ln_g), gln_b=row(gmlp_ln_b),
        conv_w=lru_conv_w, conv_b=row(lru_conv_b),
        wgate=jnp.stack([_gate_blocks(lru_w_r[l], lru_w_i[l]) for l in range(depth)]).astype(BF16),
        b_r=row(lru_b_r), b_i=row(lru_b_i), lam=row(lru_lambda),
        q_norm=row(mla_q_norm), wuqt=wuqt, kv_norm=row(mla_kv_norm),
        wkc=wkc, wkr=wkr, wvt=wvt,
        wbr=w_br.astype(BF16), wout=w_out.astype(BF16), ln_g=row(ln_g), ln_b=row(ln_b),
        wmem=jnp.concatenate([mem_w_k, mem_w_v], axis=2).astype(BF16),
    )
    for sp in sp_lens:
        pw["wsp%d" % sp] = jnp.tril(gmlp_ws[:, :, :sp, :sp]).astype(BF16)
        pw["bsp%d" % sp] = jnp.repeat(jnp.swapaxes(gmlp_bs[:, :, :sp], 1, 2), d_a // G_A, axis=2)
    return pw


def _rope_tables(pos):
    half = D_ROPE // 2
    freq = ROPE_BASE ** (-jnp.arange(half, dtype=F32) / half)
    ang = pos.astype(F32)[:, None] * freq[None, :]
    cos, sin = jnp.cos(ang), jnp.sin(ang)
    return (jnp.transpose(cos), jnp.transpose(sin),
            jnp.concatenate([cos, cos], axis=1), jnp.concatenate([-sin, sin], axis=1))


def _pad_conv_state(conv):
    return jnp.pad(conv, ((0, 0), (SUBLANES - (CONV_W - 1), 0), (0, 0)))


def _trunk_layer(x, pw, l, sp, rope, conv0, h0, mk, mv, past_ckv, past_kr, *, tm, alpha):
    first_chunk = past_ckv is None
    tq = N_BUF * tm if first_chunk else tm
    outs = _branch_call(x, pw, l, sp, _pad_conv_state(conv0), h0[:, None, :], mk, mv, rope, tm=tm, tq=tq,
                        first_chunk=first_chunk)
    partial, g2, cgst, qt, ckv_new, kr_new, conv_pad, h_new = outs[:8]
    if first_chunk:
        kcat, vt = outs[8:]
        v_rows = None
    else:
        v_rows = outs[8]
        ckv_all = jnp.concatenate([past_ckv, ckv_new], axis=1)
        kr_all = jnp.concatenate([past_kr, kr_new], axis=1)
        kcat, vt = _kv_call(ckv_all, kr_all, pw, l, tk=ckv_all.shape[1])
    oct = _attn_call(qt, kcat, vt, cgst, causal=first_chunk)
    x_new = _merge_call(x, partial, g2, oct, pw, l, tr=tm, alpha=alpha)
    return x_new, v_rows, conv_pad[:, SUBLANES - (CONV_W - 1):], h_new[:, 0], ckv_new, kr_new


def kernel(x_prompt, x_sample, mem_prompt, cache_mla_ckv, cache_mla_krope, cache_mem_k, cache_mem_v,
           state_lru_h, state_lru_conv, w_in, gmlp_ln_g, gmlp_ln_b, gmlp_ws, gmlp_bs,
           lru_conv_w, lru_conv_b, lru_w_r, lru_b_r, lru_w_i, lru_b_i, lru_lambda,
           mla_q_norm, mla_w_uq, mla_kv_norm, mla_w_ukv, mem_w_k, mem_w_v, w_br, w_out, ln_g, ln_b):
    bp, tp, d_model = x_prompt.shape
    bs, ts, _ = x_sample.shape
    depth = w_in.shape[0]
    past_len = cache_mla_ckv.shape[2]
    n_mem = mem_prompt.shape[1]
    d_b = lru_lambda.shape[1]
    alpha = (2.0 * depth) ** 0.25
    sp_p, sp_s = min(tp, A_CHUNK), min(ts, A_CHUNK)
    pw = _prep_weights(sorted({sp_p, sp_s}), w_in, gmlp_ln_g, gmlp_ln_b, gmlp_ws, gmlp_bs, lru_conv_w,
                       lru_conv_b, lru_w_r, lru_b_r, lru_w_i, lru_b_i, lru_lambda, mla_q_norm, mla_w_uq,
                       mla_kv_norm, mla_w_ukv, mem_w_k, mem_w_v, w_br, w_out, ln_g, ln_b)
    rope_p = _rope_tables(jnp.arange(tp))
    rope_s = _rope_tables(past_len + jnp.arange(ts))
    tm_p = min(tp, ROW_TILE)
    assert tp % (N_BUF * tm_p) == 0 and tm_p % CHUNK == 0 and ts % SUBLANES == 0
    zero_conv = jnp.zeros((bp, CONV_W - 1, d_b), F32)
    zero_h = jnp.zeros((bp, d_b), F32)

    xp, xs = x_prompt, x_sample
    acc = [[] for _ in range(11)]
    for l in range(depth):
        mk, mv = _mem_call(mem_prompt.reshape(bp * n_mem, d_model), pw["wmem"], l, tr=n_mem)
        mk = mk.reshape(bp, n_mem, -1)
        mv = mv.reshape(bp, n_mem, -1)
        xp, _, conv_n, h_n, ckv_n, kr_n = _trunk_layer(
            xp, pw, l, sp_p, rope_p, zero_conv, zero_h, mk, mv, None, None, tm=tm_p, alpha=alpha)
        for k, val in zip(range(6), (ckv_n, kr_n, mk.reshape(bp, n_mem, H_M, DH_M),
                                     mv.reshape(bp, n_mem, H_M, DH_M), h_n, conv_n)):
            acc[k].append(val)
        xs, v_n, conv_n, h_n, ckv_n, kr_n = _trunk_layer(
            xs, pw, l, sp_s, rope_s, state_lru_conv[l], state_lru_h[l],
            cache_mem_k[l].reshape(bs, n_mem, -1), cache_mem_v[l].reshape(bs, n_mem, -1),
            cache_mla_ckv[l], cache_mla_krope[l], tm=ts, alpha=alpha)
        for k, val in zip(range(6, 11), (ckv_n, kr_n, h_n, conv_n, v_n)):
            acc[k].append(val)
    return (xp, xs) + tuple(jnp.stack(a) for a in acc)
```

```python
import functools
import math

import jax
import jax.numpy as jnp
from jax import lax
from jax.experimental import pallas as pl
from jax.experimental.pallas import tpu as pltpu

CHUNK = 64
G_A = 4
A_CHUNK = 128
H_B = 8
CONV_W = 4
LRU_C = 8.0
H_C = 8
D_NOPE = 64
D_ROPE = 32
D_V = 64
ROPE_BASE = 10000.0
H_M = 4
DH_M = 64
N_BRANCH = 4
EPS = 1e-6

LANES = 128
SUBLANES = 8
VMEM_LIMIT = 56 * 1024 * 1024
ROW_TILE = 256
MXU_TILE = 256

F32 = jnp.float32
BF16 = jnp.bfloat16


def _dot(a, b):
    return jnp.dot(a, b, preferred_element_type=F32)


def _dot_nt(a, b):
    return lax.dot_general(a, b, (((1,), (1,)), ((), ())), preferred_element_type=F32)


def _dot_tn(a, b):
    return lax.dot_general(a, b, (((0,), (0,)), ((), ())), preferred_element_type=F32)


def _sigmoid(x):
    return 1.0 / (1.0 + jnp.exp(-x))


def _silu(x):
    return x * _sigmoid(x)


def _gelu(x):
    return jax.nn.gelu(x)


def _expm1_nonpos(x):
    u = jnp.exp(x)
    near = (u - 1.0) * x / jnp.log(jnp.where(u == 1.0, 2.0, jnp.maximum(u, 0.5)))
    return jnp.where(u == 1.0, x, jnp.where(u > 0.5, near, u - 1.0))


def _layer_norm(x, g, b):
    mu = jnp.mean(x, -1, keepdims=True)
    var = jnp.mean(jnp.square(x - mu), -1, keepdims=True)
    return (x - mu) * lax.rsqrt(var + EPS) * g + b


def _rms_norm(x, g):
    return x * lax.rsqrt(jnp.mean(jnp.square(x), -1, keepdims=True) + EPS) * g


def _const_spec(shape):
    nd = len(shape)
    return pl.BlockSpec(shape, lambda *_: (0,) * nd, pipeline_mode=pl.Buffered(1))


def _layer_spec(arr, l):
    nd = arr.ndim - 1
    return pl.BlockSpec((None,) + arr.shape[1:], lambda *_: (l,) + (0,) * nd, pipeline_mode=pl.Buffered(1))


def _params(n_axes):
    return pltpu.CompilerParams(dimension_semantics=("arbitrary",) * n_axes,
                                vmem_limit_bytes=VMEM_LIMIT)


V_ROWS = 80


def _keys_values(ckv, kr, wkc_ref, wkr_ref, wvt_ref):
    c = ckv.astype(BF16)
    kcat = (_dot(c, wkc_ref[...]) + _dot(kr.astype(BF16), wkr_ref[...])).astype(BF16)
    v_t = _dot_nt(wvt_ref[...], c)
    row = lax.broadcasted_iota(jnp.int32, (v_t.shape[0], 1), 0)
    return kcat, jnp.where(row % V_ROWS == D_V, 1.0, v_t).astype(BF16)


def _branch_kernel(x_ref, w1_ref, wcgt_ref, glng_ref, glnb_ref, wsp_ref, bsp_ref,
                   convw_ref, convb_ref, wgate_ref, br_ref, bi_ref, lam_ref, conv0_ref, h0_ref, qnorm_ref, wuqt_ref,
                   kvnorm_ref, cosq_ref, sinq_ref, ck_ref, sk_ref, mk_ref, mv_ref, wbr_ref,
                   *rest, first_chunk, tm, sp_len, d_a, d_b, d_c, d_m, q_lora, kv_lora, cols):
    if first_chunk:
        (wkc_ref, wkr_ref, wvt_ref, partial_ref, g2_ref, cgst_ref, qt_ref, ckv_ref, kr_ref, conv_ref, h_ref,
         kcat_ref, vt_ref, merged_scr) = rest
    else:
        partial_ref, g2_ref, cgst_ref, qt_ref, ckv_ref, kr_ref, conv_ref, h_ref, v_ref, merged_scr = rest
    t = pl.program_id(1)
    xb = x_ref[0].astype(BF16)

    def zin(name):
        lo, hi = cols[name]
        return _dot(xb, w1_ref[:, lo:hi])

    z_u, z_v, z_ag, z_g0 = zin("a_u"), zin("a_v"), zin("a_g"), zin("g0")
    u = _gelu(z_u)
    v = _layer_norm(_gelu(z_v), glng_ref[...], glnb_ref[...])
    if not first_chunk:
        v_ref[0] = v
    vb = v.astype(BF16)
    n_groups = d_a // LANES
    row_blocks = []
    for c in range(tm // sp_len):
        col_blocks = [_dot(wsp_ref[g], vb[c * sp_len:(c + 1) * sp_len, g * LANES:(g + 1) * LANES])
                      for g in range(n_groups)]
        row_blocks.append(jnp.concatenate(col_blocks, axis=1) + bsp_ref[...])
    s = row_blocks[0] if len(row_blocks) == 1 else jnp.concatenate(row_blocks, axis=0)
    oa = (u * s) * _silu(z_ag)
    merged_scr[...] = _sigmoid(z_g0) * _dot(oa.astype(BF16), wbr_ref[0:d_a, :])

    @pl.when(t == 0)
    def _():
        conv_ref[...] = conv0_ref[...]
        h_ref[...] = h0_ref[...]

    bx = zin("b_x")
    hist = conv_ref[0]
    row8 = lax.broadcasted_iota(jnp.int32, (SUBLANES, 1), 0)
    xc = convb_ref[...]
    for k in range(CONV_W):
        shift = CONV_W - 1 - k
        if shift == 0:
            sh = bx
        else:
            sh = pltpu.roll(bx, shift, axis=0)
            head = jnp.where(row8 < shift, pltpu.roll(hist, shift, axis=0), sh[0:SUBLANES])
            sh = head if tm == SUBLANES else jnp.concatenate([head, sh[SUBLANES:]], axis=0)
        xc = xc + sh * convw_ref[k:k + 1, :]
    conv_ref[0] = bx[tm - SUBLANES:tm]

    xcb = xc.astype(BF16)
    n_blk = d_b // MXU_TILE
    ri = [_dot(xcb[:, k * MXU_TILE:(k + 1) * MXU_TILE], wgate_ref[k]) for k in range(n_blk)]
    z_bg, z_g1, z_g2 = zin("b_g"), zin("g1"), zin("g2")
    z_mq, z_g3, z_c = zin("m_q"), zin("g3"), zin("c_all")
    cg_t = _dot_nt(wcgt_ref[...], xb)
    r = _sigmoid(jnp.concatenate([p[:, 0:MXU_TILE] for p in ri], axis=1) + br_ref[...])
    i_gate = _sigmoid(jnp.concatenate([p[:, MXU_TILE:2 * MXU_TILE] for p in ri], axis=1) + bi_ref[...])
    neg_lam = -lam_ref[...]
    softplus = jnp.maximum(neg_lam, 0.0) + jnp.log1p(jnp.exp(-jnp.abs(neg_lam)))
    log_a = (-LRU_C * r) * softplus
    a = jnp.exp(log_a)
    bval = jnp.sqrt(-_expm1_nonpos(2.0 * log_a)) * (i_gate * xc)
    rows = lax.broadcasted_iota(jnp.int32, (tm, 1), 0)
    d = 1
    while d < tm:
        keep = rows >= d
        a_sh = pltpu.roll(a, d, axis=0)
        b_sh = pltpu.roll(bval, d, axis=0)
        bval = jnp.where(keep, a * b_sh + bval, bval)
        a = jnp.where(keep, a * a_sh, a)
        d *= 2
    h = a * h_ref[0] + bval
    h_ref[0] = h[tm - 1:tm]
    ob = h * _silu(z_bg)
    merged_scr[...] += _sigmoid(z_g1) * _dot(ob.astype(BF16), wbr_ref[d_a:d_a + d_b, :])
    g2_ref[0] = _sigmoid(z_g2).astype(BF16)

    mq = z_mq.astype(BF16)
    mkb = mk_ref[0].astype(BF16)
    mvb = mv_ref[0].astype(BF16)
    lane = lax.broadcasted_iota(jnp.int32, (1, LANES), 1)
    slabs = []
    for p in range(d_m // LANES):
        mq_p = mq[:, p * LANES:(p + 1) * LANES]
        mk_p = mkb[:, p * LANES:(p + 1) * LANES]
        mv_p = mvb[:, p * LANES:(p + 1) * LANES]
        acc = None
        for half in range(LANES // DH_M):
            sel = (lane >= half * DH_M) & (lane < (half + 1) * DH_M)
            sc = _dot_nt(mq_p, jnp.where(sel, mk_p, jnp.zeros_like(mk_p))) * (DH_M ** -0.5)
            e = jnp.exp(sc - jnp.max(sc, -1, keepdims=True))
            prob = (e / jnp.sum(e, -1, keepdims=True)).astype(BF16)
            o = _dot(prob, jnp.where(sel, mv_p, jnp.zeros_like(mv_p)))
            acc = o if acc is None else acc + o
        slabs.append(acc)
    om = jnp.concatenate(slabs, axis=1)
    m_lo = d_a + d_b + d_c
    partial_ref[0] = (merged_scr[...]
                      + _sigmoid(z_g3) * _dot(om.astype(BF16), wbr_ref[m_lo:m_lo + d_m, :])).astype(BF16)
    cgst_ref[0] = _silu(cg_t)

    ckv = _rms_norm(z_c[:, q_lora:q_lora + kv_lora], kvnorm_ref[...])
    half = D_ROPE // 2
    z_kr = z_c[:, q_lora + kv_lora:q_lora + kv_lora + LANES]
    kr_swapped = jnp.where(lane < half, pltpu.roll(z_kr, LANES - half, axis=1), pltpu.roll(z_kr, half, axis=1))
    kr = z_kr[:, 0:D_ROPE] * ck_ref[...] + kr_swapped[:, 0:D_ROPE] * sk_ref[...]
    ckv_ref[0] = ckv
    kr_ref[0] = kr
    if first_chunk:
        kcat_ref[0], vt_ref[0, 0] = _keys_values(ckv, kr, wkc_ref, wkr_ref, wvt_ref)
    cqn = _rms_norm(z_c[:, 0:q_lora], qnorm_ref[...]).astype(BF16)
    q_t = _dot_nt(wuqt_ref[...], cqn)
    cos_t, sin_t = cosq_ref[...], sinq_ref[...]
    for hd in range(H_C):
        lo = hd * LANES
        x1 = q_t[lo + D_NOPE:lo + D_NOPE + half]
        x2 = q_t[lo + D_NOPE + half:lo + D_NOPE + D_ROPE]
        qt_ref[0, 0, lo:lo + D_NOPE, :] = q_t[lo:lo + D_NOPE].astype(BF16)
        qt_ref[0, 0, lo + D_NOPE:lo + D_NOPE + half, :] = (x1 * cos_t - x2 * sin_t).astype(BF16)
        qt_ref[0, 0, lo + D_NOPE + half:lo + D_NOPE + D_ROPE, :] = (x1 * sin_t + x2 * cos_t).astype(BF16)
        qt_ref[0, 0, lo + D_NOPE + D_ROPE:lo + LANES, :] = jnp.zeros((LANES - D_NOPE - D_ROPE, tm), BF16)


def _branch_call(x, pw, l, sp, conv0_pad, h0, mk, mv, rope, *, tm, tq, first_chunk):
    bsz, seq, d_model = x.shape
    d_a, d_b, d_c, d_m = pw["dims"]
    q_lora, kv_lora = pw["q_lora"], pw["kv_lora"]
    n_mem = mk.shape[1]
    cos_t, sin_t, ck_t, sk_t = rope
    grid = (bsz, seq // tm)

    def row_spec(width):
        return pl.BlockSpec((1, tm, width), lambda b, t: (b, t, 0))

    def col_spec(height):
        return pl.BlockSpec((1, height, tm), lambda b, t: (b, 0, t))

    def tab_spec(width):
        return pl.BlockSpec((tm, width), lambda b, t: (t, 0))

    def batch_spec(rows, width):
        return pl.BlockSpec((1, rows, width), lambda b, t: (b, 0, 0))

    names = ["w1", "wcgt", "gln_g", "gln_b", "wsp%d" % sp, "bsp%d" % sp, "conv_w", "conv_b",
             "wgate", "b_r", "b_i", "lam"]
    names2 = ["q_norm", "wuqt", "kv_norm"]
    in_specs = ([row_spec(d_model)] + [_layer_spec(pw[n], l) for n in names]
                + [batch_spec(SUBLANES, d_b), batch_spec(1, d_b)] + [_layer_spec(pw[n], l) for n in names2]
                + [pl.BlockSpec((D_ROPE // 2, tm), lambda b, t: (0, t)), pl.BlockSpec((D_ROPE // 2, tm), lambda b, t: (0, t)),
                   tab_spec(D_ROPE), tab_spec(D_ROPE), batch_spec(n_mem, d_m), batch_spec(n_mem, d_m),
                   _layer_spec(pw["wbr"], l)])
    operands = ([x] + [pw[n] for n in names] + [conv0_pad, h0] + [pw[n] for n in names2]
                + [cos_t, sin_t, ck_t, sk_t, mk, mv, pw["wbr"]])
    out_shape = [
        jax.ShapeDtypeStruct((bsz, seq, d_model), BF16),
        jax.ShapeDtypeStruct((bsz, seq, d_model), BF16),
        jax.ShapeDtypeStruct((bsz, d_c, seq), F32),
        jax.ShapeDtypeStruct((bsz, seq // tq, H_C * LANES, tq), BF16),
        jax.ShapeDtypeStruct((bsz, seq, kv_lora), F32),
        jax.ShapeDtypeStruct((bsz, seq, D_ROPE), F32),
        jax.ShapeDtypeStruct((bsz, SUBLANES, d_b), F32),
        jax.ShapeDtypeStruct((bsz, 1, d_b), F32),
    ]
    out_specs = [
        row_spec(d_model), row_spec(d_model), col_spec(d_c),
        pl.BlockSpec((1, 1, H_C * LANES, tm), lambda b, t: (b, t // (tq // tm), 0, t % (tq // tm))),
        row_spec(kv_lora), row_spec(D_ROPE), batch_spec(SUBLANES, d_b), batch_spec(1, d_b),
    ]
    if first_chunk:
        for name in ("wkc", "wkr", "wvt"):
            in_specs.append(_layer_spec(pw[name], l))
            operands.append(pw[name])
        out_shape += [jax.ShapeDtypeStruct((bsz, seq, H_C * LANES), BF16),
                      jax.ShapeDtypeStruct((bsz, seq // tm, H_C * V_ROWS, tm), BF16)]
        out_specs += [row_spec(H_C * LANES),
                      pl.BlockSpec((1, 1, H_C * V_ROWS, tm), lambda b, t: (b, t, 0, 0))]
    else:
        out_shape.append(jax.ShapeDtypeStruct((bsz, seq, d_a), F32))
        out_specs.append(row_spec(d_a))
    kern = functools.partial(
        _branch_kernel, first_chunk=first_chunk, tm=tm, sp_len=sp, d_a=d_a, d_b=d_b, d_c=d_c, d_m=d_m,
        q_lora=q_lora, kv_lora=kv_lora, cols=pw["cols"])
    return pl.pallas_call(
        kern, grid=grid, in_specs=in_specs, out_specs=out_specs, out_shape=out_shape,
        scratch_shapes=[pltpu.VMEM((tm, d_model), F32)],
        name="branch", compiler_params=_params(2),
    )(*operands)


def _kv_kernel(ckv_ref, kr_ref, wkc_ref, wkr_ref, wvt_ref, kcat_ref, vt_ref):
    kcat_ref[0], vt_ref[0, 0] = _keys_values(ckv_ref[0], kr_ref[0], wkc_ref, wkr_ref, wvt_ref)


def _kv_call(ckv_all, kr_all, pw, l, *, tk):
    bsz, t_k, kv_lora = ckv_all.shape
    n_k = H_C * LANES
    n_v = H_C * V_ROWS
    nkt = t_k // tk
    return pl.pallas_call(
        _kv_kernel, grid=(bsz, nkt),
        in_specs=[pl.BlockSpec((1, tk, kv_lora), lambda b, j: (b, j, 0)),
                  pl.BlockSpec((1, tk, D_ROPE), lambda b, j: (b, j, 0)),
                  _layer_spec(pw["wkc"], l), _layer_spec(pw["wkr"], l), _layer_spec(pw["wvt"], l)],
        out_specs=[pl.BlockSpec((1, tk, n_k), lambda b, j: (b, j, 0)),
                   pl.BlockSpec((1, 1, n_v, tk), lambda b, j: (b, j, 0, 0))],
        out_shape=[jax.ShapeDtypeStruct((bsz, t_k, n_k), BF16),
                   jax.ShapeDtypeStruct((bsz, nkt, n_v, tk), BF16)],
        name="kv", compiler_params=_params(2),
    )(ckv_all, kr_all, pw["wkc"], pw["wkr"], pw["wvt"])


HEADS_PER_STEP = 2
_EXP2_SCALE = (D_NOPE + D_ROPE) ** -0.5 * math.log2(math.e)


def _scores(k_tile, q_tile, out_refs=None):
    res = [_dot(k_tile[:, h * LANES:(h + 1) * LANES], q_tile[h * LANES:(h + 1) * LANES, :])
           for h in range(HEADS_PER_STEP)]
    if out_refs is None:
        return res
    for h in range(HEADS_PER_STEP):
        out_refs[h] = res[h]
    return None


def _softmax_tile(s_t, m_old):
    m_new = jnp.maximum(m_old, jnp.max(s_t, axis=0, keepdims=True))
    return jnp.exp2(s_t - m_new).astype(BF16), m_new, jnp.exp2(m_old - m_new)


def _attn_output(acc, cgs_t):
    return (acc[0:D_V] / acc[D_V:D_V + 1] * cgs_t).astype(BF16)


N_BUF = 4


def _attn_causal_kernel(qt_ref, k_ref, vt_ref, cgst_ref, bias_ref, o_ref, *scratch, tq, tk):
    s_buf, e_buf, acc_scr = scratch[0:N_BUF], scratch[N_BUF:2 * N_BUF], scratch[2 * N_BUF]
    i = pl.program_id(2)
    nh = HEADS_PER_STEP
    q_cur = qt_ref.at[0, i]
    q_nxt = qt_ref.at[0, jnp.minimum(i + 1, pl.num_programs(2) - 1)]

    def k_tile(j):
        return k_ref[0, pl.ds(pl.multiple_of(j * tk, tk), tk), :]

    def pv(j, e_in, c0):
        vt = vt_ref[0, j]
        return [_dot(vt[h * V_ROWS:(h + 1) * V_ROWS, :], e_in[h, :, c0:tq]) for h in range(nh)]

    def acc_update(pvs, alphas, c0):
        for h in range(nh):
            acc_scr[h, :, c0:tq] = acc_scr[h, :, c0:tq] * alphas[h][:, c0:tq] + pvs[h]

    def softmax(slot, carry, c0, add_bias):
        new = []
        for h in range(nh):
            m_old, a_prev, _ = carry[h]
            s_t = s_buf[slot][h, :, c0:tq]
            if add_bias:
                lead = s_t[:, 0:tk] + bias_ref[...]
                s_t = lead if c0 + tk == tq else jnp.concatenate([lead, s_t[:, tk:]], axis=1)
            e, m_new, alpha = _softmax_tile(s_t, m_old[:, c0:tq])
            e_buf[slot][h, :, c0:tq] = e
            if c0:
                m_new = jnp.concatenate([m_old[:, 0:c0], m_new], axis=1)
                alpha = jnp.concatenate([jnp.ones((1, c0), F32), alpha], axis=1)
            new.append((m_new, alpha, a_prev))
        return tuple(new)

    def stage(n, slot, carry):
        pvs = pv(jnp.maximum(n - 2, 0), e_buf[(slot + 2) % N_BUF], 0)
        _scores(k_tile(n + 3), q_cur, s_buf[(slot + 3) % N_BUF])
        new = softmax(slot, carry, 0, False)
        acc_update(pvs, [c[2] for c in carry], 0)
        return new

    for slot in (N_BUF - 2, N_BUF - 1):
        e_buf[slot][...] = jnp.zeros(e_buf[slot].shape, BF16)
    acc_scr[...] = jnp.zeros(acc_scr.shape, F32)

    @pl.when(i == 0)
    def _():
        for slot in range(N_BUF - 1):
            _scores(k_tile(slot), q_cur, s_buf[slot])

    ones = jnp.ones((1, tq), F32)
    carry = tuple((jnp.full((1, tq), -jnp.inf, F32), ones, ones) for _ in range(nh))

    def body(t, carry):
        for slot in range(N_BUF):
            carry = stage(N_BUF * t + slot, slot, carry)
        return carry

    carry = lax.fori_loop(0, i, body, carry)

    j0 = N_BUF * i
    last = N_BUF - 1
    for d in range(N_BUF):
        c_prev = max(d - 2, 0) * tk
        pvs = pv(jnp.maximum(j0 + d - 2, 0), e_buf[(d + 2) % N_BUF], c_prev)
        if d == 0:
            k_last = k_tile(j0 + last)
            for h in range(nh):
                s_buf[last][h, :, tq - tk:tq] = _dot(
                    k_last[:, h * LANES:(h + 1) * LANES],
                    q_cur[h * LANES:(h + 1) * LANES, tq - tk:tq]) + bias_ref[...]
        else:
            _scores(k_tile(d - 1), q_nxt, s_buf[d - 1])
        alphas = [c[2] for c in carry]
        carry = softmax(d, carry, d * tk, d < last)
        acc_update(pvs, alphas, c_prev)
    acc_update(pv(j0 + last - 1, e_buf[last - 1], (last - 1) * tk), [c[2] for c in carry], (last - 1) * tk)
    acc_update(pv(j0 + last, e_buf[last], last * tk), [c[1] for c in carry], last * tk)
    for h in range(nh):
        o_ref[0, h * D_V:(h + 1) * D_V, :] = _attn_output(acc_scr[h], cgst_ref[0, h * D_V:(h + 1) * D_V, :])


def _attn_full_kernel(qt_ref, k_ref, vt_ref, cgst_ref, o_ref):
    tq = qt_ref.shape[-1]
    scores = _scores(k_ref[0], qt_ref.at[0, 0])
    vt = vt_ref[0, 0]
    for h in range(HEADS_PER_STEP):
        e, _, _ = _softmax_tile(scores[h], jnp.full((1, tq), -jnp.inf, F32))
        acc = _dot(vt[h * V_ROWS:(h + 1) * V_ROWS, :], e)
        o_ref[0, h * D_V:(h + 1) * D_V, :] = _attn_output(acc, cgst_ref[0, h * D_V:(h + 1) * D_V, :])


def _attn_call(qt, kcat, vt, cgst, *, causal):
    bsz, n_qt, _, tq = qt.shape
    seq = n_qt * tq
    t_k = kcat.shape[1]
    nkt, _, tk = vt.shape[1:]
    nh = HEADS_PER_STEP
    in_specs = [pl.BlockSpec((1, n_qt, nh * LANES, tq), lambda b, p, i: (b, 0, p, 0)),
                pl.BlockSpec((1, t_k, nh * LANES), lambda b, p, i: (b, 0, p)),
                pl.BlockSpec((1, nkt, nh * V_ROWS, tk), lambda b, p, i: (b, 0, p, 0)),
                pl.BlockSpec((1, nh * D_V, tq), lambda b, p, i: (b, p, i))]
    operands = [qt, kcat, vt, cgst]
    if causal:
        assert tq == N_BUF * tk and seq == t_k and tk % CHUNK == 0
        chunk = jnp.arange(tk) // CHUNK
        operands.append(jnp.where(chunk[:, None] <= chunk[None, :], 0.0, -1e30).astype(F32))
        in_specs.append(_const_spec((tk, tk)))
        kern = functools.partial(_attn_causal_kernel, tq=tq, tk=tk)
        scratch = ([pltpu.VMEM((nh, tk, tq), F32)] * N_BUF + [pltpu.VMEM((nh, tk, tq), BF16)] * N_BUF
                   + [pltpu.VMEM((nh, V_ROWS, tq), F32)])
    else:
        assert nkt == 1 and seq == tq
        kern = _attn_full_kernel
        scratch = []
    return pl.pallas_call(
        kern, grid=(bsz, H_C // nh, seq // tq),
        in_specs=in_specs,
        out_specs=pl.BlockSpec((1, nh * D_V, tq), lambda b, p, i: (b, p, i)),
        out_shape=jax.ShapeDtypeStruct((bsz, H_C * D_V, seq), BF16),
        scratch_shapes=scratch,
        name="attn", compiler_params=_params(3),
    )(*operands)


def _merge_kernel(x_ref, partial_ref, g2_ref, oct_ref, wbr_ref, wout_ref, lng_ref, lnb_ref, o_ref,
                  *, c_lo, d_c, alpha):
    yc = _dot_tn(oct_ref[0], wbr_ref[c_lo:c_lo + d_c, :])
    merged = partial_ref[0].astype(F32) + g2_ref[0].astype(F32) * yc
    y = _dot(merged.astype(BF16), wout_ref[...])
    o_ref[0] = _layer_norm(alpha * x_ref[0] + y, lng_ref[...], lnb_ref[...])


def _merge_call(x, partial, g2, oct, pw, l, *, tr, alpha):
    bsz, seq, d_model = x.shape
    d_a, d_b, d_c, _ = pw["dims"]
    kern = functools.partial(_merge_kernel, c_lo=d_a + d_b, d_c=d_c, alpha=alpha)

    def row_spec(width):
        return pl.BlockSpec((1, tr, width), lambda b, i: (b, i, 0))

    return pl.pallas_call(
        kern, grid=(bsz, seq // tr),
        in_specs=[row_spec(d_model), row_spec(d_model), row_spec(d_model),
                  pl.BlockSpec((1, d_c, tr), lambda b, i: (b, 0, i)),
                  _layer_spec(pw["wbr"], l), _layer_spec(pw["wout"], l),
                  _layer_spec(pw["ln_g"], l), _layer_spec(pw["ln_b"], l)],
        out_specs=row_spec(d_model),
        out_shape=jax.ShapeDtypeStruct((bsz, seq, d_model), F32),
        name="merge", compiler_params=_params(2),
    )(x, partial, g2, oct, pw["wbr"], pw["wout"], pw["ln_g"], pw["ln_b"])


def _mem_kernel(m_ref, w_ref, k_ref, v_ref, *, d_m):
    kv = _dot(m_ref[...].astype(BF16), w_ref[...])
    k_ref[...] = kv[:, 0:d_m]
    v_ref[...] = kv[:, d_m:2 * d_m]


def _mem_call(mem2d, wmem, l, *, tr):
    rows, d_model = mem2d.shape
    d_m = wmem.shape[2] // 2
    return pl.pallas_call(
        functools.partial(_mem_kernel, d_m=d_m), grid=(rows // tr,),
        in_specs=[pl.BlockSpec((tr, d_model), lambda i: (i, 0)), _layer_spec(wmem, l)],
        out_specs=[pl.BlockSpec((tr, d_m), lambda i: (i, 0)), pl.BlockSpec((tr, d_m), lambda i: (i, 0))],
        out_shape=[jax.ShapeDtypeStruct((rows, d_m), F32), jax.ShapeDtypeStruct((rows, d_m), F32)],
        name="mem", compiler_params=_params(1),
    )(mem2d, wmem)


def _block_diag(w):
    h, n, _ = w.shape
    eye = jnp.eye(h, dtype=w.dtype)
    return (eye[:, None, :, None] * w[:, :, None, :]).reshape(h * n, h * n)


def _gate_blocks(w_r, w_i):
    per = MXU_TILE // w_r.shape[1]
    blocks = []
    for k in range(w_r.shape[0] // per):
        blocks.append(jnp.concatenate([_block_diag(w_r[k * per:(k + 1) * per]),
                                       _block_diag(w_i[k * per:(k + 1) * per])], axis=1))
    return jnp.stack(blocks)


def _w_in_kernel(w_ref, o_ref, *, n_head, o_tail):
    o_ref[0, :, 0:n_head] = w_ref[0, :, 0:n_head].astype(BF16)
    o_ref[0, :, n_head:] = w_ref[0, :, o_tail:].astype(BF16)


def _w_in_call(w_in, n_head, o_tail, *, tr):
    depth, d_model, d_in = w_in.shape
    n_out = n_head + d_in - o_tail
    return pl.pallas_call(
        functools.partial(_w_in_kernel, n_head=n_head, o_tail=o_tail), grid=(depth, d_model // tr),
        in_specs=[pl.BlockSpec((1, tr, d_in), lambda l, i: (l, i, 0))],
        out_specs=pl.BlockSpec((1, tr, n_out), lambda l, i: (l, i, 0)),
        out_shape=jax.ShapeDtypeStruct((depth, d_model, n_out), BF16),
        name="w_in", compiler_params=_params(2),
    )(w_in)


def _prep_weights(sp_lens, w_in, gmlp_ln_g, gmlp_ln_b, gmlp_ws, gmlp_bs, lru_conv_w,
                  lru_conv_b, lru_w_r, lru_b_r, lru_w_i, lru_b_i, lru_lambda, mla_q_norm, mla_w_uq,
                  mla_kv_norm, mla_w_ukv, mem_w_k, mem_w_v, w_br, w_out, ln_g, ln_b):
    depth, d_model, _ = w_in.shape
    d_a = gmlp_ln_g.shape[1]
    d_b = lru_lambda.shape[1]
    q_lora = mla_q_norm.shape[1]
    kv_lora = mla_kv_norm.shape[1]
    d_c = H_C * D_V
    d_m = H_M * DH_M
    o_cq = 3 * d_a + 2 * d_b
    o_kr = o_cq + q_lora + kv_lora
    o_cg = o_kr + D_ROPE
    o_mq = o_cg + d_c
    o_g = o_mq + d_m
    n_head = o_kr + LANES
    assert o_kr % LANES == 0 and (o_g - o_mq) % LANES == 0 and D_ROPE <= LANES and n_head <= o_mq
    cols = {"a_u": (0, d_a), "a_v": (d_a, 2 * d_a), "a_g": (2 * d_a, 3 * d_a),
            "b_x": (3 * d_a, 3 * d_a + d_b), "b_g": (3 * d_a + d_b, o_cq), "c_all": (o_cq, n_head),
            "m_q": (n_head, n_head + d_m)}
    for k in range(N_BRANCH):
        cols[f"g{k}"] = (n_head + d_m + k * d_model, n_head + d_m + (k + 1) * d_model)
    w1 = _w_in_call(w_in, n_head, o_mq, tr=ROW_TILE)
    wcgt = jnp.swapaxes(w_in[:, :, o_cg:o_mq], 1, 2).astype(BF16)

    wq = mla_w_uq.reshape(depth, q_lora, H_C, D_NOPE + D_ROPE) * _EXP2_SCALE
    wq = jnp.concatenate([wq, jnp.zeros((depth, q_lora, H_C, LANES - D_NOPE - D_ROPE), wq.dtype)], axis=3)
    wuqt = jnp.swapaxes(wq.reshape(depth, q_lora, H_C * LANES), 1, 2).astype(BF16)

    wkv = mla_w_ukv.reshape(depth, kv_lora, H_C, D_NOPE + D_V)
    wk = wkv[..., :D_NOPE]
    wv = wkv[..., D_NOPE:]
    wkc = jnp.concatenate([wk, jnp.zeros((depth, kv_lora, H_C, LANES - D_NOPE), wk.dtype)], axis=3)
    wkc = wkc.reshape(depth, kv_lora, H_C * LANES).astype(BF16)
    lane = jnp.arange(H_C * LANES) % LANES
    wkr = ((lane[None, :] - D_NOPE) == jnp.arange(D_ROPE)[:, None]).astype(BF16)
    wkr = jnp.broadcast_to(wkr, (depth,) + wkr.shape)
    wv = jnp.concatenate([wv, jnp.zeros((depth, kv_lora, H_C, V_ROWS - D_V), wv.dtype)], axis=3)
    wvt = jnp.swapaxes(wv.reshape(depth, kv_lora, H_C * V_ROWS), 1, 2).astype(BF16)

    def row(p):
        return p[:, None, :]

    pw = dict(
        dims=(d_a, d_b, d_c, d_m), q_lora=q_lora, kv_lora=kv_lora, cols=cols,
        w1=w1, wcgt=wcgt,
        gln_g=row(gmlp_ln_g), gln_b=row(gmlp_ln_b),
        conv_w=lru_conv_w, conv_b=row(lru_conv_b),
        wgate=jnp.stack([_gate_blocks(lru_w_r[l], lru_w_i[l]) for l in range(depth)]).astype(BF16),
        b_r=row(lru_b_r), b_i=row(lru_b_i), lam=row(lru_lambda),
        q_norm=row(mla_q_norm), wuqt=wuqt, kv_norm=row(mla_kv_norm),
        wkc=wkc, wkr=wkr, wvt=wvt,
        wbr=w_br.astype(BF16), wout=w_out.astype(BF16), ln_g=row(ln_g), ln_b=row(ln_b),
        wmem=jnp.concatenate([mem_w_k, mem_w_v], axis=2).astype(BF16),
    )
    for sp in sp_lens:
        pw["wsp%d" % sp] = jnp.tril(gmlp_ws[:, :, :sp, :sp]).astype(BF16)
        pw["bsp%d" % sp] = jnp.repeat(jnp.swapaxes(gmlp_bs[:, :, :sp], 1, 2), d_a // G_A, axis=2)
    return pw


def _rope_tables(pos):
    half = D_ROPE // 2
    freq = ROPE_BASE ** (-jnp.arange(half, dtype=F32) / half)
    ang = pos.astype(F32)[:, None] * freq[None, :]
    cos, sin = jnp.cos(ang), jnp.sin(ang)
    return (jnp.transpose(cos), jnp.transpose(sin),
            jnp.concatenate([cos, cos], axis=1), jnp.concatenate([-sin, sin], axis=1))


def _pad_conv_state(conv):
    return jnp.pad(conv, ((0, 0), (SUBLANES - (CONV_W - 1), 0), (0, 0)))


def _trunk_layer(x, pw, l, sp, rope, conv0, h0, mk, mv, past_ckv, past_kr, *, tm, alpha):
    first_chunk = past_ckv is None
    tq = N_BUF * tm if first_chunk else tm
    outs = _branch_call(x, pw, l, sp, _pad_conv_state(conv0), h0[:, None, :], mk, mv, rope, tm=tm, tq=tq,
                        first_chunk=first_chunk)
    partial, g2, cgst, qt, ckv_new, kr_new, conv_pad, h_new = outs[:8]
    if first_chunk:
        kcat, vt = outs[8:]
        v_rows = None
    else:
        v_rows = outs[8]
        ckv_all = jnp.concatenate([past_ckv, ckv_new], axis=1)
        kr_all = jnp.concatenate([past_kr, kr_new], axis=1)
        kcat, vt = _kv_call(ckv_all, kr_all, pw, l, tk=ckv_all.shape[1])
    oct = _attn_call(qt, kcat, vt, cgst, causal=first_chunk)
    x_new = _merge_call(x, partial, g2, oct, pw, l, tr=tm, alpha=alpha)
    return x_new, v_rows, conv_pad[:, SUBLANES - (CONV_W - 1):], h_new[:, 0], ckv_new, kr_new


def kernel(x_prompt, x_sample, mem_prompt, cache_mla_ckv, cache_mla_krope, cache_mem_k, cache_mem_v,
           state_lru_h, state_lru_conv, w_in, gmlp_ln_g, gmlp_ln_b, gmlp_ws, gmlp_bs,
           lru_conv_w, lru_conv_b, lru_w_r, lru_b_r, lru_w_i, lru_b_i, lru_lambda,
           mla_q_norm, mla_w_uq, mla_kv_norm, mla_w_ukv, mem_w_k, mem_w_v, w_br, w_out, ln_g, ln_b):
    bp, tp, d_model = x_prompt.shape
    bs, ts, _ = x_sample.shape
    depth = w_in.shape[0]
    past_len = cache_mla_ckv.shape[2]
    n_mem = mem_prompt.shape[1]
    d_b = lru_lambda.shape[1]
    alpha = (2.0 * depth) ** 0.25
    sp_p, sp_s = min(tp, A_CHUNK), min(ts, A_CHUNK)
    pw = _prep_weights(sorted({sp_p, sp_s}), w_in, gmlp_ln_g, gmlp_ln_b, gmlp_ws, gmlp_bs, lru_conv_w,
                       lru_conv_b, lru_w_r, lru_b_r, lru_w_i, lru_b_i, lru_lambda, mla_q_norm, mla_w_uq,
                       mla_kv_norm, mla_w_ukv, mem_w_k, mem_w_v, w_br, w_out, ln_g, ln_b)
    rope_p = _rope_tables(jnp.arange(tp))
    rope_s = _rope_tables(past_len + jnp.arange(ts))
    tm_p = min(tp, ROW_TILE)
    assert tp % (N_BUF * tm_p) == 0 and tm_p % CHUNK == 0 and ts % SUBLANES == 0
    zero_conv = jnp.zeros((bp, CONV_W - 1, d_b), F32)
    zero_h = jnp.zeros((bp, d_b), F32)

    xp, xs = x_prompt, x_sample
    acc = [[] for _ in range(11)]
    for l in range(depth):
        mk, mv = _mem_call(mem_prompt.reshape(bp * n_mem, d_model), pw["wmem"], l, tr=n_mem)
        mk = mk.reshape(bp, n_mem, -1)
        mv = mv.reshape(bp, n_mem, -1)
        xp, _, conv_n, h_n, ckv_n, kr_n = _trunk_layer(
            xp, pw, l, sp_p, rope_p, zero_conv, zero_h, mk, mv, None, None, tm=tm_p, alpha=alpha)
        for k, val in zip(range(6), (ckv_n, kr_n, mk.reshape(bp, n_mem, H_M, DH_M),
                                     mv.reshape(bp, n_mem, H_M, DH_M), h_n, conv_n)):
            acc[k].append(val)
        xs, v_n, conv_n, h_n, ckv_n, kr_n = _trunk_layer(
            xs, pw, l, sp_s, rope_s, state_lru_conv[l], state_lru_h[l],
            cache_mem_k[l].reshape(bs, n_mem, -1), cache_mem_v[l].reshape(bs, n_mem, -1),
            cache_mla_ckv[l], cache_mla_krope[l], tm=ts, alpha=alpha)
        for k, val in zip(range(6, 11), (ckv_n, kr_n, h_n, conv_n, v_n)):
            acc[k].append(val)
    return (xp, xs) + tuple(jnp.stack(a) for a in acc)
```

```python
import functools
import math

import jax
import jax.numpy as jnp
from jax import lax
from jax.experimental import pallas as pl
from jax.experimental.pallas import tpu as pltpu

CHUNK = 64
G_A = 4
A_CHUNK = 128
H_B = 8
CONV_W = 4
LRU_C = 8.0
H_C = 8
D_NOPE = 64
D_ROPE = 32
D_V = 64
ROPE_BASE = 10000.0
H_M = 4
DH_M = 64
N_BRANCH = 4
EPS = 1e-6

LANES = 128
SUBLANES = 8
VMEM_LIMIT = 56 * 1024 * 1024
ROW_TILE = 256
MXU_TILE = 256

F32 = jnp.float32
BF16 = jnp.bfloat16


def _dot(a, b):
    return jnp.dot(a, b, preferred_element_type=F32)


def _dot_nt(a, b):
    return lax.dot_general(a, b, (((1,), (1,)), ((), ())), preferred_element_type=F32)


def _dot_tn(a, b):
    return lax.dot_general(a, b, (((0,), (0,)), ((), ())), preferred_element_type=F32)


def _sigmoid(x):
    return 1.0 / (1.0 + jnp.exp(-x))


def _silu(x):
    return x * _sigmoid(x)


def _gelu(x):
    return jax.nn.gelu(x)


def _expm1_nonpos(x):
    u = jnp.exp(x)
    near = (u - 1.0) * x / jnp.log(jnp.where(u == 1.0, 2.0, jnp.maximum(u, 0.5)))
    return jnp.where(u == 1.0, x, jnp.where(u > 0.5, near, u - 1.0))


def _layer_norm(x, g, b):
    mu = jnp.mean(x, -1, keepdims=True)
    var = jnp.mean(jnp.square(x - mu), -1, keepdims=True)
    return (x - mu) * lax.rsqrt(var + EPS) * g + b


def _rms_norm(x, g):
    return x * lax.rsqrt(jnp.mean(jnp.square(x), -1, keepdims=True) + EPS) * g


def _const_spec(shape):
    nd = len(shape)
    return pl.BlockSpec(shape, lambda *_: (0,) * nd, pipeline_mode=pl.Buffered(1))


def _layer_spec(arr, l):
    nd = arr.ndim - 1
    return pl.BlockSpec((None,) + arr.shape[1:], lambda *_: (l,) + (0,) * nd, pipeline_mode=pl.Buffered(1))


def _params(n_axes):
    return pltpu.CompilerParams(dimension_semantics=("arbitrary",) * n_axes,
                                vmem_limit_bytes=VMEM_LIMIT)


V_ROWS = 80


def _keys_values(ckv, kr, wkc_ref, wkr_ref, wvt_ref):
    c = ckv.astype(BF16)
    kcat = (_dot(c, wkc_ref[...]) + _dot(kr.astype(BF16), wkr_ref[...])).astype(BF16)
    v_t = _dot_nt(wvt_ref[...], c)
    row = lax.broadcasted_iota(jnp.int32, (v_t.shape[0], 1), 0)
    return kcat, jnp.where(row % V_ROWS == D_V, 1.0, v_t).astype(BF16)


def _branch_kernel(x_ref, w1_ref, wcgt_ref, glng_ref, glnb_ref, wsp_ref, bsp_ref,
                   convw_ref, convb_ref, wgate_ref, br_ref, bi_ref, lam_ref, conv0_ref, h0_ref, qnorm_ref, wuqt_ref,
                   kvnorm_ref, cosq_ref, sinq_ref, ck_ref, sk_ref, mk_ref, mv_ref, wbr_ref,
                   *rest, first_chunk, tm, sp_len, d_a, d_b, d_c, d_m, q_lora, kv_lora, cols):
    if first_chunk:
        (wkc_ref, wkr_ref, wvt_ref, partial_ref, g2_ref, cgst_ref, qt_ref, ckv_ref, kr_ref, conv_ref, h_ref,
         kcat_ref, vt_ref, merged_scr) = rest
    else:
        partial_ref, g2_ref, cgst_ref, qt_ref, ckv_ref, kr_ref, conv_ref, h_ref, v_ref, merged_scr = rest
    t = pl.program_id(1)
    xb = x_ref[0].astype(BF16)

    def zin(name):
        lo, hi = cols[name]
        return _dot(xb, w1_ref[:, lo:hi])

    z_u, z_v, z_ag, z_g0 = zin("a_u"), zin("a_v"), zin("a_g"), zin("g0")
    u = _gelu(z_u)
    v = _layer_norm(_gelu(z_v), glng_ref[...], glnb_ref[...])
    if not first_chunk:
        v_ref[0] = v
    vb = v.astype(BF16)
    n_groups = d_a // LANES
    row_blocks = []
    for c in range(tm // sp_len):
        col_blocks = [_dot(wsp_ref[g], vb[c * sp_len:(c + 1) * sp_len, g * LANES:(g + 1) * LANES])
                      for g in range(n_groups)]
        row_blocks.append(jnp.concatenate(col_blocks, axis=1) + bsp_ref[...])
    s = row_blocks[0] if len(row_blocks) == 1 else jnp.concatenate(row_blocks, axis=0)
    oa = (u * s) * _silu(z_ag)
    merged_scr[...] = _sigmoid(z_g0) * _dot(oa.astype(BF16), wbr_ref[0:d_a, :])

    @pl.when(t == 0)
    def _():
        conv_ref[...] = conv0_ref[...]
        h_ref[...] = h0_ref[...]

    bx = zin("b_x")
    hist = conv_ref[0]
    row8 = lax.broadcasted_iota(jnp.int32, (SUBLANES, 1), 0)
    xc = convb_ref[...]
    for k in range(CONV_W):
        shift = CONV_W - 1 - k
        if shift == 0:
            sh = bx
        else:
            sh = pltpu.roll(bx, shift, axis=0)
            head = jnp.where(row8 < shift, pltpu.roll(hist, shift, axis=0), sh[0:SUBLANES])
            sh = head if tm == SUBLANES else jnp.concatenate([head, sh[SUBLANES:]], axis=0)
        xc = xc + sh * convw_ref[k:k + 1, :]
    conv_ref[0] = bx[tm - SUBLANES:tm]

    xcb = xc.astype(BF16)
    n_blk = d_b // MXU_TILE
    ri = [_dot(xcb[:, k * MXU_TILE:(k + 1) * MXU_TILE], wgate_ref[k]) for k in range(n_blk)]
    z_bg, z_g1, z_g2 = zin("b_g"), zin("g1"), zin("g2")
    z_mq, z_g3, z_c = zin("m_q"), zin("g3"), zin("c_all")
    cg_t = _dot_nt(wcgt_ref[...], xb)
    r = _sigmoid(jnp.concatenate([p[:, 0:MXU_TILE] for p in ri], axis=1) + br_ref[...])
    i_gate = _sigmoid(jnp.concatenate([p[:, MXU_TILE:2 * MXU_TILE] for p in ri], axis=1) + bi_ref[...])
    neg_lam = -lam_ref[...]
    softplus = jnp.maximum(neg_lam, 0.0) + jnp.log1p(jnp.exp(-jnp.abs(neg_lam)))
    log_a = (-LRU_C * r) * softplus
    a = jnp.exp(log_a)
    bval = jnp.sqrt(-_expm1_nonpos(2.0 * log_a)) * (i_gate * xc)
    rows = lax.broadcasted_iota(jnp.int32, (tm, 1), 0)
    d = 1
    while d < tm:
        keep = rows >= d
        a_sh = pltpu.roll(a, d, axis=0)
        b_sh = pltpu.roll(bval, d, axis=0)
        bval = jnp.where(keep, a * b_sh + bval, bval)
        a = jnp.where(keep, a * a_sh, a)
        d *= 2
    h = a * h_ref[0] + bval
    h_ref[0] = h[tm - 1:tm]
    ob = h * _silu(z_bg)
    merged_scr[...] += _sigmoid(z_g1) * _dot(ob.astype(BF16), wbr_ref[d_a:d_a + d_b, :])
    g2_ref[0] = _sigmoid(z_g2).astype(BF16)

    mq = z_mq.astype(BF16)
    mkb = mk_ref[0].astype(BF16)
    mvb = mv_ref[0].astype(BF16)
    lane = lax.broadcasted_iota(jnp.int32, (1, LANES), 1)
    slabs = []
    for p in range(d_m // LANES):
        mq_p = mq[:, p * LANES:(p + 1) * LANES]
        mk_p = mkb[:, p * LANES:(p + 1) * LANES]
        mv_p = mvb[:, p * LANES:(p + 1) * LANES]
        acc = None
        for half in range(LANES // DH_M):
            sel = (lane >= half * DH_M) & (lane < (half + 1) * DH_M)
            sc = _dot_nt(mq_p, jnp.where(sel, mk_p, jnp.zeros_like(mk_p))) * (DH_M ** -0.5)
            e = jnp.exp(sc - jnp.max(sc, -1, keepdims=True))
            prob = (e / jnp.sum(e, -1, keepdims=True)).astype(BF16)
            o = _dot(prob, jnp.where(sel, mv_p, jnp.zeros_like(mv_p)))
            acc = o if acc is None else acc + o
        slabs.append(acc)
    om = jnp.concatenate(slabs, axis=1)
    m_lo = d_a + d_b + d_c
    partial_ref[0] = (merged_scr[...]
                      + _sigmoid(z_g3) * _dot(om.astype(BF16), wbr_ref[m_lo:m_lo + d_m, :])).astype(BF16)
    cgst_ref[0] = _silu(cg_t)

    ckv = _rms_norm(z_c[:, q_lora:q_lora + kv_lora], kvnorm_ref[...])
    half = D_ROPE // 2
    z_kr = z_c[:, q_lora + kv_lora:q_lora + kv_lora + LANES]
    kr_swapped = jnp.where(lane < half, pltpu.roll(z_kr, LANES - half, axis=1), pltpu.roll(z_kr, half, axis=1))
    kr = z_kr[:, 0:D_ROPE] * ck_ref[...] + kr_swapped[:, 0:D_ROPE] * sk_ref[...]
    ckv_ref[0] = ckv
    kr_ref[0] = kr
    if first_chunk:
        kcat_ref[0], vt_ref[0, 0] = _keys_values(ckv, kr, wkc_ref, wkr_ref, wvt_ref)
    cqn = _rms_norm(z_c[:, 0:q_lora], qnorm_ref[...]).astype(BF16)
    q_t = _dot_nt(wuqt_ref[...], cqn)
    cos_t, sin_t = cosq_ref[...], sinq_ref[...]
    for hd in range(H_C):
        lo = hd * LANES
        x1 = q_t[lo + D_NOPE:lo + D_NOPE + half]
        x2 = q_t[lo + D_NOPE + half:lo + D_NOPE + D_ROPE]
        qt_ref[0, 0, lo:lo + D_NOPE, :] = q_t[lo:lo + D_NOPE].astype(BF16)
        qt_ref[0, 0, lo + D_NOPE:lo + D_NOPE + half, :] = (x1 * cos_t - x2 * sin_t).astype(BF16)
        qt_ref[0, 0, lo + D_NOPE + half:lo + D_NOPE + D_ROPE, :] = (x1 * sin_t + x2 * cos_t).astype(BF16)
        qt_ref[0, 0, lo + D_NOPE + D_ROPE:lo + LANES, :] = jnp.zeros((LANES - D_NOPE - D_ROPE, tm), BF16)


def _branch_call(x, pw, l, sp, conv0_pad, h0, mk, mv, rope, *, tm, tq, first_chunk):
    bsz, seq, d_model = x.shape
    d_a, d_b, d_c, d_m = pw["dims"]
    q_lora, kv_lora = pw["q_lora"], pw["kv_lora"]
    n_mem = mk.shape[1]
    cos_t, sin_t, ck_t, sk_t = rope
    grid = (bsz, seq // tm)

    def row_spec(width):
        return pl.BlockSpec((1, tm, width), lambda b, t: (b, t, 0))

    def col_spec(height):
        return pl.BlockSpec((1, height, tm), lambda b, t: (b, 0, t))

    def tab_spec(width):
        return pl.BlockSpec((tm, width), lambda b, t: (t, 0))

    def batch_spec(rows, width):
        return pl.BlockSpec((1, rows, width), lambda b, t: (b, 0, 0))

    names = ["w1", "wcgt", "gln_g", "gln_b", "wsp%d" % sp, "bsp%d" % sp, "conv_w", "conv_b",
             "wgate", "b_r", "b_i", "lam"]
    names2 = ["q_norm", "wuqt", "kv_norm"]
    in_specs = ([row_spec(d_model)] + [_layer_spec(pw[n], l) for n in names]
                + [batch_spec(SUBLANES, d_b), batch_spec(1, d_b)] + [_layer_spec(pw[n], l) for n in names2]
                + [pl.BlockSpec((D_ROPE // 2, tm), lambda b, t: (0, t)), pl.BlockSpec((D_ROPE // 2, tm), lambda b, t: (0, t)),
                   tab_spec(D_ROPE), tab_spec(D_ROPE), batch_spec(n_mem, d_m), batch_spec(n_mem, d_m),
                   _layer_spec(pw["wbr"], l)])
    operands = ([x] + [pw[n] for n in names] + [conv0_pad, h0] + [pw[n] for n in names2]
                + [cos_t, sin_t, ck_t, sk_t, mk, mv, pw["wbr"]])
    out_shape = [
        jax.ShapeDtypeStruct((bsz, seq, d_model), BF16),
        jax.ShapeDtypeStruct((bsz, seq, d_model), BF16),
        jax.ShapeDtypeStruct((bsz, d_c, seq), F32),
        jax.ShapeDtypeStruct((bsz, seq // tq, H_C * LANES, tq), BF16),
        jax.ShapeDtypeStruct((bsz, seq, kv_lora), F32),
        jax.ShapeDtypeStruct((bsz, seq, D_ROPE), F32),
        jax.ShapeDtypeStruct((bsz, SUBLANES, d_b), F32),
        jax.ShapeDtypeStruct((bsz, 1, d_b), F32),
    ]
    out_specs = [
        row_spec(d_model), row_spec(d_model), col_spec(d_c),
        pl.BlockSpec((1, 1, H_C * LANES, tm), lambda b, t: (b, t // (tq // tm), 0, t % (tq // tm))),
        row_spec(kv_lora), row_spec(D_ROPE), batch_spec(SUBLANES, d_b), batch_spec(1, d_b),
    ]
    if first_chunk:
        for name in ("wkc", "wkr", "wvt"):
            in_specs.append(_layer_spec(pw[name], l))
            operands.append(pw[name])
        out_shape += [jax.ShapeDtypeStruct((bsz, seq, H_C * LANES), BF16),
                      jax.ShapeDtypeStruct((bsz, seq // tm, H_C * V_ROWS, tm), BF16)]
        out_specs += [row_spec(H_C * LANES),
                      pl.BlockSpec((1, 1, H_C * V_ROWS, tm), lambda b, t: (b, t, 0, 0))]
    else:
        out_shape.append(jax.ShapeDtypeStruct((bsz, seq, d_a), F32))
        out_specs.append(row_spec(d_a))
    kern = functools.partial(
        _branch_kernel, first_chunk=first_chunk, tm=tm, sp_len=sp, d_a=d_a, d_b=d_b, d_c=d_c, d_m=d_m,
        q_lora=q_lora, kv_lora=kv_lora, cols=pw["cols"])
    return pl.pallas_call(
        kern, grid=grid, in_specs=in_specs, out_specs=out_specs, out_shape=out_shape,
        scratch_shapes=[pltpu.VMEM((tm, d_model), F32)],
        name="branch", compiler_params=_params(2),
    )(*operands)


def _kv_kernel(ckv_ref, kr_ref, wkc_ref, wkr_ref, wvt_ref, kcat_ref, vt_ref):
    kcat_ref[0], vt_ref[0, 0] = _keys_values(ckv_ref[0], kr_ref[0], wkc_ref, wkr_ref, wvt_ref)


def _kv_call(ckv_all, kr_all, pw, l, *, tk):
    bsz, t_k, kv_lora = ckv_all.shape
    n_k = H_C * LANES
    n_v = H_C * V_ROWS
    nkt = t_k // tk
    return pl.pallas_call(
        _kv_kernel, grid=(bsz, nkt),
        in_specs=[pl.BlockSpec((1, tk, kv_lora), lambda b, j: (b, j, 0)),
                  pl.BlockSpec((1, tk, D_ROPE), lambda b, j: (b, j, 0)),
                  _layer_spec(pw["wkc"], l), _layer_spec(pw["wkr"], l), _layer_spec(pw["wvt"], l)],
        out_specs=[pl.BlockSpec((1, tk, n_k), lambda b, j: (b, j, 0)),
                   pl.BlockSpec((1, 1, n_v, tk), lambda b, j: (b, j, 0, 0))],
        out_shape=[jax.ShapeDtypeStruct((bsz, t_k, n_k), BF16),
                   jax.ShapeDtypeStruct((bsz, nkt, n_v, tk), BF16)],
        name="kv", compiler_params=_params(2),
    )(ckv_all, kr_all, pw["wkc"], pw["wkr"], pw["wvt"])


HEADS_PER_STEP = 2
_EXP2_SCALE = (D_NOPE + D_ROPE) ** -0.5 * math.log2(math.e)


def _scores(k_tile, q_tile, out_refs=None):
    res = [_dot(k_tile[:, h * LANES:(h + 1) * LANES], q_tile[h * LANES:(h + 1) * LANES, :])
           for h in range(k_tile.shape[1] // LANES)]
    if out_refs is None:
        return res
    for h, r in enumerate(res):
        out_refs[h] = r
    return None


def _softmax_tile(s_t, m_old):
    m_new = jnp.maximum(m_old, jnp.max(s_t, axis=0, keepdims=True))
    return jnp.exp2(s_t - m_new).astype(BF16), m_new, jnp.exp2(m_old - m_new)


def _attn_output(acc, cgs_t):
    return (acc[0:D_V] / acc[D_V:D_V + 1] * cgs_t).astype(BF16)


N_BUF = 4


def _attn_causal_kernel(qt_ref, k_ref, vt_ref, cgst_ref, bias_ref, o_ref, *scratch, tq, tk):
    s_buf, e_buf, acc_scr = scratch[0:N_BUF], scratch[N_BUF:2 * N_BUF], scratch[2 * N_BUF]
    i = pl.program_id(2)
    nh = HEADS_PER_STEP
    q_cur = qt_ref.at[0, i]
    q_nxt = qt_ref.at[0, jnp.minimum(i + 1, pl.num_programs(2) - 1)]

    def k_tile(j):
        return k_ref[0, pl.ds(pl.multiple_of(j * tk, tk), tk), :]

    def pv(j, e_in, c0):
        vt = vt_ref[0, j]
        return [_dot(vt[h * V_ROWS:(h + 1) * V_ROWS, :], e_in[h, :, c0:tq]) for h in range(nh)]

    def acc_update(pvs, alphas, c0):
        for h in range(nh):
            acc_scr[h, :, c0:tq] = acc_scr[h, :, c0:tq] * alphas[h][:, c0:tq] + pvs[h]

    def softmax(slot, carry, c0, add_bias):
        new = []
        for h in range(nh):
            m_old, a_prev, _ = carry[h]
            s_t = s_buf[slot][h, :, c0:tq]
            if add_bias:
                lead = s_t[:, 0:tk] + bias_ref[...]
                s_t = lead if c0 + tk == tq else jnp.concatenate([lead, s_t[:, tk:]], axis=1)
            e, m_new, alpha = _softmax_tile(s_t, m_old[:, c0:tq])
            e_buf[slot][h, :, c0:tq] = e
            if c0:
                m_new = jnp.concatenate([m_old[:, 0:c0], m_new], axis=1)
                alpha = jnp.concatenate([jnp.ones((1, c0), F32), alpha], axis=1)
            new.append((m_new, alpha, a_prev))
        return tuple(new)

    def stage(n, slot, carry):
        pvs = pv(jnp.maximum(n - 2, 0), e_buf[(slot + 2) % N_BUF], 0)
        _scores(k_tile(n + 3), q_cur, s_buf[(slot + 3) % N_BUF])
        new = softmax(slot, carry, 0, False)
        acc_update(pvs, [c[2] for c in carry], 0)
        return new

    for slot in (N_BUF - 2, N_BUF - 1):
        e_buf[slot][...] = jnp.zeros(e_buf[slot].shape, BF16)
    acc_scr[...] = jnp.zeros(acc_scr.shape, F32)

    @pl.when(i == 0)
    def _():
        for slot in range(N_BUF - 1):
            _scores(k_tile(slot), q_cur, s_buf[slot])

    ones = jnp.ones((1, tq), F32)
    carry = tuple((jnp.full((1, tq), -jnp.inf, F32), ones, ones) for _ in range(nh))

    def body(t, carry):
        for slot in range(N_BUF):
            carry = stage(N_BUF * t + slot, slot, carry)
        return carry

    carry = lax.fori_loop(0, i, body, carry)

    j0 = N_BUF * i
    last = N_BUF - 1
    for d in range(N_BUF):
        c_prev = max(d - 2, 0) * tk
        pvs = pv(jnp.maximum(j0 + d - 2, 0), e_buf[(d + 2) % N_BUF], c_prev)
        if d == 0:
            k_last = k_tile(j0 + last)
            for h in range(nh):
                s_buf[last][h, :, tq - tk:tq] = _dot(
                    k_last[:, h * LANES:(h + 1) * LANES],
                    q_cur[h * LANES:(h + 1) * LANES, tq - tk:tq]) + bias_ref[...]
        else:
            _scores(k_tile(d - 1), q_nxt, s_buf[d - 1])
        alphas = [c[2] for c in carry]
        carry = softmax(d, carry, d * tk, d < last)
        acc_update(pvs, alphas, c_prev)
    acc_update(pv(j0 + last - 1, e_buf[last - 1], (last - 1) * tk), [c[2] for c in carry], (last - 1) * tk)
    acc_update(pv(j0 + last, e_buf[last], last * tk), [c[1] for c in carry], last * tk)
    for h in range(nh):
        o_ref[0, h * D_V:(h + 1) * D_V, :] = _attn_output(acc_scr[h], cgst_ref[0, h * D_V:(h + 1) * D_V, :])


def _attn_full_kernel(qt_ref, k_ref, vt_ref, cgst_ref, o_ref):
    tq = qt_ref.shape[-1]
    scores = _scores(k_ref[0], qt_ref.at[0, 0])
    vt = vt_ref[0, 0]
    for h in range(len(scores)):
        e, _, _ = _softmax_tile(scores[h], jnp.full((1, tq), -jnp.inf, F32))
        acc = _dot(vt[h * V_ROWS:(h + 1) * V_ROWS, :], e)
        o_ref[0, h * D_V:(h + 1) * D_V, :] = _attn_output(acc, cgst_ref[0, h * D_V:(h + 1) * D_V, :])


def _attn_call(qt, kcat, vt, cgst, *, causal):
    bsz, n_qt, _, tq = qt.shape
    seq = n_qt * tq
    t_k = kcat.shape[1]
    nkt, _, tk = vt.shape[1:]
    nh = HEADS_PER_STEP if causal else H_C
    in_specs = [pl.BlockSpec((1, n_qt, nh * LANES, tq), lambda b, p, i: (b, 0, p, 0)),
                pl.BlockSpec((1, t_k, nh * LANES), lambda b, p, i: (b, 0, p)),
                pl.BlockSpec((1, nkt, nh * V_ROWS, tk), lambda b, p, i: (b, 0, p, 0)),
                pl.BlockSpec((1, nh * D_V, tq), lambda b, p, i: (b, p, i))]
    operands = [qt, kcat, vt, cgst]
    if causal:
        assert tq == N_BUF * tk and seq == t_k and tk % CHUNK == 0
        chunk = jnp.arange(tk) // CHUNK
        operands.append(jnp.where(chunk[:, None] <= chunk[None, :], 0.0, -1e30).astype(F32))
        in_specs.append(_const_spec((tk, tk)))
        kern = functools.partial(_attn_causal_kernel, tq=tq, tk=tk)
        scratch = ([pltpu.VMEM((nh, tk, tq), F32)] * N_BUF + [pltpu.VMEM((nh, tk, tq), BF16)] * N_BUF
                   + [pltpu.VMEM((nh, V_ROWS, tq), F32)])
    else:
        assert nkt == 1 and seq == tq
        kern = _attn_full_kernel
        scratch = []
    return pl.pallas_call(
        kern, grid=(bsz, H_C // nh, seq // tq),
        in_specs=in_specs,
        out_specs=pl.BlockSpec((1, nh * D_V, tq), lambda b, p, i: (b, p, i)),
        out_shape=jax.ShapeDtypeStruct((bsz, H_C * D_V, seq), BF16),
        scratch_shapes=scratch,
        name="attn", compiler_params=_params(3),
    )(*operands)


def _merge_kernel(x_ref, partial_ref, g2_ref, oct_ref, wbr_ref, wout_ref, lng_ref, lnb_ref, o_ref,
                  *, c_lo, d_c, alpha):
    yc = _dot_tn(oct_ref[0], wbr_ref[c_lo:c_lo + d_c, :])
    merged = partial_ref[0].astype(F32) + g2_ref[0].astype(F32) * yc
    y = _dot(merged.astype(BF16), wout_ref[...])
    o_ref[0] = _layer_norm(alpha * x_ref[0] + y, lng_ref[...], lnb_ref[...])


def _merge_call(x, partial, g2, oct, pw, l, *, tr, alpha):
    bsz, seq, d_model = x.shape
    d_a, d_b, d_c, _ = pw["dims"]
    kern = functools.partial(_merge_kernel, c_lo=d_a + d_b, d_c=d_c, alpha=alpha)

    def row_spec(width):
        return pl.BlockSpec((1, tr, width), lambda b, i: (b, i, 0))

    return pl.pallas_call(
        kern, grid=(bsz, seq // tr),
        in_specs=[row_spec(d_model), row_spec(d_model), row_spec(d_model),
                  pl.BlockSpec((1, d_c, tr), lambda b, i: (b, 0, i)),
                  _layer_spec(pw["wbr"], l), _layer_spec(pw["wout"], l),
                  _layer_spec(pw["ln_g"], l), _layer_spec(pw["ln_b"], l)],
        out_specs=row_spec(d_model),
        out_shape=jax.ShapeDtypeStruct((bsz, seq, d_model), F32),
        name="merge", compiler_params=_params(2),
    )(x, partial, g2, oct, pw["wbr"], pw["wout"], pw["ln_g"], pw["ln_b"])


def _mem_kernel(m_ref, w_ref, k_ref, v_ref, *, d_m):
    kv = _dot(m_ref[...].astype(BF16), w_ref[...])
    k_ref[...] = kv[:, 0:d_m]
    v_ref[...] = kv[:, d_m:2 * d_m]


def _mem_call(mem2d, wmem, l, *, tr):
    rows, d_model = mem2d.shape
    d_m = wmem.shape[2] // 2
    return pl.pallas_call(
        functools.partial(_mem_kernel, d_m=d_m), grid=(rows // tr,),
        in_specs=[pl.BlockSpec((tr, d_model), lambda i: (i, 0)), _layer_spec(wmem, l)],
        out_specs=[pl.BlockSpec((tr, d_m), lambda i: (i, 0)), pl.BlockSpec((tr, d_m), lambda i: (i, 0))],
        out_shape=[jax.ShapeDtypeStruct((rows, d_m), F32), jax.ShapeDtypeStruct((rows, d_m), F32)],
        name="mem", compiler_params=_params(1),
    )(mem2d, wmem)


def _block_diag(w):
    h, n, _ = w.shape
    eye = jnp.eye(h, dtype=w.dtype)
    return (eye[:, None, :, None] * w[:, :, None, :]).reshape(h * n, h * n)


def _gate_blocks(w_r, w_i):
    per = MXU_TILE // w_r.shape[1]
    blocks = []
    for k in range(w_r.shape[0] // per):
        blocks.append(jnp.concatenate([_block_diag(w_r[k * per:(k + 1) * per]),
                                       _block_diag(w_i[k * per:(k + 1) * per])], axis=1))
    return jnp.stack(blocks)


def _w_in_kernel(wt_ref, o_ref, *, n_head, o_tail):
    o_ref[0, :, 0:n_head] = jnp.transpose(wt_ref[0, 0:n_head, :]).astype(BF16)
    o_ref[0, :, n_head:] = jnp.transpose(wt_ref[0, o_tail:, :]).astype(BF16)


def _w_in_call(w_in, n_head, o_tail, *, tk):
    depth, d_model, d_in = w_in.shape
    n_out = n_head + d_in - o_tail
    return pl.pallas_call(
        functools.partial(_w_in_kernel, n_head=n_head, o_tail=o_tail), grid=(depth, d_model // tk),
        in_specs=[pl.BlockSpec((1, d_in, tk), lambda l, i: (l, 0, i))],
        out_specs=pl.BlockSpec((1, tk, n_out), lambda l, i: (l, i, 0)),
        out_shape=jax.ShapeDtypeStruct((depth, d_model, n_out), BF16),
        name="w_in", compiler_params=_params(2),
    )(jnp.swapaxes(w_in, 1, 2))


def _prep_weights(sp_lens, w_in, gmlp_ln_g, gmlp_ln_b, gmlp_ws, gmlp_bs, lru_conv_w,
                  lru_conv_b, lru_w_r, lru_b_r, lru_w_i, lru_b_i, lru_lambda, mla_q_norm, mla_w_uq,
                  mla_kv_norm, mla_w_ukv, mem_w_k, mem_w_v, w_br, w_out, ln_g, ln_b):
    depth, d_model, _ = w_in.shape
    d_a = gmlp_ln_g.shape[1]
    d_b = lru_lambda.shape[1]
    q_lora = mla_q_norm.shape[1]
    kv_lora = mla_kv_norm.shape[1]
    d_c = H_C * D_V
    d_m = H_M * DH_M
    o_cq = 3 * d_a + 2 * d_b
    o_kr = o_cq + q_lora + kv_lora
    o_cg = o_kr + D_ROPE
    o_mq = o_cg + d_c
    o_g = o_mq + d_m
    n_head = o_kr + LANES
    assert o_kr % LANES == 0 and (o_g - o_mq) % LANES == 0 and D_ROPE <= LANES and n_head <= o_mq
    cols = {"a_u": (0, d_a), "a_v": (d_a, 2 * d_a), "a_g": (2 * d_a, 3 * d_a),
            "b_x": (3 * d_a, 3 * d_a + d_b), "b_g": (3 * d_a + d_b, o_cq), "c_all": (o_cq, n_head),
            "m_q": (n_head, n_head + d_m)}
    for k in range(N_BRANCH):
        cols[f"g{k}"] = (n_head + d_m + k * d_model, n_head + d_m + (k + 1) * d_model)
    w1 = _w_in_call(w_in, n_head, o_mq, tk=LANES)
    wcgt = jnp.swapaxes(w_in[:, :, o_cg:o_mq], 1, 2).astype(BF16)

    wq = mla_w_uq.reshape(depth, q_lora, H_C, D_NOPE + D_ROPE) * _EXP2_SCALE
    wq = jnp.concatenate([wq, jnp.zeros((depth, q_lora, H_C, LANES - D_NOPE - D_ROPE), wq.dtype)], axis=3)
    wuqt = jnp.swapaxes(wq.reshape(depth, q_lora, H_C * LANES), 1, 2).astype(BF16)

    wkv = mla_w_ukv.reshape(depth, kv_lora, H_C, D_NOPE + D_V)
    wk = wkv[..., :D_NOPE]
    wv = wkv[..., D_NOPE:]
    wkc = jnp.concatenate([wk, jnp.zeros((depth, kv_lora, H_C, LANES - D_NOPE), wk.dtype)], axis=3)
    wkc = wkc.reshape(depth, kv_lora, H_C * LANES).astype(BF16)
    lane = jnp.arange(H_C * LANES) % LANES
    wkr = ((lane[None, :] - D_NOPE) == jnp.arange(D_ROPE)[:, None]).astype(BF16)
    wkr = jnp.broadcast_to(wkr, (depth,) + wkr.shape)
    wv = jnp.concatenate([wv, jnp.zeros((depth, kv_lora, H_C, V_ROWS - D_V), wv.dtype)], axis=3)
    wvt = jnp.swapaxes(wv.reshape(depth, kv_lora, H_C * V_ROWS), 1, 2).astype(BF16)

    def row(p):
        return p[:, None, :]

    pw = dict(
        dims=(d_a, d_b, d_c, d_m), q_lora=q_lora, kv_lora=kv_lora, cols=cols,
        w1=w1, wcgt=wcgt,
        gln_g=row(gmlp_ln_g), gln_b=row(gmlp_ln_b),
        conv_w=lru_conv_w, conv_b=row(lru_conv_b),
        wgate=jnp.stack([_gate_blocks(lru_w_r[l], lru_w_i[l]) for l in range(depth)]).astype(BF16),
        b_r=row(lru_b_r), b_i=row(lru_b_i), lam=row(lru_lambda),
        q_norm=row(mla_q_norm), wuqt=wuqt, kv_norm=row(mla_kv_norm),
        wkc=wkc, wkr=wkr, wvt=wvt,
        wbr=w_br.astype(BF16), wout=w_out.astype(BF16), ln_g=row(ln_g), ln_b=row(ln_b),
        wmem=jnp.concatenate([mem_w_k, mem_w_v], axis=2).astype(BF16),
    )
    for sp in sp_lens:
        pw["wsp%d" % sp] = jnp.tril(gmlp_ws[:, :, :sp, :sp]).astype(BF16)
        pw["bsp%d" % sp] = jnp.repeat(jnp.swapaxes(gmlp_bs[:, :, :sp], 1, 2), d_a // G_A, axis=2)
    return pw


def _rope_tables(pos):
    half = D_ROPE // 2
    freq = ROPE_BASE ** (-jnp.arange(half, dtype=F32) / half)
    ang = pos.astype(F32)[:, None] * freq[None, :]
    cos, sin = jnp.cos(ang), jnp.sin(ang)
    return (jnp.transpose(cos), jnp.transpose(sin),
            jnp.concatenate([cos, cos], axis=1), jnp.concatenate([-sin, sin], axis=1))


def _pad_conv_state(conv):
    return jnp.pad(conv, ((0, 0), (SUBLANES - (CONV_W - 1), 0), (0, 0)))


def _trunk_layer(x, pw, l, sp, rope, conv0, h0, mk, mv, past_ckv, past_kr, *, tm, alpha):
    first_chunk = past_ckv is None
    tq = N_BUF * tm if first_chunk else tm
    outs = _branch_call(x, pw, l, sp, _pad_conv_state(conv0), h0[:, None, :], mk, mv, rope, tm=tm, tq=tq,
                        first_chunk=first_chunk)
    partial, g2, cgst, qt, ckv_new, kr_new, conv_pad, h_new = outs[:8]
    if first_chunk:
        kcat, vt = outs[8:]
        v_rows = None
    else:
        v_rows = outs[8]
        ckv_all = jnp.concatenate([past_ckv, ckv_new], axis=1)
        kr_all = jnp.concatenate([past_kr, kr_new], axis=1)
        kcat, vt = _kv_call(ckv_all, kr_all, pw, l, tk=ckv_all.shape[1])
    oct = _attn_call(qt, kcat, vt, cgst, causal=first_chunk)
    x_new = _merge_call(x, partial, g2, oct, pw, l, tr=tm, alpha=alpha)
    return x_new, v_rows, conv_pad[:, SUBLANES - (CONV_W - 1):], h_new[:, 0], ckv_new, kr_new


def kernel(x_prompt, x_sample, mem_prompt, cache_mla_ckv, cache_mla_krope, cache_mem_k, cache_mem_v,
           state_lru_h, state_lru_conv, w_in, gmlp_ln_g, gmlp_ln_b, gmlp_ws, gmlp_bs,
           lru_conv_w, lru_conv_b, lru_w_r, lru_b_r, lru_w_i, lru_b_i, lru_lambda,
           mla_q_norm, mla_w_uq, mla_kv_norm, mla_w_ukv, mem_w_k, mem_w_v, w_br, w_out, ln_g, ln_b):
    bp, tp, d_model = x_prompt.shape
    bs, ts, _ = x_sample.shape
    depth = w_in.shape[0]
    past_len = cache_mla_ckv.shape[2]
    n_mem = mem_prompt.shape[1]
    d_b = lru_lambda.shape[1]
    alpha = (2.0 * depth) ** 0.25
    sp_p, sp_s = min(tp, A_CHUNK), min(ts, A_CHUNK)
    pw = _prep_weights(sorted({sp_p, sp_s}), w_in, gmlp_ln_g, gmlp_ln_b, gmlp_ws, gmlp_bs, lru_conv_w,
                       lru_conv_b, lru_w_r, lru_b_r, lru_w_i, lru_b_i, lru_lambda, mla_q_norm, mla_w_uq,
                       mla_kv_norm, mla_w_ukv, mem_w_k, mem_w_v, w_br, w_out, ln_g, ln_b)
    rope_p = _rope_tables(jnp.arange(tp))
    rope_s = _rope_tables(past_len + jnp.arange(ts))
    tm_p = min(tp, ROW_TILE)
    assert tp % (N_BUF * tm_p) == 0 and tm_p % CHUNK == 0 and ts % SUBLANES == 0
    zero_conv = jnp.zeros((bp, CONV_W - 1, d_b), F32)
    zero_h = jnp.zeros((bp, d_b), F32)

    xp, xs = x_prompt, x_sample
    acc = [[] for _ in range(11)]
    for l in range(depth):
        mk, mv = _mem_call(mem_prompt.reshape(bp * n_mem, d_model), pw["wmem"], l, tr=n_mem)
        mk = mk.reshape(bp, n_mem, -1)
        mv = mv.reshape(bp, n_mem, -1)
        xp, _, conv_n, h_n, ckv_n, kr_n = _trunk_layer(
            xp, pw, l, sp_p, rope_p, zero_conv, zero_h, mk, mv, None, None, tm=tm_p, alpha=alpha)
        for k, val in zip(range(6), (ckv_n, kr_n, mk.reshape(bp, n_mem, H_M, DH_M),
                                     mv.reshape(bp, n_mem, H_M, DH_M), h_n, conv_n)):
            acc[k].append(val)
        xs, v_n, conv_n, h_n, ckv_n, kr_n = _trunk_layer(
            xs, pw, l, sp_s, rope_s, state_lru_conv[l], state_lru_h[l],
            cache_mem_k[l].reshape(bs, n_mem, -1), cache_mem_v[l].reshape(bs, n_mem, -1),
            cache_mla_ckv[l], cache_mla_krope[l], tm=ts, alpha=alpha)
        for k, val in zip(range(6, 11), (ckv_n, kr_n, h_n, conv_n, v_n)):
            acc[k].append(val)
    return (xp, xs) + tuple(jnp.stack(a) for a in acc)
```

```python
import functools
import math

import jax
import jax.numpy as jnp
from jax import lax
from jax.experimental import pallas as pl
from jax.experimental.pallas import tpu as pltpu

CHUNK = 64
G_A = 4
A_CHUNK = 128
H_B = 8
CONV_W = 4
LRU_C = 8.0
H_C = 8
D_NOPE = 64
D_ROPE = 32
D_V = 64
ROPE_BASE = 10000.0
H_M = 4
DH_M = 64
N_BRANCH = 4
EPS = 1e-6

LANES = 128
SUBLANES = 8
VMEM_LIMIT = 56 * 1024 * 1024
ROW_TILE = 256
MXU_TILE = 256

F32 = jnp.float32
BF16 = jnp.bfloat16


def _dot(a, b):
    return jnp.dot(a, b, preferred_element_type=F32)


def _dot_nt(a, b):
    return lax.dot_general(a, b, (((1,), (1,)), ((), ())), preferred_element_type=F32)


def _dot_tn(a, b):
    return lax.dot_general(a, b, (((0,), (0,)), ((), ())), preferred_element_type=F32)


def _sigmoid(x):
    return 1.0 / (1.0 + jnp.exp(-x))


def _silu(x):
    return x * _sigmoid(x)


def _gelu(x):
    return jax.nn.gelu(x)


def _expm1_nonpos(x):
    u = jnp.exp(x)
    near = (u - 1.0) * x / jnp.log(jnp.where(u == 1.0, 2.0, jnp.maximum(u, 0.5)))
    return jnp.where(u == 1.0, x, jnp.where(u > 0.5, near, u - 1.0))


def _layer_norm(x, g, b):
    mu = jnp.mean(x, -1, keepdims=True)
    var = jnp.mean(jnp.square(x - mu), -1, keepdims=True)
    return (x - mu) * lax.rsqrt(var + EPS) * g + b


def _rms_norm(x, g):
    return x * lax.rsqrt(jnp.mean(jnp.square(x), -1, keepdims=True) + EPS) * g


def _const_spec(shape):
    nd = len(shape)
    return pl.BlockSpec(shape, lambda *_: (0,) * nd, pipeline_mode=pl.Buffered(1))


def _layer_spec(arr, l):
    nd = arr.ndim - 1
    return pl.BlockSpec((None,) + arr.shape[1:], lambda *_: (l,) + (0,) * nd, pipeline_mode=pl.Buffered(1))


def _params(n_axes):
    return pltpu.CompilerParams(dimension_semantics=("arbitrary",) * n_axes,
                                vmem_limit_bytes=VMEM_LIMIT)


V_ROWS = 80


def _keys_values(ckv, kr, wkc_ref, wkr_ref, wvt_ref):
    c = ckv.astype(BF16)
    kcat = (_dot(c, wkc_ref[...]) + _dot(kr.astype(BF16), wkr_ref[...])).astype(BF16)
    v_t = _dot_nt(wvt_ref[...], c)
    row = lax.broadcasted_iota(jnp.int32, (v_t.shape[0], 1), 0)
    return kcat, jnp.where(row % V_ROWS == D_V, 1.0, v_t).astype(BF16)


def _branch_kernel(x_ref, w1_ref, wcgt_ref, glng_ref, glnb_ref, wsp_ref, bsp_ref,
                   convw_ref, convb_ref, wgate_ref, br_ref, bi_ref, lam_ref, conv0_ref, h0_ref, qnorm_ref, wuqt_ref,
                   kvnorm_ref, cosq_ref, sinq_ref, ck_ref, sk_ref, mk_ref, mv_ref, wbr_ref,
                   *rest, first_chunk, tm, seq_len, sp_len, d_a, d_b, d_c, d_m, q_lora, kv_lora, cols):
    if first_chunk:
        (wkc_ref, wkr_ref, wvt_ref, partial_ref, g2_ref, cgst_ref, qt_ref, ckv_ref, kr_ref, conv_ref, h_ref,
         kcat_ref, vt_ref, merged_scr) = rest
    else:
        partial_ref, g2_ref, cgst_ref, qt_ref, ckv_ref, kr_ref, conv_ref, h_ref, v_ref, merged_scr = rest
    t = pl.program_id(1)
    xb = x_ref[0].astype(BF16)

    def zin(name):
        lo, hi = cols[name]
        return _dot(xb, w1_ref[:, lo:hi])

    z_u, z_v, z_ag, z_g0 = zin("a_u"), zin("a_v"), zin("a_g"), zin("g0")
    u = _gelu(z_u)
    v = _layer_norm(_gelu(z_v), glng_ref[...], glnb_ref[...])
    if not first_chunk:
        v_ref[0] = v
    vb = v.astype(BF16)
    n_groups = d_a // LANES
    row_blocks = []
    for c in range(tm // sp_len):
        col_blocks = [_dot(wsp_ref[g], vb[c * sp_len:(c + 1) * sp_len, g * LANES:(g + 1) * LANES])
                      for g in range(n_groups)]
        row_blocks.append(jnp.concatenate(col_blocks, axis=1) + bsp_ref[...])
    s = row_blocks[0] if len(row_blocks) == 1 else jnp.concatenate(row_blocks, axis=0)
    oa = (u * s) * _silu(z_ag)
    merged_scr[...] = _sigmoid(z_g0) * _dot(oa.astype(BF16), wbr_ref[0:d_a, :])

    @pl.when(t == 0)
    def _():
        conv_ref[...] = conv0_ref[...]
        h_ref[...] = h0_ref[...]

    bx = zin("b_x")
    n_seq = tm // seq_len
    span = SUBLANES + seq_len
    stacked = jnp.concatenate(
        [piece for q in range(n_seq) for piece in (conv_ref[q], bx[q * seq_len:(q + 1) * seq_len])], axis=0)
    xc = convb_ref[...]
    for k in range(CONV_W):
        shift = CONV_W - 1 - k
        rolled = stacked if shift == 0 else pltpu.roll(stacked, shift, axis=0)
        sh = [rolled[q * span + SUBLANES:(q + 1) * span] for q in range(n_seq)]
        xc = xc + (sh[0] if n_seq == 1 else jnp.concatenate(sh, axis=0)) * convw_ref[k:k + 1, :]
    for q in range(n_seq):
        conv_ref[q] = bx[(q + 1) * seq_len - SUBLANES:(q + 1) * seq_len]

    xcb = xc.astype(BF16)
    n_blk = d_b // MXU_TILE
    ri = [_dot(xcb[:, k * MXU_TILE:(k + 1) * MXU_TILE], wgate_ref[k]) for k in range(n_blk)]
    z_bg, z_g1, z_g2 = zin("b_g"), zin("g1"), zin("g2")
    z_mq, z_g3, z_c = zin("m_q"), zin("g3"), zin("c_all")
    cg_t = _dot_nt(wcgt_ref[...], xb)
    r = _sigmoid(jnp.concatenate([p[:, 0:MXU_TILE] for p in ri], axis=1) + br_ref[...])
    i_gate = _sigmoid(jnp.concatenate([p[:, MXU_TILE:2 * MXU_TILE] for p in ri], axis=1) + bi_ref[...])
    neg_lam = -lam_ref[...]
    softplus = jnp.maximum(neg_lam, 0.0) + jnp.log1p(jnp.exp(-jnp.abs(neg_lam)))
    log_a = (-LRU_C * r) * softplus
    a = jnp.exp(log_a)
    bval = jnp.sqrt(-_expm1_nonpos(2.0 * log_a)) * (i_gate * xc)
    in_group = lax.broadcasted_iota(jnp.int32, (tm, 1), 0) % SUBLANES
    d = 1
    while d < SUBLANES:
        keep = in_group >= d
        a_sh = pltpu.roll(a, d, axis=0)
        b_sh = pltpu.roll(bval, d, axis=0)
        bval = jnp.where(keep, a * b_sh + bval, bval)
        a = jnp.where(keep, a * a_sh, a)
        d *= 2
    groups_per_seq = seq_len // SUBLANES
    h_groups = []
    for g in range(tm // SUBLANES):
        q = g // groups_per_seq
        if g % groups_per_seq == 0:
            carry = h_ref[q]
        lo = g * SUBLANES
        h_g = a[lo:lo + SUBLANES] * carry + bval[lo:lo + SUBLANES]
        h_groups.append(h_g)
        carry = h_g[SUBLANES - 1:SUBLANES]
        if (g + 1) % groups_per_seq == 0:
            h_ref[q] = carry
    h = jnp.concatenate(h_groups, axis=0)
    ob = h * _silu(z_bg)
    merged_scr[...] += _sigmoid(z_g1) * _dot(ob.astype(BF16), wbr_ref[d_a:d_a + d_b, :])
    g2_ref[0] = _sigmoid(z_g2).astype(BF16)

    mq = z_mq.astype(BF16)
    lane = lax.broadcasted_iota(jnp.int32, (1, LANES), 1)
    om_rows = []
    for q in range(n_seq):
        mkb = mk_ref[q].astype(BF16)
        mvb = mv_ref[q].astype(BF16)
        slabs = []
        for p in range(d_m // LANES):
            mq_p = mq[q * seq_len:(q + 1) * seq_len, p * LANES:(p + 1) * LANES]
            mk_p = mkb[:, p * LANES:(p + 1) * LANES]
            mv_p = mvb[:, p * LANES:(p + 1) * LANES]
            acc = None
            for half in range(LANES // DH_M):
                sel = (lane >= half * DH_M) & (lane < (half + 1) * DH_M)
                sc = _dot_nt(mq_p, jnp.where(sel, mk_p, jnp.zeros_like(mk_p))) * (DH_M ** -0.5)
                e = jnp.exp(sc - jnp.max(sc, -1, keepdims=True))
                prob = (e / jnp.sum(e, -1, keepdims=True)).astype(BF16)
                o = _dot(prob, jnp.where(sel, mv_p, jnp.zeros_like(mv_p)))
                acc = o if acc is None else acc + o
            slabs.append(acc)
        om_rows.append(jnp.concatenate(slabs, axis=1))
    om = om_rows[0] if n_seq == 1 else jnp.concatenate(om_rows, axis=0)
    m_lo = d_a + d_b + d_c
    partial_ref[0] = (merged_scr[...]
                      + _sigmoid(z_g3) * _dot(om.astype(BF16), wbr_ref[m_lo:m_lo + d_m, :])).astype(BF16)
    cgst_ref[0] = _silu(cg_t)

    ckv = _rms_norm(z_c[:, q_lora:q_lora + kv_lora], kvnorm_ref[...])
    half = D_ROPE // 2
    z_kr = z_c[:, q_lora + kv_lora:q_lora + kv_lora + LANES]
    kr_swapped = jnp.where(lane < half, pltpu.roll(z_kr, LANES - half, axis=1), pltpu.roll(z_kr, half, axis=1))
    kr = z_kr[:, 0:D_ROPE] * ck_ref[...] + kr_swapped[:, 0:D_ROPE] * sk_ref[...]
    ckv_ref[0] = ckv
    kr_ref[0] = kr
    if first_chunk:
        kcat_ref[0], vt_ref[0, 0] = _keys_values(ckv, kr, wkc_ref, wkr_ref, wvt_ref)
    cqn = _rms_norm(z_c[:, 0:q_lora], qnorm_ref[...]).astype(BF16)
    q_t = _dot_nt(wuqt_ref[...], cqn)
    cos_t, sin_t = cosq_ref[...], sinq_ref[...]
    for hd in range(H_C):
        lo = hd * LANES
        x1 = q_t[lo + D_NOPE:lo + D_NOPE + half]
        x2 = q_t[lo + D_NOPE + half:lo + D_NOPE + D_ROPE]
        qt_ref[0, 0, lo:lo + D_NOPE, :] = q_t[lo:lo + D_NOPE].astype(BF16)
        qt_ref[0, 0, lo + D_NOPE:lo + D_NOPE + half, :] = (x1 * cos_t - x2 * sin_t).astype(BF16)
        qt_ref[0, 0, lo + D_NOPE + half:lo + D_NOPE + D_ROPE, :] = (x1 * sin_t + x2 * cos_t).astype(BF16)
        qt_ref[0, 0, lo + D_NOPE + D_ROPE:lo + LANES, :] = jnp.zeros((LANES - D_NOPE - D_ROPE, tm), BF16)


def _branch_call(x, pw, l, sp, conv0_pad, h0, mk, mv, rope, *, tm, seq_len, tq, first_chunk):
    bsz, seq, d_model = x.shape
    d_a, d_b, d_c, d_m = pw["dims"]
    q_lora, kv_lora = pw["q_lora"], pw["kv_lora"]
    n_mem = mk.shape[1]
    cos_t, sin_t, ck_t, sk_t = rope
    grid = (bsz, seq // tm)

    def row_spec(width):
        return pl.BlockSpec((1, tm, width), lambda b, t: (b, t, 0))

    def col_spec(height):
        return pl.BlockSpec((1, height, tm), lambda b, t: (b, 0, t))

    def tab_spec(width):
        return pl.BlockSpec((tm, width), lambda b, t: (t, 0))

    n_seq = tm // seq_len

    def batch_spec(rows, width):
        return pl.BlockSpec((n_seq, rows, width), lambda b, t: (b, 0, 0))

    names = ["w1", "wcgt", "gln_g", "gln_b", "wsp%d" % sp, "bsp%d" % sp, "conv_w", "conv_b",
             "wgate", "b_r", "b_i", "lam"]
    names2 = ["q_norm", "wuqt", "kv_norm"]
    in_specs = ([row_spec(d_model)] + [_layer_spec(pw[n], l) for n in names]
                + [batch_spec(SUBLANES, d_b), batch_spec(1, d_b)] + [_layer_spec(pw[n], l) for n in names2]
                + [pl.BlockSpec((D_ROPE // 2, tm), lambda b, t: (0, t)), pl.BlockSpec((D_ROPE // 2, tm), lambda b, t: (0, t)),
                   tab_spec(D_ROPE), tab_spec(D_ROPE), batch_spec(n_mem, d_m), batch_spec(n_mem, d_m),
                   _layer_spec(pw["wbr"], l)])
    operands = ([x] + [pw[n] for n in names] + [conv0_pad, h0] + [pw[n] for n in names2]
                + [cos_t, sin_t, ck_t, sk_t, mk, mv, pw["wbr"]])
    out_shape = [
        jax.ShapeDtypeStruct((bsz, seq, d_model), BF16),
        jax.ShapeDtypeStruct((bsz, seq, d_model), BF16),
        jax.ShapeDtypeStruct((bsz, d_c, seq), F32),
        jax.ShapeDtypeStruct((bsz, seq // tq, H_C * LANES, tq), BF16),
        jax.ShapeDtypeStruct((bsz, seq, kv_lora), F32),
        jax.ShapeDtypeStruct((bsz, seq, D_ROPE), F32),
        jax.ShapeDtypeStruct((bsz * n_seq, SUBLANES, d_b), F32),
        jax.ShapeDtypeStruct((bsz * n_seq, 1, d_b), F32),
    ]
    out_specs = [
        row_spec(d_model), row_spec(d_model), col_spec(d_c),
        pl.BlockSpec((1, 1, H_C * LANES, tm), lambda b, t: (b, t // (tq // tm), 0, t % (tq // tm))),
        row_spec(kv_lora), row_spec(D_ROPE), batch_spec(SUBLANES, d_b), batch_spec(1, d_b),
    ]
    if first_chunk:
        for name in ("wkc", "wkr", "wvt"):
            in_specs.append(_layer_spec(pw[name], l))
            operands.append(pw[name])
        out_shape += [jax.ShapeDtypeStruct((bsz, seq, H_C * LANES), BF16),
                      jax.ShapeDtypeStruct((bsz, seq // tm, H_C * V_ROWS, tm), BF16)]
        out_specs += [row_spec(H_C * LANES),
                      pl.BlockSpec((1, 1, H_C * V_ROWS, tm), lambda b, t: (b, t, 0, 0))]
    else:
        out_shape.append(jax.ShapeDtypeStruct((bsz, seq, d_a), F32))
        out_specs.append(row_spec(d_a))
    kern = functools.partial(
        _branch_kernel, first_chunk=first_chunk, tm=tm, seq_len=seq_len, sp_len=sp, d_a=d_a, d_b=d_b, d_c=d_c, d_m=d_m,
        q_lora=q_lora, kv_lora=kv_lora, cols=pw["cols"])
    return pl.pallas_call(
        kern, grid=grid, in_specs=in_specs, out_specs=out_specs, out_shape=out_shape,
        scratch_shapes=[pltpu.VMEM((tm, d_model), F32)],
        name="branch", compiler_params=_params(2),
    )(*operands)


def _kv_kernel(ckv_ref, kr_ref, wkc_ref, wkr_ref, wvt_ref, kcat_ref, vt_ref):
    kcat_ref[0], vt_ref[0, 0] = _keys_values(ckv_ref[0], kr_ref[0], wkc_ref, wkr_ref, wvt_ref)


def _kv_call(ckv_all, kr_all, pw, l, *, tk):
    bsz, t_k, kv_lora = ckv_all.shape
    n_k = H_C * LANES
    n_v = H_C * V_ROWS
    nkt = t_k // tk
    return pl.pallas_call(
        _kv_kernel, grid=(bsz, nkt),
        in_specs=[pl.BlockSpec((1, tk, kv_lora), lambda b, j: (b, j, 0)),
                  pl.BlockSpec((1, tk, D_ROPE), lambda b, j: (b, j, 0)),
                  _layer_spec(pw["wkc"], l), _layer_spec(pw["wkr"], l), _layer_spec(pw["wvt"], l)],
        out_specs=[pl.BlockSpec((1, tk, n_k), lambda b, j: (b, j, 0)),
                   pl.BlockSpec((1, 1, n_v, tk), lambda b, j: (b, j, 0, 0))],
        out_shape=[jax.ShapeDtypeStruct((bsz, t_k, n_k), BF16),
                   jax.ShapeDtypeStruct((bsz, nkt, n_v, tk), BF16)],
        name="kv", compiler_params=_params(2),
    )(ckv_all, kr_all, pw["wkc"], pw["wkr"], pw["wvt"])


HEADS_PER_STEP = 2
_EXP2_SCALE = (D_NOPE + D_ROPE) ** -0.5 * math.log2(math.e)


def _scores(k_tile, q_tile, out_refs=None):
    res = [_dot(k_tile[:, h * LANES:(h + 1) * LANES], q_tile[h * LANES:(h + 1) * LANES, :])
           for h in range(k_tile.shape[1] // LANES)]
    if out_refs is None:
        return res
    for h, r in enumerate(res):
        out_refs[h] = r
    return None


def _softmax_tile(s_t, m_old):
    m_new = jnp.maximum(m_old, jnp.max(s_t, axis=0, keepdims=True))
    return jnp.exp2(s_t - m_new).astype(BF16), m_new, jnp.exp2(m_old - m_new)


def _attn_output(acc, cgs_t):
    return (acc[0:D_V] / acc[D_V:D_V + 1] * cgs_t).astype(BF16)


N_BUF = 4


def _attn_causal_kernel(qt_ref, k_ref, vt_ref, cgst_ref, bias_ref, o_ref, *scratch, tq, tk):
    s_buf, e_buf, acc_scr = scratch[0:N_BUF], scratch[N_BUF:2 * N_BUF], scratch[2 * N_BUF]
    i = pl.program_id(2)
    nh = HEADS_PER_STEP
    q_cur = qt_ref.at[0, i]
    q_nxt = qt_ref.at[0, jnp.minimum(i + 1, pl.num_programs(2) - 1)]

    def k_tile(j):
        return k_ref[0, pl.ds(pl.multiple_of(j * tk, tk), tk), :]

    def pv(j, e_in, c0):
        vt = vt_ref[0, j]
        return [_dot(vt[h * V_ROWS:(h + 1) * V_ROWS, :], e_in[h, :, c0:tq]) for h in range(nh)]

    def acc_update(pvs, alphas, c0):
        for h in range(nh):
            acc_scr[h, :, c0:tq] = acc_scr[h, :, c0:tq] * alphas[h][:, c0:tq] + pvs[h]

    def softmax(slot, carry, c0, add_bias):
        new = []
        for h in range(nh):
            m_old, a_prev, _ = carry[h]
            s_t = s_buf[slot][h, :, c0:tq]
            if add_bias:
                lead = s_t[:, 0:tk] + bias_ref[...]
                s_t = lead if c0 + tk == tq else jnp.concatenate([lead, s_t[:, tk:]], axis=1)
            e, m_new, alpha = _softmax_tile(s_t, m_old[:, c0:tq])
            e_buf[slot][h, :, c0:tq] = e
            if c0:
                m_new = jnp.concatenate([m_old[:, 0:c0], m_new], axis=1)
                alpha = jnp.concatenate([jnp.ones((1, c0), F32), alpha], axis=1)
            new.append((m_new, alpha, a_prev))
        return tuple(new)

    def stage(n, slot, carry):
        pvs = pv(jnp.maximum(n - 2, 0), e_buf[(slot + 2) % N_BUF], 0)
        _scores(k_tile(n + 3), q_cur, s_buf[(slot + 3) % N_BUF])
        new = softmax(slot, carry, 0, False)
        acc_update(pvs, [c[2] for c in carry], 0)
        return new

    for slot in (N_BUF - 2, N_BUF - 1):
        e_buf[slot][...] = jnp.zeros(e_buf[slot].shape, BF16)
    acc_scr[...] = jnp.zeros(acc_scr.shape, F32)

    @pl.when(i == 0)
    def _():
        for slot in range(N_BUF - 1):
            _scores(k_tile(slot), q_cur, s_buf[slot])

    ones = jnp.ones((1, tq), F32)
    carry = tuple((jnp.full((1, tq), -jnp.inf, F32), ones, ones) for _ in range(nh))

    def body(t, carry):
        for slot in range(N_BUF):
            carry = stage(N_BUF * t + slot, slot, carry)
        return carry

    carry = lax.fori_loop(0, i, body, carry)

    j0 = N_BUF * i
    last = N_BUF - 1
    for d in range(N_BUF):
        c_prev = max(d - 2, 0) * tk
        pvs = pv(jnp.maximum(j0 + d - 2, 0), e_buf[(d + 2) % N_BUF], c_prev)
        if d == 0:
            k_last = k_tile(j0 + last)
            for h in range(nh):
                s_buf[last][h, :, tq - tk:tq] = _dot(
                    k_last[:, h * LANES:(h + 1) * LANES],
                    q_cur[h * LANES:(h + 1) * LANES, tq - tk:tq]) + bias_ref[...]
        else:
            _scores(k_tile(d - 1), q_nxt, s_buf[d - 1])
        alphas = [c[2] for c in carry]
        carry = softmax(d, carry, d * tk, d < last)
        acc_update(pvs, alphas, c_prev)
    acc_update(pv(j0 + last - 1, e_buf[last - 1], (last - 1) * tk), [c[2] for c in carry], (last - 1) * tk)
    acc_update(pv(j0 + last, e_buf[last], last * tk), [c[1] for c in carry], last * tk)
    for h in range(nh):
        o_ref[0, h * D_V:(h + 1) * D_V, :] = _attn_output(acc_scr[h], cgst_ref[0, h * D_V:(h + 1) * D_V, :])


def _attn_full_kernel(qt_ref, k_ref, vt_ref, cgst_ref, o_ref):
    tq = qt_ref.shape[-1]
    scores = _scores(k_ref[0], qt_ref.at[0, 0])
    vt = vt_ref[0, 0]
    for h in range(len(scores)):
        e, _, _ = _softmax_tile(scores[h], jnp.full((1, tq), -jnp.inf, F32))
        acc = _dot(vt[h * V_ROWS:(h + 1) * V_ROWS, :], e)
        o_ref[0, h * D_V:(h + 1) * D_V, :] = _attn_output(acc, cgst_ref[0, h * D_V:(h + 1) * D_V, :])


def _attn_call(qt, kcat, vt, cgst, *, causal):
    bsz, n_qt, _, tq = qt.shape
    seq = n_qt * tq
    t_k = kcat.shape[1]
    nkt, _, tk = vt.shape[1:]
    nh = HEADS_PER_STEP if causal else H_C
    in_specs = [pl.BlockSpec((1, n_qt, nh * LANES, tq), lambda b, p, i: (b, 0, p, 0)),
                pl.BlockSpec((1, t_k, nh * LANES), lambda b, p, i: (b, 0, p)),
                pl.BlockSpec((1, nkt, nh * V_ROWS, tk), lambda b, p, i: (b, 0, p, 0)),
                pl.BlockSpec((1, nh * D_V, tq), lambda b, p, i: (b, p, i))]
    operands = [qt, kcat, vt, cgst]
    if causal:
        assert tq == N_BUF * tk and seq == t_k and tk % CHUNK == 0
        chunk = jnp.arange(tk) // CHUNK
        operands.append(jnp.where(chunk[:, None] <= chunk[None, :], 0.0, -1e30).astype(F32))
        in_specs.append(_const_spec((tk, tk)))
        kern = functools.partial(_attn_causal_kernel, tq=tq, tk=tk)
        scratch = ([pltpu.VMEM((nh, tk, tq), F32)] * N_BUF + [pltpu.VMEM((nh, tk, tq), BF16)] * N_BUF
                   + [pltpu.VMEM((nh, V_ROWS, tq), F32)])
    else:
        assert nkt == 1 and seq == tq
        kern = _attn_full_kernel
        scratch = []
    return pl.pallas_call(
        kern, grid=(bsz, H_C // nh, seq // tq),
        in_specs=in_specs,
        out_specs=pl.BlockSpec((1, nh * D_V, tq), lambda b, p, i: (b, p, i)),
        out_shape=jax.ShapeDtypeStruct((bsz, H_C * D_V, seq), BF16),
        scratch_shapes=scratch,
        name="attn", compiler_params=_params(3),
    )(*operands)


def _merge_kernel(x_ref, partial_ref, g2_ref, oct_ref, wbr_ref, wout_ref, lng_ref, lnb_ref, o_ref,
                  *, c_lo, d_c, alpha):
    yc = _dot_tn(oct_ref[0], wbr_ref[c_lo:c_lo + d_c, :])
    merged = partial_ref[0].astype(F32) + g2_ref[0].astype(F32) * yc
    y = _dot(merged.astype(BF16), wout_ref[...])
    o_ref[0] = _layer_norm(alpha * x_ref[0] + y, lng_ref[...], lnb_ref[...])


def _merge_call(x, partial, g2, oct, pw, l, *, tr, alpha):
    bsz, seq, d_model = x.shape
    d_a, d_b, d_c, _ = pw["dims"]
    kern = functools.partial(_merge_kernel, c_lo=d_a + d_b, d_c=d_c, alpha=alpha)

    def row_spec(width):
        return pl.BlockSpec((1, tr, width), lambda b, i: (b, i, 0))

    return pl.pallas_call(
        kern, grid=(bsz, seq // tr),
        in_specs=[row_spec(d_model), row_spec(d_model), row_spec(d_model),
                  pl.BlockSpec((1, d_c, tr), lambda b, i: (b, 0, i)),
                  _layer_spec(pw["wbr"], l), _layer_spec(pw["wout"], l),
                  _layer_spec(pw["ln_g"], l), _layer_spec(pw["ln_b"], l)],
        out_specs=row_spec(d_model),
        out_shape=jax.ShapeDtypeStruct((bsz, seq, d_model), F32),
        name="merge", compiler_params=_params(2),
    )(x, partial, g2, oct, pw["wbr"], pw["wout"], pw["ln_g"], pw["ln_b"])


def _mem_kernel(m_ref, w_ref, k_ref, v_ref, *, d_m):
    kv = _dot(m_ref[...].astype(BF16), w_ref[...])
    k_ref[...] = kv[:, 0:d_m]
    v_ref[...] = kv[:, d_m:2 * d_m]


def _mem_call(mem2d, wmem, l, *, tr):
    rows, d_model = mem2d.shape
    d_m = wmem.shape[2] // 2
    return pl.pallas_call(
        functools.partial(_mem_kernel, d_m=d_m), grid=(rows // tr,),
        in_specs=[pl.BlockSpec((tr, d_model), lambda i: (i, 0)), _layer_spec(wmem, l)],
        out_specs=[pl.BlockSpec((tr, d_m), lambda i: (i, 0)), pl.BlockSpec((tr, d_m), lambda i: (i, 0))],
        out_shape=[jax.ShapeDtypeStruct((rows, d_m), F32), jax.ShapeDtypeStruct((rows, d_m), F32)],
        name="mem", compiler_params=_params(1),
    )(mem2d, wmem)


def _block_diag(w):
    h, n, _ = w.shape
    eye = jnp.eye(h, dtype=w.dtype)
    return (eye[:, None, :, None] * w[:, :, None, :]).reshape(h * n, h * n)


def _gate_blocks(w_r, w_i):
    per = MXU_TILE // w_r.shape[1]
    blocks = []
    for k in range(w_r.shape[0] // per):
        blocks.append(jnp.concatenate([_block_diag(w_r[k * per:(k + 1) * per]),
                                       _block_diag(w_i[k * per:(k + 1) * per])], axis=1))
    return jnp.stack(blocks)


def _w_in_kernel(wt_ref, o_ref, *, n_head, o_tail):
    o_ref[0, :, 0:n_head] = jnp.transpose(wt_ref[0, 0:n_head, :]).astype(BF16)
    o_ref[0, :, n_head:] = jnp.transpose(wt_ref[0, o_tail:, :]).astype(BF16)


def _w_in_call(w_in, n_head, o_tail, *, tk):
    depth, d_model, d_in = w_in.shape
    n_out = n_head + d_in - o_tail
    return pl.pallas_call(
        functools.partial(_w_in_kernel, n_head=n_head, o_tail=o_tail), grid=(depth, d_model // tk),
        in_specs=[pl.BlockSpec((1, d_in, tk), lambda l, i: (l, 0, i))],
        out_specs=pl.BlockSpec((1, tk, n_out), lambda l, i: (l, i, 0)),
        out_shape=jax.ShapeDtypeStruct((depth, d_model, n_out), BF16),
        name="w_in", compiler_params=_params(2),
    )(jnp.swapaxes(w_in, 1, 2))


def _prep_weights(sp_lens, w_in, gmlp_ln_g, gmlp_ln_b, gmlp_ws, gmlp_bs, lru_conv_w,
                  lru_conv_b, lru_w_r, lru_b_r, lru_w_i, lru_b_i, lru_lambda, mla_q_norm, mla_w_uq,
                  mla_kv_norm, mla_w_ukv, mem_w_k, mem_w_v, w_br, w_out, ln_g, ln_b):
    depth, d_model, _ = w_in.shape
    d_a = gmlp_ln_g.shape[1]
    d_b = lru_lambda.shape[1]
    q_lora = mla_q_norm.shape[1]
    kv_lora = mla_kv_norm.shape[1]
    d_c = H_C * D_V
    d_m = H_M * DH_M
    o_cq = 3 * d_a + 2 * d_b
    o_kr = o_cq + q_lora + kv_lora
    o_cg = o_kr + D_ROPE
    o_mq = o_cg + d_c
    o_g = o_mq + d_m
    n_head = o_kr + LANES
    assert o_kr % LANES == 0 and (o_g - o_mq) % LANES == 0 and D_ROPE <= LANES and n_head <= o_mq
    cols = {"a_u": (0, d_a), "a_v": (d_a, 2 * d_a), "a_g": (2 * d_a, 3 * d_a),
            "b_x": (3 * d_a, 3 * d_a + d_b), "b_g": (3 * d_a + d_b, o_cq), "c_all": (o_cq, n_head),
            "m_q": (n_head, n_head + d_m)}
    for k in range(N_BRANCH):
        cols[f"g{k}"] = (n_head + d_m + k * d_model, n_head + d_m + (k + 1) * d_model)
    w1 = _w_in_call(w_in, n_head, o_mq, tk=LANES)
    wcgt = jnp.swapaxes(w_in[:, :, o_cg:o_mq], 1, 2).astype(BF16)

    wq = mla_w_uq.reshape(depth, q_lora, H_C, D_NOPE + D_ROPE) * _EXP2_SCALE
    wq = jnp.concatenate([wq, jnp.zeros((depth, q_lora, H_C, LANES - D_NOPE - D_ROPE), wq.dtype)], axis=3)
    wuqt = jnp.swapaxes(wq.reshape(depth, q_lora, H_C * LANES), 1, 2).astype(BF16)

    wkv = mla_w_ukv.reshape(depth, kv_lora, H_C, D_NOPE + D_V)
    wk = wkv[..., :D_NOPE]
    wv = wkv[..., D_NOPE:]
    wkc = jnp.concatenate([wk, jnp.zeros((depth, kv_lora, H_C, LANES - D_NOPE), wk.dtype)], axis=3)
    wkc = wkc.reshape(depth, kv_lora, H_C * LANES).astype(BF16)
    lane = jnp.arange(H_C * LANES) % LANES
    wkr = ((lane[None, :] - D_NOPE) == jnp.arange(D_ROPE)[:, None]).astype(BF16)
    wkr = jnp.broadcast_to(wkr, (depth,) + wkr.shape)
    wv = jnp.concatenate([wv, jnp.zeros((depth, kv_lora, H_C, V_ROWS - D_V), wv.dtype)], axis=3)
    wvt = jnp.swapaxes(wv.reshape(depth, kv_lora, H_C * V_ROWS), 1, 2).astype(BF16)

    def row(p):
        return p[:, None, :]

    pw = dict(
        dims=(d_a, d_b, d_c, d_m), q_lora=q_lora, kv_lora=kv_lora, cols=cols,
        w1=w1, wcgt=wcgt,
        gln_g=row(gmlp_ln_g), gln_b=row(gmlp_ln_b),
        conv_w=lru_conv_w, conv_b=row(lru_conv_b),
        wgate=jnp.stack([_gate_blocks(lru_w_r[l], lru_w_i[l]) for l in range(depth)]).astype(BF16),
        b_r=row(lru_b_r), b_i=row(lru_b_i), lam=row(lru_lambda),
        q_norm=row(mla_q_norm), wuqt=wuqt, kv_norm=row(mla_kv_norm),
        wkc=wkc, wkr=wkr, wvt=wvt,
        wbr=w_br.astype(BF16), wout=w_out.astype(BF16), ln_g=row(ln_g), ln_b=row(ln_b),
        wmem=jnp.concatenate([mem_w_k, mem_w_v], axis=2).astype(BF16),
    )
    for sp in sp_lens:
        pw["wsp%d" % sp] = jnp.tril(gmlp_ws[:, :, :sp, :sp]).astype(BF16)
        pw["bsp%d" % sp] = jnp.repeat(jnp.swapaxes(gmlp_bs[:, :, :sp], 1, 2), d_a // G_A, axis=2)
    return pw


def _rope_tables(pos):
    half = D_ROPE // 2
    freq = ROPE_BASE ** (-jnp.arange(half, dtype=F32) / half)
    ang = pos.astype(F32)[:, None] * freq[None, :]
    cos, sin = jnp.cos(ang), jnp.sin(ang)
    return (jnp.transpose(cos), jnp.transpose(sin),
            jnp.concatenate([cos, cos], axis=1), jnp.concatenate([-sin, sin], axis=1))


def _pad_conv_state(conv):
    return jnp.pad(conv, ((0, 0), (SUBLANES - (CONV_W - 1), 0), (0, 0)))


def _trunk_layer(x, pw, l, sp, rope, conv0, h0, mk, mv, past_ckv, past_kr, *, tm, alpha):
    bsz, seq, d_model = x.shape
    first_chunk = past_ckv is None
    if first_chunk:
        x_rows, seq_len, tq = x, tm, N_BUF * tm
    else:
        x_rows, seq_len, tq = x.reshape(1, bsz * seq, d_model), seq, bsz * seq
        cos_t, sin_t, ck_t, sk_t = rope
        rope = (jnp.tile(cos_t, (1, bsz)), jnp.tile(sin_t, (1, bsz)), jnp.tile(ck_t, (bsz, 1)), jnp.tile(sk_t, (bsz, 1)))
    outs = _branch_call(x_rows, pw, l, sp, _pad_conv_state(conv0), h0[:, None, :], mk, mv, rope,
                        tm=tm, seq_len=seq_len, tq=tq, first_chunk=first_chunk)
    partial, g2, cgst, qt, ckv_new, kr_new, conv_pad, h_new = outs[:8]
    if first_chunk:
        kcat, vt = outs[8:]
        v_rows = None
        oct = _attn_call(qt, kcat, vt, cgst, causal=True)
    else:
        ckv_new, kr_new, v_rows = (o.reshape(bsz, seq, -1) for o in (ckv_new, kr_new, outs[8]))
        ckv_all = jnp.concatenate([past_ckv, ckv_new], axis=1)
        kr_all = jnp.concatenate([past_kr, kr_new], axis=1)
        kcat, vt = _kv_call(ckv_all, kr_all, pw, l, tk=ckv_all.shape[1])

        def per_seq(a):
            return jnp.transpose(a.reshape(a.shape[-2], bsz, seq), (1, 0, 2))

        oct = _attn_call(per_seq(qt)[:, None], kcat, vt, per_seq(cgst), causal=False)
        oct = jnp.transpose(oct, (1, 0, 2)).reshape(1, oct.shape[1], bsz * seq)
    x_new = _merge_call(x_rows, partial, g2, oct, pw, l, tr=tm, alpha=alpha)
    return (x_new.reshape(bsz, seq, d_model), v_rows, conv_pad[:, SUBLANES - (CONV_W - 1):], h_new[:, 0],
            ckv_new, kr_new)


def kernel(x_prompt, x_sample, mem_prompt, cache_mla_ckv, cache_mla_krope, cache_mem_k, cache_mem_v,
           state_lru_h, state_lru_conv, w_in, gmlp_ln_g, gmlp_ln_b, gmlp_ws, gmlp_bs,
           lru_conv_w, lru_conv_b, lru_w_r, lru_b_r, lru_w_i, lru_b_i, lru_lambda,
           mla_q_norm, mla_w_uq, mla_kv_norm, mla_w_ukv, mem_w_k, mem_w_v, w_br, w_out, ln_g, ln_b):
    bp, tp, d_model = x_prompt.shape
    bs, ts, _ = x_sample.shape
    depth = w_in.shape[0]
    past_len = cache_mla_ckv.shape[2]
    n_mem = mem_prompt.shape[1]
    d_b = lru_lambda.shape[1]
    alpha = (2.0 * depth) ** 0.25
    sp_p, sp_s = min(tp, A_CHUNK), min(ts, A_CHUNK)
    pw = _prep_weights(sorted({sp_p, sp_s}), w_in, gmlp_ln_g, gmlp_ln_b, gmlp_ws, gmlp_bs, lru_conv_w,
                       lru_conv_b, lru_w_r, lru_b_r, lru_w_i, lru_b_i, lru_lambda, mla_q_norm, mla_w_uq,
                       mla_kv_norm, mla_w_ukv, mem_w_k, mem_w_v, w_br, w_out, ln_g, ln_b)
    rope_p = _rope_tables(jnp.arange(tp))
    rope_s = _rope_tables(past_len + jnp.arange(ts))
    tm_p = min(tp, ROW_TILE)
    assert tp % (N_BUF * tm_p) == 0 and tm_p % CHUNK == 0 and ts % SUBLANES == 0
    zero_conv = jnp.zeros((bp, CONV_W - 1, d_b), F32)
    zero_h = jnp.zeros((bp, d_b), F32)

    xp, xs = x_prompt, x_sample
    acc = [[] for _ in range(11)]
    for l in range(depth):
        mk, mv = _mem_call(mem_prompt.reshape(bp * n_mem, d_model), pw["wmem"], l, tr=n_mem)
        mk = mk.reshape(bp, n_mem, -1)
        mv = mv.reshape(bp, n_mem, -1)
        xp, _, conv_n, h_n, ckv_n, kr_n = _trunk_layer(
            xp, pw, l, sp_p, rope_p, zero_conv, zero_h, mk, mv, None, None, tm=tm_p, alpha=alpha)
        for k, val in zip(range(6), (ckv_n, kr_n, mk.reshape(bp, n_mem, H_M, DH_M),
                                     mv.reshape(bp, n_mem, H_M, DH_M), h_n, conv_n)):
            acc[k].append(val)
        xs, v_n, conv_n, h_n, ckv_n, kr_n = _trunk_layer(
            xs, pw, l, sp_s, rope_s, state_lru_conv[l], state_lru_h[l],
            cache_mem_k[l].reshape(bs, n_mem, -1), cache_mem_v[l].reshape(bs, n_mem, -1),
            cache_mla_ckv[l], cache_mla_krope[l], tm=bs * ts, alpha=alpha)
        for k, val in zip(range(6, 11), (ckv_n, kr_n, h_n, conv_n, v_n)):
            acc[k].append(val)
    return (xp, xs) + tuple(jnp.stack(a) for a in acc)
```

```python
import functools
import math

import jax
import jax.numpy as jnp
from jax import lax
from jax.experimental import pallas as pl
from jax.experimental.pallas import tpu as pltpu

CHUNK = 64
G_A = 4
A_CHUNK = 128
H_B = 8
CONV_W = 4
LRU_C = 8.0
H_C = 8
D_NOPE = 64
D_ROPE = 32
D_V = 64
ROPE_BASE = 10000.0
H_M = 4
DH_M = 64
N_BRANCH = 4
EPS = 1e-6

LANES = 128
SUBLANES = 8
VMEM_LIMIT = 56 * 1024 * 1024
ROW_TILE = 256
MXU_TILE = 256
MERGE_TILE = 512

F32 = jnp.float32
BF16 = jnp.bfloat16


def _dot(a, b):
    return jnp.dot(a, b, preferred_element_type=F32)


def _dot_nt(a, b):
    return lax.dot_general(a, b, (((1,), (1,)), ((), ())), preferred_element_type=F32)


def _dot_tn(a, b):
    return lax.dot_general(a, b, (((0,), (0,)), ((), ())), preferred_element_type=F32)


def _sigmoid(x):
    return 1.0 / (1.0 + jnp.exp(-x))


def _silu(x):
    return x * _sigmoid(x)


def _gelu(x):
    return jax.nn.gelu(x)


def _expm1_nonpos(x):
    u = jnp.exp(x)
    near = (u - 1.0) * x / jnp.log(jnp.where(u == 1.0, 2.0, jnp.maximum(u, 0.5)))
    return jnp.where(u == 1.0, x, jnp.where(u > 0.5, near, u - 1.0))


def _layer_norm(x, g, b):
    mu = jnp.mean(x, -1, keepdims=True)
    var = jnp.mean(jnp.square(x - mu), -1, keepdims=True)
    return (x - mu) * lax.rsqrt(var + EPS) * g + b


def _rms_norm(x, g):
    return x * lax.rsqrt(jnp.mean(jnp.square(x), -1, keepdims=True) + EPS) * g


def _const_spec(shape):
    nd = len(shape)
    return pl.BlockSpec(shape, lambda *_: (0,) * nd, pipeline_mode=pl.Buffered(1))


def _layer_spec(arr, l):
    nd = arr.ndim - 1
    return pl.BlockSpec((None,) + arr.shape[1:], lambda *_: (l,) + (0,) * nd, pipeline_mode=pl.Buffered(1))


def _params(n_axes):
    return pltpu.CompilerParams(dimension_semantics=("arbitrary",) * n_axes,
                                vmem_limit_bytes=VMEM_LIMIT)


V_ROWS = 80


def _keys_values(ckv, kr, wkc_ref, wkr_ref, wvt_ref):
    c = ckv.astype(BF16)
    kcat = (_dot(c, wkc_ref[...]) + _dot(kr.astype(BF16), wkr_ref[...])).astype(BF16)
    v_t = _dot_nt(wvt_ref[...], c)
    row = lax.broadcasted_iota(jnp.int32, (v_t.shape[0], 1), 0)
    return kcat, jnp.where(row % V_ROWS == D_V, 1.0, v_t).astype(BF16)


def _branch_kernel(x_ref, w1_ref, wcgt_ref, glng_ref, glnb_ref, wsp_ref, bsp_ref,
                   convw_ref, convb_ref, wgate_ref, br_ref, bi_ref, lam_ref, conv0_ref, h0_ref, qnorm_ref, wuqt_ref,
                   kvnorm_ref, cosq_ref, sinq_ref, ck_ref, sk_ref, mk_ref, mv_ref, wbr_ref,
                   *rest, first_chunk, tm, seq_len, sp_len, d_a, d_b, d_c, d_m, q_lora, kv_lora, cols):
    if first_chunk:
        (wkc_ref, wkr_ref, wvt_ref, partial_ref, g2_ref, cgst_ref, qt_ref, ckv_ref, kr_ref, conv_ref, h_ref,
         kcat_ref, vt_ref, merged_scr) = rest
    else:
        partial_ref, g2_ref, cgst_ref, qt_ref, ckv_ref, kr_ref, conv_ref, h_ref, v_ref, merged_scr = rest
    t = pl.program_id(1)
    xb = x_ref[0].astype(BF16)

    def zin(name):
        lo, hi = cols[name]
        return _dot(xb, w1_ref[:, lo:hi])

    z_u, z_v, z_ag, z_g0 = zin("a_u"), zin("a_v"), zin("a_g"), zin("g0")
    u = _gelu(z_u)
    v = _layer_norm(_gelu(z_v), glng_ref[...], glnb_ref[...])
    if not first_chunk:
        v_ref[0] = v
    vb = v.astype(BF16)
    n_groups = d_a // LANES
    row_blocks = []
    for c in range(tm // sp_len):
        col_blocks = [_dot(wsp_ref[g], vb[c * sp_len:(c + 1) * sp_len, g * LANES:(g + 1) * LANES])
                      for g in range(n_groups)]
        row_blocks.append(jnp.concatenate(col_blocks, axis=1) + bsp_ref[...])
    s = row_blocks[0] if len(row_blocks) == 1 else jnp.concatenate(row_blocks, axis=0)
    oa = (u * s) * _silu(z_ag)
    merged_scr[...] = _sigmoid(z_g0) * _dot(oa.astype(BF16), wbr_ref[0:d_a, :])

    @pl.when(t == 0)
    def _():
        conv_ref[...] = conv0_ref[...]
        h_ref[...] = h0_ref[...]

    bx = zin("b_x")
    n_seq = tm // seq_len
    span = SUBLANES + seq_len
    stacked = jnp.concatenate(
        [piece for q in range(n_seq) for piece in (conv_ref[q], bx[q * seq_len:(q + 1) * seq_len])], axis=0)
    xc = convb_ref[...]
    for k in range(CONV_W):
        shift = CONV_W - 1 - k
        rolled = stacked if shift == 0 else pltpu.roll(stacked, shift, axis=0)
        sh = [rolled[q * span + SUBLANES:(q + 1) * span] for q in range(n_seq)]
        xc = xc + (sh[0] if n_seq == 1 else jnp.concatenate(sh, axis=0)) * convw_ref[k:k + 1, :]
    for q in range(n_seq):
        conv_ref[q] = bx[(q + 1) * seq_len - SUBLANES:(q + 1) * seq_len]

    xcb = xc.astype(BF16)
    n_blk = d_b // MXU_TILE
    ri = [_dot(xcb[:, k * MXU_TILE:(k + 1) * MXU_TILE], wgate_ref[k]) for k in range(n_blk)]
    z_bg, z_g1, z_g2 = zin("b_g"), zin("g1"), zin("g2")
    z_mq, z_g3, z_c = zin("m_q"), zin("g3"), zin("c_all")
    cg_t = _dot_nt(wcgt_ref[...], xb)
    r = _sigmoid(jnp.concatenate([p[:, 0:MXU_TILE] for p in ri], axis=1) + br_ref[...])
    i_gate = _sigmoid(jnp.concatenate([p[:, MXU_TILE:2 * MXU_TILE] for p in ri], axis=1) + bi_ref[...])
    neg_lam = -lam_ref[...]
    softplus = jnp.maximum(neg_lam, 0.0) + jnp.log1p(jnp.exp(-jnp.abs(neg_lam)))
    log_a = (-LRU_C * r) * softplus
    a = jnp.exp(log_a)
    bval = jnp.sqrt(-_expm1_nonpos(2.0 * log_a)) * (i_gate * xc)
    in_group = lax.broadcasted_iota(jnp.int32, (tm, 1), 0) % SUBLANES
    d = 1
    while d < SUBLANES:
        keep = in_group >= d
        a_sh = pltpu.roll(a, d, axis=0)
        b_sh = pltpu.roll(bval, d, axis=0)
        bval = jnp.where(keep, a * b_sh + bval, bval)
        a = jnp.where(keep, a * a_sh, a)
        d *= 2
    groups_per_seq = seq_len // SUBLANES
    h_groups = []
    for g in range(tm // SUBLANES):
        q = g // groups_per_seq
        if g % groups_per_seq == 0:
            carry = h_ref[q]
        lo = g * SUBLANES
        h_g = a[lo:lo + SUBLANES] * carry + bval[lo:lo + SUBLANES]
        h_groups.append(h_g)
        carry = h_g[SUBLANES - 1:SUBLANES]
        if (g + 1) % groups_per_seq == 0:
            h_ref[q] = carry
    h = jnp.concatenate(h_groups, axis=0)
    ob = h * _silu(z_bg)
    merged_scr[...] += _sigmoid(z_g1) * _dot(ob.astype(BF16), wbr_ref[d_a:d_a + d_b, :])
    g2_ref[0] = _sigmoid(z_g2).astype(BF16)

    mq = z_mq.astype(BF16)
    lane = lax.broadcasted_iota(jnp.int32, (1, LANES), 1)
    om_rows = []
    for q in range(n_seq):
        mkb = mk_ref[q].astype(BF16)
        mvb = mv_ref[q].astype(BF16)
        slabs = []
        for p in range(d_m // LANES):
            mq_p = mq[q * seq_len:(q + 1) * seq_len, p * LANES:(p + 1) * LANES]
            mk_p = mkb[:, p * LANES:(p + 1) * LANES]
            mv_p = mvb[:, p * LANES:(p + 1) * LANES]
            acc = None
            for half in range(LANES // DH_M):
                sel = (lane >= half * DH_M) & (lane < (half + 1) * DH_M)
                sc = _dot_nt(mq_p, jnp.where(sel, mk_p, jnp.zeros_like(mk_p))) * (DH_M ** -0.5)
                e = jnp.exp(sc - jnp.max(sc, -1, keepdims=True))
                prob = (e / jnp.sum(e, -1, keepdims=True)).astype(BF16)
                o = _dot(prob, jnp.where(sel, mv_p, jnp.zeros_like(mv_p)))
                acc = o if acc is None else acc + o
            slabs.append(acc)
        om_rows.append(jnp.concatenate(slabs, axis=1))
    om = om_rows[0] if n_seq == 1 else jnp.concatenate(om_rows, axis=0)
    m_lo = d_a + d_b + d_c
    partial_ref[0] = (merged_scr[...]
                      + _sigmoid(z_g3) * _dot(om.astype(BF16), wbr_ref[m_lo:m_lo + d_m, :])).astype(BF16)
    cgst_ref[0] = _silu(cg_t)

    ckv = _rms_norm(z_c[:, q_lora:q_lora + kv_lora], kvnorm_ref[...])
    half = D_ROPE // 2
    z_kr = z_c[:, q_lora + kv_lora:q_lora + kv_lora + LANES]
    kr_swapped = jnp.where(lane < half, pltpu.roll(z_kr, LANES - half, axis=1), pltpu.roll(z_kr, half, axis=1))
    kr = z_kr[:, 0:D_ROPE] * ck_ref[...] + kr_swapped[:, 0:D_ROPE] * sk_ref[...]
    ckv_ref[0] = ckv
    kr_ref[0] = kr
    if first_chunk:
        kcat_ref[0], vt_ref[0, 0] = _keys_values(ckv, kr, wkc_ref, wkr_ref, wvt_ref)
    cqn = _rms_norm(z_c[:, 0:q_lora], qnorm_ref[...]).astype(BF16)
    q_t = _dot_nt(wuqt_ref[...], cqn)
    cos_t, sin_t = cosq_ref[...], sinq_ref[...]
    for hd in range(H_C):
        lo = hd * LANES
        x1 = q_t[lo + D_NOPE:lo + D_NOPE + half]
        x2 = q_t[lo + D_NOPE + half:lo + D_NOPE + D_ROPE]
        qt_ref[0, 0, lo:lo + D_NOPE, :] = q_t[lo:lo + D_NOPE].astype(BF16)
        qt_ref[0, 0, lo + D_NOPE:lo + D_NOPE + half, :] = (x1 * cos_t - x2 * sin_t).astype(BF16)
        qt_ref[0, 0, lo + D_NOPE + half:lo + D_NOPE + D_ROPE, :] = (x1 * sin_t + x2 * cos_t).astype(BF16)
        qt_ref[0, 0, lo + D_NOPE + D_ROPE:lo + LANES, :] = jnp.zeros((LANES - D_NOPE - D_ROPE, tm), BF16)


def _branch_call(x, pw, l, sp, conv0_pad, h0, mk, mv, rope, *, tm, seq_len, tq, first_chunk):
    bsz, seq, d_model = x.shape
    d_a, d_b, d_c, d_m = pw["dims"]
    q_lora, kv_lora = pw["q_lora"], pw["kv_lora"]
    n_mem = mk.shape[1]
    cos_t, sin_t, ck_t, sk_t = rope
    grid = (bsz, seq // tm)

    def row_spec(width):
        return pl.BlockSpec((1, tm, width), lambda b, t: (b, t, 0))

    def col_spec(height):
        return pl.BlockSpec((1, height, tm), lambda b, t: (b, 0, t))

    def tab_spec(width):
        return pl.BlockSpec((tm, width), lambda b, t: (t, 0))

    n_seq = tm // seq_len

    def batch_spec(rows, width):
        return pl.BlockSpec((n_seq, rows, width), lambda b, t: (b, 0, 0))

    names = ["w1", "wcgt", "gln_g", "gln_b", "wsp%d" % sp, "bsp%d" % sp, "conv_w", "conv_b",
             "wgate", "b_r", "b_i", "lam"]
    names2 = ["q_norm", "wuqt", "kv_norm"]
    in_specs = ([row_spec(d_model)] + [_layer_spec(pw[n], l) for n in names]
                + [batch_spec(SUBLANES, d_b), batch_spec(1, d_b)] + [_layer_spec(pw[n], l) for n in names2]
                + [pl.BlockSpec((D_ROPE // 2, tm), lambda b, t: (0, t)), pl.BlockSpec((D_ROPE // 2, tm), lambda b, t: (0, t)),
                   tab_spec(D_ROPE), tab_spec(D_ROPE), batch_spec(n_mem, d_m), batch_spec(n_mem, d_m),
                   _layer_spec(pw["wbr"], l)])
    operands = ([x] + [pw[n] for n in names] + [conv0_pad, h0] + [pw[n] for n in names2]
                + [cos_t, sin_t, ck_t, sk_t, mk, mv, pw["wbr"]])
    out_shape = [
        jax.ShapeDtypeStruct((bsz, seq, d_model), BF16),
        jax.ShapeDtypeStruct((bsz, seq, d_model), BF16),
        jax.ShapeDtypeStruct((bsz, d_c, seq), F32),
        jax.ShapeDtypeStruct((bsz, seq // tq, H_C * LANES, tq), BF16),
        jax.ShapeDtypeStruct((bsz, seq, kv_lora), F32),
        jax.ShapeDtypeStruct((bsz, seq, D_ROPE), F32),
        jax.ShapeDtypeStruct((bsz * n_seq, SUBLANES, d_b), F32),
        jax.ShapeDtypeStruct((bsz * n_seq, 1, d_b), F32),
    ]
    out_specs = [
        row_spec(d_model), row_spec(d_model), col_spec(d_c),
        pl.BlockSpec((1, 1, H_C * LANES, tm), lambda b, t: (b, t // (tq // tm), 0, t % (tq // tm))),
        row_spec(kv_lora), row_spec(D_ROPE), batch_spec(SUBLANES, d_b), batch_spec(1, d_b),
    ]
    if first_chunk:
        for name in ("wkc", "wkr", "wvt"):
            in_specs.append(_layer_spec(pw[name], l))
            operands.append(pw[name])
        out_shape += [jax.ShapeDtypeStruct((bsz, seq, H_C * LANES), BF16),
                      jax.ShapeDtypeStruct((bsz, seq // tm, H_C * V_ROWS, tm), BF16)]
        out_specs += [row_spec(H_C * LANES),
                      pl.BlockSpec((1, 1, H_C * V_ROWS, tm), lambda b, t: (b, t, 0, 0))]
    else:
        out_shape.append(jax.ShapeDtypeStruct((bsz, seq, d_a), F32))
        out_specs.append(row_spec(d_a))
    kern = functools.partial(
        _branch_kernel, first_chunk=first_chunk, tm=tm, seq_len=seq_len, sp_len=sp, d_a=d_a, d_b=d_b, d_c=d_c, d_m=d_m,
        q_lora=q_lora, kv_lora=kv_lora, cols=pw["cols"])
    return pl.pallas_call(
        kern, grid=grid, in_specs=in_specs, out_specs=out_specs, out_shape=out_shape,
        scratch_shapes=[pltpu.VMEM((tm, d_model), F32)],
        name="branch", compiler_params=_params(2),
    )(*operands)


HEADS_PER_STEP = 2
_EXP2_SCALE = (D_NOPE + D_ROPE) ** -0.5 * math.log2(math.e)


def _scores(k_tile, q_tile, out_refs=None):
    res = [_dot(k_tile[:, h * LANES:(h + 1) * LANES], q_tile[h * LANES:(h + 1) * LANES, :])
           for h in range(k_tile.shape[1] // LANES)]
    if out_refs is None:
        return res
    for h, r in enumerate(res):
        out_refs[h] = r
    return None


def _softmax_tile(s_t, m_old):
    m_new = jnp.maximum(m_old, jnp.max(s_t, axis=0, keepdims=True))
    return jnp.exp2(s_t - m_new).astype(BF16), m_new, jnp.exp2(m_old - m_new)


def _attn_output(acc, cgs_t):
    return (acc[0:D_V] / acc[D_V:D_V + 1] * cgs_t).astype(BF16)


N_BUF = 4


def _attn_causal_kernel(qt_ref, k_ref, vt_ref, cgst_ref, bias_ref, o_ref, *scratch, tq, tk):
    s_buf, e_buf, acc_scr = scratch[0:N_BUF], scratch[N_BUF:2 * N_BUF], scratch[2 * N_BUF]
    i = pl.program_id(2)
    nh = HEADS_PER_STEP
    q_cur = qt_ref.at[0, i]
    q_nxt = qt_ref.at[0, jnp.minimum(i + 1, pl.num_programs(2) - 1)]

    def k_tile(j):
        return k_ref[0, pl.ds(pl.multiple_of(j * tk, tk), tk), :]

    def pv(j, e_in, c0):
        vt = vt_ref[0, j]
        return [_dot(vt[h * V_ROWS:(h + 1) * V_ROWS, :], e_in[h, :, c0:tq]) for h in range(nh)]

    def acc_update(pvs, alphas, c0):
        for h in range(nh):
            acc_scr[h, :, c0:tq] = acc_scr[h, :, c0:tq] * alphas[h][:, c0:tq] + pvs[h]

    def softmax(slot, carry, c0, add_bias):
        new = []
        for h in range(nh):
            m_old, a_prev, _ = carry[h]
            s_t = s_buf[slot][h, :, c0:tq]
            if add_bias:
                lead = s_t[:, 0:tk] + bias_ref[...]
                s_t = lead if c0 + tk == tq else jnp.concatenate([lead, s_t[:, tk:]], axis=1)
            e, m_new, alpha = _softmax_tile(s_t, m_old[:, c0:tq])
            e_buf[slot][h, :, c0:tq] = e
            if c0:
                m_new = jnp.concatenate([m_old[:, 0:c0], m_new], axis=1)
                alpha = jnp.concatenate([jnp.ones((1, c0), F32), alpha], axis=1)
            new.append((m_new, alpha, a_prev))
        return tuple(new)

    def stage(n, slot, carry):
        pvs = pv(jnp.maximum(n - 2, 0), e_buf[(slot + 2) % N_BUF], 0)
        _scores(k_tile(n + 3), q_cur, s_buf[(slot + 3) % N_BUF])
        new = softmax(slot, carry, 0, False)
        acc_update(pvs, [c[2] for c in carry], 0)
        return new

    for slot in (N_BUF - 2, N_BUF - 1):
        e_buf[slot][...] = jnp.zeros(e_buf[slot].shape, BF16)
    acc_scr[...] = jnp.zeros(acc_scr.shape, F32)

    @pl.when(i == 0)
    def _():
        for slot in range(N_BUF - 1):
            _scores(k_tile(slot), q_cur, s_buf[slot])

    ones = jnp.ones((1, tq), F32)
    carry = tuple((jnp.full((1, tq), -jnp.inf, F32), ones, ones) for _ in range(nh))

    def body(t, carry):
        for slot in range(N_BUF):
            carry = stage(N_BUF * t + slot, slot, carry)
        return carry

    carry = lax.fori_loop(0, i, body, carry)

    j0 = N_BUF * i
    last = N_BUF - 1
    for d in range(N_BUF):
        c_prev = max(d - 2, 0) * tk
        pvs = pv(jnp.maximum(j0 + d - 2, 0), e_buf[(d + 2) % N_BUF], c_prev)
        if d == 0:
            k_last = k_tile(j0 + last)
            for h in range(nh):
                s_buf[last][h, :, tq - tk:tq] = _dot(
                    k_last[:, h * LANES:(h + 1) * LANES],
                    q_cur[h * LANES:(h + 1) * LANES, tq - tk:tq]) + bias_ref[...]
        else:
            _scores(k_tile(d - 1), q_nxt, s_buf[d - 1])
        alphas = [c[2] for c in carry]
        carry = softmax(d, carry, d * tk, d < last)
        acc_update(pvs, alphas, c_prev)
    acc_update(pv(j0 + last - 1, e_buf[last - 1], (last - 1) * tk), [c[2] for c in carry], (last - 1) * tk)
    acc_update(pv(j0 + last, e_buf[last], last * tk), [c[1] for c in carry], last * tk)
    for h in range(nh):
        o_ref[0, h * D_V:(h + 1) * D_V, :] = _attn_output(acc_scr[h], cgst_ref[0, h * D_V:(h + 1) * D_V, :])


def _attn_cached_kernel(qt_ref, pckv_ref, pkr_ref, ckv_ref, kr_ref, cgst_ref, wkc_ref, wkr_ref, wvt_ref, o_ref):
    ckv = jnp.concatenate([pckv_ref[0], ckv_ref[0]], axis=0)
    kr = jnp.concatenate([pkr_ref[0], kr_ref[0]], axis=0)
    kcat, vt = _keys_values(ckv, kr, wkc_ref, wkr_ref, wvt_ref)
    tq = qt_ref.shape[-1]
    scores = _scores(kcat, qt_ref.at[0])
    for h in range(len(scores)):
        e, _, _ = _softmax_tile(scores[h], jnp.full((1, tq), -jnp.inf, F32))
        acc = _dot(vt[h * V_ROWS:(h + 1) * V_ROWS, :], e)
        o_ref[0, h * D_V:(h + 1) * D_V, :] = _attn_output(acc, cgst_ref[0, h * D_V:(h + 1) * D_V, :])


def _attn_causal_call(qt, kcat, vt, cgst):
    bsz, n_qt, _, tq = qt.shape
    seq = n_qt * tq
    t_k = kcat.shape[1]
    nkt, _, tk = vt.shape[1:]
    nh = HEADS_PER_STEP
    assert tq == N_BUF * tk and seq == t_k and tk % CHUNK == 0
    chunk = jnp.arange(tk) // CHUNK
    bias = jnp.where(chunk[:, None] <= chunk[None, :], 0.0, -1e30).astype(F32)
    return pl.pallas_call(
        functools.partial(_attn_causal_kernel, tq=tq, tk=tk), grid=(bsz, H_C // nh, n_qt),
        in_specs=[pl.BlockSpec((1, n_qt, nh * LANES, tq), lambda b, p, i: (b, 0, p, 0)),
                  pl.BlockSpec((1, t_k, nh * LANES), lambda b, p, i: (b, 0, p)),
                  pl.BlockSpec((1, nkt, nh * V_ROWS, tk), lambda b, p, i: (b, 0, p, 0)),
                  pl.BlockSpec((1, nh * D_V, tq), lambda b, p, i: (b, p, i)),
                  _const_spec((tk, tk))],
        out_specs=pl.BlockSpec((1, nh * D_V, tq), lambda b, p, i: (b, p, i)),
        out_shape=jax.ShapeDtypeStruct((bsz, H_C * D_V, seq), BF16),
        scratch_shapes=([pltpu.VMEM((nh, tk, tq), F32)] * N_BUF + [pltpu.VMEM((nh, tk, tq), BF16)] * N_BUF
                        + [pltpu.VMEM((nh, V_ROWS, tq), F32)]),
        name="attn", compiler_params=_params(3),
    )(qt, kcat, vt, cgst, bias)


def _attn_cached_call(qt, past_ckv, past_kr, ckv_new, kr_new, cgst, pw, l):
    bsz, n_q, tq = qt.shape
    past_len, kv_lora = past_ckv.shape[2:]
    return pl.pallas_call(
        _attn_cached_kernel, grid=(bsz,),
        in_specs=[pl.BlockSpec((1, n_q, tq), lambda b: (b, 0, 0)),
                  pl.BlockSpec((None, 1, past_len, kv_lora), lambda b: (l, b, 0, 0)),
                  pl.BlockSpec((None, 1, past_len, D_ROPE), lambda b: (l, b, 0, 0)),
                  pl.BlockSpec((1, tq, kv_lora), lambda b: (b, 0, 0)),
                  pl.BlockSpec((1, tq, D_ROPE), lambda b: (b, 0, 0)),
                  pl.BlockSpec((1, H_C * D_V, tq), lambda b: (b, 0, 0)),
                  _layer_spec(pw["wkc"], l), _layer_spec(pw["wkr"], l), _layer_spec(pw["wvt"], l)],
        out_specs=pl.BlockSpec((1, H_C * D_V, tq), lambda b: (b, 0, 0)),
        out_shape=jax.ShapeDtypeStruct((bsz, H_C * D_V, tq), BF16),
        name="attn_cached", compiler_params=_params(1),
    )(qt, past_ckv, past_kr, ckv_new, kr_new, cgst, pw["wkc"], pw["wkr"], pw["wvt"])


def _merge_kernel(x_ref, partial_ref, g2_ref, oct_ref, wbr_ref, wout_ref, lng_ref, lnb_ref, o_ref,
                  *, c_lo, d_c, alpha):
    yc = _dot_tn(oct_ref[0], wbr_ref[c_lo:c_lo + d_c, :])
    merged = partial_ref[0].astype(F32) + g2_ref[0].astype(F32) * yc
    y = _dot(merged.astype(BF16), wout_ref[...])
    o_ref[0] = _layer_norm(alpha * x_ref[0] + y, lng_ref[...], lnb_ref[...])


def _merge_call(x, partial, g2, oct, pw, l, *, tr, alpha):
    bsz, seq, d_model = x.shape
    d_a, d_b, d_c, _ = pw["dims"]
    kern = functools.partial(_merge_kernel, c_lo=d_a + d_b, d_c=d_c, alpha=alpha)

    def row_spec(width):
        return pl.BlockSpec((1, tr, width), lambda b, i: (b, i, 0))

    return pl.pallas_call(
        kern, grid=(bsz, seq // tr),
        in_specs=[row_spec(d_model), row_spec(d_model), row_spec(d_model),
                  pl.BlockSpec((1, d_c, tr), lambda b, i: (b, 0, i)),
                  _layer_spec(pw["wbr"], l), _layer_spec(pw["wout"], l),
                  _layer_spec(pw["ln_g"], l), _layer_spec(pw["ln_b"], l)],
        out_specs=row_spec(d_model),
        out_shape=jax.ShapeDtypeStruct((bsz, seq, d_model), F32),
        name="merge", compiler_params=_params(2),
    )(x, partial, g2, oct, pw["wbr"], pw["wout"], pw["ln_g"], pw["ln_b"])


def _mem_kernel(m_ref, w_ref, k_ref, v_ref, *, d_m):
    kv = _dot(m_ref[...].astype(BF16), w_ref[...])
    k_ref[...] = kv[:, 0:d_m]
    v_ref[...] = kv[:, d_m:2 * d_m]


def _mem_call(mem2d, wmem, l, *, tr):
    rows, d_model = mem2d.shape
    d_m = wmem.shape[2] // 2
    return pl.pallas_call(
        functools.partial(_mem_kernel, d_m=d_m), grid=(rows // tr,),
        in_specs=[pl.BlockSpec((tr, d_model), lambda i: (i, 0)), _layer_spec(wmem, l)],
        out_specs=[pl.BlockSpec((tr, d_m), lambda i: (i, 0)), pl.BlockSpec((tr, d_m), lambda i: (i, 0))],
        out_shape=[jax.ShapeDtypeStruct((rows, d_m), F32), jax.ShapeDtypeStruct((rows, d_m), F32)],
        name="mem", compiler_params=_params(1),
    )(mem2d, wmem)


def _block_diag(w):
    h, n, _ = w.shape
    eye = jnp.eye(h, dtype=w.dtype)
    return (eye[:, None, :, None] * w[:, :, None, :]).reshape(h * n, h * n)


def _gate_blocks(w_r, w_i):
    per = MXU_TILE // w_r.shape[1]
    blocks = []
    for k in range(w_r.shape[0] // per):
        blocks.append(jnp.concatenate([_block_diag(w_r[k * per:(k + 1) * per]),
                                       _block_diag(w_i[k * per:(k + 1) * per])], axis=1))
    return jnp.stack(blocks)


def _w_in_kernel(wt_ref, o_ref, *, n_head, o_tail):
    o_ref[0, :, 0:n_head] = jnp.transpose(wt_ref[0, 0:n_head, :]).astype(BF16)
    o_ref[0, :, n_head:] = jnp.transpose(wt_ref[0, o_tail:, :]).astype(BF16)


def _w_in_call(w_in, n_head, o_tail, *, tk):
    depth, d_model, d_in = w_in.shape
    n_out = n_head + d_in - o_tail
    return pl.pallas_call(
        functools.partial(_w_in_kernel, n_head=n_head, o_tail=o_tail), grid=(depth, d_model // tk),
        in_specs=[pl.BlockSpec((1, d_in, tk), lambda l, i: (l, 0, i))],
        out_specs=pl.BlockSpec((1, tk, n_out), lambda l, i: (l, i, 0)),
        out_shape=jax.ShapeDtypeStruct((depth, d_model, n_out), BF16),
        name="w_in", compiler_params=_params(2),
    )(jnp.swapaxes(w_in, 1, 2))


def _prep_weights(sp_lens, w_in, gmlp_ln_g, gmlp_ln_b, gmlp_ws, gmlp_bs, lru_conv_w,
                  lru_conv_b, lru_w_r, lru_b_r, lru_w_i, lru_b_i, lru_lambda, mla_q_norm, mla_w_uq,
                  mla_kv_norm, mla_w_ukv, mem_w_k, mem_w_v, w_br, w_out, ln_g, ln_b):
    depth, d_model, _ = w_in.shape
    d_a = gmlp_ln_g.shape[1]
    d_b = lru_lambda.shape[1]
    q_lora = mla_q_norm.shape[1]
    kv_lora = mla_kv_norm.shape[1]
    d_c = H_C * D_V
    d_m = H_M * DH_M
    o_cq = 3 * d_a + 2 * d_b
    o_kr = o_cq + q_lora + kv_lora
    o_cg = o_kr + D_ROPE
    o_mq = o_cg + d_c
    o_g = o_mq + d_m
    n_head = o_kr + LANES
    assert o_kr % LANES == 0 and (o_g - o_mq) % LANES == 0 and D_ROPE <= LANES and n_head <= o_mq
    cols = {"a_u": (0, d_a), "a_v": (d_a, 2 * d_a), "a_g": (2 * d_a, 3 * d_a),
            "b_x": (3 * d_a, 3 * d_a + d_b), "b_g": (3 * d_a + d_b, o_cq), "c_all": (o_cq, n_head),
            "m_q": (n_head, n_head + d_m)}
    for k in range(N_BRANCH):
        cols[f"g{k}"] = (n_head + d_m + k * d_model, n_head + d_m + (k + 1) * d_model)
    w1 = _w_in_call(w_in, n_head, o_mq, tk=LANES)
    wcgt = jnp.swapaxes(w_in[:, :, o_cg:o_mq], 1, 2).astype(BF16)

    wq = mla_w_uq.reshape(depth, q_lora, H_C, D_NOPE + D_ROPE) * _EXP2_SCALE
    wq = jnp.concatenate([wq, jnp.zeros((depth, q_lora, H_C, LANES - D_NOPE - D_ROPE), wq.dtype)], axis=3)
    wuqt = jnp.swapaxes(wq.reshape(depth, q_lora, H_C * LANES), 1, 2).astype(BF16)

    wkv = mla_w_ukv.reshape(depth, kv_lora, H_C, D_NOPE + D_V)
    wk = wkv[..., :D_NOPE]
    wv = wkv[..., D_NOPE:]
    wkc = jnp.concatenate([wk, jnp.zeros((depth, kv_lora, H_C, LANES - D_NOPE), wk.dtype)], axis=3)
    wkc = wkc.reshape(depth, kv_lora, H_C * LANES).astype(BF16)
    lane = jnp.arange(H_C * LANES) % LANES
    wkr = ((lane[None, :] - D_NOPE) == jnp.arange(D_ROPE)[:, None]).astype(BF16)
    wkr = jnp.broadcast_to(wkr, (depth,) + wkr.shape)
    wv = jnp.concatenate([wv, jnp.zeros((depth, kv_lora, H_C, V_ROWS - D_V), wv.dtype)], axis=3)
    wvt = jnp.swapaxes(wv.reshape(depth, kv_lora, H_C * V_ROWS), 1, 2).astype(BF16)

    def row(p):
        return p[:, None, :]

    pw = dict(
        dims=(d_a, d_b, d_c, d_m), q_lora=q_lora, kv_lora=kv_lora, cols=cols,
        w1=w1, wcgt=wcgt,
        gln_g=row(gmlp_ln_g), gln_b=row(gmlp_ln_b),
        conv_w=lru_conv_w, conv_b=row(lru_conv_b),
        wgate=jnp.stack([_gate_blocks(lru_w_r[l], lru_w_i[l]) for l in range(depth)]).astype(BF16),
        b_r=row(lru_b_r), b_i=row(lru_b_i), lam=row(lru_lambda),
        q_norm=row(mla_q_norm), wuqt=wuqt, kv_norm=row(mla_kv_norm),
        wkc=wkc, wkr=wkr, wvt=wvt,
        wbr=w_br.astype(BF16), wout=w_out.astype(BF16), ln_g=row(ln_g), ln_b=row(ln_b),
        wmem=jnp.concatenate([mem_w_k, mem_w_v], axis=2).astype(BF16),
    )
    for sp in sp_lens:
        pw["wsp%d" % sp] = jnp.tril(gmlp_ws[:, :, :sp, :sp]).astype(BF16)
        pw["bsp%d" % sp] = jnp.repeat(jnp.swapaxes(gmlp_bs[:, :, :sp], 1, 2), d_a // G_A, axis=2)
    return pw


def _rope_tables(pos):
    half = D_ROPE // 2
    freq = ROPE_BASE ** (-jnp.arange(half, dtype=F32) / half)
    ang = pos.astype(F32)[:, None] * freq[None, :]
    cos, sin = jnp.cos(ang), jnp.sin(ang)
    return (jnp.transpose(cos), jnp.transpose(sin),
            jnp.concatenate([cos, cos], axis=1), jnp.concatenate([-sin, sin], axis=1))


def _pad_conv_state(conv):
    return jnp.pad(conv, ((0, 0), (SUBLANES - (CONV_W - 1), 0), (0, 0)))


def _trunk_layer(x, pw, l, sp, rope, conv0, h0, mk, mv, past_ckv, past_kr, *, tm, alpha):
    bsz, seq, d_model = x.shape
    first_chunk = past_ckv is None
    if first_chunk:
        x_rows, seq_len, tq = x, tm, N_BUF * tm
    else:
        x_rows, seq_len, tq = x.reshape(1, bsz * seq, d_model), seq, bsz * seq
        cos_t, sin_t, ck_t, sk_t = rope
        rope = (jnp.tile(cos_t, (1, bsz)), jnp.tile(sin_t, (1, bsz)), jnp.tile(ck_t, (bsz, 1)), jnp.tile(sk_t, (bsz, 1)))
    outs = _branch_call(x_rows, pw, l, sp, _pad_conv_state(conv0), h0[:, None, :], mk, mv, rope,
                        tm=tm, seq_len=seq_len, tq=tq, first_chunk=first_chunk)
    partial, g2, cgst, qt, ckv_new, kr_new, conv_pad, h_new = outs[:8]
    if first_chunk:
        kcat, vt = outs[8:]
        v_rows = None
        oct = _attn_causal_call(qt, kcat, vt, cgst)
    else:
        ckv_new, kr_new, v_rows = (o.reshape(bsz, seq, -1) for o in (ckv_new, kr_new, outs[8]))

        def per_seq(a):
            return jnp.transpose(a.reshape(a.shape[-2], bsz, seq), (1, 0, 2))

        oct = _attn_cached_call(per_seq(qt), past_ckv, past_kr, ckv_new, kr_new, per_seq(cgst), pw, l)
        oct = jnp.transpose(oct, (1, 0, 2)).reshape(1, oct.shape[1], bsz * seq)
    x_new = _merge_call(x_rows, partial, g2, oct, pw, l, tr=MERGE_TILE if first_chunk else tm, alpha=alpha)
    return (x_new.reshape(bsz, seq, d_model), v_rows, conv_pad[:, SUBLANES - (CONV_W - 1):], h_new[:, 0],
            ckv_new, kr_new)


def kernel(x_prompt, x_sample, mem_prompt, cache_mla_ckv, cache_mla_krope, cache_mem_k, cache_mem_v,
           state_lru_h, state_lru_conv, w_in, gmlp_ln_g, gmlp_ln_b, gmlp_ws, gmlp_bs,
           lru_conv_w, lru_conv_b, lru_w_r, lru_b_r, lru_w_i, lru_b_i, lru_lambda,
           mla_q_norm, mla_w_uq, mla_kv_norm, mla_w_ukv, mem_w_k, mem_w_v, w_br, w_out, ln_g, ln_b):
    bp, tp, d_model = x_prompt.shape
    bs, ts, _ = x_sample.shape
    depth = w_in.shape[0]
    past_len = cache_mla_ckv.shape[2]
    n_mem = mem_prompt.shape[1]
    d_b = lru_lambda.shape[1]
    alpha = (2.0 * depth) ** 0.25
    sp_p, sp_s = min(tp, A_CHUNK), min(ts, A_CHUNK)
    pw = _prep_weights(sorted({sp_p, sp_s}), w_in, gmlp_ln_g, gmlp_ln_b, gmlp_ws, gmlp_bs, lru_conv_w,
                       lru_conv_b, lru_w_r, lru_b_r, lru_w_i, lru_b_i, lru_lambda, mla_q_norm, mla_w_uq,
                       mla_kv_norm, mla_w_ukv, mem_w_k, mem_w_v, w_br, w_out, ln_g, ln_b)
    rope_p = _rope_tables(jnp.arange(tp))
    rope_s = _rope_tables(past_len + jnp.arange(ts))
    tm_p = min(tp, ROW_TILE)
    assert tp % (N_BUF * tm_p) == 0 and tp % MERGE_TILE == 0 and tm_p % CHUNK == 0 and ts % SUBLANES == 0
    zero_conv = jnp.zeros((bp, CONV_W - 1, d_b), F32)
    zero_h = jnp.zeros((bp, d_b), F32)

    xp, xs = x_prompt, x_sample
    acc = [[] for _ in range(11)]
    for l in range(depth):
        mk, mv = _mem_call(mem_prompt.reshape(bp * n_mem, d_model), pw["wmem"], l, tr=n_mem)
        mk = mk.reshape(bp, n_mem, -1)
        mv = mv.reshape(bp, n_mem, -1)
        xp, _, conv_n, h_n, ckv_n, kr_n = _trunk_layer(
            xp, pw, l, sp_p, rope_p, zero_conv, zero_h, mk, mv, None, None, tm=tm_p, alpha=alpha)
        for k, val in zip(range(6), (ckv_n, kr_n, mk.reshape(bp, n_mem, H_M, DH_M),
                                     mv.reshape(bp, n_mem, H_M, DH_M), h_n, conv_n)):
            acc[k].append(val)
        xs, v_n, conv_n, h_n, ckv_n, kr_n = _trunk_layer(
            xs, pw, l, sp_s, rope_s, state_lru_conv[l], state_lru_h[l],
            cache_mem_k[l].reshape(bs, n_mem, -1), cache_mem_v[l].reshape(bs, n_mem, -1),
            cache_mla_ckv, cache_mla_krope, tm=bs * ts, alpha=alpha)
        for k, val in zip(range(6, 11), (ckv_n, kr_n, h_n, conv_n, v_n)):
            acc[k].append(val)
    return (xp, xs) + tuple(jnp.stack(a) for a in acc)
```

```python
import functools
import math

import jax
import jax.numpy as jnp
from jax import lax
from jax.experimental import pallas as pl
from jax.experimental.pallas import tpu as pltpu

CHUNK = 64
G_A = 4
A_CHUNK = 128
H_B = 8
CONV_W = 4
LRU_C = 8.0
H_C = 8
D_NOPE = 64
D_ROPE = 32
D_V = 64
ROPE_BASE = 10000.0
H_M = 4
DH_M = 64
N_BRANCH = 4
EPS = 1e-6

LANES = 128
SUBLANES = 8
VMEM_LIMIT = 56 * 1024 * 1024
ROW_TILE = 512
ATTN_TK = 256
MXU_TILE = 256
MERGE_TILE = 1024

F32 = jnp.float32
BF16 = jnp.bfloat16


def _dot(a, b):
    return jnp.dot(a, b, preferred_element_type=F32)


def _dot_nt(a, b):
    return lax.dot_general(a, b, (((1,), (1,)), ((), ())), preferred_element_type=F32)


def _dot_tn(a, b):
    return lax.dot_general(a, b, (((0,), (0,)), ((), ())), preferred_element_type=F32)


def _sigmoid(x):
    return 1.0 / (1.0 + jnp.exp(-x))


def _silu(x):
    return x * _sigmoid(x)


def _gelu(x):
    return jax.nn.gelu(x)


def _expm1_nonpos(x):
    u = jnp.exp(x)
    near = (u - 1.0) * x / jnp.log(jnp.where(u == 1.0, 2.0, jnp.maximum(u, 0.5)))
    return jnp.where(u == 1.0, x, jnp.where(u > 0.5, near, u - 1.0))


def _layer_norm(x, g, b):
    mu = jnp.mean(x, -1, keepdims=True)
    var = jnp.mean(jnp.square(x - mu), -1, keepdims=True)
    return (x - mu) * lax.rsqrt(var + EPS) * g + b


def _rms_norm(x, g):
    return x * lax.rsqrt(jnp.mean(jnp.square(x), -1, keepdims=True) + EPS) * g


def _const_spec(shape):
    nd = len(shape)
    return pl.BlockSpec(shape, lambda *_: (0,) * nd, pipeline_mode=pl.Buffered(1))


def _layer_spec(arr, l):
    nd = arr.ndim - 1
    return pl.BlockSpec((None,) + arr.shape[1:], lambda *_: (l,) + (0,) * nd, pipeline_mode=pl.Buffered(1))


def _params(n_axes):
    return pltpu.CompilerParams(dimension_semantics=("arbitrary",) * n_axes,
                                vmem_limit_bytes=VMEM_LIMIT)


V_ROWS = 80


def _keys_values(ckv, kr, wkc_ref, wkr_ref, wvt_ref):
    c = ckv.astype(BF16)
    kcat = (_dot(c, wkc_ref[...]) + _dot(kr.astype(BF16), wkr_ref[...])).astype(BF16)
    v_t = _dot_nt(wvt_ref[...], c)
    row = lax.broadcasted_iota(jnp.int32, (v_t.shape[0], 1), 0)
    return kcat, jnp.where(row % V_ROWS == D_V, 1.0, v_t).astype(BF16)


def _branch_kernel(x_ref, w1_ref, wcgt_ref, glng_ref, glnb_ref, wsp_ref, bsp_ref,
                   convw_ref, convb_ref, wgate_ref, br_ref, bi_ref, lam_ref, conv0_ref, h0_ref, qnorm_ref, wuqt_ref,
                   kvnorm_ref, cosq_ref, sinq_ref, ck_ref, sk_ref, mk_ref, mv_ref, wbr_ref,
                   *rest, first_chunk, tm, seq_len, sp_len, d_a, d_b, d_c, d_m, q_lora, kv_lora, cols):
    if first_chunk:
        (wkc_ref, wkr_ref, wvt_ref, partial_ref, g2_ref, cgst_ref, qt_ref, ckv_ref, kr_ref, conv_ref, h_ref,
         kcat_ref, vt_ref, merged_scr) = rest
    else:
        partial_ref, g2_ref, cgst_ref, qt_ref, ckv_ref, kr_ref, conv_ref, h_ref, v_ref, merged_scr = rest
    t = pl.program_id(1)
    xb = x_ref[0].astype(BF16)

    def zin(name):
        lo, hi = cols[name]
        return _dot(xb, w1_ref[:, lo:hi])

    z_u, z_v, z_ag, z_g0 = zin("a_u"), zin("a_v"), zin("a_g"), zin("g0")
    u = _gelu(z_u)
    v = _layer_norm(_gelu(z_v), glng_ref[...], glnb_ref[...])
    if not first_chunk:
        v_ref[0] = v
    vb = v.astype(BF16)
    n_groups = d_a // LANES
    row_blocks = []
    for c in range(tm // sp_len):
        col_blocks = [_dot(wsp_ref[g], vb[c * sp_len:(c + 1) * sp_len, g * LANES:(g + 1) * LANES])
                      for g in range(n_groups)]
        row_blocks.append(jnp.concatenate(col_blocks, axis=1) + bsp_ref[...])
    s = row_blocks[0] if len(row_blocks) == 1 else jnp.concatenate(row_blocks, axis=0)
    oa = (u * s) * _silu(z_ag)
    merged_scr[...] = _sigmoid(z_g0) * _dot(oa.astype(BF16), wbr_ref[0:d_a, :])

    @pl.when(t == 0)
    def _():
        conv_ref[...] = conv0_ref[...]
        h_ref[...] = h0_ref[...]

    bx = zin("b_x")
    n_seq = tm // seq_len
    span = SUBLANES + seq_len
    stacked = jnp.concatenate(
        [piece for q in range(n_seq) for piece in (conv_ref[q], bx[q * seq_len:(q + 1) * seq_len])], axis=0)
    xc = convb_ref[...]
    for k in range(CONV_W):
        shift = CONV_W - 1 - k
        rolled = stacked if shift == 0 else pltpu.roll(stacked, shift, axis=0)
        sh = [rolled[q * span + SUBLANES:(q + 1) * span] for q in range(n_seq)]
        xc = xc + (sh[0] if n_seq == 1 else jnp.concatenate(sh, axis=0)) * convw_ref[k:k + 1, :]
    for q in range(n_seq):
        conv_ref[q] = bx[(q + 1) * seq_len - SUBLANES:(q + 1) * seq_len]

    xcb = xc.astype(BF16)
    n_blk = d_b // MXU_TILE
    ri = [_dot(xcb[:, k * MXU_TILE:(k + 1) * MXU_TILE], wgate_ref[k]) for k in range(n_blk)]
    z_bg, z_g1, z_g2 = zin("b_g"), zin("g1"), zin("g2")
    z_mq, z_g3, z_c = zin("m_q"), zin("g3"), zin("c_all")
    cg_t = _dot_nt(wcgt_ref[...], xb)
    r = _sigmoid(jnp.concatenate([p[:, 0:MXU_TILE] for p in ri], axis=1) + br_ref[...])
    i_gate = _sigmoid(jnp.concatenate([p[:, MXU_TILE:2 * MXU_TILE] for p in ri], axis=1) + bi_ref[...])
    neg_lam = -lam_ref[...]
    softplus = jnp.maximum(neg_lam, 0.0) + jnp.log1p(jnp.exp(-jnp.abs(neg_lam)))
    log_a = (-LRU_C * r) * softplus
    a = jnp.exp(log_a)
    bval = jnp.sqrt(-_expm1_nonpos(2.0 * log_a)) * (i_gate * xc)
    in_group = lax.broadcasted_iota(jnp.int32, (tm, 1), 0) % SUBLANES
    d = 1
    while d < SUBLANES:
        keep = in_group >= d
        a_sh = pltpu.roll(a, d, axis=0)
        b_sh = pltpu.roll(bval, d, axis=0)
        bval = jnp.where(keep, a * b_sh + bval, bval)
        a = jnp.where(keep, a * a_sh, a)
        d *= 2
    groups_per_seq = seq_len // SUBLANES
    h_groups = []
    for g in range(tm // SUBLANES):
        q = g // groups_per_seq
        if g % groups_per_seq == 0:
            carry = h_ref[q]
        lo = g * SUBLANES
        h_g = a[lo:lo + SUBLANES] * carry + bval[lo:lo + SUBLANES]
        h_groups.append(h_g)
        carry = h_g[SUBLANES - 1:SUBLANES]
        if (g + 1) % groups_per_seq == 0:
            h_ref[q] = carry
    h = jnp.concatenate(h_groups, axis=0)
    ob = h * _silu(z_bg)
    merged_scr[...] += _sigmoid(z_g1) * _dot(ob.astype(BF16), wbr_ref[d_a:d_a + d_b, :])
    g2_ref[0] = _sigmoid(z_g2).astype(BF16)

    mq = z_mq.astype(BF16)
    lane = lax.broadcasted_iota(jnp.int32, (1, LANES), 1)
    om_rows = []
    for q in range(n_seq):
        mkb = mk_ref[q].astype(BF16)
        mvb = mv_ref[q].astype(BF16)
        slabs = []
        for p in range(d_m // LANES):
            mq_p = mq[q * seq_len:(q + 1) * seq_len, p * LANES:(p + 1) * LANES]
            mk_p = mkb[:, p * LANES:(p + 1) * LANES]
            mv_p = mvb[:, p * LANES:(p + 1) * LANES]
            acc = None
            for half in range(LANES // DH_M):
                sel = (lane >= half * DH_M) & (lane < (half + 1) * DH_M)
                sc = _dot_nt(mq_p, jnp.where(sel, mk_p, jnp.zeros_like(mk_p))) * (DH_M ** -0.5)
                e = jnp.exp(sc - jnp.max(sc, -1, keepdims=True))
                prob = (e / jnp.sum(e, -1, keepdims=True)).astype(BF16)
                o = _dot(prob, jnp.where(sel, mv_p, jnp.zeros_like(mv_p)))
                acc = o if acc is None else acc + o
            slabs.append(acc)
        om_rows.append(jnp.concatenate(slabs, axis=1))
    om = om_rows[0] if n_seq == 1 else jnp.concatenate(om_rows, axis=0)
    m_lo = d_a + d_b + d_c
    partial_ref[0] = (merged_scr[...]
                      + _sigmoid(z_g3) * _dot(om.astype(BF16), wbr_ref[m_lo:m_lo + d_m, :])).astype(BF16)
    cgst_ref[0] = _silu(cg_t)

    ckv = _rms_norm(z_c[:, q_lora:q_lora + kv_lora], kvnorm_ref[...])
    half = D_ROPE // 2
    z_kr = z_c[:, q_lora + kv_lora:q_lora + kv_lora + LANES]
    kr_swapped = jnp.where(lane < half, pltpu.roll(z_kr, LANES - half, axis=1), pltpu.roll(z_kr, half, axis=1))
    kr = z_kr[:, 0:D_ROPE] * ck_ref[...] + kr_swapped[:, 0:D_ROPE] * sk_ref[...]
    ckv_ref[0] = ckv
    kr_ref[0] = kr
    if first_chunk:
        kcat_ref[0], vt_all = _keys_values(ckv, kr, wkc_ref, wkr_ref, wvt_ref)
        for j in range(tm // ATTN_TK):
            vt_ref[0, j] = vt_all[:, j * ATTN_TK:(j + 1) * ATTN_TK]
    cqn =_rms_norm(z_c[:, 0:q_lora], qnorm_ref[...]).astype(BF16)
    q_t = _dot_nt(wuqt_ref[...], cqn)
    cos_t, sin_t = cosq_ref[...], sinq_ref[...]
    for hd in range(H_C):
        lo = hd * LANES
        x1 = q_t[lo + D_NOPE:lo + D_NOPE + half]
        x2 = q_t[lo + D_NOPE + half:lo + D_NOPE + D_ROPE]
        qt_ref[0, 0, lo:lo + D_NOPE, :] = q_t[lo:lo + D_NOPE].astype(BF16)
        qt_ref[0, 0, lo + D_NOPE:lo + D_NOPE + half, :] = (x1 * cos_t - x2 * sin_t).astype(BF16)
        qt_ref[0, 0, lo + D_NOPE + half:lo + D_NOPE + D_ROPE, :] = (x1 * sin_t + x2 * cos_t).astype(BF16)
        qt_ref[0, 0, lo + D_NOPE + D_ROPE:lo + LANES, :] = jnp.zeros((LANES - D_NOPE - D_ROPE, tm), BF16)


def _branch_call(x, pw, l, sp, conv0_pad, h0, mk, mv, rope, *, tm, seq_len, tq, first_chunk):
    bsz, seq, d_model = x.shape
    d_a, d_b, d_c, d_m = pw["dims"]
    q_lora, kv_lora = pw["q_lora"], pw["kv_lora"]
    n_mem = mk.shape[1]
    cos_t, sin_t, ck_t, sk_t = rope
    grid = (bsz, seq // tm)

    def row_spec(width):
        return pl.BlockSpec((1, tm, width), lambda b, t: (b, t, 0))

    def col_spec(height):
        return pl.BlockSpec((1, height, tm), lambda b, t: (b, 0, t))

    def tab_spec(width):
        return pl.BlockSpec((tm, width), lambda b, t: (t, 0))

    n_seq = tm // seq_len

    def batch_spec(rows, width):
        return pl.BlockSpec((n_seq, rows, width), lambda b, t: (b, 0, 0))

    names = ["w1", "wcgt", "gln_g", "gln_b", "wsp%d" % sp, "bsp%d" % sp, "conv_w", "conv_b",
             "wgate", "b_r", "b_i", "lam"]
    names2 = ["q_norm", "wuqt", "kv_norm"]
    in_specs = ([row_spec(d_model)] + [_layer_spec(pw[n], l) for n in names]
                + [batch_spec(SUBLANES, d_b), batch_spec(1, d_b)] + [_layer_spec(pw[n], l) for n in names2]
                + [pl.BlockSpec((D_ROPE // 2, tm), lambda b, t: (0, t)), pl.BlockSpec((D_ROPE // 2, tm), lambda b, t: (0, t)),
                   tab_spec(D_ROPE), tab_spec(D_ROPE), batch_spec(n_mem, d_m), batch_spec(n_mem, d_m),
                   _layer_spec(pw["wbr"], l)])
    operands = ([x] + [pw[n] for n in names] + [conv0_pad, h0] + [pw[n] for n in names2]
                + [cos_t, sin_t, ck_t, sk_t, mk, mv, pw["wbr"]])
    out_shape = [
        jax.ShapeDtypeStruct((bsz, seq, d_model), BF16),
        jax.ShapeDtypeStruct((bsz, seq, d_model), BF16),
        jax.ShapeDtypeStruct((bsz, d_c, seq), F32),
        jax.ShapeDtypeStruct((bsz, seq // tq, H_C * LANES, tq), BF16),
        jax.ShapeDtypeStruct((bsz, seq, kv_lora), F32),
        jax.ShapeDtypeStruct((bsz, seq, D_ROPE), F32),
        jax.ShapeDtypeStruct((bsz * n_seq, SUBLANES, d_b), F32),
        jax.ShapeDtypeStruct((bsz * n_seq, 1, d_b), F32),
    ]
    out_specs = [
        row_spec(d_model), row_spec(d_model), col_spec(d_c),
        pl.BlockSpec((1, 1, H_C * LANES, tm), lambda b, t: (b, t // (tq // tm), 0, t % (tq // tm))),
        row_spec(kv_lora), row_spec(D_ROPE), batch_spec(SUBLANES, d_b), batch_spec(1, d_b),
    ]
    if first_chunk:
        for name in ("wkc", "wkr", "wvt"):
            in_specs.append(_layer_spec(pw[name], l))
            operands.append(pw[name])
        out_shape += [jax.ShapeDtypeStruct((bsz, seq, H_C * LANES), BF16),
                      jax.ShapeDtypeStruct((bsz, seq // ATTN_TK, H_C * V_ROWS, ATTN_TK), BF16)]
        out_specs += [row_spec(H_C * LANES),
                      pl.BlockSpec((1, tm // ATTN_TK, H_C * V_ROWS, ATTN_TK), lambda b, t: (b, t, 0, 0))]
    else:
        out_shape.append(jax.ShapeDtypeStruct((bsz, seq, d_a), F32))
        out_specs.append(row_spec(d_a))
    kern = functools.partial(
        _branch_kernel, first_chunk=first_chunk, tm=tm, seq_len=seq_len, sp_len=sp, d_a=d_a, d_b=d_b, d_c=d_c, d_m=d_m,
        q_lora=q_lora, kv_lora=kv_lora, cols=pw["cols"])
    return pl.pallas_call(
        kern, grid=grid, in_specs=in_specs, out_specs=out_specs, out_shape=out_shape,
        scratch_shapes=[pltpu.VMEM((tm, d_model), F32)],
        name="branch", compiler_params=_params(2),
    )(*operands)


HEADS_PER_STEP = 2
_EXP2_SCALE = (D_NOPE + D_ROPE) ** -0.5 * math.log2(math.e)


def _scores(k_tile, q_tile, out_refs=None):
    res = [_dot(k_tile[:, h * LANES:(h + 1) * LANES], q_tile[h * LANES:(h + 1) * LANES, :])
           for h in range(k_tile.shape[1] // LANES)]
    if out_refs is None:
        return res
    for h, r in enumerate(res):
        out_refs[h] = r
    return None


def _softmax_tile(s_t, m_old):
    m_new = jnp.maximum(m_old, jnp.max(s_t, axis=0, keepdims=True))
    return jnp.exp2(s_t - m_new).astype(BF16), m_new, jnp.exp2(m_old - m_new)


def _attn_output(acc, cgs_t):
    return (acc[0:D_V] / acc[D_V:D_V + 1] * cgs_t).astype(BF16)


N_BUF = 4


def _attn_causal_kernel(qt_ref, k_ref, vt_ref, cgst_ref, bias_ref, o_ref, *scratch, tq, tk):
    s_buf, e_buf, acc_scr = scratch[0:N_BUF], scratch[N_BUF:2 * N_BUF], scratch[2 * N_BUF]
    i = pl.program_id(2)
    nh = HEADS_PER_STEP
    q_cur = qt_ref.at[0, i]
    q_nxt = qt_ref.at[0, jnp.minimum(i + 1, pl.num_programs(2) - 1)]

    def k_tile(j):
        return k_ref[0, pl.ds(pl.multiple_of(j * tk, tk), tk), :]

    def pv(j, e_in, c0):
        vt = vt_ref[0, j]
        return [_dot(vt[h * V_ROWS:(h + 1) * V_ROWS, :], e_in[h, :, c0:tq]) for h in range(nh)]

    def acc_update(pvs, alphas, c0):
        for h in range(nh):
            acc_scr[h, :, c0:tq] = acc_scr[h, :, c0:tq] * alphas[h][:, c0:tq] + pvs[h]

    def softmax(slot, carry, c0, add_bias):
        new = []
        for h in range(nh):
            m_old, a_prev, _ = carry[h]
            s_t = s_buf[slot][h, :, c0:tq]
            if add_bias:
                lead = s_t[:, 0:tk] + bias_ref[...]
                s_t = lead if c0 + tk == tq else jnp.concatenate([lead, s_t[:, tk:]], axis=1)
            e, m_new, alpha = _softmax_tile(s_t, m_old[:, c0:tq])
            e_buf[slot][h, :, c0:tq] = e
            if c0:
                m_new = jnp.concatenate([m_old[:, 0:c0], m_new], axis=1)
                alpha = jnp.concatenate([jnp.ones((1, c0), F32), alpha], axis=1)
            new.append((m_new, alpha, a_prev))
        return tuple(new)

    def stage(n, slot, carry):
        pvs = pv(jnp.maximum(n - 2, 0), e_buf[(slot + 2) % N_BUF], 0)
        _scores(k_tile(n + 3), q_cur, s_buf[(slot + 3) % N_BUF])
        new = softmax(slot, carry, 0, False)
        acc_update(pvs, [c[2] for c in carry], 0)
        return new

    for slot in (N_BUF - 2, N_BUF - 1):
        e_buf[slot][...] = jnp.zeros(e_buf[slot].shape, BF16)
    acc_scr[...] = jnp.zeros(acc_scr.shape, F32)

    @pl.when(i == 0)
    def _():
        for slot in range(N_BUF - 1):
            _scores(k_tile(slot), q_cur, s_buf[slot])

    ones = jnp.ones((1, tq), F32)
    carry = tuple((jnp.full((1, tq), -jnp.inf, F32), ones, ones) for _ in range(nh))

    def body(t, carry):
        for slot in range(N_BUF):
            carry = stage(N_BUF * t + slot, slot, carry)
        return carry

    carry = lax.fori_loop(0, i, body, carry)

    j0 = N_BUF * i
    last = N_BUF - 1
    for d in range(N_BUF):
        c_prev = max(d - 2, 0) * tk
        pvs = pv(jnp.maximum(j0 + d - 2, 0), e_buf[(d + 2) % N_BUF], c_prev)
        if d == 0:
            k_last = k_tile(j0 + last)
            for h in range(nh):
                s_buf[last][h, :, tq - tk:tq] = _dot(
                    k_last[:, h * LANES:(h + 1) * LANES],
                    q_cur[h * LANES:(h + 1) * LANES, tq - tk:tq]) + bias_ref[...]
        else:
            _scores(k_tile(d - 1), q_nxt, s_buf[d - 1])
        alphas = [c[2] for c in carry]
        carry = softmax(d, carry, d * tk, d < last)
        acc_update(pvs, alphas, c_prev)
    acc_update(pv(j0 + last - 1, e_buf[last - 1], (last - 1) * tk), [c[2] for c in carry], (last - 1) * tk)
    acc_update(pv(j0 + last, e_buf[last], last * tk), [c[1] for c in carry], last * tk)
    for h in range(nh):
        o_ref[0, h * D_V:(h + 1) * D_V, :] = _attn_output(acc_scr[h], cgst_ref[0, h * D_V:(h + 1) * D_V, :])


def _attn_cached_kernel(qt_ref, pckv_ref, pkr_ref, ckv_ref, kr_ref, cgst_ref, wkc_ref, wkr_ref, wvt_ref, o_ref):
    ckv = jnp.concatenate([pckv_ref[0], ckv_ref[0]], axis=0)
    kr = jnp.concatenate([pkr_ref[0], kr_ref[0]], axis=0)
    kcat, vt = _keys_values(ckv, kr, wkc_ref, wkr_ref, wvt_ref)
    tq = qt_ref.shape[-1]
    scores = _scores(kcat, qt_ref.at[0])
    for h in range(len(scores)):
        e, _, _ = _softmax_tile(scores[h], jnp.full((1, tq), -jnp.inf, F32))
        acc = _dot(vt[h * V_ROWS:(h + 1) * V_ROWS, :], e)
        o_ref[0, h * D_V:(h + 1) * D_V, :] = _attn_output(acc, cgst_ref[0, h * D_V:(h + 1) * D_V, :])


def _attn_causal_call(qt, kcat, vt, cgst):
    bsz, n_qt, _, tq = qt.shape
    seq = n_qt * tq
    t_k = kcat.shape[1]
    nkt, _, tk = vt.shape[1:]
    nh = HEADS_PER_STEP
    assert tq == N_BUF * tk and seq == t_k and tk % CHUNK == 0
    chunk = jnp.arange(tk) // CHUNK
    bias = jnp.where(chunk[:, None] <= chunk[None, :], 0.0, -1e30).astype(F32)
    return pl.pallas_call(
        functools.partial(_attn_causal_kernel, tq=tq, tk=tk), grid=(bsz, H_C // nh, n_qt),
        in_specs=[pl.BlockSpec((1, n_qt, nh * LANES, tq), lambda b, p, i: (b, 0, p, 0)),
                  pl.BlockSpec((1, t_k, nh * LANES), lambda b, p, i: (b, 0, p)),
                  pl.BlockSpec((1, nkt, nh * V_ROWS, tk), lambda b, p, i: (b, 0, p, 0)),
                  pl.BlockSpec((1, nh * D_V, tq), lambda b, p, i: (b, p, i)),
                  _const_spec((tk, tk))],
        out_specs=pl.BlockSpec((1, nh * D_V, tq), lambda b, p, i: (b, p, i)),
        out_shape=jax.ShapeDtypeStruct((bsz, H_C * D_V, seq), BF16),
        scratch_shapes=([pltpu.VMEM((nh, tk, tq), F32)] * N_BUF + [pltpu.VMEM((nh, tk, tq), BF16)] * N_BUF
                        + [pltpu.VMEM((nh, V_ROWS, tq), F32)]),
        name="attn", compiler_params=_params(3),
    )(qt, kcat, vt, cgst, bias)


def _attn_cached_call(qt, past_ckv, past_kr, ckv_new, kr_new, cgst, pw, l):
    bsz, n_q, tq = qt.shape
    past_len, kv_lora = past_ckv.shape[2:]
    return pl.pallas_call(
        _attn_cached_kernel, grid=(bsz,),
        in_specs=[pl.BlockSpec((1, n_q, tq), lambda b: (b, 0, 0)),
                  pl.BlockSpec((None, 1, past_len, kv_lora), lambda b: (l, b, 0, 0)),
                  pl.BlockSpec((None, 1, past_len, D_ROPE), lambda b: (l, b, 0, 0)),
                  pl.BlockSpec((1, tq, kv_lora), lambda b: (b, 0, 0)),
                  pl.BlockSpec((1, tq, D_ROPE), lambda b: (b, 0, 0)),
                  pl.BlockSpec((1, H_C * D_V, tq), lambda b: (b, 0, 0)),
                  _layer_spec(pw["wkc"], l), _layer_spec(pw["wkr"], l), _layer_spec(pw["wvt"], l)],
        out_specs=pl.BlockSpec((1, H_C * D_V, tq), lambda b: (b, 0, 0)),
        out_shape=jax.ShapeDtypeStruct((bsz, H_C * D_V, tq), BF16),
        name="attn_cached", compiler_params=_params(1),
    )(qt, past_ckv, past_kr, ckv_new, kr_new, cgst, pw["wkc"], pw["wkr"], pw["wvt"])


def _merge_kernel(x_ref, partial_ref, g2_ref, oct_ref, wbr_ref, wout_ref, lng_ref, lnb_ref, o_ref,
                  *, c_lo, d_c, alpha):
    yc = _dot_tn(oct_ref[0], wbr_ref[c_lo:c_lo + d_c, :])
    merged = partial_ref[0].astype(F32) + g2_ref[0].astype(F32) * yc
    y = _dot(merged.astype(BF16), wout_ref[...])
    o_ref[0] = _layer_norm(alpha * x_ref[0] + y, lng_ref[...], lnb_ref[...])


def _merge_call(x, partial, g2, oct, pw, l, *, tr, alpha):
    bsz, seq, d_model = x.shape
    d_a, d_b, d_c, _ = pw["dims"]
    kern = functools.partial(_merge_kernel, c_lo=d_a + d_b, d_c=d_c, alpha=alpha)

    def row_spec(width):
        return pl.BlockSpec((1, tr, width), lambda b, i: (b, i, 0))

    return pl.pallas_call(
        kern, grid=(bsz, seq // tr),
        in_specs=[row_spec(d_model), row_spec(d_model), row_spec(d_model),
                  pl.BlockSpec((1, d_c, tr), lambda b, i: (b, 0, i)),
                  _layer_spec(pw["wbr"], l), _layer_spec(pw["wout"], l),
                  _layer_spec(pw["ln_g"], l), _layer_spec(pw["ln_b"], l)],
        out_specs=row_spec(d_model),
        out_shape=jax.ShapeDtypeStruct((bsz, seq, d_model), F32),
        name="merge", compiler_params=_params(2),
    )(x, partial, g2, oct, pw["wbr"], pw["wout"], pw["ln_g"], pw["ln_b"])


def _mem_kernel(m_ref, w_ref, k_ref, v_ref, *, d_m):
    kv = _dot(m_ref[...].astype(BF16), w_ref[...])
    k_ref[...] = kv[:, 0:d_m]
    v_ref[...] = kv[:, d_m:2 * d_m]


def _mem_call(mem2d, wmem, l, *, tr):
    rows, d_model = mem2d.shape
    d_m = wmem.shape[2] // 2
    return pl.pallas_call(
        functools.partial(_mem_kernel, d_m=d_m), grid=(rows // tr,),
        in_specs=[pl.BlockSpec((tr, d_model), lambda i: (i, 0)), _layer_spec(wmem, l)],
        out_specs=[pl.BlockSpec((tr, d_m), lambda i: (i, 0)), pl.BlockSpec((tr, d_m), lambda i: (i, 0))],
        out_shape=[jax.ShapeDtypeStruct((rows, d_m), F32), jax.ShapeDtypeStruct((rows, d_m), F32)],
        name="mem", compiler_params=_params(1),
    )(mem2d, wmem)


def _block_diag(w):
    h, n, _ = w.shape
    eye = jnp.eye(h, dtype=w.dtype)
    return (eye[:, None, :, None] * w[:, :, None, :]).reshape(h * n, h * n)


def _gate_blocks(w_r, w_i):
    per = MXU_TILE // w_r.shape[1]
    blocks = []
    for k in range(w_r.shape[0] // per):
        blocks.append(jnp.concatenate([_block_diag(w_r[k * per:(k + 1) * per]),
                                       _block_diag(w_i[k * per:(k + 1) * per])], axis=1))
    return jnp.stack(blocks)


def _w_in_kernel(wt_ref, o_ref, *, n_head, o_tail):
    o_ref[0, :, 0:n_head] = jnp.transpose(wt_ref[0, 0:n_head, :]).astype(BF16)
    o_ref[0, :, n_head:] = jnp.transpose(wt_ref[0, o_tail:, :]).astype(BF16)


def _w_in_call(w_in, n_head, o_tail, *, tk):
    depth, d_model, d_in = w_in.shape
    n_out = n_head + d_in - o_tail
    return pl.pallas_call(
        functools.partial(_w_in_kernel, n_head=n_head, o_tail=o_tail), grid=(depth, d_model // tk),
        in_specs=[pl.BlockSpec((1, d_in, tk), lambda l, i: (l, 0, i))],
        out_specs=pl.BlockSpec((1, tk, n_out), lambda l, i: (l, i, 0)),
        out_shape=jax.ShapeDtypeStruct((depth, d_model, n_out), BF16),
        name="w_in", compiler_params=_params(2),
    )(jnp.swapaxes(w_in, 1, 2))


def _prep_weights(sp_lens, w_in, gmlp_ln_g, gmlp_ln_b, gmlp_ws, gmlp_bs, lru_conv_w,
                  lru_conv_b, lru_w_r, lru_b_r, lru_w_i, lru_b_i, lru_lambda, mla_q_norm, mla_w_uq,
                  mla_kv_norm, mla_w_ukv, mem_w_k, mem_w_v, w_br, w_out, ln_g, ln_b):
    depth, d_model, _ = w_in.shape
    d_a = gmlp_ln_g.shape[1]
    d_b = lru_lambda.shape[1]
    q_lora = mla_q_norm.shape[1]
    kv_lora = mla_kv_norm.shape[1]
    d_c = H_C * D_V
    d_m = H_M * DH_M
    o_cq = 3 * d_a + 2 * d_b
    o_kr = o_cq + q_lora + kv_lora
    o_cg = o_kr + D_ROPE
    o_mq = o_cg + d_c
    o_g = o_mq + d_m
    n_head = o_kr + LANES
    assert o_kr % LANES == 0 and (o_g - o_mq) % LANES == 0 and D_ROPE <= LANES and n_head <= o_mq
    cols = {"a_u": (0, d_a), "a_v": (d_a, 2 * d_a), "a_g": (2 * d_a, 3 * d_a),
            "b_x": (3 * d_a, 3 * d_a + d_b), "b_g": (3 * d_a + d_b, o_cq), "c_all": (o_cq, n_head),
            "m_q": (n_head, n_head + d_m)}
    for k in range(N_BRANCH):
        cols[f"g{k}"] = (n_head + d_m + k * d_model, n_head + d_m + (k + 1) * d_model)
    w1 = _w_in_call(w_in, n_head, o_mq, tk=LANES)
    wcgt = jnp.swapaxes(w_in[:, :, o_cg:o_mq], 1, 2).astype(BF16)

    wq = mla_w_uq.reshape(depth, q_lora, H_C, D_NOPE + D_ROPE) * _EXP2_SCALE
    wq = jnp.concatenate([wq, jnp.zeros((depth, q_lora, H_C, LANES - D_NOPE - D_ROPE), wq.dtype)], axis=3)
    wuqt = jnp.swapaxes(wq.reshape(depth, q_lora, H_C * LANES), 1, 2).astype(BF16)

    wkv = mla_w_ukv.reshape(depth, kv_lora, H_C, D_NOPE + D_V)
    wk = wkv[..., :D_NOPE]
    wv = wkv[..., D_NOPE:]
    wkc = jnp.concatenate([wk, jnp.zeros((depth, kv_lora, H_C, LANES - D_NOPE), wk.dtype)], axis=3)
    wkc = wkc.reshape(depth, kv_lora, H_C * LANES).astype(BF16)
    lane = jnp.arange(H_C * LANES) % LANES
    wkr = ((lane[None, :] - D_NOPE) == jnp.arange(D_ROPE)[:, None]).astype(BF16)
    wkr = jnp.broadcast_to(wkr, (depth,) + wkr.shape)
    wv = jnp.concatenate([wv, jnp.zeros((depth, kv_lora, H_C, V_ROWS - D_V), wv.dtype)], axis=3)
    wvt = jnp.swapaxes(wv.reshape(depth, kv_lora, H_C * V_ROWS), 1, 2).astype(BF16)

    def row(p):
        return p[:, None, :]

    pw = dict(
        dims=(d_a, d_b, d_c, d_m), q_lora=q_lora, kv_lora=kv_lora, cols=cols,
        w1=w1, wcgt=wcgt,
        gln_g=row(gmlp_ln_g), gln_b=row(gmlp_ln_b),
        conv_w=lru_conv_w, conv_b=row(lru_conv_b),
        wgate=jnp.stack([_gate_blocks(lru_w_r[l], lru_w_i[l]) for l in range(depth)]).astype(BF16),
        b_r=row(lru_b_r), b_i=row(lru_b_i), lam=row(lru_lambda),
        q_norm=row(mla_q_norm), wuqt=wuqt, kv_norm=row(mla_kv_norm),
        wkc=wkc, wkr=wkr, wvt=wvt,
        wbr=w_br.astype(BF16), wout=w_out.astype(BF16), ln_g=row(ln_g), ln_b=row(ln_b),
        wmem=jnp.concatenate([mem_w_k, mem_w_v], axis=2).astype(BF16),
    )
    for sp in sp_lens:
        pw["wsp%d" % sp] = jnp.tril(gmlp_ws[:, :, :sp, :sp]).astype(BF16)
        pw["bsp%d" % sp] = jnp.repeat(jnp.swapaxes(gmlp_bs[:, :, :sp], 1, 2), d_a // G_A, axis=2)
    return pw


def _rope_tables(pos):
    half = D_ROPE // 2
    freq = ROPE_BASE ** (-jnp.arange(half, dtype=F32) / half)
    ang = pos.astype(F32)[:, None] * freq[None, :]
    cos, sin = jnp.cos(ang), jnp.sin(ang)
    return (jnp.transpose(cos), jnp.transpose(sin),
            jnp.concatenate([cos, cos], axis=1), jnp.concatenate([-sin, sin], axis=1))


def _pad_conv_state(conv):
    return jnp.pad(conv, ((0, 0), (SUBLANES - (CONV_W - 1), 0), (0, 0)))


def _trunk_layer(x, pw, l, sp, rope, conv0, h0, mk, mv, past_ckv, past_kr, *, tm, alpha):
    bsz, seq, d_model = x.shape
    first_chunk = past_ckv is None
    if first_chunk:
        x_rows, seq_len, tq = x, tm, N_BUF * ATTN_TK
    else:
        x_rows, seq_len, tq = x.reshape(1, bsz * seq, d_model), seq, bsz * seq
        cos_t, sin_t, ck_t, sk_t = rope
        rope = (jnp.tile(cos_t, (1, bsz)), jnp.tile(sin_t, (1, bsz)), jnp.tile(ck_t, (bsz, 1)), jnp.tile(sk_t, (bsz, 1)))
    outs = _branch_call(x_rows, pw, l, sp, _pad_conv_state(conv0), h0[:, None, :], mk, mv, rope,
                        tm=tm, seq_len=seq_len, tq=tq, first_chunk=first_chunk)
    partial, g2, cgst, qt, ckv_new, kr_new, conv_pad, h_new = outs[:8]
    if first_chunk:
        kcat, vt = outs[8:]
        v_rows = None
        oct = _attn_causal_call(qt, kcat, vt, cgst)
    else:
        ckv_new, kr_new, v_rows = (o.reshape(bsz, seq, -1) for o in (ckv_new, kr_new, outs[8]))

        def per_seq(a):
            return jnp.transpose(a.reshape(a.shape[-2], bsz, seq), (1, 0, 2))

        oct = _attn_cached_call(per_seq(qt), past_ckv, past_kr, ckv_new, kr_new, per_seq(cgst), pw, l)
        oct = jnp.transpose(oct, (1, 0, 2)).reshape(1, oct.shape[1], bsz * seq)
    x_new = _merge_call(x_rows, partial, g2, oct, pw, l, tr=MERGE_TILE if first_chunk else tm, alpha=alpha)
    return (x_new.reshape(bsz, seq, d_model), v_rows, conv_pad[:, SUBLANES - (CONV_W - 1):], h_new[:, 0],
            ckv_new, kr_new)


def kernel(x_prompt, x_sample, mem_prompt, cache_mla_ckv, cache_mla_krope, cache_mem_k, cache_mem_v,
           state_lru_h, state_lru_conv, w_in, gmlp_ln_g, gmlp_ln_b, gmlp_ws, gmlp_bs,
           lru_conv_w, lru_conv_b, lru_w_r, lru_b_r, lru_w_i, lru_b_i, lru_lambda,
           mla_q_norm, mla_w_uq, mla_kv_norm, mla_w_ukv, mem_w_k, mem_w_v, w_br, w_out, ln_g, ln_b):
    bp, tp, d_model = x_prompt.shape
    bs, ts, _ = x_sample.shape
    depth = w_in.shape[0]
    past_len = cache_mla_ckv.shape[2]
    n_mem = mem_prompt.shape[1]
    d_b = lru_lambda.shape[1]
    alpha = (2.0 * depth) ** 0.25
    sp_p, sp_s = min(tp, A_CHUNK), min(ts, A_CHUNK)
    pw = _prep_weights(sorted({sp_p, sp_s}), w_in, gmlp_ln_g, gmlp_ln_b, gmlp_ws, gmlp_bs, lru_conv_w,
                       lru_conv_b, lru_w_r, lru_b_r, lru_w_i, lru_b_i, lru_lambda, mla_q_norm, mla_w_uq,
                       mla_kv_norm, mla_w_ukv, mem_w_k, mem_w_v, w_br, w_out, ln_g, ln_b)
    rope_p = _rope_tables(jnp.arange(tp))
    rope_s = _rope_tables(past_len + jnp.arange(ts))
    tm_p = min(tp, ROW_TILE)
    assert tp % (N_BUF * ATTN_TK) == 0 and tm_p % ATTN_TK == 0 and tp % MERGE_TILE == 0 and ts % SUBLANES == 0
    zero_conv = jnp.zeros((bp, CONV_W - 1, d_b), F32)
    zero_h = jnp.zeros((bp, d_b), F32)

    xp, xs = x_prompt, x_sample
    acc = [[] for _ in range(11)]
    for l in range(depth):
        mk, mv = _mem_call(mem_prompt.reshape(bp * n_mem, d_model), pw["wmem"], l, tr=n_mem)
        mk = mk.reshape(bp, n_mem, -1)
        mv = mv.reshape(bp, n_mem, -1)
        xp, _, conv_n, h_n, ckv_n, kr_n = _trunk_layer(
            xp, pw, l, sp_p, rope_p, zero_conv, zero_h, mk, mv, None, None, tm=tm_p, alpha=alpha)
        for k, val in zip(range(6), (ckv_n, kr_n, mk.reshape(bp, n_mem, H_M, DH_M),
                                     mv.reshape(bp, n_mem, H_M, DH_M), h_n, conv_n)):
            acc[k].append(val)
        xs, v_n, conv_n, h_n, ckv_n, kr_n = _trunk_layer(
            xs, pw, l, sp_s, rope_s, state_lru_conv[l], state_lru_h[l],
            cache_mem_k[l].reshape(bs, n_mem, -1), cache_mem_v[l].reshape(bs, n_mem, -1),
            cache_mla_ckv, cache_mla_krope, tm=bs * ts, alpha=alpha)
        for k, val in zip(range(6, 11), (ckv_n, kr_n, h_n, conv_n, v_n)):
            acc[k].append(val)
    return (xp, xs) + tuple(jnp.stack(a) for a in acc)
```

```python
import functools
import math

import jax
import jax.numpy as jnp
from jax import lax
from jax.experimental import pallas as pl
from jax.experimental.pallas import tpu as pltpu

CHUNK = 64
G_A = 4
A_CHUNK = 128
H_B = 8
CONV_W = 4
LRU_C = 8.0
H_C = 8
D_NOPE = 64
D_ROPE = 32
D_V = 64
ROPE_BASE = 10000.0
H_M = 4
DH_M = 64
N_BRANCH = 4
EPS = 1e-6

LANES = 128
SUBLANES = 8
VMEM_LIMIT = 56 * 1024 * 1024
ROW_TILE = 512
ATTN_TK = 256
MXU_TILE = 256
MERGE_TILE = 1024

F32 = jnp.float32
BF16 = jnp.bfloat16


def _dot(a, b):
    return jnp.dot(a, b, preferred_element_type=F32)


def _dot_nt(a, b):
    return lax.dot_general(a, b, (((1,), (1,)), ((), ())), preferred_element_type=F32)


def _dot_tn(a, b):
    return lax.dot_general(a, b, (((0,), (0,)), ((), ())), preferred_element_type=F32)


def _sigmoid(x):
    return 1.0 / (1.0 + jnp.exp(-x))


def _silu(x):
    return x * _sigmoid(x)


def _gelu(x):
    return jax.nn.gelu(x)


def _expm1_nonpos(x):
    u = jnp.exp(x)
    near = (u - 1.0) * x / jnp.log(jnp.where(u == 1.0, 2.0, jnp.maximum(u, 0.5)))
    return jnp.where(u == 1.0, x, jnp.where(u > 0.5, near, u - 1.0))


def _layer_norm(x, g, b):
    mu = jnp.mean(x, -1, keepdims=True)
    var = jnp.mean(jnp.square(x - mu), -1, keepdims=True)
    return (x - mu) * lax.rsqrt(var + EPS) * g + b


def _rms_norm(x, g):
    return x * lax.rsqrt(jnp.mean(jnp.square(x), -1, keepdims=True) + EPS) * g


def _const_spec(shape):
    nd = len(shape)
    return pl.BlockSpec(shape, lambda *_: (0,) * nd, pipeline_mode=pl.Buffered(1))


def _layer_spec(arr, l):
    nd = arr.ndim - 1
    return pl.BlockSpec((None,) + arr.shape[1:], lambda *_: (l,) + (0,) * nd, pipeline_mode=pl.Buffered(1))


def _params(n_axes):
    return pltpu.CompilerParams(dimension_semantics=("arbitrary",) * n_axes,
                                vmem_limit_bytes=VMEM_LIMIT)


V_ROWS = 80


def _keys_values(ckv, kr, wkc_ref, wkr_ref, wvt_ref):
    c = ckv.astype(BF16)
    kcat = (_dot(c, wkc_ref[...]) + _dot(kr.astype(BF16), wkr_ref[...])).astype(BF16)
    v_t = _dot_nt(wvt_ref[...], c)
    row = lax.broadcasted_iota(jnp.int32, (v_t.shape[0], 1), 0)
    return kcat, jnp.where(row % V_ROWS == D_V, 1.0, v_t).astype(BF16)


def _branch_kernel(x_ref, w1_ref, wcgt_ref, glng_ref, glnb_ref, wsp_ref, bsp_ref,
                   convw_ref, convb_ref, wgate_ref, br_ref, bi_ref, lam_ref, conv0_ref, h0_ref, qnorm_ref, wuqt_ref,
                   kvnorm_ref, cosq_ref, sinq_ref, ck_ref, sk_ref, mk_ref, mv_ref, wbr_ref,
                   *rest, first_chunk, tm, seq_len, sp_len, d_a, d_b, d_c, d_m, q_lora, kv_lora, cols):
    if first_chunk:
        (wkc_ref, wkr_ref, wvt_ref, partial_ref, g2_ref, cgst_ref, qt_ref, ckv_ref, kr_ref, conv_ref, h_ref,
         kcat_ref, vt_ref, merged_scr) = rest
    else:
        partial_ref, g2_ref, cgst_ref, qt_ref, ckv_ref, kr_ref, conv_ref, h_ref, v_ref, merged_scr = rest
    t = pl.program_id(1)
    xb = x_ref[0].astype(BF16)

    def zin(name):
        lo, hi = cols[name]
        return _dot(xb, w1_ref[:, lo:hi])

    z_u, z_v, z_ag, z_g0 = zin("a_u"), zin("a_v"), zin("a_g"), zin("g0")
    u = _gelu(z_u)
    v = _layer_norm(_gelu(z_v), glng_ref[...], glnb_ref[...])
    if not first_chunk:
        v_ref[0] = v
    vb = v.astype(BF16)
    n_groups = d_a // LANES
    row_blocks = []
    for c in range(tm // sp_len):
        col_blocks = [_dot(wsp_ref[g], vb[c * sp_len:(c + 1) * sp_len, g * LANES:(g + 1) * LANES])
                      for g in range(n_groups)]
        row_blocks.append(jnp.concatenate(col_blocks, axis=1) + bsp_ref[...])
    s = row_blocks[0] if len(row_blocks) == 1 else jnp.concatenate(row_blocks, axis=0)
    oa = (u * s) * _silu(z_ag)
    merged_scr[...] = _sigmoid(z_g0) * _dot(oa.astype(BF16), wbr_ref[0:d_a, :])

    @pl.when(t == 0)
    def _():
        conv_ref[...] = conv0_ref[...]
        h_ref[...] = h0_ref[...]

    bx = zin("b_x")
    n_seq = tm // seq_len
    span = SUBLANES + seq_len
    stacked = jnp.concatenate(
        [piece for q in range(n_seq) for piece in (conv_ref[q], bx[q * seq_len:(q + 1) * seq_len])], axis=0)
    xc = convb_ref[...]
    for k in range(CONV_W):
        shift = CONV_W - 1 - k
        rolled = stacked if shift == 0 else pltpu.roll(stacked, shift, axis=0)
        sh = [rolled[q * span + SUBLANES:(q + 1) * span] for q in range(n_seq)]
        xc = xc + (sh[0] if n_seq == 1 else jnp.concatenate(sh, axis=0)) * convw_ref[k:k + 1, :]
    for q in range(n_seq):
        conv_ref[q] = bx[(q + 1) * seq_len - SUBLANES:(q + 1) * seq_len]

    xcb = xc.astype(BF16)
    n_blk = d_b // MXU_TILE
    ri = [_dot(xcb[:, k * MXU_TILE:(k + 1) * MXU_TILE], wgate_ref[k]) for k in range(n_blk)]
    z_bg, z_g1, z_g2 = zin("b_g"), zin("g1"), zin("g2")
    z_mq, z_g3, z_c = zin("m_q"), zin("g3"), zin("c_all")
    cg_t = _dot_nt(wcgt_ref[...], xb)
    r = _sigmoid(jnp.concatenate([p[:, 0:MXU_TILE] for p in ri], axis=1) + br_ref[...])
    i_gate = _sigmoid(jnp.concatenate([p[:, MXU_TILE:2 * MXU_TILE] for p in ri], axis=1) + bi_ref[...])
    neg_lam = -lam_ref[...]
    softplus = jnp.maximum(neg_lam, 0.0) + jnp.log1p(jnp.exp(-jnp.abs(neg_lam)))
    log_a = (-LRU_C * r) * softplus
    a = jnp.exp(log_a)
    bval = jnp.sqrt(-_expm1_nonpos(2.0 * log_a)) * (i_gate * xc)
    in_group = lax.broadcasted_iota(jnp.int32, (tm, 1), 0) % SUBLANES
    d = 1
    while d < SUBLANES:
        keep = in_group >= d
        a_sh = pltpu.roll(a, d, axis=0)
        b_sh = pltpu.roll(bval, d, axis=0)
        bval = jnp.where(keep, a * b_sh + bval, bval)
        a = jnp.where(keep, a * a_sh, a)
        d *= 2
    groups_per_seq = seq_len // SUBLANES
    h_groups = []
    for g in range(tm // SUBLANES):
        q = g // groups_per_seq
        if g % groups_per_seq == 0:
            carry = h_ref[q]
        lo = g * SUBLANES
        h_g = a[lo:lo + SUBLANES] * carry + bval[lo:lo + SUBLANES]
        h_groups.append(h_g)
        carry = h_g[SUBLANES - 1:SUBLANES]
        if (g + 1) % groups_per_seq == 0:
            h_ref[q] = carry
    h = jnp.concatenate(h_groups, axis=0)
    ob = h * _silu(z_bg)
    merged_scr[...] += _sigmoid(z_g1) * _dot(ob.astype(BF16), wbr_ref[d_a:d_a + d_b, :])
    g2_ref[0] = _sigmoid(z_g2).astype(BF16)

    mq = z_mq.astype(BF16)
    lane = lax.broadcasted_iota(jnp.int32, (1, LANES), 1)
    om_rows = []
    for q in range(n_seq):
        mkb = mk_ref[q].astype(BF16)
        mvb = mv_ref[q].astype(BF16)
        slabs = []
        for p in range(d_m // LANES):
            mq_p = mq[q * seq_len:(q + 1) * seq_len, p * LANES:(p + 1) * LANES]
            mk_p = mkb[:, p * LANES:(p + 1) * LANES]
            mv_p = mvb[:, p * LANES:(p + 1) * LANES]
            acc = None
            for half in range(LANES // DH_M):
                sel = (lane >= half * DH_M) & (lane < (half + 1) * DH_M)
                sc = _dot_nt(mq_p, jnp.where(sel, mk_p, jnp.zeros_like(mk_p))) * (DH_M ** -0.5)
                e = jnp.exp(sc - jnp.max(sc, -1, keepdims=True))
                prob = (e / jnp.sum(e, -1, keepdims=True)).astype(BF16)
                o = _dot(prob, jnp.where(sel, mv_p, jnp.zeros_like(mv_p)))
                acc = o if acc is None else acc + o
            slabs.append(acc)
        om_rows.append(jnp.concatenate(slabs, axis=1))
    om = om_rows[0] if n_seq == 1 else jnp.concatenate(om_rows, axis=0)
    m_lo = d_a + d_b + d_c
    partial_ref[0] = (merged_scr[...]
                      + _sigmoid(z_g3) * _dot(om.astype(BF16), wbr_ref[m_lo:m_lo + d_m, :])).astype(BF16)
    cgst_ref[0] = _silu(cg_t)

    ckv = _rms_norm(z_c[:, q_lora:q_lora + kv_lora], kvnorm_ref[...])
    half = D_ROPE // 2
    z_kr = z_c[:, q_lora + kv_lora:q_lora + kv_lora + LANES]
    kr_swapped = jnp.where(lane < half, pltpu.roll(z_kr, LANES - half, axis=1), pltpu.roll(z_kr, half, axis=1))
    kr = z_kr[:, 0:D_ROPE] * ck_ref[...] + kr_swapped[:, 0:D_ROPE] * sk_ref[...]
    ckv_ref[0] = ckv
    kr_ref[0] = kr
    if first_chunk:
        kcat_ref[0], vt_all = _keys_values(ckv, kr, wkc_ref, wkr_ref, wvt_ref)
        for j in range(tm // ATTN_TK):
            vt_ref[0, j] = vt_all[:, j * ATTN_TK:(j + 1) * ATTN_TK]
    cqn =_rms_norm(z_c[:, 0:q_lora], qnorm_ref[...]).astype(BF16)
    q_t = _dot_nt(wuqt_ref[...], cqn)
    cos_t, sin_t = cosq_ref[...], sinq_ref[...]
    for hd in range(H_C):
        lo = hd * LANES
        x1 = q_t[lo + D_NOPE:lo + D_NOPE + half]
        x2 = q_t[lo + D_NOPE + half:lo + D_NOPE + D_ROPE]
        qt_ref[0, 0, lo:lo + D_NOPE, :] = q_t[lo:lo + D_NOPE].astype(BF16)
        qt_ref[0, 0, lo + D_NOPE:lo + D_NOPE + half, :] = (x1 * cos_t - x2 * sin_t).astype(BF16)
        qt_ref[0, 0, lo + D_NOPE + half:lo + D_NOPE + D_ROPE, :] = (x1 * sin_t + x2 * cos_t).astype(BF16)
        qt_ref[0, 0, lo + D_NOPE + D_ROPE:lo + LANES, :] = jnp.zeros((LANES - D_NOPE - D_ROPE, tm), BF16)


def _branch_call(x, pw, l, sp, conv0_pad, h0, mk, mv, rope, *, tm, seq_len, tq, first_chunk):
    bsz, seq, d_model = x.shape
    d_a, d_b, d_c, d_m = pw["dims"]
    q_lora, kv_lora = pw["q_lora"], pw["kv_lora"]
    n_mem = mk.shape[1]
    cos_t, sin_t, ck_t, sk_t = rope
    grid = (bsz, seq // tm)

    def row_spec(width):
        return pl.BlockSpec((1, tm, width), lambda b, t: (b, t, 0))

    def col_spec(height):
        return pl.BlockSpec((1, height, tm), lambda b, t: (b, 0, t))

    def tab_spec(width):
        return pl.BlockSpec((tm, width), lambda b, t: (t, 0))

    n_seq = tm // seq_len

    def batch_spec(rows, width):
        return pl.BlockSpec((n_seq, rows, width), lambda b, t: (b, 0, 0))

    names = ["w1", "wcgt", "gln_g", "gln_b", "wsp%d" % sp, "bsp%d" % sp, "conv_w", "conv_b",
             "wgate", "b_r", "b_i", "lam"]
    names2 = ["q_norm", "wuqt", "kv_norm"]
    in_specs = ([row_spec(d_model)] + [_layer_spec(pw[n], l) for n in names]
                + [batch_spec(SUBLANES, d_b), batch_spec(1, d_b)] + [_layer_spec(pw[n], l) for n in names2]
                + [pl.BlockSpec((D_ROPE // 2, tm), lambda b, t: (0, t)), pl.BlockSpec((D_ROPE // 2, tm), lambda b, t: (0, t)),
                   tab_spec(D_ROPE), tab_spec(D_ROPE), batch_spec(n_mem, d_m), batch_spec(n_mem, d_m),
                   _layer_spec(pw["wbr"], l)])
    operands = ([x] + [pw[n] for n in names] + [conv0_pad, h0] + [pw[n] for n in names2]
                + [cos_t, sin_t, ck_t, sk_t, mk, mv, pw["wbr"]])
    out_shape = [
        jax.ShapeDtypeStruct((bsz, seq, d_model), BF16),
        jax.ShapeDtypeStruct((bsz, seq, d_model), BF16),
        jax.ShapeDtypeStruct((bsz, d_c, seq), F32),
        jax.ShapeDtypeStruct((bsz, seq // tq, H_C * LANES, tq), BF16),
        jax.ShapeDtypeStruct((bsz, seq, kv_lora), F32),
        jax.ShapeDtypeStruct((bsz, seq, D_ROPE), F32),
        jax.ShapeDtypeStruct((bsz * n_seq, SUBLANES, d_b), F32),
        jax.ShapeDtypeStruct((bsz * n_seq, 1, d_b), F32),
    ]
    out_specs = [
        row_spec(d_model), row_spec(d_model), col_spec(d_c),
        pl.BlockSpec((1, 1, H_C * LANES, tm), lambda b, t: (b, t // (tq // tm), 0, t % (tq // tm))),
        row_spec(kv_lora), row_spec(D_ROPE), batch_spec(SUBLANES, d_b), batch_spec(1, d_b),
    ]
    if first_chunk:
        for name in ("wkc", "wkr", "wvt"):
            in_specs.append(_layer_spec(pw[name], l))
            operands.append(pw[name])
        out_shape += [jax.ShapeDtypeStruct((bsz, seq, H_C * LANES), BF16),
                      jax.ShapeDtypeStruct((bsz, seq // ATTN_TK, H_C * V_ROWS, ATTN_TK), BF16)]
        out_specs += [row_spec(H_C * LANES),
                      pl.BlockSpec((1, tm // ATTN_TK, H_C * V_ROWS, ATTN_TK), lambda b, t: (b, t, 0, 0))]
    else:
        out_shape.append(jax.ShapeDtypeStruct((bsz, seq, d_a), F32))
        out_specs.append(row_spec(d_a))
    kern = functools.partial(
        _branch_kernel, first_chunk=first_chunk, tm=tm, seq_len=seq_len, sp_len=sp, d_a=d_a, d_b=d_b, d_c=d_c, d_m=d_m,
        q_lora=q_lora, kv_lora=kv_lora, cols=pw["cols"])
    return pl.pallas_call(
        kern, grid=grid, in_specs=in_specs, out_specs=out_specs, out_shape=out_shape,
        scratch_shapes=[pltpu.VMEM((tm, d_model), F32)],
        name="branch", compiler_params=_params(2),
    )(*operands)


HEADS_PER_STEP = 4
_EXP2_SCALE = (D_NOPE + D_ROPE) ** -0.5 * math.log2(math.e)


def _scores(k_tile, q_tile, out_refs=None):
    res = [_dot(k_tile[:, h * LANES:(h + 1) * LANES], q_tile[h * LANES:(h + 1) * LANES, :])
           for h in range(k_tile.shape[1] // LANES)]
    if out_refs is None:
        return res
    for h, r in enumerate(res):
        out_refs[h] = r
    return None


def _softmax_tile(s_t, m_old):
    m_new = jnp.maximum(m_old, jnp.max(s_t, axis=0, keepdims=True))
    return jnp.exp2(s_t - m_new).astype(BF16), m_new, jnp.exp2(m_old - m_new)


def _attn_output(acc, cgs_t):
    return (acc[0:D_V] / acc[D_V:D_V + 1] * cgs_t).astype(BF16)


N_BUF = 4


def _attn_causal_kernel(qt_ref, k_ref, vt_ref, cgst_ref, bias_ref, o_ref, *scratch, tq, tk):
    s_buf, e_buf, acc_scr = scratch[0:N_BUF], scratch[N_BUF:2 * N_BUF], scratch[2 * N_BUF]
    i = pl.program_id(2)
    nh = HEADS_PER_STEP
    q_cur = qt_ref.at[0, i]
    q_nxt = qt_ref.at[0, jnp.minimum(i + 1, pl.num_programs(2) - 1)]

    def k_tile(j):
        return k_ref[0, pl.ds(pl.multiple_of(j * tk, tk), tk), :]

    def pv(j, e_in, c0):
        vt = vt_ref[0, j]
        return [_dot(vt[h * V_ROWS:(h + 1) * V_ROWS, :], e_in[h, :, c0:tq]) for h in range(nh)]

    def acc_update(pvs, alphas, c0):
        for h in range(nh):
            acc_scr[h, :, c0:tq] = acc_scr[h, :, c0:tq] * alphas[h][:, c0:tq] + pvs[h]

    def softmax(slot, carry, c0, add_bias):
        new = []
        for h in range(nh):
            m_old, a_prev, _ = carry[h]
            s_t = s_buf[slot][h, :, c0:tq]
            if add_bias:
                lead = s_t[:, 0:tk] + bias_ref[...]
                s_t = lead if c0 + tk == tq else jnp.concatenate([lead, s_t[:, tk:]], axis=1)
            e, m_new, alpha = _softmax_tile(s_t, m_old[:, c0:tq])
            e_buf[slot][h, :, c0:tq] = e
            if c0:
                m_new = jnp.concatenate([m_old[:, 0:c0], m_new], axis=1)
                alpha = jnp.concatenate([jnp.ones((1, c0), F32), alpha], axis=1)
            new.append((m_new, alpha, a_prev))
        return tuple(new)

    def stage(n, slot, carry):
        pvs = pv(jnp.maximum(n - 2, 0), e_buf[(slot + 2) % N_BUF], 0)
        _scores(k_tile(n + 3), q_cur, s_buf[(slot + 3) % N_BUF])
        new = softmax(slot, carry, 0, False)
        acc_update(pvs, [c[2] for c in carry], 0)
        return new

    for slot in (N_BUF - 2, N_BUF - 1):
        e_buf[slot][...] = jnp.zeros(e_buf[slot].shape, BF16)
    acc_scr[...] = jnp.zeros(acc_scr.shape, F32)

    @pl.when(i == 0)
    def _():
        for slot in range(N_BUF - 1):
            _scores(k_tile(slot), q_cur, s_buf[slot])

    ones = jnp.ones((1, tq), F32)
    carry = tuple((jnp.full((1, tq), -jnp.inf, F32), ones, ones) for _ in range(nh))

    def body(t, carry):
        for slot in range(N_BUF):
            carry = stage(N_BUF * t + slot, slot, carry)
        return carry

    carry = lax.fori_loop(0, i, body, carry)

    j0 = N_BUF * i
    last = N_BUF - 1
    for d in range(N_BUF):
        c_prev = max(d - 2, 0) * tk
        pvs = pv(jnp.maximum(j0 + d - 2, 0), e_buf[(d + 2) % N_BUF], c_prev)
        if d == 0:
            k_last = k_tile(j0 + last)
            for h in range(nh):
                s_buf[last][h, :, tq - tk:tq] = _dot(
                    k_last[:, h * LANES:(h + 1) * LANES],
                    q_cur[h * LANES:(h + 1) * LANES, tq - tk:tq]) + bias_ref[...]
        else:
            _scores(k_tile(d - 1), q_nxt, s_buf[d - 1])
        alphas = [c[2] for c in carry]
        carry = softmax(d, carry, d * tk, d < last)
        acc_update(pvs, alphas, c_prev)
    acc_update(pv(j0 + last - 1, e_buf[last - 1], (last - 1) * tk), [c[2] for c in carry], (last - 1) * tk)
    acc_update(pv(j0 + last, e_buf[last], last * tk), [c[1] for c in carry], last * tk)
    for h in range(nh):
        o_ref[0, h * D_V:(h + 1) * D_V, :] = _attn_output(acc_scr[h], cgst_ref[0, h * D_V:(h + 1) * D_V, :])


def _attn_cached_kernel(qt_ref, pckv_ref, pkr_ref, ckv_ref, kr_ref, cgst_ref, wkc_ref, wkr_ref, wvt_ref, o_ref):
    ckv = jnp.concatenate([pckv_ref[0], ckv_ref[0]], axis=0)
    kr = jnp.concatenate([pkr_ref[0], kr_ref[0]], axis=0)
    kcat, vt = _keys_values(ckv, kr, wkc_ref, wkr_ref, wvt_ref)
    tq = qt_ref.shape[-1]
    scores = _scores(kcat, qt_ref.at[0])
    for h in range(len(scores)):
        e, _, _ = _softmax_tile(scores[h], jnp.full((1, tq), -jnp.inf, F32))
        acc = _dot(vt[h * V_ROWS:(h + 1) * V_ROWS, :], e)
        o_ref[0, h * D_V:(h + 1) * D_V, :] = _attn_output(acc, cgst_ref[0, h * D_V:(h + 1) * D_V, :])


def _attn_causal_call(qt, kcat, vt, cgst):
    bsz, n_qt, _, tq = qt.shape
    seq = n_qt * tq
    t_k = kcat.shape[1]
    nkt, _, tk = vt.shape[1:]
    nh = HEADS_PER_STEP
    assert tq == N_BUF * tk and seq == t_k and tk % CHUNK == 0
    chunk = jnp.arange(tk) // CHUNK
    bias = jnp.where(chunk[:, None] <= chunk[None, :], 0.0, -1e30).astype(F32)
    return pl.pallas_call(
        functools.partial(_attn_causal_kernel, tq=tq, tk=tk), grid=(bsz, H_C // nh, n_qt),
        in_specs=[pl.BlockSpec((1, n_qt, nh * LANES, tq), lambda b, p, i: (b, 0, p, 0)),
                  pl.BlockSpec((1, t_k, nh * LANES), lambda b, p, i: (b, 0, p)),
                  pl.BlockSpec((1, nkt, nh * V_ROWS, tk), lambda b, p, i: (b, 0, p, 0)),
                  pl.BlockSpec((1, nh * D_V, tq), lambda b, p, i: (b, p, i)),
                  _const_spec((tk, tk))],
        out_specs=pl.BlockSpec((1, nh * D_V, tq), lambda b, p, i: (b, p, i)),
        out_shape=jax.ShapeDtypeStruct((bsz, H_C * D_V, seq), BF16),
        scratch_shapes=([pltpu.VMEM((nh, tk, tq), F32)] * N_BUF + [pltpu.VMEM((nh, tk, tq), BF16)] * N_BUF
                        + [pltpu.VMEM((nh, V_ROWS, tq), F32)]),
        name="attn", compiler_params=_params(3),
    )(qt, kcat, vt, cgst, bias)


def _attn_cached_call(qt, past_ckv, past_kr, ckv_new, kr_new, cgst, pw, l):
    bsz, n_q, tq = qt.shape
    past_len, kv_lora = past_ckv.shape[2:]
    return pl.pallas_call(
        _attn_cached_kernel, grid=(bsz,),
        in_specs=[pl.BlockSpec((1, n_q, tq), lambda b: (b, 0, 0)),
                  pl.BlockSpec((None, 1, past_len, kv_lora), lambda b: (l, b, 0, 0)),
                  pl.BlockSpec((None, 1, past_len, D_ROPE), lambda b: (l, b, 0, 0)),
                  pl.BlockSpec((1, tq, kv_lora), lambda b: (b, 0, 0)),
                  pl.BlockSpec((1, tq, D_ROPE), lambda b: (b, 0, 0)),
                  pl.BlockSpec((1, H_C * D_V, tq), lambda b: (b, 0, 0)),
                  _layer_spec(pw["wkc"], l), _layer_spec(pw["wkr"], l), _layer_spec(pw["wvt"], l)],
        out_specs=pl.BlockSpec((1, H_C * D_V, tq), lambda b: (b, 0, 0)),
        out_shape=jax.ShapeDtypeStruct((bsz, H_C * D_V, tq), BF16),
        name="attn_cached", compiler_params=_params(1),
    )(qt, past_ckv, past_kr, ckv_new, kr_new, cgst, pw["wkc"], pw["wkr"], pw["wvt"])


def _merge_kernel(x_ref, partial_ref, g2_ref, oct_ref, wbr_ref, wout_ref, lng_ref, lnb_ref, o_ref,
                  *, c_lo, d_c, alpha):
    yc = _dot_tn(oct_ref[0], wbr_ref[c_lo:c_lo + d_c, :])
    merged = partial_ref[0].astype(F32) + g2_ref[0].astype(F32) * yc
    y = _dot(merged.astype(BF16), wout_ref[...])
    o_ref[0] = _layer_norm(alpha * x_ref[0] + y, lng_ref[...], lnb_ref[...])


def _merge_call(x, partial, g2, oct, pw, l, *, tr, alpha):
    bsz, seq, d_model = x.shape
    d_a, d_b, d_c, _ = pw["dims"]
    kern = functools.partial(_merge_kernel, c_lo=d_a + d_b, d_c=d_c, alpha=alpha)

    def row_spec(width):
        return pl.BlockSpec((1, tr, width), lambda b, i: (b, i, 0))

    return pl.pallas_call(
        kern, grid=(bsz, seq // tr),
        in_specs=[row_spec(d_model), row_spec(d_model), row_spec(d_model),
                  pl.BlockSpec((1, d_c, tr), lambda b, i: (b, 0, i)),
                  _layer_spec(pw["wbr"], l), _layer_spec(pw["wout"], l),
                  _layer_spec(pw["ln_g"], l), _layer_spec(pw["ln_b"], l)],
        out_specs=row_spec(d_model),
        out_shape=jax.ShapeDtypeStruct((bsz, seq, d_model), F32),
        name="merge", compiler_params=_params(2),
    )(x, partial, g2, oct, pw["wbr"], pw["wout"], pw["ln_g"], pw["ln_b"])


def _mem_kernel(m_ref, w_ref, k_ref, v_ref, *, d_m):
    kv = _dot(m_ref[...].astype(BF16), w_ref[...])
    k_ref[...] = kv[:, 0:d_m]
    v_ref[...] = kv[:, d_m:2 * d_m]


def _mem_call(mem2d, wmem, l, *, tr):
    rows, d_model = mem2d.shape
    d_m = wmem.shape[2] // 2
    return pl.pallas_call(
        functools.partial(_mem_kernel, d_m=d_m), grid=(rows // tr,),
        in_specs=[pl.BlockSpec((tr, d_model), lambda i: (i, 0)), _layer_spec(wmem, l)],
        out_specs=[pl.BlockSpec((tr, d_m), lambda i: (i, 0)), pl.BlockSpec((tr, d_m), lambda i: (i, 0))],
        out_shape=[jax.ShapeDtypeStruct((rows, d_m), F32), jax.ShapeDtypeStruct((rows, d_m), F32)],
        name="mem", compiler_params=_params(1),
    )(mem2d, wmem)


def _block_diag(w):
    h, n, _ = w.shape
    eye = jnp.eye(h, dtype=w.dtype)
    return (eye[:, None, :, None] * w[:, :, None, :]).reshape(h * n, h * n)


def _gate_blocks(w_r, w_i):
    per = MXU_TILE // w_r.shape[1]
    blocks = []
    for k in range(w_r.shape[0] // per):
        blocks.append(jnp.concatenate([_block_diag(w_r[k * per:(k + 1) * per]),
                                       _block_diag(w_i[k * per:(k + 1) * per])], axis=1))
    return jnp.stack(blocks)


def _w_in_kernel(wt_ref, o_ref, *, n_head, o_tail):
    o_ref[0, :, 0:n_head] = jnp.transpose(wt_ref[0, 0:n_head, :]).astype(BF16)
    o_ref[0, :, n_head:] = jnp.transpose(wt_ref[0, o_tail:, :]).astype(BF16)


def _w_in_call(w_in, n_head, o_tail, *, tk):
    depth, d_model, d_in = w_in.shape
    n_out = n_head + d_in - o_tail
    return pl.pallas_call(
        functools.partial(_w_in_kernel, n_head=n_head, o_tail=o_tail), grid=(depth, d_model // tk),
        in_specs=[pl.BlockSpec((1, d_in, tk), lambda l, i: (l, 0, i))],
        out_specs=pl.BlockSpec((1, tk, n_out), lambda l, i: (l, i, 0)),
        out_shape=jax.ShapeDtypeStruct((depth, d_model, n_out), BF16),
        name="w_in", compiler_params=_params(2),
    )(jnp.swapaxes(w_in, 1, 2))


def _prep_weights(sp_lens, w_in, gmlp_ln_g, gmlp_ln_b, gmlp_ws, gmlp_bs, lru_conv_w,
                  lru_conv_b, lru_w_r, lru_b_r, lru_w_i, lru_b_i, lru_lambda, mla_q_norm, mla_w_uq,
                  mla_kv_norm, mla_w_ukv, mem_w_k, mem_w_v, w_br, w_out, ln_g, ln_b):
    depth, d_model, _ = w_in.shape
    d_a = gmlp_ln_g.shape[1]
    d_b = lru_lambda.shape[1]
    q_lora = mla_q_norm.shape[1]
    kv_lora = mla_kv_norm.shape[1]
    d_c = H_C * D_V
    d_m = H_M * DH_M
    o_cq = 3 * d_a + 2 * d_b
    o_kr = o_cq + q_lora + kv_lora
    o_cg = o_kr + D_ROPE
    o_mq = o_cg + d_c
    o_g = o_mq + d_m
    n_head = o_kr + LANES
    assert o_kr % LANES == 0 and (o_g - o_mq) % LANES == 0 and D_ROPE <= LANES and n_head <= o_mq
    cols = {"a_u": (0, d_a), "a_v": (d_a, 2 * d_a), "a_g": (2 * d_a, 3 * d_a),
            "b_x": (3 * d_a, 3 * d_a + d_b), "b_g": (3 * d_a + d_b, o_cq), "c_all": (o_cq, n_head),
            "m_q": (n_head, n_head + d_m)}
    for k in range(N_BRANCH):
        cols[f"g{k}"] = (n_head + d_m + k * d_model, n_head + d_m + (k + 1) * d_model)
    w1 = _w_in_call(w_in, n_head, o_mq, tk=LANES)
    wcgt = jnp.swapaxes(w_in[:, :, o_cg:o_mq], 1, 2).astype(BF16)

    wq = mla_w_uq.reshape(depth, q_lora, H_C, D_NOPE + D_ROPE) * _EXP2_SCALE
    wq = jnp.concatenate([wq, jnp.zeros((depth, q_lora, H_C, LANES - D_NOPE - D_ROPE), wq.dtype)], axis=3)
    wuqt = jnp.swapaxes(wq.reshape(depth, q_lora, H_C * LANES), 1, 2).astype(BF16)

    wkv = mla_w_ukv.reshape(depth, kv_lora, H_C, D_NOPE + D_V)
    wk = wkv[..., :D_NOPE]
    wv = wkv[..., D_NOPE:]
    wkc = jnp.concatenate([wk, jnp.zeros((depth, kv_lora, H_C, LANES - D_NOPE), wk.dtype)], axis=3)
    wkc = wkc.reshape(depth, kv_lora, H_C * LANES).astype(BF16)
    lane = jnp.arange(H_C * LANES) % LANES
    wkr = ((lane[None, :] - D_NOPE) == jnp.arange(D_ROPE)[:, None]).astype(BF16)
    wkr = jnp.broadcast_to(wkr, (depth,) + wkr.shape)
    wv = jnp.concatenate([wv, jnp.zeros((depth, kv_lora, H_C, V_ROWS - D_V), wv.dtype)], axis=3)
    wvt = jnp.swapaxes(wv.reshape(depth, kv_lora, H_C * V_ROWS), 1, 2).astype(BF16)

    def row(p):
        return p[:, None, :]

    pw = dict(
        dims=(d_a, d_b, d_c, d_m), q_lora=q_lora, kv_lora=kv_lora, cols=cols,
        w1=w1, wcgt=wcgt,
        gln_g=row(gmlp_ln_g), gln_b=row(gmlp_ln_b),
        conv_w=lru_conv_w, conv_b=row(lru_conv_b),
        wgate=jnp.stack([_gate_blocks(lru_w_r[l], lru_w_i[l]) for l in range(depth)]).astype(BF16),
        b_r=row(lru_b_r), b_i=row(lru_b_i), lam=row(lru_lambda),
        q_norm=row(mla_q_norm), wuqt=wuqt, kv_norm=row(mla_kv_norm),
        wkc=wkc, wkr=wkr, wvt=wvt,
        wbr=w_br.astype(BF16), wout=w_out.astype(BF16), ln_g=row(ln_g), ln_b=row(ln_b),
        wmem=jnp.concatenate([mem_w_k, mem_w_v], axis=2).astype(BF16),
    )
    for sp in sp_lens:
        pw["wsp%d" % sp] = jnp.tril(gmlp_ws[:, :, :sp, :sp]).astype(BF16)
        pw["bsp%d" % sp] = jnp.repeat(jnp.swapaxes(gmlp_bs[:, :, :sp], 1, 2), d_a // G_A, axis=2)
    return pw


def _rope_tables(pos):
    half = D_ROPE // 2
    freq = ROPE_BASE ** (-jnp.arange(half, dtype=F32) / half)
    ang = pos.astype(F32)[:, None] * freq[None, :]
    cos, sin = jnp.cos(ang), jnp.sin(ang)
    return (jnp.transpose(cos), jnp.transpose(sin),
            jnp.concatenate([cos, cos], axis=1), jnp.concatenate([-sin, sin], axis=1))


def _pad_conv_state(conv):
    return jnp.pad(conv, ((0, 0), (SUBLANES - (CONV_W - 1), 0), (0, 0)))


def _trunk_layer(x, pw, l, sp, rope, conv0, h0, mk, mv, past_ckv, past_kr, *, tm, alpha):
    bsz, seq, d_model = x.shape
    first_chunk = past_ckv is None
    if first_chunk:
        x_rows, seq_len, tq = x, tm, N_BUF * ATTN_TK
    else:
        x_rows, seq_len, tq = x.reshape(1, bsz * seq, d_model), seq, bsz * seq
        cos_t, sin_t, ck_t, sk_t = rope
        rope = (jnp.tile(cos_t, (1, bsz)), jnp.tile(sin_t, (1, bsz)), jnp.tile(ck_t, (bsz, 1)), jnp.tile(sk_t, (bsz, 1)))
    outs = _branch_call(x_rows, pw, l, sp, _pad_conv_state(conv0), h0[:, None, :], mk, mv, rope,
                        tm=tm, seq_len=seq_len, tq=tq, first_chunk=first_chunk)
    partial, g2, cgst, qt, ckv_new, kr_new, conv_pad, h_new = outs[:8]
    if first_chunk:
        kcat, vt = outs[8:]
        v_rows = None
        oct = _attn_causal_call(qt, kcat, vt, cgst)
    else:
        ckv_new, kr_new, v_rows = (o.reshape(bsz, seq, -1) for o in (ckv_new, kr_new, outs[8]))

        def per_seq(a):
            return jnp.transpose(a.reshape(a.shape[-2], bsz, seq), (1, 0, 2))

        oct = _attn_cached_call(per_seq(qt), past_ckv, past_kr, ckv_new, kr_new, per_seq(cgst), pw, l)
        oct = jnp.transpose(oct, (1, 0, 2)).reshape(1, oct.shape[1], bsz * seq)
    x_new = _merge_call(x_rows, partial, g2, oct, pw, l, tr=MERGE_TILE if first_chunk else tm, alpha=alpha)
    return (x_new.reshape(bsz, seq, d_model), v_rows, conv_pad[:, SUBLANES - (CONV_W - 1):], h_new[:, 0],
            ckv_new, kr_new)


def kernel(x_prompt, x_sample, mem_prompt, cache_mla_ckv, cache_mla_krope, cache_mem_k, cache_mem_v,
           state_lru_h, state_lru_conv, w_in, gmlp_ln_g, gmlp_ln_b, gmlp_ws, gmlp_bs,
           lru_conv_w, lru_conv_b, lru_w_r, lru_b_r, lru_w_i, lru_b_i, lru_lambda,
           mla_q_norm, mla_w_uq, mla_kv_norm, mla_w_ukv, mem_w_k, mem_w_v, w_br, w_out, ln_g, ln_b):
    bp, tp, d_model = x_prompt.shape
    bs, ts, _ = x_sample.shape
    depth = w_in.shape[0]
    past_len = cache_mla_ckv.shape[2]
    n_mem = mem_prompt.shape[1]
    d_b = lru_lambda.shape[1]
    alpha = (2.0 * depth) ** 0.25
    sp_p, sp_s = min(tp, A_CHUNK), min(ts, A_CHUNK)
    pw = _prep_weights(sorted({sp_p, sp_s}), w_in, gmlp_ln_g, gmlp_ln_b, gmlp_ws, gmlp_bs, lru_conv_w,
                       lru_conv_b, lru_w_r, lru_b_r, lru_w_i, lru_b_i, lru_lambda, mla_q_norm, mla_w_uq,
                       mla_kv_norm, mla_w_ukv, mem_w_k, mem_w_v, w_br, w_out, ln_g, ln_b)
    rope_p = _rope_tables(jnp.arange(tp))
    rope_s = _rope_tables(past_len + jnp.arange(ts))
    tm_p = min(tp, ROW_TILE)
    assert tp % (N_BUF * ATTN_TK) == 0 and tm_p % ATTN_TK == 0 and tp % MERGE_TILE == 0 and ts % SUBLANES == 0
    zero_conv = jnp.zeros((bp, CONV_W - 1, d_b), F32)
    zero_h = jnp.zeros((bp, d_b), F32)

    xp, xs = x_prompt, x_sample
    acc = [[] for _ in range(11)]
    for l in range(depth):
        mk, mv = _mem_call(mem_prompt.reshape(bp * n_mem, d_model), pw["wmem"], l, tr=n_mem)
        mk = mk.reshape(bp, n_mem, -1)
        mv = mv.reshape(bp, n_mem, -1)
        xp, _, conv_n, h_n, ckv_n, kr_n = _trunk_layer(
            xp, pw, l, sp_p, rope_p, zero_conv, zero_h, mk, mv, None, None, tm=tm_p, alpha=alpha)
        for k, val in zip(range(6), (ckv_n, kr_n, mk.reshape(bp, n_mem, H_M, DH_M),
                                     mv.reshape(bp, n_mem, H_M, DH_M), h_n, conv_n)):
            acc[k].append(val)
        xs, v_n, conv_n, h_n, ckv_n, kr_n = _trunk_layer(
            xs, pw, l, sp_s, rope_s, state_lru_conv[l], state_lru_h[l],
            cache_mem_k[l].reshape(bs, n_mem, -1), cache_mem_v[l].reshape(bs, n_mem, -1),
            cache_mla_ckv, cache_mla_krope, tm=bs * ts, alpha=alpha)
        for k, val in zip(range(6, 11), (ckv_n, kr_n, h_n, conv_n, v_n)):
            acc[k].append(val)
    return (xp, xs) + tuple(jnp.stack(a) for a in acc)
```

```python
import functools
import math

import jax
import jax.numpy as jnp
from jax import lax
from jax.experimental import pallas as pl
from jax.experimental.pallas import tpu as pltpu

CHUNK = 64
G_A = 4
A_CHUNK = 128
H_B = 8
CONV_W = 4
LRU_C = 8.0
H_C = 8
D_NOPE = 64
D_ROPE = 32
D_V = 64
ROPE_BASE = 10000.0
H_M = 4
DH_M = 64
N_BRANCH = 4
EPS = 1e-6

LANES = 128
SUBLANES = 8
VMEM_LIMIT = 56 * 1024 * 1024
ROW_TILE = 512
ATTN_TK = 256
MXU_TILE = 256
MERGE_TILE = 1024

F32 = jnp.float32
BF16 = jnp.bfloat16


def _dot(a, b):
    return jnp.dot(a, b, preferred_element_type=F32)


def _dot_nt(a, b):
    return lax.dot_general(a, b, (((1,), (1,)), ((), ())), preferred_element_type=F32)


def _dot_tn(a, b):
    return lax.dot_general(a, b, (((0,), (0,)), ((), ())), preferred_element_type=F32)


def _sigmoid(x):
    return 1.0 / (1.0 + jnp.exp(-x))


def _silu(x):
    return x * _sigmoid(x)


def _gelu(x):
    return jax.nn.gelu(x)


def _expm1_nonpos(x):
    u = jnp.exp(x)
    near = (u - 1.0) * x / jnp.log(jnp.where(u == 1.0, 2.0, jnp.maximum(u, 0.5)))
    return jnp.where(u == 1.0, x, jnp.where(u > 0.5, near, u - 1.0))


def _layer_norm(x, g, b):
    mu = jnp.mean(x, -1, keepdims=True)
    var = jnp.mean(jnp.square(x - mu), -1, keepdims=True)
    return (x - mu) * lax.rsqrt(var + EPS) * g + b


def _rms_norm(x, g):
    return x * lax.rsqrt(jnp.mean(jnp.square(x), -1, keepdims=True) + EPS) * g


def _const_spec(shape):
    nd = len(shape)
    return pl.BlockSpec(shape, lambda *_: (0,) * nd, pipeline_mode=pl.Buffered(1))


def _layer_spec(arr, l):
    if arr.ndim == 2:
        return _const_spec(arr.shape)
    nd = arr.ndim - 1
    return pl.BlockSpec((None,) + arr.shape[1:], lambda *_: (l,) + (0,) * nd, pipeline_mode=pl.Buffered(1))


def _params(n_axes):
    return pltpu.CompilerParams(dimension_semantics=("arbitrary",) * n_axes,
                                vmem_limit_bytes=VMEM_LIMIT)


V_ROWS = 80


def _keys_values(ckv, kr, wkc_ref, wkr_ref, wvt_ref):
    c = ckv.astype(BF16)
    kcat = (_dot(c, wkc_ref[...]) + _dot(kr.astype(BF16), wkr_ref[...])).astype(BF16)
    v_t = _dot_nt(wvt_ref[...], c)
    row = lax.broadcasted_iota(jnp.int32, (v_t.shape[0], 1), 0)
    return kcat, jnp.where(row % V_ROWS == D_V, 1.0, v_t).astype(BF16)


def _branch_kernel(x_ref, w1_ref, wcgt_ref, glng_ref, glnb_ref, wsp_ref, bsp_ref,
                   convw_ref, convb_ref, wgate_ref, br_ref, bi_ref, lam_ref, conv0_ref, h0_ref, qnorm_ref, wuqt_ref,
                   kvnorm_ref, cosq_ref, sinq_ref, ck_ref, sk_ref, mk_ref, mv_ref, wbr_ref,
                   *rest, layer, first_chunk, tm, seq_len, sp_len, d_a, d_b, d_c, d_m, q_lora, kv_lora, cols):
    if first_chunk:
        (wkc_ref, wkr_ref, wvt_ref, partial_ref, g2_ref, cgst_ref, qt_ref, ckv_ref, kr_ref, conv_ref, h_ref,
         kcat_ref, vt_ref, merged_scr) = rest
    else:
        partial_ref, g2_ref, cgst_ref, qt_ref, ckv_ref, kr_ref, conv_ref, h_ref, v_ref, merged_scr = rest
    t = pl.program_id(1)
    xb = x_ref[0].astype(BF16)

    def vec(ref):
        return ref[layer:layer + 1, :]

    def zin(name):
        lo, hi = cols[name]
        return _dot(xb, w1_ref[:, lo:hi])

    z_u, z_v, z_ag, z_g0 = zin("a_u"), zin("a_v"), zin("a_g"), zin("g0")
    u = _gelu(z_u)
    v = _layer_norm(_gelu(z_v), vec(glng_ref), vec(glnb_ref))
    if not first_chunk:
        v_ref[0] = v
    vb = v.astype(BF16)
    n_groups = d_a // LANES
    row_blocks = []
    for c in range(tm // sp_len):
        col_blocks = [_dot(wsp_ref[g], vb[c * sp_len:(c + 1) * sp_len, g * LANES:(g + 1) * LANES])
                      for g in range(n_groups)]
        row_blocks.append(jnp.concatenate(col_blocks, axis=1) + bsp_ref[...])
    s = row_blocks[0] if len(row_blocks) == 1 else jnp.concatenate(row_blocks, axis=0)
    oa = (u * s) * _silu(z_ag)
    merged_scr[...] = _sigmoid(z_g0) * _dot(oa.astype(BF16), wbr_ref[0:d_a, :])

    @pl.when(t == 0)
    def _():
        conv_ref[...] = conv0_ref[...]
        h_ref[...] = h0_ref[...]

    bx = zin("b_x")
    n_seq = tm // seq_len
    span = SUBLANES + seq_len
    stacked = jnp.concatenate(
        [piece for q in range(n_seq) for piece in (conv_ref[q], bx[q * seq_len:(q + 1) * seq_len])], axis=0)
    xc = vec(convb_ref)
    for k in range(CONV_W):
        shift = CONV_W - 1 - k
        rolled = stacked if shift == 0 else pltpu.roll(stacked, shift, axis=0)
        sh = [rolled[q * span + SUBLANES:(q + 1) * span] for q in range(n_seq)]
        xc = xc + (sh[0] if n_seq == 1 else jnp.concatenate(sh, axis=0)) * convw_ref[k:k + 1, :]
    for q in range(n_seq):
        conv_ref[q] = bx[(q + 1) * seq_len - SUBLANES:(q + 1) * seq_len]

    xcb = xc.astype(BF16)
    n_blk = d_b // MXU_TILE
    ri = [_dot(xcb[:, k * MXU_TILE:(k + 1) * MXU_TILE], wgate_ref[k]) for k in range(n_blk)]
    z_bg, z_g1, z_g2 = zin("b_g"), zin("g1"), zin("g2")
    z_mq, z_g3, z_c = zin("m_q"), zin("g3"), zin("c_all")
    cg_t = _dot_nt(wcgt_ref[...], xb)
    r = _sigmoid(jnp.concatenate([p[:, 0:MXU_TILE] for p in ri], axis=1) + vec(br_ref))
    i_gate = _sigmoid(jnp.concatenate([p[:, MXU_TILE:2 * MXU_TILE] for p in ri], axis=1) + vec(bi_ref))
    neg_lam = -vec(lam_ref)
    softplus = jnp.maximum(neg_lam, 0.0) + jnp.log1p(jnp.exp(-jnp.abs(neg_lam)))
    log_a = (-LRU_C * r) * softplus
    a = jnp.exp(log_a)
    bval = jnp.sqrt(-_expm1_nonpos(2.0 * log_a)) * (i_gate * xc)
    in_group = lax.broadcasted_iota(jnp.int32, (tm, 1), 0) % SUBLANES
    d = 1
    while d < SUBLANES:
        keep = in_group >= d
        a_sh = pltpu.roll(a, d, axis=0)
        b_sh = pltpu.roll(bval, d, axis=0)
        bval = jnp.where(keep, a * b_sh + bval, bval)
        a = jnp.where(keep, a * a_sh, a)
        d *= 2
    groups_per_seq = seq_len // SUBLANES
    h_groups = []
    for g in range(tm // SUBLANES):
        q = g // groups_per_seq
        if g % groups_per_seq == 0:
            carry = h_ref[q]
        lo = g * SUBLANES
        h_g = a[lo:lo + SUBLANES] * carry + bval[lo:lo + SUBLANES]
        h_groups.append(h_g)
        carry = h_g[SUBLANES - 1:SUBLANES]
        if (g + 1) % groups_per_seq == 0:
            h_ref[q] = carry
    h = jnp.concatenate(h_groups, axis=0)
    ob = h * _silu(z_bg)
    merged_scr[...] += _sigmoid(z_g1) * _dot(ob.astype(BF16), wbr_ref[d_a:d_a + d_b, :])
    g2_ref[0] = _sigmoid(z_g2).astype(BF16)

    mq = z_mq.astype(BF16)
    lane = lax.broadcasted_iota(jnp.int32, (1, LANES), 1)
    om_rows = []
    for q in range(n_seq):
        mkb = mk_ref[q].astype(BF16)
        mvb = mv_ref[q].astype(BF16)
        slabs = []
        for p in range(d_m // LANES):
            mq_p = mq[q * seq_len:(q + 1) * seq_len, p * LANES:(p + 1) * LANES]
            mk_p = mkb[:, p * LANES:(p + 1) * LANES]
            mv_p = mvb[:, p * LANES:(p + 1) * LANES]
            acc = None
            for half in range(LANES // DH_M):
                sel = (lane >= half * DH_M) & (lane < (half + 1) * DH_M)
                sc = _dot_nt(mq_p, jnp.where(sel, mk_p, jnp.zeros_like(mk_p))) * (DH_M ** -0.5)
                e = jnp.exp(sc - jnp.max(sc, -1, keepdims=True))
                prob = (e / jnp.sum(e, -1, keepdims=True)).astype(BF16)
                o = _dot(prob, jnp.where(sel, mv_p, jnp.zeros_like(mv_p)))
                acc = o if acc is None else acc + o
            slabs.append(acc)
        om_rows.append(jnp.concatenate(slabs, axis=1))
    om = om_rows[0] if n_seq == 1 else jnp.concatenate(om_rows, axis=0)
    m_lo = d_a + d_b + d_c
    partial_ref[0] = (merged_scr[...]
                      + _sigmoid(z_g3) * _dot(om.astype(BF16), wbr_ref[m_lo:m_lo + d_m, :])).astype(BF16)
    cgst_ref[0] = _silu(cg_t)

    ckv = _rms_norm(z_c[:, q_lora:q_lora + kv_lora], vec(kvnorm_ref))
    half = D_ROPE // 2
    z_kr = z_c[:, q_lora + kv_lora:q_lora + kv_lora + LANES]
    kr_swapped = jnp.where(lane < half, pltpu.roll(z_kr, LANES - half, axis=1), pltpu.roll(z_kr, half, axis=1))
    kr = z_kr[:, 0:D_ROPE] * ck_ref[...] + kr_swapped[:, 0:D_ROPE] * sk_ref[...]
    ckv_ref[0] = ckv
    kr_ref[0] = kr
    if first_chunk:
        kcat_ref[0], vt_all = _keys_values(ckv, kr, wkc_ref, wkr_ref, wvt_ref)
        for j in range(tm // ATTN_TK):
            vt_ref[0, j] = vt_all[:, j * ATTN_TK:(j + 1) * ATTN_TK]
    cqn =_rms_norm(z_c[:, 0:q_lora], vec(qnorm_ref)).astype(BF16)
    q_t = _dot_nt(wuqt_ref[...], cqn)
    cos_t, sin_t = cosq_ref[...], sinq_ref[...]
    for hd in range(H_C):
        lo = hd * LANES
        x1 = q_t[lo + D_NOPE:lo + D_NOPE + half]
        x2 = q_t[lo + D_NOPE + half:lo + D_NOPE + D_ROPE]
        qt_ref[0, 0, lo:lo + D_NOPE, :] = q_t[lo:lo + D_NOPE].astype(BF16)
        qt_ref[0, 0, lo + D_NOPE:lo + D_NOPE + half, :] = (x1 * cos_t - x2 * sin_t).astype(BF16)
        qt_ref[0, 0, lo + D_NOPE + half:lo + D_NOPE + D_ROPE, :] = (x1 * sin_t + x2 * cos_t).astype(BF16)
        qt_ref[0, 0, lo + D_NOPE + D_ROPE:lo + LANES, :] = jnp.zeros((LANES - D_NOPE - D_ROPE, tm), BF16)


def _branch_call(x, pw, l, sp, conv0_pad, h0, mk, mv, rope, *, tm, seq_len, tq, first_chunk):
    bsz, seq, d_model = x.shape
    d_a, d_b, d_c, d_m = pw["dims"]
    q_lora, kv_lora = pw["q_lora"], pw["kv_lora"]
    n_mem = mk.shape[1]
    cos_t, sin_t, ck_t, sk_t = rope
    grid = (bsz, seq // tm)

    def row_spec(width):
        return pl.BlockSpec((1, tm, width), lambda b, t: (b, t, 0))

    def col_spec(height):
        return pl.BlockSpec((1, height, tm), lambda b, t: (b, 0, t))

    def tab_spec(width):
        return pl.BlockSpec((tm, width), lambda b, t: (t, 0))

    n_seq = tm // seq_len

    def batch_spec(rows, width):
        return pl.BlockSpec((n_seq, rows, width), lambda b, t: (b, 0, 0))

    names = ["w1", "wcgt", "gln_g", "gln_b", "wsp%d" % sp, "bsp%d" % sp, "conv_w", "conv_b",
             "wgate", "b_r", "b_i", "lam"]
    names2 = ["q_norm", "wuqt", "kv_norm"]
    in_specs = ([row_spec(d_model)] + [_layer_spec(pw[n], l) for n in names]
                + [batch_spec(SUBLANES, d_b), batch_spec(1, d_b)] + [_layer_spec(pw[n], l) for n in names2]
                + [pl.BlockSpec((D_ROPE // 2, tm), lambda b, t: (0, t)), pl.BlockSpec((D_ROPE // 2, tm), lambda b, t: (0, t)),
                   tab_spec(D_ROPE), tab_spec(D_ROPE), batch_spec(n_mem, d_m), batch_spec(n_mem, d_m),
                   _layer_spec(pw["wbr"], l)])
    operands = ([x] + [pw[n] for n in names] + [conv0_pad, h0] + [pw[n] for n in names2]
                + [cos_t, sin_t, ck_t, sk_t, mk, mv, pw["wbr"]])
    out_shape = [
        jax.ShapeDtypeStruct((bsz, seq, d_model), BF16),
        jax.ShapeDtypeStruct((bsz, seq, d_model), BF16),
        jax.ShapeDtypeStruct((bsz, d_c, seq), F32),
        jax.ShapeDtypeStruct((bsz, seq // tq, H_C * LANES, tq), BF16),
        jax.ShapeDtypeStruct((bsz, seq, kv_lora), F32),
        jax.ShapeDtypeStruct((bsz, seq, D_ROPE), F32),
        jax.ShapeDtypeStruct((bsz * n_seq, SUBLANES, d_b), F32),
        jax.ShapeDtypeStruct((bsz * n_seq, 1, d_b), F32),
    ]
    out_specs = [
        row_spec(d_model), row_spec(d_model), col_spec(d_c),
        pl.BlockSpec((1, 1, H_C * LANES, tm), lambda b, t: (b, t // (tq // tm), 0, t % (tq // tm))),
        row_spec(kv_lora), row_spec(D_ROPE), batch_spec(SUBLANES, d_b), batch_spec(1, d_b),
    ]
    if first_chunk:
        for name in ("wkc", "wkr", "wvt"):
            in_specs.append(_layer_spec(pw[name], l))
            operands.append(pw[name])
        out_shape += [jax.ShapeDtypeStruct((bsz, seq, H_C * LANES), BF16),
                      jax.ShapeDtypeStruct((bsz, seq // ATTN_TK, H_C * V_ROWS, ATTN_TK), BF16)]
        out_specs += [row_spec(H_C * LANES),
                      pl.BlockSpec((1, tm // ATTN_TK, H_C * V_ROWS, ATTN_TK), lambda b, t: (b, t, 0, 0))]
    else:
        out_shape.append(jax.ShapeDtypeStruct((bsz, seq, d_a), F32))
        out_specs.append(row_spec(d_a))
    kern = functools.partial(
        _branch_kernel, layer=l, first_chunk=first_chunk, tm=tm, seq_len=seq_len, sp_len=sp, d_a=d_a, d_b=d_b, d_c=d_c, d_m=d_m,
        q_lora=q_lora, kv_lora=kv_lora, cols=pw["cols"])
    return pl.pallas_call(
        kern, grid=grid, in_specs=in_specs, out_specs=out_specs, out_shape=out_shape,
        scratch_shapes=[pltpu.VMEM((tm, d_model), F32)],
        name="branch", compiler_params=_params(2),
    )(*operands)


HEADS_PER_STEP = 4
_EXP2_SCALE = (D_NOPE + D_ROPE) ** -0.5 * math.log2(math.e)


def _scores(k_tile, q_tile, out_refs=None):
    res = [_dot(k_tile[:, h * LANES:(h + 1) * LANES], q_tile[h * LANES:(h + 1) * LANES, :])
           for h in range(k_tile.shape[1] // LANES)]
    if out_refs is None:
        return res
    for h, r in enumerate(res):
        out_refs[h] = r
    return None


def _softmax_tile(s_t, m_old):
    m_new = jnp.maximum(m_old, jnp.max(s_t, axis=0, keepdims=True))
    return jnp.exp2(s_t - m_new).astype(BF16), m_new, jnp.exp2(m_old - m_new)


def _attn_output(acc, cgs_t):
    return (acc[0:D_V] / acc[D_V:D_V + 1] * cgs_t).astype(BF16)


N_BUF = 4


def _attn_causal_kernel(qt_ref, k_ref, vt_ref, cgst_ref, bias_ref, o_ref, *scratch, tq, tk):
    s_buf, e_buf, acc_scr = scratch[0:N_BUF], scratch[N_BUF:2 * N_BUF], scratch[2 * N_BUF]
    i = pl.program_id(2)
    nh = HEADS_PER_STEP
    q_cur = qt_ref.at[0, i]
    q_nxt = qt_ref.at[0, jnp.minimum(i + 1, pl.num_programs(2) - 1)]

    def k_tile(j):
        return k_ref[0, pl.ds(pl.multiple_of(j * tk, tk), tk), :]

    def pv(j, e_in, c0):
        vt = vt_ref[0, j]
        return [_dot(vt[h * V_ROWS:(h + 1) * V_ROWS, :], e_in[h, :, c0:tq]) for h in range(nh)]

    def acc_update(pvs, alphas, c0):
        for h in range(nh):
            acc_scr[h, :, c0:tq] = acc_scr[h, :, c0:tq] * alphas[h][:, c0:tq] + pvs[h]

    def softmax(slot, carry, c0, add_bias):
        new = []
        for h in range(nh):
            m_old, a_prev, _ = carry[h]
            s_t = s_buf[slot][h, :, c0:tq]
            if add_bias:
                lead = s_t[:, 0:tk] + bias_ref[...]
                s_t = lead if c0 + tk == tq else jnp.concatenate([lead, s_t[:, tk:]], axis=1)
            e, m_new, alpha = _softmax_tile(s_t, m_old[:, c0:tq])
            e_buf[slot][h, :, c0:tq] = e
            if c0:
                m_new = jnp.concatenate([m_old[:, 0:c0], m_new], axis=1)
                alpha = jnp.concatenate([jnp.ones((1, c0), F32), alpha], axis=1)
            new.append((m_new, alpha, a_prev))
        return tuple(new)

    def stage(n, slot, carry):
        pvs = pv(jnp.maximum(n - 2, 0), e_buf[(slot + 2) % N_BUF], 0)
        _scores(k_tile(n + 3), q_cur, s_buf[(slot + 3) % N_BUF])
        new = softmax(slot, carry, 0, False)
        acc_update(pvs, [c[2] for c in carry], 0)
        return new

    for slot in (N_BUF - 2, N_BUF - 1):
        e_buf[slot][...] = jnp.zeros(e_buf[slot].shape, BF16)
    acc_scr[...] = jnp.zeros(acc_scr.shape, F32)

    @pl.when(i == 0)
    def _():
        for slot in range(N_BUF - 1):
            _scores(k_tile(slot), q_cur, s_buf[slot])

    ones = jnp.ones((1, tq), F32)
    carry = tuple((jnp.full((1, tq), -jnp.inf, F32), ones, ones) for _ in range(nh))

    def body(t, carry):
        for slot in range(N_BUF):
            carry = stage(N_BUF * t + slot, slot, carry)
        return carry

    carry = lax.fori_loop(0, i, body, carry)

    j0 = N_BUF * i
    last = N_BUF - 1
    for d in range(N_BUF):
        c_prev = max(d - 2, 0) * tk
        pvs = pv(jnp.maximum(j0 + d - 2, 0), e_buf[(d + 2) % N_BUF], c_prev)
        if d == 0:
            k_last = k_tile(j0 + last)
            for h in range(nh):
                s_buf[last][h, :, tq - tk:tq] = _dot(
                    k_last[:, h * LANES:(h + 1) * LANES],
                    q_cur[h * LANES:(h + 1) * LANES, tq - tk:tq]) + bias_ref[...]
        else:
            _scores(k_tile(d - 1), q_nxt, s_buf[d - 1])
        alphas = [c[2] for c in carry]
        carry = softmax(d, carry, d * tk, d < last)
        acc_update(pvs, alphas, c_prev)
    acc_update(pv(j0 + last - 1, e_buf[last - 1], (last - 1) * tk), [c[2] for c in carry], (last - 1) * tk)
    acc_update(pv(j0 + last, e_buf[last], last * tk), [c[1] for c in carry], last * tk)
    for h in range(nh):
        o_ref[0, h * D_V:(h + 1) * D_V, :] = _attn_output(acc_scr[h], cgst_ref[0, h * D_V:(h + 1) * D_V, :])


def _attn_cached_kernel(qt_ref, pckv_ref, pkr_ref, ckv_ref, kr_ref, cgst_ref, wkc_ref, wkr_ref, wvt_ref, o_ref):
    ckv = jnp.concatenate([pckv_ref[0], ckv_ref[0]], axis=0)
    kr = jnp.concatenate([pkr_ref[0], kr_ref[0]], axis=0)
    kcat, vt = _keys_values(ckv, kr, wkc_ref, wkr_ref, wvt_ref)
    tq = qt_ref.shape[-1]
    scores = _scores(kcat, qt_ref.at[0])
    for h in range(len(scores)):
        e, _, _ = _softmax_tile(scores[h], jnp.full((1, tq), -jnp.inf, F32))
        acc = _dot(vt[h * V_ROWS:(h + 1) * V_ROWS, :], e)
        o_ref[0, h * D_V:(h + 1) * D_V, :] = _attn_output(acc, cgst_ref[0, h * D_V:(h + 1) * D_V, :])


def _attn_causal_call(qt, kcat, vt, cgst):
    bsz, n_qt, _, tq = qt.shape
    seq = n_qt * tq
    t_k = kcat.shape[1]
    nkt, _, tk = vt.shape[1:]
    nh = HEADS_PER_STEP
    assert tq == N_BUF * tk and seq == t_k and tk % CHUNK == 0
    chunk = jnp.arange(tk) // CHUNK
    bias = jnp.where(chunk[:, None] <= chunk[None, :], 0.0, -1e30).astype(F32)
    return pl.pallas_call(
        functools.partial(_attn_causal_kernel, tq=tq, tk=tk), grid=(bsz, H_C // nh, n_qt),
        in_specs=[pl.BlockSpec((1, n_qt, nh * LANES, tq), lambda b, p, i: (b, 0, p, 0)),
                  pl.BlockSpec((1, t_k, nh * LANES), lambda b, p, i: (b, 0, p)),
                  pl.BlockSpec((1, nkt, nh * V_ROWS, tk), lambda b, p, i: (b, 0, p, 0)),
                  pl.BlockSpec((1, nh * D_V, tq), lambda b, p, i: (b, p, i)),
                  _const_spec((tk, tk))],
        out_specs=pl.BlockSpec((1, nh * D_V, tq), lambda b, p, i: (b, p, i)),
        out_shape=jax.ShapeDtypeStruct((bsz, H_C * D_V, seq), BF16),
        scratch_shapes=([pltpu.VMEM((nh, tk, tq), F32)] * N_BUF + [pltpu.VMEM((nh, tk, tq), BF16)] * N_BUF
                        + [pltpu.VMEM((nh, V_ROWS, tq), F32)]),
        name="attn", compiler_params=_params(3),
    )(qt, kcat, vt, cgst, bias)


def _attn_cached_call(qt, past_ckv, past_kr, ckv_new, kr_new, cgst, pw, l):
    bsz, n_q, tq = qt.shape
    past_len, kv_lora = past_ckv.shape[2:]
    return pl.pallas_call(
        _attn_cached_kernel, grid=(bsz,),
        in_specs=[pl.BlockSpec((1, n_q, tq), lambda b: (b, 0, 0)),
                  pl.BlockSpec((None, 1, past_len, kv_lora), lambda b: (l, b, 0, 0)),
                  pl.BlockSpec((None, 1, past_len, D_ROPE), lambda b: (l, b, 0, 0)),
                  pl.BlockSpec((1, tq, kv_lora), lambda b: (b, 0, 0)),
                  pl.BlockSpec((1, tq, D_ROPE), lambda b: (b, 0, 0)),
                  pl.BlockSpec((1, H_C * D_V, tq), lambda b: (b, 0, 0)),
                  _layer_spec(pw["wkc"], l), _layer_spec(pw["wkr"], l), _layer_spec(pw["wvt"], l)],
        out_specs=pl.BlockSpec((1, H_C * D_V, tq), lambda b: (b, 0, 0)),
        out_shape=jax.ShapeDtypeStruct((bsz, H_C * D_V, tq), BF16),
        name="attn_cached", compiler_params=_params(1),
    )(qt, past_ckv, past_kr, ckv_new, kr_new, cgst, pw["wkc"], pw["wkr"], pw["wvt"])


def _merge_kernel(x_ref, partial_ref, g2_ref, oct_ref, wbr_ref, wout_ref, lng_ref, lnb_ref, o_ref,
                  *, layer, c_lo, d_c, alpha):
    yc = _dot_tn(oct_ref[0], wbr_ref[c_lo:c_lo + d_c, :])
    merged = partial_ref[0].astype(F32) + g2_ref[0].astype(F32) * yc
    y = _dot(merged.astype(BF16), wout_ref[...])
    o_ref[0] = _layer_norm(alpha * x_ref[0] + y, lng_ref[layer:layer + 1, :], lnb_ref[layer:layer + 1, :])


def _merge_call(x, partial, g2, oct, pw, l, *, tr, alpha):
    bsz, seq, d_model = x.shape
    d_a, d_b, d_c, _ = pw["dims"]
    kern = functools.partial(_merge_kernel, layer=l, c_lo=d_a + d_b, d_c=d_c, alpha=alpha)

    def row_spec(width):
        return pl.BlockSpec((1, tr, width), lambda b, i: (b, i, 0))

    return pl.pallas_call(
        kern, grid=(bsz, seq // tr),
        in_specs=[row_spec(d_model), row_spec(d_model), row_spec(d_model),
                  pl.BlockSpec((1, d_c, tr), lambda b, i: (b, 0, i)),
                  _layer_spec(pw["wbr"], l), _layer_spec(pw["wout"], l),
                  _layer_spec(pw["ln_g"], l), _layer_spec(pw["ln_b"], l)],
        out_specs=row_spec(d_model),
        out_shape=jax.ShapeDtypeStruct((bsz, seq, d_model), F32),
        name="merge", compiler_params=_params(2),
    )(x, partial, g2, oct, pw["wbr"], pw["wout"], pw["ln_g"], pw["ln_b"])


def _mem_kernel(m_ref, w_ref, k_ref, v_ref, *, d_m):
    kv = _dot(m_ref[...].astype(BF16), w_ref[...])
    k_ref[...] = kv[:, 0:d_m]
    v_ref[...] = kv[:, d_m:2 * d_m]


def _mem_call(mem2d, wmem, l, *, tr):
    rows, d_model = mem2d.shape
    d_m = wmem.shape[2] // 2
    return pl.pallas_call(
        functools.partial(_mem_kernel, d_m=d_m), grid=(rows // tr,),
        in_specs=[pl.BlockSpec((tr, d_model), lambda i: (i, 0)), _layer_spec(wmem, l)],
        out_specs=[pl.BlockSpec((tr, d_m), lambda i: (i, 0)), pl.BlockSpec((tr, d_m), lambda i: (i, 0))],
        out_shape=[jax.ShapeDtypeStruct((rows, d_m), F32), jax.ShapeDtypeStruct((rows, d_m), F32)],
        name="mem", compiler_params=_params(1),
    )(mem2d, wmem)


def _block_diag(w):
    h, n, _ = w.shape
    eye = jnp.eye(h, dtype=w.dtype)
    return (eye[:, None, :, None] * w[:, :, None, :]).reshape(h * n, h * n)


def _gate_blocks(w_r, w_i):
    per = MXU_TILE // w_r.shape[1]
    blocks = []
    for k in range(w_r.shape[0] // per):
        blocks.append(jnp.concatenate([_block_diag(w_r[k * per:(k + 1) * per]),
                                       _block_diag(w_i[k * per:(k + 1) * per])], axis=1))
    return jnp.stack(blocks)


def _w_in_kernel(wt_ref, o_ref, *, n_head, o_tail):
    o_ref[0, :, 0:n_head] = jnp.transpose(wt_ref[0, 0:n_head, :]).astype(BF16)
    o_ref[0, :, n_head:] = jnp.transpose(wt_ref[0, o_tail:, :]).astype(BF16)


def _w_in_call(w_in, n_head, o_tail, *, tk):
    depth, d_model, d_in = w_in.shape
    n_out = n_head + d_in - o_tail
    return pl.pallas_call(
        functools.partial(_w_in_kernel, n_head=n_head, o_tail=o_tail), grid=(depth, d_model // tk),
        in_specs=[pl.BlockSpec((1, d_in, tk), lambda l, i: (l, 0, i))],
        out_specs=pl.BlockSpec((1, tk, n_out), lambda l, i: (l, i, 0)),
        out_shape=jax.ShapeDtypeStruct((depth, d_model, n_out), BF16),
        name="w_in", compiler_params=_params(2),
    )(jnp.swapaxes(w_in, 1, 2))


def _prep_weights(sp_lens, w_in, gmlp_ln_g, gmlp_ln_b, gmlp_ws, gmlp_bs, lru_conv_w,
                  lru_conv_b, lru_w_r, lru_b_r, lru_w_i, lru_b_i, lru_lambda, mla_q_norm, mla_w_uq,
                  mla_kv_norm, mla_w_ukv, mem_w_k, mem_w_v, w_br, w_out, ln_g, ln_b):
    depth, d_model, _ = w_in.shape
    d_a = gmlp_ln_g.shape[1]
    d_b = lru_lambda.shape[1]
    q_lora = mla_q_norm.shape[1]
    kv_lora = mla_kv_norm.shape[1]
    d_c = H_C * D_V
    d_m = H_M * DH_M
    o_cq = 3 * d_a + 2 * d_b
    o_kr = o_cq + q_lora + kv_lora
    o_cg = o_kr + D_ROPE
    o_mq = o_cg + d_c
    o_g = o_mq + d_m
    n_head = o_kr + LANES
    assert o_kr % LANES == 0 and (o_g - o_mq) % LANES == 0 and D_ROPE <= LANES and n_head <= o_mq
    cols = {"a_u": (0, d_a), "a_v": (d_a, 2 * d_a), "a_g": (2 * d_a, 3 * d_a),
            "b_x": (3 * d_a, 3 * d_a + d_b), "b_g": (3 * d_a + d_b, o_cq), "c_all": (o_cq, n_head),
            "m_q": (n_head, n_head + d_m)}
    for k in range(N_BRANCH):
        cols[f"g{k}"] = (n_head + d_m + k * d_model, n_head + d_m + (k + 1) * d_model)
    w1 = _w_in_call(w_in, n_head, o_mq, tk=LANES)
    wcgt = jnp.swapaxes(w_in[:, :, o_cg:o_mq], 1, 2).astype(BF16)

    wq = mla_w_uq.reshape(depth, q_lora, H_C, D_NOPE + D_ROPE) * _EXP2_SCALE
    wq = jnp.concatenate([wq, jnp.zeros((depth, q_lora, H_C, LANES - D_NOPE - D_ROPE), wq.dtype)], axis=3)
    wuqt = jnp.swapaxes(wq.reshape(depth, q_lora, H_C * LANES), 1, 2).astype(BF16)

    wkv = mla_w_ukv.reshape(depth, kv_lora, H_C, D_NOPE + D_V)
    wk = wkv[..., :D_NOPE]
    wv = wkv[..., D_NOPE:]
    wkc = jnp.concatenate([wk, jnp.zeros((depth, kv_lora, H_C, LANES - D_NOPE), wk.dtype)], axis=3)
    wkc = wkc.reshape(depth, kv_lora, H_C * LANES).astype(BF16)
    lane = jnp.arange(H_C * LANES) % LANES
    wkr = ((lane[None, :] - D_NOPE) == jnp.arange(D_ROPE)[:, None]).astype(BF16)
    wkr = jnp.broadcast_to(wkr, (depth,) + wkr.shape)
    wv = jnp.concatenate([wv, jnp.zeros((depth, kv_lora, H_C, V_ROWS - D_V), wv.dtype)], axis=3)
    wvt = jnp.swapaxes(wv.reshape(depth, kv_lora, H_C * V_ROWS), 1, 2).astype(BF16)

    pw = dict(
        dims=(d_a, d_b, d_c, d_m), q_lora=q_lora, kv_lora=kv_lora, cols=cols,
        w1=w1, wcgt=wcgt,
        gln_g=gmlp_ln_g, gln_b=gmlp_ln_b,
        conv_w=lru_conv_w, conv_b=lru_conv_b,
        wgate=jnp.stack([_gate_blocks(lru_w_r[l], lru_w_i[l]) for l in range(depth)]).astype(BF16),
        b_r=lru_b_r, b_i=lru_b_i, lam=lru_lambda,
        q_norm=mla_q_norm, wuqt=wuqt, kv_norm=mla_kv_norm,
        wkc=wkc, wkr=wkr, wvt=wvt,
        wbr=w_br.astype(BF16), wout=w_out.astype(BF16), ln_g=ln_g, ln_b=ln_b,
        wmem=jnp.concatenate([mem_w_k, mem_w_v], axis=2).astype(BF16),
    )
    for sp in sp_lens:
        pw["wsp%d" % sp] = jnp.tril(gmlp_ws[:, :, :sp, :sp]).astype(BF16)
        pw["bsp%d" % sp] = jnp.repeat(jnp.swapaxes(gmlp_bs[:, :, :sp], 1, 2), d_a // G_A, axis=2)
    return pw


def _rope_tables(pos):
    half = D_ROPE // 2
    freq = ROPE_BASE ** (-jnp.arange(half, dtype=F32) / half)
    ang = pos.astype(F32)[:, None] * freq[None, :]
    cos, sin = jnp.cos(ang), jnp.sin(ang)
    return (jnp.transpose(cos), jnp.transpose(sin),
            jnp.concatenate([cos, cos], axis=1), jnp.concatenate([-sin, sin], axis=1))


def _pad_conv_state(conv):
    return jnp.pad(conv, ((0, 0), (SUBLANES - (CONV_W - 1), 0), (0, 0)))


def _trunk_layer(x, pw, l, sp, rope, conv0, h0, mk, mv, past_ckv, past_kr, *, tm, alpha):
    bsz, seq, d_model = x.shape
    first_chunk = past_ckv is None
    if first_chunk:
        x_rows, seq_len, tq = x, tm, N_BUF * ATTN_TK
    else:
        x_rows, seq_len, tq = x.reshape(1, bsz * seq, d_model), seq, bsz * seq
        cos_t, sin_t, ck_t, sk_t = rope
        rope = (jnp.tile(cos_t, (1, bsz)), jnp.tile(sin_t, (1, bsz)), jnp.tile(ck_t, (bsz, 1)), jnp.tile(sk_t, (bsz, 1)))
    outs = _branch_call(x_rows, pw, l, sp, _pad_conv_state(conv0), h0[:, None, :], mk, mv, rope,
                        tm=tm, seq_len=seq_len, tq=tq, first_chunk=first_chunk)
    partial, g2, cgst, qt, ckv_new, kr_new, conv_pad, h_new = outs[:8]
    if first_chunk:
        kcat, vt = outs[8:]
        v_rows = None
        oct = _attn_causal_call(qt, kcat, vt, cgst)
    else:
        ckv_new, kr_new, v_rows = (o.reshape(bsz, seq, -1) for o in (ckv_new, kr_new, outs[8]))

        def per_seq(a):
            return jnp.transpose(a.reshape(a.shape[-2], bsz, seq), (1, 0, 2))

        oct = _attn_cached_call(per_seq(qt), past_ckv, past_kr, ckv_new, kr_new, per_seq(cgst), pw, l)
        oct = jnp.transpose(oct, (1, 0, 2)).reshape(1, oct.shape[1], bsz * seq)
    x_new = _merge_call(x_rows, partial, g2, oct, pw, l, tr=MERGE_TILE if first_chunk else tm, alpha=alpha)
    return (x_new.reshape(bsz, seq, d_model), v_rows, conv_pad[:, SUBLANES - (CONV_W - 1):], h_new[:, 0],
            ckv_new, kr_new)


def kernel(x_prompt, x_sample, mem_prompt, cache_mla_ckv, cache_mla_krope, cache_mem_k, cache_mem_v,
           state_lru_h, state_lru_conv, w_in, gmlp_ln_g, gmlp_ln_b, gmlp_ws, gmlp_bs,
           lru_conv_w, lru_conv_b, lru_w_r, lru_b_r, lru_w_i, lru_b_i, lru_lambda,
           mla_q_norm, mla_w_uq, mla_kv_norm, mla_w_ukv, mem_w_k, mem_w_v, w_br, w_out, ln_g, ln_b):
    bp, tp, d_model = x_prompt.shape
    bs, ts, _ = x_sample.shape
    depth = w_in.shape[0]
    past_len = cache_mla_ckv.shape[2]
    n_mem = mem_prompt.shape[1]
    d_b = lru_lambda.shape[1]
    alpha = (2.0 * depth) ** 0.25
    sp_p, sp_s = min(tp, A_CHUNK), min(ts, A_CHUNK)
    pw = _prep_weights(sorted({sp_p, sp_s}), w_in, gmlp_ln_g, gmlp_ln_b, gmlp_ws, gmlp_bs, lru_conv_w,
                       lru_conv_b, lru_w_r, lru_b_r, lru_w_i, lru_b_i, lru_lambda, mla_q_norm, mla_w_uq,
                       mla_kv_norm, mla_w_ukv, mem_w_k, mem_w_v, w_br, w_out, ln_g, ln_b)
    rope_p = _rope_tables(jnp.arange(tp))
    rope_s = _rope_tables(past_len + jnp.arange(ts))
    tm_p = min(tp, ROW_TILE)
    assert tp % (N_BUF * ATTN_TK) == 0 and tm_p % ATTN_TK == 0 and tp % MERGE_TILE == 0 and ts % SUBLANES == 0
    zero_conv = jnp.zeros((bp, CONV_W - 1, d_b), F32)
    zero_h = jnp.zeros((bp, d_b), F32)

    xp, xs = x_prompt, x_sample
    acc = [[] for _ in range(11)]
    for l in range(depth):
        mk, mv = _mem_call(mem_prompt.reshape(bp * n_mem, d_model), pw["wmem"], l, tr=n_mem)
        mk = mk.reshape(bp, n_mem, -1)
        mv = mv.reshape(bp, n_mem, -1)
        xp, _, conv_n, h_n, ckv_n, kr_n = _trunk_layer(
            xp, pw, l, sp_p, rope_p, zero_conv, zero_h, mk, mv, None, None, tm=tm_p, alpha=alpha)
        for k, val in zip(range(6), (ckv_n, kr_n, mk.reshape(bp, n_mem, H_M, DH_M),
                                     mv.reshape(bp, n_mem, H_M, DH_M), h_n, conv_n)):
            acc[k].append(val)
        xs, v_n, conv_n, h_n, ckv_n, kr_n = _trunk_layer(
            xs, pw, l, sp_s, rope_s, state_lru_conv[l], state_lru_h[l],
            cache_mem_k[l].reshape(bs, n_mem, -1), cache_mem_v[l].reshape(bs, n_mem, -1),
            cache_mla_ckv, cache_mla_krope, tm=bs * ts, alpha=alpha)
        for k, val in zip(range(6, 11), (ckv_n, kr_n, h_n, conv_n, v_n)):
            acc[k].append(val)
    return (xp, xs) + tuple(jnp.stack(a) for a in acc)
```

```python
import functools
import math

import jax
import jax.numpy as jnp
from jax import lax
from jax.experimental import pallas as pl
from jax.experimental.pallas import tpu as pltpu

CHUNK = 64
G_A = 4
A_CHUNK = 128
H_B = 8
CONV_W = 4
LRU_C = 8.0
H_C = 8
D_NOPE = 64
D_ROPE = 32
D_V = 64
ROPE_BASE = 10000.0
H_M = 4
DH_M = 64
N_BRANCH = 4
EPS = 1e-6

LANES = 128
SUBLANES = 8
VMEM_LIMIT = 56 * 1024 * 1024
ROW_TILE = 512
ATTN_TK = 256
MXU_TILE = 256
MERGE_TILE = 1024

F32 = jnp.float32
BF16 = jnp.bfloat16


def _dot(a, b):
    return jnp.dot(a, b, preferred_element_type=F32)


def _dot_nt(a, b):
    return lax.dot_general(a, b, (((1,), (1,)), ((), ())), preferred_element_type=F32)


def _dot_tn(a, b):
    return lax.dot_general(a, b, (((0,), (0,)), ((), ())), preferred_element_type=F32)


def _sigmoid(x):
    return 1.0 / (1.0 + jnp.exp(-x))


def _silu(x):
    return x * _sigmoid(x)


def _gelu(x):
    return jax.nn.gelu(x)


def _expm1_nonpos(x):
    u = jnp.exp(x)
    near = (u - 1.0) * x / jnp.log(jnp.where(u == 1.0, 2.0, jnp.maximum(u, 0.5)))
    return jnp.where(u == 1.0, x, jnp.where(u > 0.5, near, u - 1.0))


def _layer_norm(x, g, b):
    mu = jnp.mean(x, -1, keepdims=True)
    var = jnp.mean(jnp.square(x - mu), -1, keepdims=True)
    return (x - mu) * lax.rsqrt(var + EPS) * g + b


def _rms_norm(x, g):
    return x * lax.rsqrt(jnp.mean(jnp.square(x), -1, keepdims=True) + EPS) * g


def _const_spec(shape):
    nd = len(shape)
    return pl.BlockSpec(shape, lambda *_: (0,) * nd, pipeline_mode=pl.Buffered(1))


def _layer_spec(arr, l):
    if arr.ndim == 2:
        return _const_spec(arr.shape)
    nd = arr.ndim - 1
    return pl.BlockSpec((None,) + arr.shape[1:], lambda *_: (l,) + (0,) * nd, pipeline_mode=pl.Buffered(1))


def _params(n_axes):
    return pltpu.CompilerParams(dimension_semantics=("arbitrary",) * n_axes,
                                vmem_limit_bytes=VMEM_LIMIT)


V_ROWS = 80


def _keys_values(ckv, kr, wkc_ref, wkr_ref, wvt_ref):
    c = ckv.astype(BF16)
    kcat = (_dot(c, wkc_ref[...]) + _dot(kr.astype(BF16), wkr_ref[...])).astype(BF16)
    v_t = _dot_nt(wvt_ref[...], c)
    row = lax.broadcasted_iota(jnp.int32, (v_t.shape[0], 1), 0)
    return kcat, jnp.where(row % V_ROWS == D_V, 1.0, v_t).astype(BF16)


def _branch_kernel(x_ref, w1_ref, wcgt_ref, glng_ref, glnb_ref, wsp_ref, bsp_ref,
                   convw_ref, convb_ref, wgate_ref, br_ref, bi_ref, lam_ref, conv0_ref, h0_ref, qnorm_ref, wuqt_ref,
                   kvnorm_ref, cosq_ref, sinq_ref, ck_ref, sk_ref, mk_ref, mv_ref, wbr_ref,
                   *rest, layer, first_chunk, tm, seq_len, sp_len, d_a, d_b, d_c, d_m, q_lora, kv_lora, cols):
    if first_chunk:
        (wkc_ref, wkr_ref, wvt_ref, partial_ref, g2_ref, cgst_ref, qt_ref, ckv_ref, kr_ref, conv_ref, h_ref,
         kcat_ref, vt_ref, merged_scr) = rest
    else:
        partial_ref, g2_ref, cgst_ref, qt_ref, ckv_ref, kr_ref, conv_ref, h_ref, v_ref, merged_scr = rest
    t = pl.program_id(1)
    xb = x_ref[0].astype(BF16)

    def vec(ref):
        return ref[layer:layer + 1, :]

    def zin(name):
        lo, hi = cols[name]
        return _dot(xb, w1_ref[:, lo:hi])

    z_u, z_v, z_ag, z_g0 = zin("a_u"), zin("a_v"), zin("a_g"), zin("g0")
    u = _gelu(z_u)
    v = _layer_norm(_gelu(z_v), vec(glng_ref), vec(glnb_ref))
    if not first_chunk:
        v_ref[0] = v
    vb = v.astype(BF16)
    n_groups = d_a // LANES
    row_blocks = []
    for c in range(tm // sp_len):
        col_blocks = [_dot(wsp_ref[g], vb[c * sp_len:(c + 1) * sp_len, g * LANES:(g + 1) * LANES])
                      for g in range(n_groups)]
        row_blocks.append(jnp.concatenate(col_blocks, axis=1) + bsp_ref[...])
    s = row_blocks[0] if len(row_blocks) == 1 else jnp.concatenate(row_blocks, axis=0)
    oa = (u * s) * _silu(z_ag)
    merged_scr[...] = _sigmoid(z_g0) * _dot(oa.astype(BF16), wbr_ref[0:d_a, :])

    @pl.when(t == 0)
    def _():
        conv_ref[...] = conv0_ref[...]
        h_ref[...] = h0_ref[...]

    bx = zin("b_x")
    n_seq = tm // seq_len
    span = SUBLANES + seq_len
    stacked = jnp.concatenate(
        [piece for q in range(n_seq) for piece in (conv_ref[q], bx[q * seq_len:(q + 1) * seq_len])], axis=0)
    xc = vec(convb_ref)
    for k in range(CONV_W):
        shift = CONV_W - 1 - k
        rolled = stacked if shift == 0 else pltpu.roll(stacked, shift, axis=0)
        sh = [rolled[q * span + SUBLANES:(q + 1) * span] for q in range(n_seq)]
        xc = xc + (sh[0] if n_seq == 1 else jnp.concatenate(sh, axis=0)) * convw_ref[k:k + 1, :]
    for q in range(n_seq):
        conv_ref[q] = bx[(q + 1) * seq_len - SUBLANES:(q + 1) * seq_len]

    xcb = xc.astype(BF16)
    n_blk = d_b // MXU_TILE
    ri = [_dot(xcb[:, k * MXU_TILE:(k + 1) * MXU_TILE], wgate_ref[k]) for k in range(n_blk)]
    z_bg, z_g1, z_g2 = zin("b_g"), zin("g1"), zin("g2")
    z_mq, z_g3, z_c = zin("m_q"), zin("g3"), zin("c_all")
    cg_t = _dot_nt(wcgt_ref[...], xb)
    r = _sigmoid(jnp.concatenate([p[:, 0:MXU_TILE] for p in ri], axis=1) + vec(br_ref))
    i_gate = _sigmoid(jnp.concatenate([p[:, MXU_TILE:2 * MXU_TILE] for p in ri], axis=1) + vec(bi_ref))
    neg_lam = -vec(lam_ref)
    softplus = jnp.maximum(neg_lam, 0.0) + jnp.log1p(jnp.exp(-jnp.abs(neg_lam)))
    log_a = (-LRU_C * r) * softplus
    a = jnp.exp(log_a)
    bval = jnp.sqrt(-_expm1_nonpos(2.0 * log_a)) * (i_gate * xc)
    in_group = lax.broadcasted_iota(jnp.int32, (tm, 1), 0) % SUBLANES
    d = 1
    while d < SUBLANES:
        keep = in_group >= d
        a_sh = pltpu.roll(a, d, axis=0)
        b_sh = pltpu.roll(bval, d, axis=0)
        bval = jnp.where(keep, a * b_sh + bval, bval)
        a = jnp.where(keep, a * a_sh, a)
        d *= 2
    groups_per_seq = seq_len // SUBLANES
    h_groups = []
    for g in range(tm // SUBLANES):
        q = g // groups_per_seq
        if g % groups_per_seq == 0:
            carry = h_ref[q]
        lo = g * SUBLANES
        h_g = a[lo:lo + SUBLANES] * carry + bval[lo:lo + SUBLANES]
        h_groups.append(h_g)
        carry = h_g[SUBLANES - 1:SUBLANES]
        if (g + 1) % groups_per_seq == 0:
            h_ref[q] = carry
    h = jnp.concatenate(h_groups, axis=0)
    ob = h * _silu(z_bg)
    merged_scr[...] += _sigmoid(z_g1) * _dot(ob.astype(BF16), wbr_ref[d_a:d_a + d_b, :])
    g2_ref[0] = _sigmoid(z_g2).astype(BF16)

    mq = z_mq.astype(BF16)
    lane = lax.broadcasted_iota(jnp.int32, (1, LANES), 1)
    om_rows = []
    for q in range(n_seq):
        mkb = mk_ref[q].astype(BF16)
        mvb = mv_ref[q].astype(BF16)
        slabs = []
        for p in range(d_m // LANES):
            mq_p = mq[q * seq_len:(q + 1) * seq_len, p * LANES:(p + 1) * LANES]
            mk_p = mkb[:, p * LANES:(p + 1) * LANES]
            mv_p = mvb[:, p * LANES:(p + 1) * LANES]
            acc = None
            for half in range(LANES // DH_M):
                sel = (lane >= half * DH_M) & (lane < (half + 1) * DH_M)
                sc = _dot_nt(mq_p, jnp.where(sel, mk_p, jnp.zeros_like(mk_p))) * (DH_M ** -0.5)
                e = jnp.exp(sc - jnp.max(sc, -1, keepdims=True))
                prob = (e / jnp.sum(e, -1, keepdims=True)).astype(BF16)
                o = _dot(prob, jnp.where(sel, mv_p, jnp.zeros_like(mv_p)))
                acc = o if acc is None else acc + o
            slabs.append(acc)
        om_rows.append(jnp.concatenate(slabs, axis=1))
    om = om_rows[0] if n_seq == 1 else jnp.concatenate(om_rows, axis=0)
    m_lo = d_a + d_b + d_c
    partial_ref[0] = (merged_scr[...]
                      + _sigmoid(z_g3) * _dot(om.astype(BF16), wbr_ref[m_lo:m_lo + d_m, :])).astype(BF16)
    cgst_ref[0] = _silu(cg_t)

    ckv = _rms_norm(z_c[:, q_lora:q_lora + kv_lora], vec(kvnorm_ref))
    half = D_ROPE // 2
    z_kr = z_c[:, q_lora + kv_lora:q_lora + kv_lora + LANES]
    kr_swapped = jnp.where(lane < half, pltpu.roll(z_kr, LANES - half, axis=1), pltpu.roll(z_kr, half, axis=1))
    kr = z_kr[:, 0:D_ROPE] * ck_ref[...] + kr_swapped[:, 0:D_ROPE] * sk_ref[...]
    ckv_ref[0] = ckv
    kr_ref[0] = kr
    if first_chunk:
        kcat_ref[0], vt_all = _keys_values(ckv, kr, wkc_ref, wkr_ref, wvt_ref)
        for j in range(tm // ATTN_TK):
            vt_ref[0, j] = vt_all[:, j * ATTN_TK:(j + 1) * ATTN_TK]
    cqn =_rms_norm(z_c[:, 0:q_lora], vec(qnorm_ref)).astype(BF16)
    q_t = _dot_nt(wuqt_ref[...], cqn)
    cos_t, sin_t = cosq_ref[...], sinq_ref[...]
    for hd in range(H_C):
        lo = hd * LANES
        x1 = q_t[lo + D_NOPE:lo + D_NOPE + half]
        x2 = q_t[lo + D_NOPE + half:lo + D_NOPE + D_ROPE]
        qt_ref[0, 0, lo:lo + D_NOPE, :] = q_t[lo:lo + D_NOPE].astype(BF16)
        qt_ref[0, 0, lo + D_NOPE:lo + D_NOPE + half, :] = (x1 * cos_t - x2 * sin_t).astype(BF16)
        qt_ref[0, 0, lo + D_NOPE + half:lo + D_NOPE + D_ROPE, :] = (x1 * sin_t + x2 * cos_t).astype(BF16)
        qt_ref[0, 0, lo + D_NOPE + D_ROPE:lo + LANES, :] = jnp.zeros((LANES - D_NOPE - D_ROPE, tm), BF16)


def _branch_call(x, pw, l, sp, conv0_pad, h0, mk, mv, rope, *, tm, seq_len, tq, first_chunk):
    bsz, seq, d_model = x.shape
    d_a, d_b, d_c, d_m = pw["dims"]
    q_lora, kv_lora = pw["q_lora"], pw["kv_lora"]
    n_mem = mk.shape[1]
    cos_t, sin_t, ck_t, sk_t = rope
    grid = (bsz, seq // tm)

    def row_spec(width):
        return pl.BlockSpec((1, tm, width), lambda b, t: (b, t, 0))

    def col_spec(height):
        return pl.BlockSpec((1, height, tm), lambda b, t: (b, 0, t))

    def tab_spec(width):
        return pl.BlockSpec((tm, width), lambda b, t: (t, 0))

    n_seq = tm // seq_len

    def batch_spec(rows, width):
        return pl.BlockSpec((n_seq, rows, width), lambda b, t: (b, 0, 0))

    names = ["w1", "wcgt", "gln_g", "gln_b", "wsp%d" % sp, "bsp%d" % sp, "conv_w", "conv_b",
             "wgate", "b_r", "b_i", "lam"]
    names2 = ["q_norm", "wuqt", "kv_norm"]
    in_specs = ([row_spec(d_model)] + [_layer_spec(pw[n], l) for n in names]
                + [batch_spec(SUBLANES, d_b), batch_spec(1, d_b)] + [_layer_spec(pw[n], l) for n in names2]
                + [pl.BlockSpec((D_ROPE // 2, tm), lambda b, t: (0, t)), pl.BlockSpec((D_ROPE // 2, tm), lambda b, t: (0, t)),
                   tab_spec(D_ROPE), tab_spec(D_ROPE), batch_spec(n_mem, d_m), batch_spec(n_mem, d_m),
                   _layer_spec(pw["wbr"], l)])
    operands = ([x] + [pw[n] for n in names] + [conv0_pad, h0] + [pw[n] for n in names2]
                + [cos_t, sin_t, ck_t, sk_t, mk, mv, pw["wbr"]])
    out_shape = [
        jax.ShapeDtypeStruct((bsz, seq, d_model), BF16),
        jax.ShapeDtypeStruct((bsz, seq, d_model), BF16),
        jax.ShapeDtypeStruct((bsz, d_c, seq), F32),
        jax.ShapeDtypeStruct((bsz, seq // tq, H_C * LANES, tq), BF16),
        jax.ShapeDtypeStruct((bsz, seq, kv_lora), F32),
        jax.ShapeDtypeStruct((bsz, seq, D_ROPE), F32),
        jax.ShapeDtypeStruct((bsz * n_seq, SUBLANES, d_b), F32),
        jax.ShapeDtypeStruct((bsz * n_seq, 1, d_b), F32),
    ]
    out_specs = [
        row_spec(d_model), row_spec(d_model), col_spec(d_c),
        pl.BlockSpec((1, 1, H_C * LANES, tm), lambda b, t: (b, t // (tq // tm), 0, t % (tq // tm))),
        row_spec(kv_lora), row_spec(D_ROPE), batch_spec(SUBLANES, d_b), batch_spec(1, d_b),
    ]
    if first_chunk:
        for name in ("wkc", "wkr", "wvt"):
            in_specs.append(_layer_spec(pw[name], l))
            operands.append(pw[name])
        out_shape += [jax.ShapeDtypeStruct((bsz, seq, H_C * LANES), BF16),
                      jax.ShapeDtypeStruct((bsz, seq // ATTN_TK, H_C * V_ROWS, ATTN_TK), BF16)]
        out_specs += [row_spec(H_C * LANES),
                      pl.BlockSpec((1, tm // ATTN_TK, H_C * V_ROWS, ATTN_TK), lambda b, t: (b, t, 0, 0))]
    else:
        out_shape.append(jax.ShapeDtypeStruct((bsz, seq, d_a), F32))
        out_specs.append(row_spec(d_a))
    kern = functools.partial(
        _branch_kernel, layer=l, first_chunk=first_chunk, tm=tm, seq_len=seq_len, sp_len=sp, d_a=d_a, d_b=d_b, d_c=d_c, d_m=d_m,
        q_lora=q_lora, kv_lora=kv_lora, cols=pw["cols"])
    return pl.pallas_call(
        kern, grid=grid, in_specs=in_specs, out_specs=out_specs, out_shape=out_shape,
        scratch_shapes=[pltpu.VMEM((tm, d_model), F32)],
        name="branch", compiler_params=_params(2),
    )(*operands)


HEADS_PER_STEP = 4
_EXP2_SCALE = (D_NOPE + D_ROPE) ** -0.5 * math.log2(math.e)


def _scores(k_tile, q_tile, out_refs=None):
    res = [_dot(k_tile[:, h * LANES:(h + 1) * LANES], q_tile[h * LANES:(h + 1) * LANES, :])
           for h in range(k_tile.shape[1] // LANES)]
    if out_refs is None:
        return res
    for h, r in enumerate(res):
        out_refs[h] = r
    return None


def _softmax_tile(s_t, m_old):
    m_new = jnp.maximum(m_old, jnp.max(s_t, axis=0, keepdims=True))
    return jnp.exp2(s_t - m_new).astype(BF16), m_new, jnp.exp2(m_old - m_new)


def _attn_output(acc, cgs_t):
    return (acc[0:D_V] / acc[D_V:D_V + 1] * cgs_t).astype(BF16)


N_BUF = 4


def _attn_causal_kernel(qt_ref, k_ref, vt_ref, cgst_ref, bias_ref, o_ref, *scratch, tq, tk):
    s_buf, e_buf, acc_scr = scratch[0:N_BUF], scratch[N_BUF:2 * N_BUF], scratch[2 * N_BUF]
    i = pl.program_id(2)
    nh = HEADS_PER_STEP
    q_cur = qt_ref.at[0, i]
    q_nxt = qt_ref.at[0, jnp.minimum(i + 1, pl.num_programs(2) - 1)]

    def k_tile(j):
        return k_ref[0, pl.ds(pl.multiple_of(j * tk, tk), tk), :]

    def pv(j, e_in, c0):
        vt = vt_ref[0, j]
        return [_dot(vt[h * V_ROWS:(h + 1) * V_ROWS, :], e_in[h, :, c0:tq]) for h in range(nh)]

    def acc_update(pvs, alphas, c0):
        for h in range(nh):
            acc_scr[h, :, c0:tq] = acc_scr[h, :, c0:tq] * alphas[h][:, c0:tq] + pvs[h]

    def softmax(slot, carry, c0, add_bias):
        new = []
        for h in range(nh):
            m_old, a_prev, _ = carry[h]
            s_t = s_buf[slot][h, :, c0:tq]
            if add_bias:
                lead = s_t[:, 0:tk] + bias_ref[...]
                s_t = lead if c0 + tk == tq else jnp.concatenate([lead, s_t[:, tk:]], axis=1)
            e, m_new, alpha = _softmax_tile(s_t, m_old[:, c0:tq])
            e_buf[slot][h, :, c0:tq] = e
            if c0:
                m_new = jnp.concatenate([m_old[:, 0:c0], m_new], axis=1)
                alpha = jnp.concatenate([jnp.ones((1, c0), F32), alpha], axis=1)
            new.append((m_new, alpha, a_prev))
        return tuple(new)

    def stage(n, slot, carry):
        pvs = pv(jnp.maximum(n - 2, 0), e_buf[(slot + 2) % N_BUF], 0)
        _scores(k_tile(n + 3), q_cur, s_buf[(slot + 3) % N_BUF])
        new = softmax(slot, carry, 0, False)
        acc_update(pvs, [c[2] for c in carry], 0)
        return new

    for slot in (N_BUF - 2, N_BUF - 1):
        e_buf[slot][...] = jnp.zeros(e_buf[slot].shape, BF16)
    acc_scr[...] = jnp.zeros(acc_scr.shape, F32)

    @pl.when(i == 0)
    def _():
        for slot in range(N_BUF - 1):
            _scores(k_tile(slot), q_cur, s_buf[slot])

    ones = jnp.ones((1, tq), F32)
    carry = tuple((jnp.full((1, tq), -jnp.inf, F32), ones, ones) for _ in range(nh))

    def body(t, carry):
        for slot in range(N_BUF):
            carry = stage(N_BUF * t + slot, slot, carry)
        return carry

    carry = lax.fori_loop(0, i, body, carry)

    j0 = N_BUF * i
    last = N_BUF - 1
    for d in range(N_BUF):
        c_prev = max(d - 2, 0) * tk
        pvs = pv(jnp.maximum(j0 + d - 2, 0), e_buf[(d + 2) % N_BUF], c_prev)
        if d == 0:
            k_last = k_tile(j0 + last)
            for h in range(nh):
                s_buf[last][h, :, tq - tk:tq] = _dot(
                    k_last[:, h * LANES:(h + 1) * LANES],
                    q_cur[h * LANES:(h + 1) * LANES, tq - tk:tq]) + bias_ref[...]
        else:
            _scores(k_tile(d - 1), q_nxt, s_buf[d - 1])
        alphas = [c[2] for c in carry]
        carry = softmax(d, carry, d * tk, d < last)
        acc_update(pvs, alphas, c_prev)
    acc_update(pv(j0 + last - 1, e_buf[last - 1], (last - 1) * tk), [c[2] for c in carry], (last - 1) * tk)
    acc_update(pv(j0 + last, e_buf[last], last * tk), [c[1] for c in carry], last * tk)
    for h in range(nh):
        o_ref[0, h * D_V:(h + 1) * D_V, :] = _attn_output(acc_scr[h], cgst_ref[0, h * D_V:(h + 1) * D_V, :])


def _attn_latent_kernel(qt_ref, pckv_ref, pkr_ref, ckv_ref, kr_ref, cgst_ref, wuk_ref, wuvt_ref, place_ref, o_ref):
    tq = qt_ref.shape[-1]
    c = jnp.concatenate([pckv_ref[0], ckv_ref[0]], axis=0).astype(BF16)
    kr = jnp.concatenate([pkr_ref[0], kr_ref[0]], axis=0).astype(BF16)
    q_lat, q_rope = None, None
    for h in range(H_C):
        lo = h * LANES
        q_h = _dot(wuk_ref[h], qt_ref[0, lo:lo + D_NOPE, :]).astype(BF16)
        a_h = _dot(q_h, place_ref[h])
        r_h = _dot(qt_ref[0, lo + D_NOPE:lo + D_NOPE + D_ROPE, :], place_ref[h])
        q_lat = a_h if q_lat is None else q_lat + a_h
        q_rope = r_h if q_rope is None else q_rope + r_h
    s = _dot(c, q_lat.astype(BF16)) + _dot(kr, q_rope.astype(BF16))
    e = jnp.exp2(s - jnp.max(s, axis=0, keepdims=True)).astype(BF16)
    denom = jnp.sum(e.astype(F32), axis=0, keepdims=True)
    lat = _dot_tn(c, e).astype(BF16)
    for h in range(H_C):
        o_h = _dot(wuvt_ref[h], lat[:, h * tq:(h + 1) * tq])
        o_ref[0, h * D_V:(h + 1) * D_V, :] = (
            o_h / denom[:, h * tq:(h + 1) * tq] * cgst_ref[0, h * D_V:(h + 1) * D_V, :]).astype(BF16)


def _attn_causal_call(qt, kcat, vt, cgst):
    bsz, n_qt, _, tq = qt.shape
    seq = n_qt * tq
    t_k = kcat.shape[1]
    nkt, _, tk = vt.shape[1:]
    nh = HEADS_PER_STEP
    assert tq == N_BUF * tk and seq == t_k and tk % CHUNK == 0
    chunk = jnp.arange(tk) // CHUNK
    bias = jnp.where(chunk[:, None] <= chunk[None, :], 0.0, -1e30).astype(F32)
    return pl.pallas_call(
        functools.partial(_attn_causal_kernel, tq=tq, tk=tk), grid=(bsz, H_C // nh, n_qt),
        in_specs=[pl.BlockSpec((1, n_qt, nh * LANES, tq), lambda b, p, i: (b, 0, p, 0)),
                  pl.BlockSpec((1, t_k, nh * LANES), lambda b, p, i: (b, 0, p)),
                  pl.BlockSpec((1, nkt, nh * V_ROWS, tk), lambda b, p, i: (b, 0, p, 0)),
                  pl.BlockSpec((1, nh * D_V, tq), lambda b, p, i: (b, p, i)),
                  _const_spec((tk, tk))],
        out_specs=pl.BlockSpec((1, nh * D_V, tq), lambda b, p, i: (b, p, i)),
        out_shape=jax.ShapeDtypeStruct((bsz, H_C * D_V, seq), BF16),
        scratch_shapes=([pltpu.VMEM((nh, tk, tq), F32)] * N_BUF + [pltpu.VMEM((nh, tk, tq), BF16)] * N_BUF
                        + [pltpu.VMEM((nh, V_ROWS, tq), F32)]),
        name="attn", compiler_params=_params(3),
    )(qt, kcat, vt, cgst, bias)


def _attn_cached_call(qt, past_ckv, past_kr, ckv_new, kr_new, cgst, pw, l):
    bsz, n_q, tq = qt.shape
    past_len, kv_lora = past_ckv.shape[2:]
    place = (jnp.eye(H_C, dtype=BF16)[:, None, :, None] * jnp.eye(tq, dtype=BF16)[None, :, None, :])
    place = place.reshape(H_C, tq, H_C * tq)
    return pl.pallas_call(
        _attn_latent_kernel, grid=(bsz,),
        in_specs=[pl.BlockSpec((1, n_q, tq), lambda b: (b, 0, 0)),
                  pl.BlockSpec((None, 1, past_len, kv_lora), lambda b: (l, b, 0, 0)),
                  pl.BlockSpec((None, 1, past_len, D_ROPE), lambda b: (l, b, 0, 0)),
                  pl.BlockSpec((1, tq, kv_lora), lambda b: (b, 0, 0)),
                  pl.BlockSpec((1, tq, D_ROPE), lambda b: (b, 0, 0)),
                  pl.BlockSpec((1, H_C * D_V, tq), lambda b: (b, 0, 0)),
                  _layer_spec(pw["wuk"], l), _layer_spec(pw["wuvt"], l), _const_spec(place.shape)],
        out_specs=pl.BlockSpec((1, H_C * D_V, tq), lambda b: (b, 0, 0)),
        out_shape=jax.ShapeDtypeStruct((bsz, H_C * D_V, tq), BF16),
        name="attn_cached", compiler_params=_params(1),
    )(qt, past_ckv, past_kr, ckv_new, kr_new, cgst, pw["wuk"], pw["wuvt"], place)


def _merge_kernel(x_ref, partial_ref, g2_ref, oct_ref, wbr_ref, wout_ref, lng_ref, lnb_ref, o_ref,
                  *, layer, c_lo, d_c, alpha):
    yc = _dot_tn(oct_ref[0], wbr_ref[c_lo:c_lo + d_c, :])
    merged = partial_ref[0].astype(F32) + g2_ref[0].astype(F32) * yc
    y = _dot(merged.astype(BF16), wout_ref[...])
    o_ref[0] = _layer_norm(alpha * x_ref[0] + y, lng_ref[layer:layer + 1, :], lnb_ref[layer:layer + 1, :])


def _merge_call(x, partial, g2, oct, pw, l, *, tr, alpha):
    bsz, seq, d_model = x.shape
    d_a, d_b, d_c, _ = pw["dims"]
    kern = functools.partial(_merge_kernel, layer=l, c_lo=d_a + d_b, d_c=d_c, alpha=alpha)

    def row_spec(width):
        return pl.BlockSpec((1, tr, width), lambda b, i: (b, i, 0))

    return pl.pallas_call(
        kern, grid=(bsz, seq // tr),
        in_specs=[row_spec(d_model), row_spec(d_model), row_spec(d_model),
                  pl.BlockSpec((1, d_c, tr), lambda b, i: (b, 0, i)),
                  _layer_spec(pw["wbr"], l), _layer_spec(pw["wout"], l),
                  _layer_spec(pw["ln_g"], l), _layer_spec(pw["ln_b"], l)],
        out_specs=row_spec(d_model),
        out_shape=jax.ShapeDtypeStruct((bsz, seq, d_model), F32),
        name="merge", compiler_params=_params(2),
    )(x, partial, g2, oct, pw["wbr"], pw["wout"], pw["ln_g"], pw["ln_b"])


def _mem_kernel(m_ref, w_ref, k_ref, v_ref, *, d_m):
    kv = _dot(m_ref[...].astype(BF16), w_ref[...])
    k_ref[...] = kv[:, 0:d_m]
    v_ref[...] = kv[:, d_m:2 * d_m]


def _mem_call(mem2d, wmem, l, *, tr):
    rows, d_model = mem2d.shape
    d_m = wmem.shape[2] // 2
    return pl.pallas_call(
        functools.partial(_mem_kernel, d_m=d_m), grid=(rows // tr,),
        in_specs=[pl.BlockSpec((tr, d_model), lambda i: (i, 0)), _layer_spec(wmem, l)],
        out_specs=[pl.BlockSpec((tr, d_m), lambda i: (i, 0)), pl.BlockSpec((tr, d_m), lambda i: (i, 0))],
        out_shape=[jax.ShapeDtypeStruct((rows, d_m), F32), jax.ShapeDtypeStruct((rows, d_m), F32)],
        name="mem", compiler_params=_params(1),
    )(mem2d, wmem)


def _block_diag(w):
    h, n, _ = w.shape
    eye = jnp.eye(h, dtype=w.dtype)
    return (eye[:, None, :, None] * w[:, :, None, :]).reshape(h * n, h * n)


def _gate_blocks(w_r, w_i):
    per = MXU_TILE // w_r.shape[1]
    blocks = []
    for k in range(w_r.shape[0] // per):
        blocks.append(jnp.concatenate([_block_diag(w_r[k * per:(k + 1) * per]),
                                       _block_diag(w_i[k * per:(k + 1) * per])], axis=1))
    return jnp.stack(blocks)


def _w_in_kernel(wt_ref, o_ref, *, n_head, o_tail):
    o_ref[0, :, 0:n_head] = jnp.transpose(wt_ref[0, 0:n_head, :]).astype(BF16)
    o_ref[0, :, n_head:] = jnp.transpose(wt_ref[0, o_tail:, :]).astype(BF16)


def _w_in_call(w_in, n_head, o_tail, *, tk):
    depth, d_model, d_in = w_in.shape
    n_out = n_head + d_in - o_tail
    return pl.pallas_call(
        functools.partial(_w_in_kernel, n_head=n_head, o_tail=o_tail), grid=(depth, d_model // tk),
        in_specs=[pl.BlockSpec((1, d_in, tk), lambda l, i: (l, 0, i))],
        out_specs=pl.BlockSpec((1, tk, n_out), lambda l, i: (l, i, 0)),
        out_shape=jax.ShapeDtypeStruct((depth, d_model, n_out), BF16),
        name="w_in", compiler_params=_params(2),
    )(jnp.swapaxes(w_in, 1, 2))


def _prep_weights(sp_lens, w_in, gmlp_ln_g, gmlp_ln_b, gmlp_ws, gmlp_bs, lru_conv_w,
                  lru_conv_b, lru_w_r, lru_b_r, lru_w_i, lru_b_i, lru_lambda, mla_q_norm, mla_w_uq,
                  mla_kv_norm, mla_w_ukv, mem_w_k, mem_w_v, w_br, w_out, ln_g, ln_b):
    depth, d_model, _ = w_in.shape
    d_a = gmlp_ln_g.shape[1]
    d_b = lru_lambda.shape[1]
    q_lora = mla_q_norm.shape[1]
    kv_lora = mla_kv_norm.shape[1]
    d_c = H_C * D_V
    d_m = H_M * DH_M
    o_cq = 3 * d_a + 2 * d_b
    o_kr = o_cq + q_lora + kv_lora
    o_cg = o_kr + D_ROPE
    o_mq = o_cg + d_c
    o_g = o_mq + d_m
    n_head = o_kr + LANES
    assert o_kr % LANES == 0 and (o_g - o_mq) % LANES == 0 and D_ROPE <= LANES and n_head <= o_mq
    cols = {"a_u": (0, d_a), "a_v": (d_a, 2 * d_a), "a_g": (2 * d_a, 3 * d_a),
            "b_x": (3 * d_a, 3 * d_a + d_b), "b_g": (3 * d_a + d_b, o_cq), "c_all": (o_cq, n_head),
            "m_q": (n_head, n_head + d_m)}
    for k in range(N_BRANCH):
        cols[f"g{k}"] = (n_head + d_m + k * d_model, n_head + d_m + (k + 1) * d_model)
    w1 = _w_in_call(w_in, n_head, o_mq, tk=LANES)
    wcgt = jnp.swapaxes(w_in[:, :, o_cg:o_mq], 1, 2).astype(BF16)

    wq = mla_w_uq.reshape(depth, q_lora, H_C, D_NOPE + D_ROPE) * _EXP2_SCALE
    wq = jnp.concatenate([wq, jnp.zeros((depth, q_lora, H_C, LANES - D_NOPE - D_ROPE), wq.dtype)], axis=3)
    wuqt = jnp.swapaxes(wq.reshape(depth, q_lora, H_C * LANES), 1, 2).astype(BF16)

    wkv = mla_w_ukv.reshape(depth, kv_lora, H_C, D_NOPE + D_V)
    wk = wkv[..., :D_NOPE]
    wv = wkv[..., D_NOPE:]
    wkc = jnp.concatenate([wk, jnp.zeros((depth, kv_lora, H_C, LANES - D_NOPE), wk.dtype)], axis=3)
    wkc = wkc.reshape(depth, kv_lora, H_C * LANES).astype(BF16)
    lane = jnp.arange(H_C * LANES) % LANES
    wkr = ((lane[None, :] - D_NOPE) == jnp.arange(D_ROPE)[:, None]).astype(BF16)
    wkr = jnp.broadcast_to(wkr, (depth,) + wkr.shape)
    wv = jnp.concatenate([wv, jnp.zeros((depth, kv_lora, H_C, V_ROWS - D_V), wv.dtype)], axis=3)
    wvt = jnp.swapaxes(wv.reshape(depth, kv_lora, H_C * V_ROWS), 1, 2).astype(BF16)

    pw = dict(
        dims=(d_a, d_b, d_c, d_m), q_lora=q_lora, kv_lora=kv_lora, cols=cols,
        w1=w1, wcgt=wcgt,
        gln_g=gmlp_ln_g, gln_b=gmlp_ln_b,
        conv_w=lru_conv_w, conv_b=lru_conv_b,
        wgate=jnp.stack([_gate_blocks(lru_w_r[l], lru_w_i[l]) for l in range(depth)]).astype(BF16),
        b_r=lru_b_r, b_i=lru_b_i, lam=lru_lambda,
        q_norm=mla_q_norm, wuqt=wuqt, kv_norm=mla_kv_norm,
        wkc=wkc, wkr=wkr, wvt=wvt,
        wuk=jnp.transpose(wk, (0, 2, 1, 3)).astype(BF16),
        wuvt=jnp.transpose(wkv[..., D_NOPE:], (0, 2, 3, 1)).astype(BF16),
        wbr=w_br.astype(BF16), wout=w_out.astype(BF16), ln_g=ln_g, ln_b=ln_b,
        wmem=jnp.concatenate([mem_w_k, mem_w_v], axis=2).astype(BF16),
    )
    for sp in sp_lens:
        pw["wsp%d" % sp] = jnp.tril(gmlp_ws[:, :, :sp, :sp]).astype(BF16)
        pw["bsp%d" % sp] = jnp.repeat(jnp.swapaxes(gmlp_bs[:, :, :sp], 1, 2), d_a // G_A, axis=2)
    return pw


def _rope_tables(pos):
    half = D_ROPE // 2
    freq = ROPE_BASE ** (-jnp.arange(half, dtype=F32) / half)
    ang = pos.astype(F32)[:, None] * freq[None, :]
    cos, sin = jnp.cos(ang), jnp.sin(ang)
    return (jnp.transpose(cos), jnp.transpose(sin),
            jnp.concatenate([cos, cos], axis=1), jnp.concatenate([-sin, sin], axis=1))


def _pad_conv_state(conv):
    return jnp.pad(conv, ((0, 0), (SUBLANES - (CONV_W - 1), 0), (0, 0)))


def _trunk_layer(x, pw, l, sp, rope, conv0, h0, mk, mv, past_ckv, past_kr, *, tm, alpha):
    bsz, seq, d_model = x.shape
    first_chunk = past_ckv is None
    if first_chunk:
        x_rows, seq_len, tq = x, tm, N_BUF * ATTN_TK
    else:
        x_rows, seq_len, tq = x.reshape(1, bsz * seq, d_model), seq, bsz * seq
        cos_t, sin_t, ck_t, sk_t = rope
        rope = (jnp.tile(cos_t, (1, bsz)), jnp.tile(sin_t, (1, bsz)), jnp.tile(ck_t, (bsz, 1)), jnp.tile(sk_t, (bsz, 1)))
    outs = _branch_call(x_rows, pw, l, sp, _pad_conv_state(conv0), h0[:, None, :], mk, mv, rope,
                        tm=tm, seq_len=seq_len, tq=tq, first_chunk=first_chunk)
    partial, g2, cgst, qt, ckv_new, kr_new, conv_pad, h_new = outs[:8]
    if first_chunk:
        kcat, vt = outs[8:]
        v_rows = None
        oct = _attn_causal_call(qt, kcat, vt, cgst)
    else:
        ckv_new, kr_new, v_rows = (o.reshape(bsz, seq, -1) for o in (ckv_new, kr_new, outs[8]))

        def per_seq(a):
            return jnp.transpose(a.reshape(a.shape[-2], bsz, seq), (1, 0, 2))

        oct = _attn_cached_call(per_seq(qt), past_ckv, past_kr, ckv_new, kr_new, per_seq(cgst), pw, l)
        oct = jnp.transpose(oct, (1, 0, 2)).reshape(1, oct.shape[1], bsz * seq)
    x_new = _merge_call(x_rows, partial, g2, oct, pw, l, tr=MERGE_TILE if first_chunk else tm, alpha=alpha)
    return (x_new.reshape(bsz, seq, d_model), v_rows, conv_pad[:, SUBLANES - (CONV_W - 1):], h_new[:, 0],
            ckv_new, kr_new)


def kernel(x_prompt, x_sample, mem_prompt, cache_mla_ckv, cache_mla_krope, cache_mem_k, cache_mem_v,
           state_lru_h, state_lru_conv, w_in, gmlp_ln_g, gmlp_ln_b, gmlp_ws, gmlp_bs,
           lru_conv_w, lru_conv_b, lru_w_r, lru_b_r, lru_w_i, lru_b_i, lru_lambda,
           mla_q_norm, mla_w_uq, mla_kv_norm, mla_w_ukv, mem_w_k, mem_w_v, w_br, w_out, ln_g, ln_b):
    bp, tp, d_model = x_prompt.shape
    bs, ts, _ = x_sample.shape
    depth = w_in.shape[0]
    past_len = cache_mla_ckv.shape[2]
    n_mem = mem_prompt.shape[1]
    d_b = lru_lambda.shape[1]
    alpha = (2.0 * depth) ** 0.25
    sp_p, sp_s = min(tp, A_CHUNK), min(ts, A_CHUNK)
    pw = _prep_weights(sorted({sp_p, sp_s}), w_in, gmlp_ln_g, gmlp_ln_b, gmlp_ws, gmlp_bs, lru_conv_w,
                       lru_conv_b, lru_w_r, lru_b_r, lru_w_i, lru_b_i, lru_lambda, mla_q_norm, mla_w_uq,
                       mla_kv_norm, mla_w_ukv, mem_w_k, mem_w_v, w_br, w_out, ln_g, ln_b)
    rope_p = _rope_tables(jnp.arange(tp))
    rope_s = _rope_tables(past_len + jnp.arange(ts))
    tm_p = min(tp, ROW_TILE)
    assert tp % (N_BUF * ATTN_TK) == 0 and tm_p % ATTN_TK == 0 and tp % MERGE_TILE == 0 and ts % SUBLANES == 0
    zero_conv = jnp.zeros((bp, CONV_W - 1, d_b), F32)
    zero_h = jnp.zeros((bp, d_b), F32)

    xp, xs = x_prompt, x_sample
    acc = [[] for _ in range(11)]
    for l in range(depth):
        mk, mv = _mem_call(mem_prompt.reshape(bp * n_mem, d_model), pw["wmem"], l, tr=n_mem)
        mk = mk.reshape(bp, n_mem, -1)
        mv = mv.reshape(bp, n_mem, -1)
        xp, _, conv_n, h_n, ckv_n, kr_n = _trunk_layer(
            xp, pw, l, sp_p, rope_p, zero_conv, zero_h, mk, mv, None, None, tm=tm_p, alpha=alpha)
        for k, val in zip(range(6), (ckv_n, kr_n, mk.reshape(bp, n_mem, H_M, DH_M),
                                     mv.reshape(bp, n_mem, H_M, DH_M), h_n, conv_n)):
            acc[k].append(val)
        xs, v_n, conv_n, h_n, ckv_n, kr_n = _trunk_layer(
            xs, pw, l, sp_s, rope_s, state_lru_conv[l], state_lru_h[l],
            cache_mem_k[l].reshape(bs, n_mem, -1), cache_mem_v[l].reshape(bs, n_mem, -1),
            cache_mla_ckv, cache_mla_krope, tm=bs * ts, alpha=alpha)
        for k, val in zip(range(6, 11), (ckv_n, kr_n, h_n, conv_n, v_n)):
            acc[k].append(val)
    return (xp, xs) + tuple(jnp.stack(a) for a in acc)
```

```python
import functools
import math

import jax
import jax.numpy as jnp
from jax import lax
from jax.experimental import pallas as pl
from jax.experimental.pallas import tpu as pltpu

CHUNK = 64
G_A = 4
A_CHUNK = 128
H_B = 8
CONV_W = 4
LRU_C = 8.0
H_C = 8
D_NOPE = 64
D_ROPE = 32
D_V = 64
ROPE_BASE = 10000.0
H_M = 4
DH_M = 64
N_BRANCH = 4
EPS = 1e-6

LANES = 128
SUBLANES = 8
VMEM_LIMIT = 56 * 1024 * 1024
ROW_TILE = 512
ATTN_TK = 256
MXU_TILE = 256
MERGE_TILE = 1024

F32 = jnp.float32
BF16 = jnp.bfloat16


def _dot(a, b):
    return jnp.dot(a, b, preferred_element_type=F32)


def _dot_nt(a, b):
    return lax.dot_general(a, b, (((1,), (1,)), ((), ())), preferred_element_type=F32)


def _dot_tn(a, b):
    return lax.dot_general(a, b, (((0,), (0,)), ((), ())), preferred_element_type=F32)


def _sigmoid(x):
    return 1.0 / (1.0 + jnp.exp(-x))


def _silu(x):
    return x * _sigmoid(x)


def _gelu(x):
    return jax.nn.gelu(x)


def _expm1_nonpos(x):
    u = jnp.exp(x)
    near = (u - 1.0) * x / jnp.log(jnp.where(u == 1.0, 2.0, jnp.maximum(u, 0.5)))
    return jnp.where(u == 1.0, x, jnp.where(u > 0.5, near, u - 1.0))


def _layer_norm(x, g, b):
    mu = jnp.mean(x, -1, keepdims=True)
    var = jnp.mean(jnp.square(x - mu), -1, keepdims=True)
    return (x - mu) * lax.rsqrt(var + EPS) * g + b


def _rms_norm(x, g):
    return x * lax.rsqrt(jnp.mean(jnp.square(x), -1, keepdims=True) + EPS) * g


def _const_spec(shape):
    nd = len(shape)
    return pl.BlockSpec(shape, lambda *_: (0,) * nd, pipeline_mode=pl.Buffered(1))


def _layer_spec(arr, l):
    if arr.ndim == 2:
        return _const_spec(arr.shape)
    nd = arr.ndim - 1
    return pl.BlockSpec((None,) + arr.shape[1:], lambda *_: (l,) + (0,) * nd, pipeline_mode=pl.Buffered(1))


def _params(n_axes):
    return pltpu.CompilerParams(dimension_semantics=("arbitrary",) * n_axes,
                                vmem_limit_bytes=VMEM_LIMIT)


V_ROWS = 80


def _keys_values(ckv, kr, wkc_ref, wkr_ref, wvt_ref):
    c = ckv.astype(BF16)
    kcat = (_dot(c, wkc_ref[...]) + _dot(kr.astype(BF16), wkr_ref[...])).astype(BF16)
    v_t = _dot_nt(wvt_ref[...], c)
    row = lax.broadcasted_iota(jnp.int32, (v_t.shape[0], 1), 0)
    return kcat, jnp.where(row % V_ROWS == D_V, 1.0, v_t).astype(BF16)


def _branch_kernel(x_ref, w1_ref, wcgt_ref, glng_ref, glnb_ref, wsp_ref, bsp_ref,
                   convw_ref, convb_ref, wgate_ref, br_ref, bi_ref, lam_ref, conv0_ref, h0_ref, qnorm_ref, wuqt_ref,
                   kvnorm_ref, cosq_ref, sinq_ref, ck_ref, sk_ref, mk_ref, mv_ref, wbr_ref,
                   *rest, layer, first_chunk, tm, seq_len, sp_len, d_a, d_b, d_c, d_m, q_lora, kv_lora, cols):
    if first_chunk:
        (wkc_ref, wkr_ref, wvt_ref, partial_ref, g2_ref, cgst_ref, qt_ref, ckv_ref, kr_ref, conv_ref, h_ref,
         kcat_ref, vt_ref, merged_scr) = rest
    else:
        partial_ref, g2_ref, cgst_ref, qt_ref, ckv_ref, kr_ref, conv_ref, h_ref, v_ref, merged_scr = rest
    t = pl.program_id(1)
    xb = x_ref[0].astype(BF16)

    def vec(ref):
        return ref[layer:layer + 1, :]

    def zin(name):
        lo, hi = cols[name]
        return _dot(xb, w1_ref[:, lo:hi])

    z_u, z_v, z_ag, z_g0 = zin("a_u"), zin("a_v"), zin("a_g"), zin("g0")
    u = _gelu(z_u)
    v = _layer_norm(_gelu(z_v), vec(glng_ref), vec(glnb_ref))
    if not first_chunk:
        v_ref[0] = v
    vb = v.astype(BF16)
    n_groups = d_a // LANES
    row_blocks = []
    for c in range(tm // sp_len):
        col_blocks = [_dot(wsp_ref[g], vb[c * sp_len:(c + 1) * sp_len, g * LANES:(g + 1) * LANES])
                      for g in range(n_groups)]
        row_blocks.append(jnp.concatenate(col_blocks, axis=1) + bsp_ref[...])
    s = row_blocks[0] if len(row_blocks) == 1 else jnp.concatenate(row_blocks, axis=0)
    oa = (u * s) * _silu(z_ag)
    merged_scr[...] = _sigmoid(z_g0) * _dot(oa.astype(BF16), wbr_ref[0:d_a, :])

    @pl.when(t == 0)
    def _():
        conv_ref[...] = conv0_ref[...]
        h_ref[...] = h0_ref[...]

    bx = zin("b_x")
    n_seq = tm // seq_len
    span = SUBLANES + seq_len
    stacked = jnp.concatenate(
        [piece for q in range(n_seq) for piece in (conv_ref[q], bx[q * seq_len:(q + 1) * seq_len])], axis=0)
    xc = vec(convb_ref)
    for k in range(CONV_W):
        shift = CONV_W - 1 - k
        rolled = stacked if shift == 0 else pltpu.roll(stacked, shift, axis=0)
        sh = [rolled[q * span + SUBLANES:(q + 1) * span] for q in range(n_seq)]
        xc = xc + (sh[0] if n_seq == 1 else jnp.concatenate(sh, axis=0)) * convw_ref[k:k + 1, :]
    for q in range(n_seq):
        conv_ref[q] = bx[(q + 1) * seq_len - SUBLANES:(q + 1) * seq_len]

    xcb = xc.astype(BF16)
    n_blk = d_b // MXU_TILE
    ri = [_dot(xcb[:, k * MXU_TILE:(k + 1) * MXU_TILE], wgate_ref[k]) for k in range(n_blk)]
    z_bg, z_g1, z_g2 = zin("b_g"), zin("g1"), zin("g2")
    z_mq, z_g3, z_c = zin("m_q"), zin("g3"), zin("c_all")
    cg_t = _dot_nt(wcgt_ref[...], xb)
    r = _sigmoid(jnp.concatenate([p[:, 0:MXU_TILE] for p in ri], axis=1) + vec(br_ref))
    i_gate = _sigmoid(jnp.concatenate([p[:, MXU_TILE:2 * MXU_TILE] for p in ri], axis=1) + vec(bi_ref))
    neg_lam = -vec(lam_ref)
    softplus = jnp.maximum(neg_lam, 0.0) + jnp.log1p(jnp.exp(-jnp.abs(neg_lam)))
    log_a = (-LRU_C * r) * softplus
    a = jnp.exp(log_a)
    bval = jnp.sqrt(-_expm1_nonpos(2.0 * log_a)) * (i_gate * xc)
    in_group = lax.broadcasted_iota(jnp.int32, (tm, 1), 0) % SUBLANES
    d = 1
    while d < SUBLANES:
        keep = in_group >= d
        a_sh = pltpu.roll(a, d, axis=0)
        b_sh = pltpu.roll(bval, d, axis=0)
        bval = jnp.where(keep, a * b_sh + bval, bval)
        a = jnp.where(keep, a * a_sh, a)
        d *= 2
    groups_per_seq = seq_len // SUBLANES
    h_groups = []
    for g in range(tm // SUBLANES):
        q = g // groups_per_seq
        if g % groups_per_seq == 0:
            carry = h_ref[q]
        lo = g * SUBLANES
        h_g = a[lo:lo + SUBLANES] * carry + bval[lo:lo + SUBLANES]
        h_groups.append(h_g)
        carry = h_g[SUBLANES - 1:SUBLANES]
        if (g + 1) % groups_per_seq == 0:
            h_ref[q] = carry
    h = jnp.concatenate(h_groups, axis=0)
    ob = h * _silu(z_bg)
    merged_scr[...] += _sigmoid(z_g1) * _dot(ob.astype(BF16), wbr_ref[d_a:d_a + d_b, :])
    g2_ref[0] = _sigmoid(z_g2).astype(BF16)

    mq = z_mq.astype(BF16)
    lane = lax.broadcasted_iota(jnp.int32, (1, LANES), 1)
    om_rows = []
    for q in range(n_seq):
        mkb = mk_ref[q].astype(BF16)
        mvb = mv_ref[q].astype(BF16)
        slabs = []
        for p in range(d_m // LANES):
            mq_p = mq[q * seq_len:(q + 1) * seq_len, p * LANES:(p + 1) * LANES]
            mk_p = mkb[:, p * LANES:(p + 1) * LANES]
            mv_p = mvb[:, p * LANES:(p + 1) * LANES]
            acc = None
            for half in range(LANES // DH_M):
                sel = (lane >= half * DH_M) & (lane < (half + 1) * DH_M)
                sc = _dot_nt(mq_p, jnp.where(sel, mk_p, jnp.zeros_like(mk_p))) * (DH_M ** -0.5)
                e = jnp.exp(sc - jnp.max(sc, -1, keepdims=True))
                prob = (e / jnp.sum(e, -1, keepdims=True)).astype(BF16)
                o = _dot(prob, jnp.where(sel, mv_p, jnp.zeros_like(mv_p)))
                acc = o if acc is None else acc + o
            slabs.append(acc)
        om_rows.append(jnp.concatenate(slabs, axis=1))
    om = om_rows[0] if n_seq == 1 else jnp.concatenate(om_rows, axis=0)
    m_lo = d_a + d_b + d_c
    partial_ref[0] = (merged_scr[...]
                      + _sigmoid(z_g3) * _dot(om.astype(BF16), wbr_ref[m_lo:m_lo + d_m, :])).astype(BF16)
    cgst_ref[0] = _silu(cg_t)

    ckv = _rms_norm(z_c[:, q_lora:q_lora + kv_lora], vec(kvnorm_ref))
    half = D_ROPE // 2
    z_kr = z_c[:, q_lora + kv_lora:q_lora + kv_lora + LANES]
    kr_swapped = jnp.where(lane < half, pltpu.roll(z_kr, LANES - half, axis=1), pltpu.roll(z_kr, half, axis=1))
    kr = z_kr[:, 0:D_ROPE] * ck_ref[...] + kr_swapped[:, 0:D_ROPE] * sk_ref[...]
    ckv_ref[0] = ckv
    kr_ref[0] = kr
    if first_chunk:
        kcat_ref[0], vt_all = _keys_values(ckv, kr, wkc_ref, wkr_ref, wvt_ref)
        for j in range(tm // ATTN_TK):
            vt_ref[0, j] = vt_all[:, j * ATTN_TK:(j + 1) * ATTN_TK]
    cqn =_rms_norm(z_c[:, 0:q_lora], vec(qnorm_ref)).astype(BF16)
    q_t = _dot_nt(wuqt_ref[...], cqn)
    cos_t, sin_t = cosq_ref[...], sinq_ref[...]
    for hd in range(H_C):
        lo = hd * LANES
        x1 = q_t[lo + D_NOPE:lo + D_NOPE + half]
        x2 = q_t[lo + D_NOPE + half:lo + D_NOPE + D_ROPE]
        qt_ref[0, 0, lo:lo + D_NOPE, :] = q_t[lo:lo + D_NOPE].astype(BF16)
        qt_ref[0, 0, lo + D_NOPE:lo + D_NOPE + half, :] = (x1 * cos_t - x2 * sin_t).astype(BF16)
        qt_ref[0, 0, lo + D_NOPE + half:lo + D_NOPE + D_ROPE, :] = (x1 * sin_t + x2 * cos_t).astype(BF16)
        qt_ref[0, 0, lo + D_NOPE + D_ROPE:lo + LANES, :] = jnp.zeros((LANES - D_NOPE - D_ROPE, tm), BF16)


def _branch_call(x, pw, l, sp, conv0_pad, h0, mk, mv, rope, *, tm, seq_len, tq, first_chunk):
    bsz, seq, d_model = x.shape
    d_a, d_b, d_c, d_m = pw["dims"]
    q_lora, kv_lora = pw["q_lora"], pw["kv_lora"]
    n_mem = mk.shape[1]
    cos_t, sin_t, ck_t, sk_t = rope
    grid = (bsz, seq // tm)

    def row_spec(width):
        return pl.BlockSpec((1, tm, width), lambda b, t: (b, t, 0))

    def col_spec(height):
        return pl.BlockSpec((1, height, tm), lambda b, t: (b, 0, t))

    def tab_spec(width):
        return pl.BlockSpec((tm, width), lambda b, t: (t, 0))

    n_seq = tm // seq_len

    def batch_spec(rows, width):
        return pl.BlockSpec((n_seq, rows, width), lambda b, t: (b, 0, 0))

    names = ["w1", "wcgt", "gln_g", "gln_b", "wsp%d" % sp, "bsp%d" % sp, "conv_w", "conv_b",
             "wgate", "b_r", "b_i", "lam"]
    names2 = ["q_norm", "wuqt", "kv_norm"]
    in_specs = ([row_spec(d_model)] + [_layer_spec(pw[n], l) for n in names]
                + [batch_spec(SUBLANES, d_b), batch_spec(1, d_b)] + [_layer_spec(pw[n], l) for n in names2]
                + [pl.BlockSpec((D_ROPE // 2, tm), lambda b, t: (0, t)), pl.BlockSpec((D_ROPE // 2, tm), lambda b, t: (0, t)),
                   tab_spec(D_ROPE), tab_spec(D_ROPE), batch_spec(n_mem, d_m), batch_spec(n_mem, d_m),
                   _layer_spec(pw["wbr"], l)])
    operands = ([x] + [pw[n] for n in names] + [conv0_pad, h0] + [pw[n] for n in names2]
                + [cos_t, sin_t, ck_t, sk_t, mk, mv, pw["wbr"]])
    out_shape = [
        jax.ShapeDtypeStruct((bsz, seq, d_model), BF16),
        jax.ShapeDtypeStruct((bsz, seq, d_model), BF16),
        jax.ShapeDtypeStruct((bsz, d_c, seq), F32),
        jax.ShapeDtypeStruct((bsz, seq // tq, H_C * LANES, tq), BF16),
        jax.ShapeDtypeStruct((bsz, seq, kv_lora), F32),
        jax.ShapeDtypeStruct((bsz, seq, D_ROPE), F32),
        jax.ShapeDtypeStruct((bsz * n_seq, SUBLANES, d_b), F32),
        jax.ShapeDtypeStruct((bsz * n_seq, 1, d_b), F32),
    ]
    out_specs = [
        row_spec(d_model), row_spec(d_model), col_spec(d_c),
        pl.BlockSpec((1, 1, H_C * LANES, tm), lambda b, t: (b, t // (tq // tm), 0, t % (tq // tm))),
        row_spec(kv_lora), row_spec(D_ROPE), batch_spec(SUBLANES, d_b), batch_spec(1, d_b),
    ]
    if first_chunk:
        for name in ("wkc", "wkr", "wvt"):
            in_specs.append(_layer_spec(pw[name], l))
            operands.append(pw[name])
        out_shape += [jax.ShapeDtypeStruct((bsz, seq, H_C * LANES), BF16),
                      jax.ShapeDtypeStruct((bsz, seq // ATTN_TK, H_C * V_ROWS, ATTN_TK), BF16)]
        out_specs += [row_spec(H_C * LANES),
                      pl.BlockSpec((1, tm // ATTN_TK, H_C * V_ROWS, ATTN_TK), lambda b, t: (b, t, 0, 0))]
    else:
        out_shape.append(jax.ShapeDtypeStruct((bsz, seq, d_a), F32))
        out_specs.append(row_spec(d_a))
    kern = functools.partial(
        _branch_kernel, layer=l, first_chunk=first_chunk, tm=tm, seq_len=seq_len, sp_len=sp, d_a=d_a, d_b=d_b, d_c=d_c, d_m=d_m,
        q_lora=q_lora, kv_lora=kv_lora, cols=pw["cols"])
    return pl.pallas_call(
        kern, grid=grid, in_specs=in_specs, out_specs=out_specs, out_shape=out_shape,
        scratch_shapes=[pltpu.VMEM((tm, d_model), F32)],
        name="branch", compiler_params=_params(2),
    )(*operands)


HEADS_PER_STEP = 4
_EXP2_SCALE = (D_NOPE + D_ROPE) ** -0.5 * math.log2(math.e)


def _scores(k_tile, q_tile, out_refs):
    res = [_dot(k_tile[:, h * LANES:(h + 1) * LANES], q_tile[h * LANES:(h + 1) * LANES, :])
           for h in range(k_tile.shape[1] // LANES)]
    for h, r in enumerate(res):
        out_refs[h] = r


def _softmax_tile(s_t, m_old):
    m_new = jnp.maximum(m_old, jnp.max(s_t, axis=0, keepdims=True))
    return jnp.exp2(s_t - m_new).astype(BF16), m_new, jnp.exp2(m_old - m_new)


def _attn_output(acc, cgs_t):
    return (acc[0:D_V] / acc[D_V:D_V + 1] * cgs_t).astype(BF16)


N_BUF = 4


def _attn_causal_kernel(qt_ref, k_ref, vt_ref, cgst_ref, bias_ref, o_ref, *scratch, tq, tk):
    s_buf, e_buf, acc_scr = scratch[0:N_BUF], scratch[N_BUF:2 * N_BUF], scratch[2 * N_BUF]
    i = pl.program_id(2)
    nh = HEADS_PER_STEP
    q_cur = qt_ref.at[0, i]
    q_nxt = qt_ref.at[0, jnp.minimum(i + 1, pl.num_programs(2) - 1)]

    def k_tile(j):
        return k_ref[0, pl.ds(pl.multiple_of(j * tk, tk), tk), :]

    def pv(j, e_in, c0):
        vt = vt_ref[0, j]
        return [_dot(vt[h * V_ROWS:(h + 1) * V_ROWS, :], e_in[h, :, c0:tq]) for h in range(nh)]

    def acc_update(pvs, alphas, c0):
        for h in range(nh):
            acc_scr[h, :, c0:tq] = acc_scr[h, :, c0:tq] * alphas[h][:, c0:tq] + pvs[h]

    def softmax(slot, carry, c0, add_bias):
        new = []
        for h in range(nh):
            m_old, a_prev, _ = carry[h]
            s_t = s_buf[slot][h, :, c0:tq]
            if add_bias:
                lead = s_t[:, 0:tk] + bias_ref[...]
                s_t = lead if c0 + tk == tq else jnp.concatenate([lead, s_t[:, tk:]], axis=1)
            e, m_new, alpha = _softmax_tile(s_t, m_old[:, c0:tq])
            e_buf[slot][h, :, c0:tq] = e
            if c0:
                m_new = jnp.concatenate([m_old[:, 0:c0], m_new], axis=1)
                alpha = jnp.concatenate([jnp.ones((1, c0), F32), alpha], axis=1)
            new.append((m_new, alpha, a_prev))
        return tuple(new)

    def stage(n, slot, carry):
        pvs = pv(jnp.maximum(n - 2, 0), e_buf[(slot + 2) % N_BUF], 0)
        _scores(k_tile(n + 3), q_cur, s_buf[(slot + 3) % N_BUF])
        new = softmax(slot, carry, 0, False)
        acc_update(pvs, [c[2] for c in carry], 0)
        return new

    for slot in (N_BUF - 2, N_BUF - 1):
        e_buf[slot][...] = jnp.zeros(e_buf[slot].shape, BF16)
    acc_scr[...] = jnp.zeros(acc_scr.shape, F32)

    @pl.when(i == 0)
    def _():
        for slot in range(N_BUF - 1):
            _scores(k_tile(slot), q_cur, s_buf[slot])

    ones = jnp.ones((1, tq), F32)
    carry = tuple((jnp.full((1, tq), -jnp.inf, F32), ones, ones) for _ in range(nh))

    def body(t, carry):
        for slot in range(N_BUF):
            carry = stage(N_BUF * t + slot, slot, carry)
        return carry

    carry = lax.fori_loop(0, i, body, carry)

    j0 = N_BUF * i
    last = N_BUF - 1
    for d in range(N_BUF):
        c_prev = max(d - 2, 0) * tk
        pvs = pv(jnp.maximum(j0 + d - 2, 0), e_buf[(d + 2) % N_BUF], c_prev)
        if d == 0:
            k_last = k_tile(j0 + last)
            for h in range(nh):
                s_buf[last][h, :, tq - tk:tq] = _dot(
                    k_last[:, h * LANES:(h + 1) * LANES],
                    q_cur[h * LANES:(h + 1) * LANES, tq - tk:tq]) + bias_ref[...]
        else:
            _scores(k_tile(d - 1), q_nxt, s_buf[d - 1])
        alphas = [c[2] for c in carry]
        carry = softmax(d, carry, d * tk, d < last)
        acc_update(pvs, alphas, c_prev)
    acc_update(pv(j0 + last - 1, e_buf[last - 1], (last - 1) * tk), [c[2] for c in carry], (last - 1) * tk)
    acc_update(pv(j0 + last, e_buf[last], last * tk), [c[1] for c in carry], last * tk)
    for h in range(nh):
        o_ref[0, h * D_V:(h + 1) * D_V, :] = _attn_output(acc_scr[h], cgst_ref[0, h * D_V:(h + 1) * D_V, :])


def _attn_latent_kernel(qt_ref, pckv_ref, pkr_ref, ckv_ref, kr_ref, cgst_ref, wuk_ref, wuvt_ref, place_ref, o_ref):
    tq = qt_ref.shape[-1]
    c = jnp.concatenate([pckv_ref[0], ckv_ref[0]], axis=0).astype(BF16)
    kr = jnp.concatenate([pkr_ref[0], kr_ref[0]], axis=0).astype(BF16)
    q_lat, q_rope = None, None
    for h in range(H_C):
        lo = h * LANES
        q_h = _dot(wuk_ref[h], qt_ref[0, lo:lo + D_NOPE, :]).astype(BF16)
        a_h = _dot(q_h, place_ref[h])
        r_h = _dot(qt_ref[0, lo + D_NOPE:lo + D_NOPE + D_ROPE, :], place_ref[h])
        q_lat = a_h if q_lat is None else q_lat + a_h
        q_rope = r_h if q_rope is None else q_rope + r_h
    s = _dot(c, q_lat.astype(BF16)) + _dot(kr, q_rope.astype(BF16))
    e = jnp.exp2(s - jnp.max(s, axis=0, keepdims=True)).astype(BF16)
    denom = jnp.sum(e.astype(F32), axis=0, keepdims=True)
    lat = _dot_tn(c, e).astype(BF16)
    for h in range(H_C):
        o_h = _dot(wuvt_ref[h], lat[:, h * tq:(h + 1) * tq])
        o_ref[0, h * D_V:(h + 1) * D_V, :] = (
            o_h / denom[:, h * tq:(h + 1) * tq] * cgst_ref[0, h * D_V:(h + 1) * D_V, :]).astype(BF16)


def _attn_causal_call(qt, kcat, vt, cgst):
    bsz, n_qt, _, tq = qt.shape
    seq = n_qt * tq
    t_k = kcat.shape[1]
    nkt, _, tk = vt.shape[1:]
    nh = HEADS_PER_STEP
    assert tq == N_BUF * tk and seq == t_k and tk % CHUNK == 0
    chunk = jnp.arange(tk) // CHUNK
    bias = jnp.where(chunk[:, None] <= chunk[None, :], 0.0, -1e30).astype(F32)
    return pl.pallas_call(
        functools.partial(_attn_causal_kernel, tq=tq, tk=tk), grid=(bsz, H_C // nh, n_qt),
        in_specs=[pl.BlockSpec((1, n_qt, nh * LANES, tq), lambda b, p, i: (b, 0, p, 0)),
                  pl.BlockSpec((1, t_k, nh * LANES), lambda b, p, i: (b, 0, p)),
                  pl.BlockSpec((1, nkt, nh * V_ROWS, tk), lambda b, p, i: (b, 0, p, 0)),
                  pl.BlockSpec((1, nh * D_V, tq), lambda b, p, i: (b, p, i)),
                  _const_spec((tk, tk))],
        out_specs=pl.BlockSpec((1, nh * D_V, tq), lambda b, p, i: (b, p, i)),
        out_shape=jax.ShapeDtypeStruct((bsz, H_C * D_V, seq), BF16),
        scratch_shapes=([pltpu.VMEM((nh, tk, tq), F32)] * N_BUF + [pltpu.VMEM((nh, tk, tq), BF16)] * N_BUF
                        + [pltpu.VMEM((nh, V_ROWS, tq), F32)]),
        name="attn", compiler_params=_params(3),
    )(qt, kcat, vt, cgst, bias)


def _attn_cached_call(qt, past_ckv, past_kr, ckv_new, kr_new, cgst, pw, l):
    bsz, n_q, tq = qt.shape
    past_len, kv_lora = past_ckv.shape[2:]
    place = (jnp.eye(H_C, dtype=BF16)[:, None, :, None] * jnp.eye(tq, dtype=BF16)[None, :, None, :])
    place = place.reshape(H_C, tq, H_C * tq)
    return pl.pallas_call(
        _attn_latent_kernel, grid=(bsz,),
        in_specs=[pl.BlockSpec((1, n_q, tq), lambda b: (b, 0, 0)),
                  pl.BlockSpec((None, 1, past_len, kv_lora), lambda b: (l, b, 0, 0)),
                  pl.BlockSpec((None, 1, past_len, D_ROPE), lambda b: (l, b, 0, 0)),
                  pl.BlockSpec((1, tq, kv_lora), lambda b: (b, 0, 0)),
                  pl.BlockSpec((1, tq, D_ROPE), lambda b: (b, 0, 0)),
                  pl.BlockSpec((1, H_C * D_V, tq), lambda b: (b, 0, 0)),
                  _layer_spec(pw["wuk"], l), _layer_spec(pw["wuvt"], l), _const_spec(place.shape)],
        out_specs=pl.BlockSpec((1, H_C * D_V, tq), lambda b: (b, 0, 0)),
        out_shape=jax.ShapeDtypeStruct((bsz, H_C * D_V, tq), BF16),
        name="attn_cached", compiler_params=_params(1),
    )(qt, past_ckv, past_kr, ckv_new, kr_new, cgst, pw["wuk"], pw["wuvt"], place)


def _merge_kernel(x_ref, partial_ref, g2_ref, oct_ref, wbr_ref, wout_ref, lng_ref, lnb_ref, o_ref,
                  *, layer, c_lo, d_c, alpha):
    yc = _dot_tn(oct_ref[0], wbr_ref[c_lo:c_lo + d_c, :])
    merged = partial_ref[0].astype(F32) + g2_ref[0].astype(F32) * yc
    y = _dot(merged.astype(BF16), wout_ref[...])
    o_ref[0] = _layer_norm(alpha * x_ref[0] + y, lng_ref[layer:layer + 1, :], lnb_ref[layer:layer + 1, :])


def _merge_call(x, partial, g2, oct, pw, l, *, tr, alpha):
    bsz, seq, d_model = x.shape
    d_a, d_b, d_c, _ = pw["dims"]
    kern = functools.partial(_merge_kernel, layer=l, c_lo=d_a + d_b, d_c=d_c, alpha=alpha)

    def row_spec(width):
        return pl.BlockSpec((1, tr, width), lambda b, i: (b, i, 0))

    return pl.pallas_call(
        kern, grid=(bsz, seq // tr),
        in_specs=[row_spec(d_model), row_spec(d_model), row_spec(d_model),
                  pl.BlockSpec((1, d_c, tr), lambda b, i: (b, 0, i)),
                  _layer_spec(pw["wbr"], l), _layer_spec(pw["wout"], l),
                  _layer_spec(pw["ln_g"], l), _layer_spec(pw["ln_b"], l)],
        out_specs=row_spec(d_model),
        out_shape=jax.ShapeDtypeStruct((bsz, seq, d_model), F32),
        name="merge", compiler_params=_params(2),
    )(x, partial, g2, oct, pw["wbr"], pw["wout"], pw["ln_g"], pw["ln_b"])


def _mem_kernel(m_ref, w_ref, k_ref, v_ref, *, d_m):
    kv = _dot(m_ref[...].astype(BF16), w_ref[...])
    k_ref[...] = kv[:, 0:d_m]
    v_ref[...] = kv[:, d_m:2 * d_m]


def _mem_call(mem2d, wmem, l, *, tr):
    rows, d_model = mem2d.shape
    d_m = wmem.shape[2] // 2
    return pl.pallas_call(
        functools.partial(_mem_kernel, d_m=d_m), grid=(rows // tr,),
        in_specs=[pl.BlockSpec((tr, d_model), lambda i: (i, 0)), _layer_spec(wmem, l)],
        out_specs=[pl.BlockSpec((tr, d_m), lambda i: (i, 0)), pl.BlockSpec((tr, d_m), lambda i: (i, 0))],
        out_shape=[jax.ShapeDtypeStruct((rows, d_m), F32), jax.ShapeDtypeStruct((rows, d_m), F32)],
        name="mem", compiler_params=_params(1),
    )(mem2d, wmem)


def _block_diag(w):
    h, n, _ = w.shape
    eye = jnp.eye(h, dtype=w.dtype)
    return (eye[:, None, :, None] * w[:, :, None, :]).reshape(h * n, h * n)


def _gate_blocks(w_r, w_i):
    per = MXU_TILE // w_r.shape[1]
    blocks = []
    for k in range(w_r.shape[0] // per):
        blocks.append(jnp.concatenate([_block_diag(w_r[k * per:(k + 1) * per]),
                                       _block_diag(w_i[k * per:(k + 1) * per])], axis=1))
    return jnp.stack(blocks)


def _w_in_kernel(wt_ref, o_ref, *, n_head, o_tail):
    o_ref[0, :, 0:n_head] = jnp.transpose(wt_ref[0, 0:n_head, :]).astype(BF16)
    o_ref[0, :, n_head:] = jnp.transpose(wt_ref[0, o_tail:, :]).astype(BF16)


def _w_in_call(w_in, n_head, o_tail, *, tk):
    depth, d_model, d_in = w_in.shape
    n_out = n_head + d_in - o_tail
    return pl.pallas_call(
        functools.partial(_w_in_kernel, n_head=n_head, o_tail=o_tail), grid=(depth, d_model // tk),
        in_specs=[pl.BlockSpec((1, d_in, tk), lambda l, i: (l, 0, i))],
        out_specs=pl.BlockSpec((1, tk, n_out), lambda l, i: (l, i, 0)),
        out_shape=jax.ShapeDtypeStruct((depth, d_model, n_out), BF16),
        name="w_in", compiler_params=_params(2),
    )(jnp.swapaxes(w_in, 1, 2))


def _prep_weights(sp_lens, w_in, gmlp_ln_g, gmlp_ln_b, gmlp_ws, gmlp_bs, lru_conv_w,
                  lru_conv_b, lru_w_r, lru_b_r, lru_w_i, lru_b_i, lru_lambda, mla_q_norm, mla_w_uq,
                  mla_kv_norm, mla_w_ukv, mem_w_k, mem_w_v, w_br, w_out, ln_g, ln_b):
    depth, d_model, _ = w_in.shape
    d_a = gmlp_ln_g.shape[1]
    d_b = lru_lambda.shape[1]
    q_lora = mla_q_norm.shape[1]
    kv_lora = mla_kv_norm.shape[1]
    d_c = H_C * D_V
    d_m = H_M * DH_M
    o_cq = 3 * d_a + 2 * d_b
    o_kr = o_cq + q_lora + kv_lora
    o_cg = o_kr + D_ROPE
    o_mq = o_cg + d_c
    o_g = o_mq + d_m
    n_head = o_kr + LANES
    assert o_kr % LANES == 0 and (o_g - o_mq) % LANES == 0 and D_ROPE <= LANES and n_head <= o_mq
    cols = {"a_u": (0, d_a), "a_v": (d_a, 2 * d_a), "a_g": (2 * d_a, 3 * d_a),
            "b_x": (3 * d_a, 3 * d_a + d_b), "b_g": (3 * d_a + d_b, o_cq), "c_all": (o_cq, n_head),
            "m_q": (n_head, n_head + d_m)}
    for k in range(N_BRANCH):
        cols[f"g{k}"] = (n_head + d_m + k * d_model, n_head + d_m + (k + 1) * d_model)
    w1 = _w_in_call(w_in, n_head, o_mq, tk=LANES)
    wcgt = jnp.swapaxes(w_in[:, :, o_cg:o_mq], 1, 2).astype(BF16)

    wq = mla_w_uq.reshape(depth, q_lora, H_C, D_NOPE + D_ROPE) * _EXP2_SCALE
    wq = jnp.concatenate([wq, jnp.zeros((depth, q_lora, H_C, LANES - D_NOPE - D_ROPE), wq.dtype)], axis=3)
    wuqt = jnp.swapaxes(wq.reshape(depth, q_lora, H_C * LANES), 1, 2).astype(BF16)

    wkv = mla_w_ukv.reshape(depth, kv_lora, H_C, D_NOPE + D_V)
    wk = wkv[..., :D_NOPE]
    wv = wkv[..., D_NOPE:]
    wkc = jnp.concatenate([wk, jnp.zeros((depth, kv_lora, H_C, LANES - D_NOPE), wk.dtype)], axis=3)
    wkc = wkc.reshape(depth, kv_lora, H_C * LANES).astype(BF16)
    lane = jnp.arange(H_C * LANES) % LANES
    wkr = ((lane[None, :] - D_NOPE) == jnp.arange(D_ROPE)[:, None]).astype(BF16)
    wkr = jnp.broadcast_to(wkr, (depth,) + wkr.shape)
    wv = jnp.concatenate([wv, jnp.zeros((depth, kv_lora, H_C, V_ROWS - D_V), wv.dtype)], axis=3)
    wvt = jnp.swapaxes(wv.reshape(depth, kv_lora, H_C * V_ROWS), 1, 2).astype(BF16)

    pw = dict(
        dims=(d_a, d_b, d_c, d_m), q_lora=q_lora, kv_lora=kv_lora, cols=cols,
        w1=w1, wcgt=wcgt,
        gln_g=gmlp_ln_g, gln_b=gmlp_ln_b,
        conv_w=lru_conv_w, conv_b=lru_conv_b,
        wgate=jnp.stack([_gate_blocks(lru_w_r[l], lru_w_i[l]) for l in range(depth)]).astype(BF16),
        b_r=lru_b_r, b_i=lru_b_i, lam=lru_lambda,
        q_norm=mla_q_norm, wuqt=wuqt, kv_norm=mla_kv_norm,
        wkc=wkc, wkr=wkr, wvt=wvt,
        wuk=jnp.transpose(wk, (0, 2, 1, 3)).astype(BF16),
        wuvt=jnp.transpose(wkv[..., D_NOPE:], (0, 2, 3, 1)).astype(BF16),
        wbr=w_br.astype(BF16), wout=w_out.astype(BF16), ln_g=ln_g, ln_b=ln_b,
        wmem=jnp.concatenate([mem_w_k, mem_w_v], axis=2).astype(BF16),
    )
    for sp in sp_lens:
        pw["wsp%d" % sp] = jnp.tril(gmlp_ws[:, :, :sp, :sp]).astype(BF16)
        pw["bsp%d" % sp] = jnp.repeat(jnp.swapaxes(gmlp_bs[:, :, :sp], 1, 2), d_a // G_A, axis=2)
    return pw


def _rope_tables(pos):
    half = D_ROPE // 2
    freq = ROPE_BASE ** (-jnp.arange(half, dtype=F32) / half)
    ang = pos.astype(F32)[:, None] * freq[None, :]
    cos, sin = jnp.cos(ang), jnp.sin(ang)
    return (jnp.transpose(cos), jnp.transpose(sin),
            jnp.concatenate([cos, cos], axis=1), jnp.concatenate([-sin, sin], axis=1))


def _pad_conv_state(conv):
    return jnp.pad(conv, ((0, 0), (SUBLANES - (CONV_W - 1), 0), (0, 0)))


def _trunk_layer(x, pw, l, sp, rope, conv0, h0, mk, mv, past_ckv, past_kr, *, tm, alpha):
    bsz, seq, d_model = x.shape
    first_chunk = past_ckv is None
    if first_chunk:
        x_rows, seq_len, tq = x, tm, N_BUF * ATTN_TK
    else:
        x_rows, seq_len, tq = x.reshape(1, bsz * seq, d_model), seq, bsz * seq
        cos_t, sin_t, ck_t, sk_t = rope
        rope = (jnp.tile(cos_t, (1, bsz)), jnp.tile(sin_t, (1, bsz)), jnp.tile(ck_t, (bsz, 1)), jnp.tile(sk_t, (bsz, 1)))
    outs = _branch_call(x_rows, pw, l, sp, _pad_conv_state(conv0), h0[:, None, :], mk, mv, rope,
                        tm=tm, seq_len=seq_len, tq=tq, first_chunk=first_chunk)
    partial, g2, cgst, qt, ckv_new, kr_new, conv_pad, h_new = outs[:8]
    if first_chunk:
        kcat, vt = outs[8:]
        v_rows = None
        oct = _attn_causal_call(qt, kcat, vt, cgst)
    else:
        ckv_new, kr_new, v_rows = (o.reshape(bsz, seq, -1) for o in (ckv_new, kr_new, outs[8]))

        def per_seq(a):
            return jnp.transpose(a.reshape(a.shape[-2], bsz, seq), (1, 0, 2))

        oct = _attn_cached_call(per_seq(qt), past_ckv, past_kr, ckv_new, kr_new, per_seq(cgst), pw, l)
        oct = jnp.transpose(oct, (1, 0, 2)).reshape(1, oct.shape[1], bsz * seq)
    x_new = _merge_call(x_rows, partial, g2, oct, pw, l, tr=MERGE_TILE if first_chunk else tm, alpha=alpha)
    return (x_new.reshape(bsz, seq, d_model), v_rows, conv_pad[:, SUBLANES - (CONV_W - 1):], h_new[:, 0],
            ckv_new, kr_new)


def kernel(x_prompt, x_sample, mem_prompt, cache_mla_ckv, cache_mla_krope, cache_mem_k, cache_mem_v,
           state_lru_h, state_lru_conv, w_in, gmlp_ln_g, gmlp_ln_b, gmlp_ws, gmlp_bs,
           lru_conv_w, lru_conv_b, lru_w_r, lru_b_r, lru_w_i, lru_b_i, lru_lambda,
           mla_q_norm, mla_w_uq, mla_kv_norm, mla_w_ukv, mem_w_k, mem_w_v, w_br, w_out, ln_g, ln_b):
    bp, tp, d_model = x_prompt.shape
    bs, ts, _ = x_sample.shape
    depth = w_in.shape[0]
    past_len = cache_mla_ckv.shape[2]
    n_mem = mem_prompt.shape[1]
    d_b = lru_lambda.shape[1]
    alpha = (2.0 * depth) ** 0.25
    sp_p, sp_s = min(tp, A_CHUNK), min(ts, A_CHUNK)
    pw = _prep_weights(sorted({sp_p, sp_s}), w_in, gmlp_ln_g, gmlp_ln_b, gmlp_ws, gmlp_bs, lru_conv_w,
                       lru_conv_b, lru_w_r, lru_b_r, lru_w_i, lru_b_i, lru_lambda, mla_q_norm, mla_w_uq,
                       mla_kv_norm, mla_w_ukv, mem_w_k, mem_w_v, w_br, w_out, ln_g, ln_b)
    rope_p = _rope_tables(jnp.arange(tp))
    rope_s = _rope_tables(past_len + jnp.arange(ts))
    tm_p = min(tp, ROW_TILE)
    assert tp % (N_BUF * ATTN_TK) == 0 and tm_p % ATTN_TK == 0 and tp % MERGE_TILE == 0 and ts % SUBLANES == 0
    zero_conv = jnp.zeros((bp, CONV_W - 1, d_b), F32)
    zero_h = jnp.zeros((bp, d_b), F32)

    xp, xs = x_prompt, x_sample
    acc = [[] for _ in range(11)]
    for l in range(depth):
        mk, mv = _mem_call(mem_prompt.reshape(bp * n_mem, d_model), pw["wmem"], l, tr=n_mem)
        mk = mk.reshape(bp, n_mem, -1)
        mv = mv.reshape(bp, n_mem, -1)
        xp, _, conv_n, h_n, ckv_n, kr_n = _trunk_layer(
            xp, pw, l, sp_p, rope_p, zero_conv, zero_h, mk, mv, None, None, tm=tm_p, alpha=alpha)
        for k, val in zip(range(6), (ckv_n, kr_n, mk.reshape(bp, n_mem, H_M, DH_M),
                                     mv.reshape(bp, n_mem, H_M, DH_M), h_n, conv_n)):
            acc[k].append(val)
        xs, v_n, conv_n, h_n, ckv_n, kr_n = _trunk_layer(
            xs, pw, l, sp_s, rope_s, state_lru_conv[l], state_lru_h[l],
            cache_mem_k[l].reshape(bs, n_mem, -1), cache_mem_v[l].reshape(bs, n_mem, -1),
            cache_mla_ckv, cache_mla_krope, tm=bs * ts, alpha=alpha)
        for k, val in zip(range(6, 11), (ckv_n, kr_n, h_n, conv_n, v_n)):
            acc[k].append(val)
    return (xp, xs) + tuple(jnp.stack(a) for a in acc)
```

```python
import functools
import math

import jax
import jax.numpy as jnp
from jax import lax
from jax.experimental import pallas as pl
from jax.experimental.pallas import tpu as pltpu

CHUNK = 64
G_A = 4
A_CHUNK = 128
H_B = 8
CONV_W = 4
LRU_C = 8.0
H_C = 8
D_NOPE = 64
D_ROPE = 32
D_V = 64
ROPE_BASE = 10000.0
H_M = 4
DH_M = 64
N_BRANCH = 4
EPS = 1e-6

LANES = 128
SUBLANES = 8
VMEM_LIMIT = 56 * 1024 * 1024
ROW_TILE = 512
ATTN_TK = 256
MXU_TILE = 256
MERGE_TILE = 1024

F32 = jnp.float32
BF16 = jnp.bfloat16


def _dot(a, b):
    return jnp.dot(a, b, preferred_element_type=F32)


def _dot_nt(a, b):
    return lax.dot_general(a, b, (((1,), (1,)), ((), ())), preferred_element_type=F32)


def _dot_tn(a, b):
    return lax.dot_general(a, b, (((0,), (0,)), ((), ())), preferred_element_type=F32)


def _sigmoid(x):
    return 1.0 / (1.0 + jnp.exp(-x))


def _silu(x):
    return x * _sigmoid(x)


def _gelu(x):
    return jax.nn.gelu(x)


def _expm1_nonpos(x):
    u = jnp.exp(x)
    near = (u - 1.0) * x / jnp.log(jnp.where(u == 1.0, 2.0, jnp.maximum(u, 0.5)))
    return jnp.where(u == 1.0, x, jnp.where(u > 0.5, near, u - 1.0))


def _layer_norm(x, g, b):
    mu = jnp.mean(x, -1, keepdims=True)
    var = jnp.mean(jnp.square(x - mu), -1, keepdims=True)
    return (x - mu) * lax.rsqrt(var + EPS) * g + b


def _rms_norm(x, g):
    return x * lax.rsqrt(jnp.mean(jnp.square(x), -1, keepdims=True) + EPS) * g


def _const_spec(shape):
    nd = len(shape)
    return pl.BlockSpec(shape, lambda *_: (0,) * nd, pipeline_mode=pl.Buffered(1))


def _layer_spec(arr, l):
    if arr.ndim == 2:
        return _const_spec(arr.shape)
    nd = arr.ndim - 1
    return pl.BlockSpec((None,) + arr.shape[1:], lambda *_: (l,) + (0,) * nd, pipeline_mode=pl.Buffered(1))


def _params(n_axes):
    return pltpu.CompilerParams(dimension_semantics=("arbitrary",) * n_axes,
                                vmem_limit_bytes=VMEM_LIMIT)


V_ROWS = 80


def _keys_values(ckv, kr, wkc_ref, wkr_ref, wvt_ref):
    c = ckv.astype(BF16)
    kcat = (_dot(c, wkc_ref[...]) + _dot(kr.astype(BF16), wkr_ref[...])).astype(BF16)
    v_t = _dot_nt(wvt_ref[...], c)
    row = lax.broadcasted_iota(jnp.int32, (v_t.shape[0], 1), 0)
    return kcat, jnp.where(row % V_ROWS == D_V, 1.0, v_t).astype(BF16)


def _branch_kernel(x_ref, w1_ref, wcgt_ref, glng_ref, glnb_ref, wsp_ref, bsp_ref,
                   convw_ref, convb_ref, wgate_ref, br_ref, bi_ref, lam_ref, conv0_ref, h0_ref, qnorm_ref, wuqt_ref,
                   kvnorm_ref, cosq_ref, sinq_ref, ck_ref, sk_ref, mk_ref, mv_ref, wbr_ref,
                   *rest, layer, first_chunk, tm, seq_len, sp_len, d_a, d_b, d_c, d_m, q_lora, kv_lora, cols):
    if first_chunk:
        (wkc_ref, wkr_ref, wvt_ref, partial_ref, g2_ref, cgst_ref, qt_ref, ckv_ref, kr_ref, conv_ref, h_ref,
         kcat_ref, vt_ref, merged_scr) = rest
    else:
        partial_ref, g2_ref, cgst_ref, qt_ref, ckv_ref, kr_ref, conv_ref, h_ref, v_ref, merged_scr = rest
    t = pl.program_id(1)
    xb = x_ref[0].astype(BF16)

    def vec(ref):
        return ref[layer:layer + 1, :]

    def zin(name):
        lo, hi = cols[name]
        return _dot(xb, w1_ref[:, lo:hi])

    z_u, z_v, z_ag, z_g0 = zin("a_u"), zin("a_v"), zin("a_g"), zin("g0")
    u = _gelu(z_u)
    v = _layer_norm(_gelu(z_v), vec(glng_ref), vec(glnb_ref))
    if not first_chunk:
        v_ref[0] = v
    vb = v.astype(BF16)
    n_groups = d_a // LANES
    row_blocks = []
    for c in range(tm // sp_len):
        col_blocks = [_dot(wsp_ref[g], vb[c * sp_len:(c + 1) * sp_len, g * LANES:(g + 1) * LANES])
                      for g in range(n_groups)]
        row_blocks.append(jnp.concatenate(col_blocks, axis=1) + bsp_ref[...])
    s = row_blocks[0] if len(row_blocks) == 1 else jnp.concatenate(row_blocks, axis=0)
    oa = (u * s) * _silu(z_ag)
    merged_scr[...] = _sigmoid(z_g0) * _dot(oa.astype(BF16), wbr_ref[0:d_a, :])

    @pl.when(t == 0)
    def _():
        conv_ref[...] = conv0_ref[...]
        h_ref[...] = h0_ref[...]

    bx = zin("b_x")
    n_seq = tm // seq_len
    span = SUBLANES + seq_len
    stacked = jnp.concatenate(
        [piece for q in range(n_seq) for piece in (conv_ref[q], bx[q * seq_len:(q + 1) * seq_len])], axis=0)
    xc = vec(convb_ref)
    for k in range(CONV_W):
        shift = CONV_W - 1 - k
        rolled = stacked if shift == 0 else pltpu.roll(stacked, shift, axis=0)
        sh = [rolled[q * span + SUBLANES:(q + 1) * span] for q in range(n_seq)]
        xc = xc + (sh[0] if n_seq == 1 else jnp.concatenate(sh, axis=0)) * convw_ref[k:k + 1, :]
    for q in range(n_seq):
        conv_ref[q] = bx[(q + 1) * seq_len - SUBLANES:(q + 1) * seq_len]

    xcb = xc.astype(BF16)
    n_blk = d_b // MXU_TILE
    ri = [_dot(xcb[:, k * MXU_TILE:(k + 1) * MXU_TILE], wgate_ref[k]) for k in range(n_blk)]
    z_bg, z_g1, z_g2 = zin("b_g"), zin("g1"), zin("g2")
    z_mq, z_g3, z_c = zin("m_q"), zin("g3"), zin("c_all")
    cg_t = _dot_nt(wcgt_ref[...], xb)
    r = _sigmoid(jnp.concatenate([p[:, 0:MXU_TILE] for p in ri], axis=1) + vec(br_ref))
    i_gate = _sigmoid(jnp.concatenate([p[:, MXU_TILE:2 * MXU_TILE] for p in ri], axis=1) + vec(bi_ref))
    neg_lam = -vec(lam_ref)
    softplus = jnp.maximum(neg_lam, 0.0) + jnp.log1p(jnp.exp(-jnp.abs(neg_lam)))
    log_a = (-LRU_C * r) * softplus
    a = jnp.exp(log_a)
    bval = jnp.sqrt(-_expm1_nonpos(2.0 * log_a)) * (i_gate * xc)
    in_group = lax.broadcasted_iota(jnp.int32, (tm, 1), 0) % SUBLANES
    d = 1
    while d < SUBLANES:
        keep = in_group >= d
        a_sh = pltpu.roll(a, d, axis=0)
        b_sh = pltpu.roll(bval, d, axis=0)
        bval = jnp.where(keep, a * b_sh + bval, bval)
        a = jnp.where(keep, a * a_sh, a)
        d *= 2
    groups_per_seq = seq_len // SUBLANES
    h_groups = []
    for g in range(tm // SUBLANES):
        q = g // groups_per_seq
        if g % groups_per_seq == 0:
            carry = h_ref[q]
        lo = g * SUBLANES
        h_g = a[lo:lo + SUBLANES] * carry + bval[lo:lo + SUBLANES]
        h_groups.append(h_g)
        carry = h_g[SUBLANES - 1:SUBLANES]
        if (g + 1) % groups_per_seq == 0:
            h_ref[q] = carry
    h = jnp.concatenate(h_groups, axis=0)
    ob = h * _silu(z_bg)
    merged_scr[...] += _sigmoid(z_g1) * _dot(ob.astype(BF16), wbr_ref[d_a:d_a + d_b, :])
    g2_ref[0] = _sigmoid(z_g2).astype(BF16)

    mq = z_mq.astype(BF16)
    lane = lax.broadcasted_iota(jnp.int32, (1, LANES), 1)
    om_rows = []
    for q in range(n_seq):
        mkb = mk_ref[q].astype(BF16)
        mvb = mv_ref[q].astype(BF16)
        slabs = []
        for p in range(d_m // LANES):
            mq_p = mq[q * seq_len:(q + 1) * seq_len, p * LANES:(p + 1) * LANES]
            mk_p = mkb[:, p * LANES:(p + 1) * LANES]
            mv_p = mvb[:, p * LANES:(p + 1) * LANES]
            acc = None
            for half in range(LANES // DH_M):
                sel = (lane >= half * DH_M) & (lane < (half + 1) * DH_M)
                sc = _dot_nt(mq_p, jnp.where(sel, mk_p, jnp.zeros_like(mk_p))) * (DH_M ** -0.5)
                e = jnp.exp(sc - jnp.max(sc, -1, keepdims=True))
                prob = (e / jnp.sum(e, -1, keepdims=True)).astype(BF16)
                o = _dot(prob, jnp.where(sel, mv_p, jnp.zeros_like(mv_p)))
                acc = o if acc is None else acc + o
            slabs.append(acc)
        om_rows.append(jnp.concatenate(slabs, axis=1))
    om = om_rows[0] if n_seq == 1 else jnp.concatenate(om_rows, axis=0)
    m_lo = d_a + d_b + d_c
    partial_ref[0] = (merged_scr[...]
                      + _sigmoid(z_g3) * _dot(om.astype(BF16), wbr_ref[m_lo:m_lo + d_m, :])).astype(BF16)
    cgst_ref[0] = _silu(cg_t)

    ckv = _rms_norm(z_c[:, q_lora:q_lora + kv_lora], vec(kvnorm_ref))
    half = D_ROPE // 2
    z_kr = z_c[:, q_lora + kv_lora:q_lora + kv_lora + LANES]
    kr_swapped = jnp.where(lane < half, pltpu.roll(z_kr, LANES - half, axis=1), pltpu.roll(z_kr, half, axis=1))
    kr = z_kr[:, 0:D_ROPE] * ck_ref[...] + kr_swapped[:, 0:D_ROPE] * sk_ref[...]
    ckv_ref[0] = ckv
    kr_ref[0] = kr
    if first_chunk:
        kcat_ref[0], vt_all = _keys_values(ckv, kr, wkc_ref, wkr_ref, wvt_ref)
        for j in range(tm // ATTN_TK):
            vt_ref[0, j] = vt_all[:, j * ATTN_TK:(j + 1) * ATTN_TK]
    cqn =_rms_norm(z_c[:, 0:q_lora], vec(qnorm_ref)).astype(BF16)
    q_t = _dot_nt(wuqt_ref[...], cqn)
    cos_t, sin_t = cosq_ref[...], sinq_ref[...]
    for hd in range(H_C):
        lo = hd * LANES
        x1 = q_t[lo + D_NOPE:lo + D_NOPE + half]
        x2 = q_t[lo + D_NOPE + half:lo + D_NOPE + D_ROPE]
        qt_ref[0, 0, lo:lo + D_NOPE, :] = q_t[lo:lo + D_NOPE].astype(BF16)
        qt_ref[0, 0, lo + D_NOPE:lo + D_NOPE + half, :] = (x1 * cos_t - x2 * sin_t).astype(BF16)
        qt_ref[0, 0, lo + D_NOPE + half:lo + D_NOPE + D_ROPE, :] = (x1 * sin_t + x2 * cos_t).astype(BF16)
        qt_ref[0, 0, lo + D_NOPE + D_ROPE:lo + LANES, :] = jnp.zeros((LANES - D_NOPE - D_ROPE, tm), BF16)


def _branch_call(x, pw, l, sp, conv0_pad, h0, mk, mv, rope, *, tm, seq_len, tq, first_chunk):
    bsz, seq, d_model = x.shape
    d_a, d_b, d_c, d_m = pw["dims"]
    q_lora, kv_lora = pw["q_lora"], pw["kv_lora"]
    n_mem = mk.shape[1]
    cos_t, sin_t, ck_t, sk_t = rope
    grid = (bsz, seq // tm)

    def row_spec(width):
        return pl.BlockSpec((1, tm, width), lambda b, t: (b, t, 0))

    def col_spec(height):
        return pl.BlockSpec((1, height, tm), lambda b, t: (b, 0, t))

    def tab_spec(width):
        return pl.BlockSpec((tm, width), lambda b, t: (t, 0))

    n_seq = tm // seq_len

    def batch_spec(rows, width):
        return pl.BlockSpec((n_seq, rows, width), lambda b, t: (b, 0, 0))

    names = ["w1", "wcgt", "gln_g", "gln_b", "wsp%d" % sp, "bsp%d" % sp, "conv_w", "conv_b",
             "wgate", "b_r", "b_i", "lam"]
    names2 = ["q_norm", "wuqt", "kv_norm"]
    in_specs = ([row_spec(d_model)] + [_layer_spec(pw[n], l) for n in names]
                + [batch_spec(SUBLANES, d_b), batch_spec(1, d_b)] + [_layer_spec(pw[n], l) for n in names2]
                + [pl.BlockSpec((D_ROPE // 2, tm), lambda b, t: (0, t)), pl.BlockSpec((D_ROPE // 2, tm), lambda b, t: (0, t)),
                   tab_spec(D_ROPE), tab_spec(D_ROPE), batch_spec(n_mem, d_m), batch_spec(n_mem, d_m),
                   _layer_spec(pw["wbr"], l)])
    operands = ([x] + [pw[n] for n in names] + [conv0_pad, h0] + [pw[n] for n in names2]
                + [cos_t, sin_t, ck_t, sk_t, mk, mv, pw["wbr"]])
    out_shape = [
        jax.ShapeDtypeStruct((bsz, seq, d_model), BF16),
        jax.ShapeDtypeStruct((bsz, seq, d_model), BF16),
        jax.ShapeDtypeStruct((bsz, d_c, seq), F32),
        jax.ShapeDtypeStruct((bsz, seq // tq, H_C * LANES, tq), BF16),
        jax.ShapeDtypeStruct((bsz, seq, kv_lora), F32),
        jax.ShapeDtypeStruct((bsz, seq, D_ROPE), F32),
        jax.ShapeDtypeStruct((bsz * n_seq, SUBLANES, d_b), F32),
        jax.ShapeDtypeStruct((bsz * n_seq, 1, d_b), F32),
    ]
    out_specs = [
        row_spec(d_model), row_spec(d_model), col_spec(d_c),
        pl.BlockSpec((1, 1, H_C * LANES, tm), lambda b, t: (b, t // (tq // tm), 0, t % (tq // tm))),
        row_spec(kv_lora), row_spec(D_ROPE), batch_spec(SUBLANES, d_b), batch_spec(1, d_b),
    ]
    if first_chunk:
        for name in ("wkc", "wkr", "wvt"):
            in_specs.append(_layer_spec(pw[name], l))
            operands.append(pw[name])
        out_shape += [jax.ShapeDtypeStruct((bsz, seq, H_C * LANES), BF16),
                      jax.ShapeDtypeStruct((bsz, seq // ATTN_TK, H_C * V_ROWS, ATTN_TK), BF16)]
        out_specs += [row_spec(H_C * LANES),
                      pl.BlockSpec((1, tm // ATTN_TK, H_C * V_ROWS, ATTN_TK), lambda b, t: (b, t, 0, 0))]
    else:
        out_shape.append(jax.ShapeDtypeStruct((bsz, seq, d_a), F32))
        out_specs.append(row_spec(d_a))
    kern = functools.partial(
        _branch_kernel, layer=l, first_chunk=first_chunk, tm=tm, seq_len=seq_len, sp_len=sp, d_a=d_a, d_b=d_b, d_c=d_c, d_m=d_m,
        q_lora=q_lora, kv_lora=kv_lora, cols=pw["cols"])
    return pl.pallas_call(
        kern, grid=grid, in_specs=in_specs, out_specs=out_specs, out_shape=out_shape,
        scratch_shapes=[pltpu.VMEM((tm, d_model), F32)],
        name="branch", compiler_params=_params(2),
    )(*operands)


HEADS_PER_STEP = 4
_EXP2_SCALE = (D_NOPE + D_ROPE) ** -0.5 * math.log2(math.e)


def _scores(k_tile, q_tile, out_refs):
    res = [_dot(k_tile[:, h * LANES:(h + 1) * LANES], q_tile[h * LANES:(h + 1) * LANES, :])
           for h in range(k_tile.shape[1] // LANES)]
    for h, r in enumerate(res):
        out_refs[h] = r


def _softmax_tile(s_t, m_old):
    m_new = jnp.maximum(m_old, jnp.max(s_t, axis=0, keepdims=True))
    return jnp.exp2(s_t - m_new).astype(BF16), m_new, jnp.exp2(m_old - m_new)


def _attn_output(acc, cgs_t):
    return (acc[0:D_V] / acc[D_V:D_V + 1] * cgs_t).astype(BF16)


N_BUF = 4


def _attn_causal_kernel(qt_ref, k_ref, vt_ref, cgst_ref, bias_ref, o_ref, *scratch, tq, tk):
    s_buf, e_buf, acc_scr = scratch[0:N_BUF], scratch[N_BUF:2 * N_BUF], scratch[2 * N_BUF]
    i = pl.program_id(2)
    nh = HEADS_PER_STEP
    q_cur = qt_ref.at[0, i]
    q_nxt = qt_ref.at[0, jnp.minimum(i + 1, pl.num_programs(2) - 1)]

    def k_tile(j):
        return k_ref[0, pl.ds(pl.multiple_of(j * tk, tk), tk), :]

    def pv(j, e_in, c0):
        vt = vt_ref[0, j]
        return [_dot(vt[h * V_ROWS:(h + 1) * V_ROWS, :], e_in[h, :, c0:tq]) for h in range(nh)]

    def acc_update(pvs, alphas, c0):
        for h in range(nh):
            acc_scr[h, :, c0:tq] = acc_scr[h, :, c0:tq] * alphas[h][:, c0:tq] + pvs[h]

    def softmax(slot, carry, c0, add_bias):
        new = []
        for h in range(nh):
            m_old, a_prev, _ = carry[h]
            s_t = s_buf[slot][h, :, c0:tq]
            if add_bias:
                lead = s_t[:, 0:tk] + bias_ref[...]
                s_t = lead if c0 + tk == tq else jnp.concatenate([lead, s_t[:, tk:]], axis=1)
            e, m_new, alpha = _softmax_tile(s_t, m_old[:, c0:tq])
            e_buf[slot][h, :, c0:tq] = e
            if c0:
                m_new = jnp.concatenate([m_old[:, 0:c0], m_new], axis=1)
                alpha = jnp.concatenate([jnp.ones((1, c0), F32), alpha], axis=1)
            new.append((m_new, alpha, a_prev))
        return tuple(new)

    def stage(n, slot, carry):
        pvs = pv(jnp.maximum(n - 2, 0), e_buf[(slot + 2) % N_BUF], 0)
        _scores(k_tile(n + 3), q_cur, s_buf[(slot + 3) % N_BUF])
        new = softmax(slot, carry, 0, False)
        acc_update(pvs, [c[2] for c in carry], 0)
        return new

    for slot in (N_BUF - 2, N_BUF - 1):
        e_buf[slot][...] = jnp.zeros(e_buf[slot].shape, BF16)
    acc_scr[...] = jnp.zeros(acc_scr.shape, F32)

    @pl.when(i == 0)
    def _():
        for slot in range(N_BUF - 1):
            _scores(k_tile(slot), q_cur, s_buf[slot])

    ones = jnp.ones((1, tq), F32)
    carry = tuple((jnp.full((1, tq), -jnp.inf, F32), ones, ones) for _ in range(nh))

    def body(t, carry):
        for slot in range(N_BUF):
            carry = stage(N_BUF * t + slot, slot, carry)
        return carry

    carry = lax.fori_loop(0, i, body, carry)

    j0 = N_BUF * i
    last = N_BUF - 1
    for d in range(N_BUF):
        c_prev = max(d - 2, 0) * tk
        pvs = pv(jnp.maximum(j0 + d - 2, 0), e_buf[(d + 2) % N_BUF], c_prev)
        if d == 0:
            k_last = k_tile(j0 + last)
            for h in range(nh):
                s_buf[last][h, :, tq - tk:tq] = _dot(
                    k_last[:, h * LANES:(h + 1) * LANES],
                    q_cur[h * LANES:(h + 1) * LANES, tq - tk:tq]) + bias_ref[...]
        else:
            _scores(k_tile(d - 1), q_nxt, s_buf[d - 1])
        alphas = [c[2] for c in carry]
        carry = softmax(d, carry, d * tk, d < last)
        acc_update(pvs, alphas, c_prev)
    acc_update(pv(j0 + last - 1, e_buf[last - 1], (last - 1) * tk), [c[2] for c in carry], (last - 1) * tk)
    acc_update(pv(j0 + last, e_buf[last], last * tk), [c[1] for c in carry], last * tk)
    for h in range(nh):
        o_ref[0, h * D_V:(h + 1) * D_V, :] = _attn_output(acc_scr[h], cgst_ref[0, h * D_V:(h + 1) * D_V, :])


def _attn_latent_kernel(qt_ref, pckv_ref, pkr_ref, ckv_ref, kr_ref, cgst_ref, wuk_ref, wuvt_ref, place_ref, place_seq_ref,
                        o_ref, *, n_seq):
    tq = place_ref.shape[1]
    out = None
    for q in range(n_seq):
        cols = slice(q * tq, (q + 1) * tq)
        c = jnp.concatenate([pckv_ref[q], ckv_ref[0, cols, :]], axis=0).astype(BF16)
        kr = jnp.concatenate([pkr_ref[q], kr_ref[0, cols, :]], axis=0).astype(BF16)
        q_lat, q_rope = None, None
        for h in range(H_C):
            lo = h * LANES
            q_h = _dot(wuk_ref[h], qt_ref[0, 0, lo:lo + D_NOPE, cols]).astype(BF16)
            a_h = _dot(q_h, place_ref[h])
            r_h = _dot(qt_ref[0, 0, lo + D_NOPE:lo + D_NOPE + D_ROPE, cols], place_ref[h])
            q_lat = a_h if q_lat is None else q_lat + a_h
            q_rope = r_h if q_rope is None else q_rope + r_h
        s = _dot(c, q_lat.astype(BF16)) + _dot(kr, q_rope.astype(BF16))
        e = jnp.exp2(s - jnp.max(s, axis=0, keepdims=True)).astype(BF16)
        denom = jnp.sum(e.astype(F32), axis=0, keepdims=True)
        lat = _dot_tn(c, e).astype(BF16)
        o_heads = []
        for h in range(H_C):
            o_h = _dot(wuvt_ref[h], lat[:, h * tq:(h + 1) * tq])
            o_heads.append(o_h / denom[:, h * tq:(h + 1) * tq] * cgst_ref[0, h * D_V:(h + 1) * D_V, cols])
        o_q = _dot(jnp.concatenate(o_heads, axis=0).astype(BF16), place_seq_ref[q])
        out = o_q if out is None else out + o_q
    o_ref[0] = out.astype(BF16)


def _attn_causal_call(qt, kcat, vt, cgst):
    bsz, n_qt, _, tq = qt.shape
    seq = n_qt * tq
    t_k = kcat.shape[1]
    nkt, _, tk = vt.shape[1:]
    nh = HEADS_PER_STEP
    assert tq == N_BUF * tk and seq == t_k and tk % CHUNK == 0
    chunk = jnp.arange(tk) // CHUNK
    bias = jnp.where(chunk[:, None] <= chunk[None, :], 0.0, -1e30).astype(F32)
    return pl.pallas_call(
        functools.partial(_attn_causal_kernel, tq=tq, tk=tk), grid=(bsz, H_C // nh, n_qt),
        in_specs=[pl.BlockSpec((1, n_qt, nh * LANES, tq), lambda b, p, i: (b, 0, p, 0)),
                  pl.BlockSpec((1, t_k, nh * LANES), lambda b, p, i: (b, 0, p)),
                  pl.BlockSpec((1, nkt, nh * V_ROWS, tk), lambda b, p, i: (b, 0, p, 0)),
                  pl.BlockSpec((1, nh * D_V, tq), lambda b, p, i: (b, p, i)),
                  _const_spec((tk, tk))],
        out_specs=pl.BlockSpec((1, nh * D_V, tq), lambda b, p, i: (b, p, i)),
        out_shape=jax.ShapeDtypeStruct((bsz, H_C * D_V, seq), BF16),
        scratch_shapes=([pltpu.VMEM((nh, tk, tq), F32)] * N_BUF + [pltpu.VMEM((nh, tk, tq), BF16)] * N_BUF
                        + [pltpu.VMEM((nh, V_ROWS, tq), F32)]),
        name="attn", compiler_params=_params(3),
    )(qt, kcat, vt, cgst, bias)


def _attn_cached_call(qt, past_ckv, past_kr, ckv_new, kr_new, cgst, pw, l, *, n_seq):
    n_cols = qt.shape[-1]
    tq = n_cols // n_seq
    past_len, kv_lora = past_ckv.shape[2:]

    def placement(n):
        p = jnp.eye(n, dtype=BF16)[:, None, :, None] * jnp.eye(tq, dtype=BF16)[None, :, None, :]
        return p.reshape(n, tq, n * tq)

    place, place_seq = placement(H_C), placement(n_seq)
    return pl.pallas_call(
        functools.partial(_attn_latent_kernel, n_seq=n_seq), grid=(1,),
        in_specs=[_const_spec(qt.shape),
                  pl.BlockSpec((None, n_seq, past_len, kv_lora), lambda i: (l, 0, 0, 0)),
                  pl.BlockSpec((None, n_seq, past_len, D_ROPE), lambda i: (l, 0, 0, 0)),
                  _const_spec(ckv_new.shape), _const_spec(kr_new.shape), _const_spec(cgst.shape),
                  _layer_spec(pw["wuk"], l), _layer_spec(pw["wuvt"], l),
                  _const_spec(place.shape), _const_spec(place_seq.shape)],
        out_specs=pl.BlockSpec((1, H_C * D_V, n_cols), lambda i: (0, 0, 0)),
        out_shape=jax.ShapeDtypeStruct((1, H_C * D_V, n_cols), BF16),
        name="attn_cached", compiler_params=_params(1),
    )(qt, past_ckv, past_kr, ckv_new, kr_new, cgst, pw["wuk"], pw["wuvt"], place, place_seq)


def _merge_kernel(x_ref, partial_ref, g2_ref, oct_ref, wbr_ref, wout_ref, lng_ref, lnb_ref, o_ref,
                  *, layer, c_lo, d_c, alpha):
    yc = _dot_tn(oct_ref[0], wbr_ref[c_lo:c_lo + d_c, :])
    merged = partial_ref[0].astype(F32) + g2_ref[0].astype(F32) * yc
    y = _dot(merged.astype(BF16), wout_ref[...])
    o_ref[0] = _layer_norm(alpha * x_ref[0] + y, lng_ref[layer:layer + 1, :], lnb_ref[layer:layer + 1, :])


def _merge_call(x, partial, g2, oct, pw, l, *, tr, alpha):
    bsz, seq, d_model = x.shape
    d_a, d_b, d_c, _ = pw["dims"]
    kern = functools.partial(_merge_kernel, layer=l, c_lo=d_a + d_b, d_c=d_c, alpha=alpha)

    def row_spec(width):
        return pl.BlockSpec((1, tr, width), lambda b, i: (b, i, 0))

    return pl.pallas_call(
        kern, grid=(bsz, seq // tr),
        in_specs=[row_spec(d_model), row_spec(d_model), row_spec(d_model),
                  pl.BlockSpec((1, d_c, tr), lambda b, i: (b, 0, i)),
                  _layer_spec(pw["wbr"], l), _layer_spec(pw["wout"], l),
                  _layer_spec(pw["ln_g"], l), _layer_spec(pw["ln_b"], l)],
        out_specs=row_spec(d_model),
        out_shape=jax.ShapeDtypeStruct((bsz, seq, d_model), F32),
        name="merge", compiler_params=_params(2),
    )(x, partial, g2, oct, pw["wbr"], pw["wout"], pw["ln_g"], pw["ln_b"])


def _mem_kernel(m_ref, w_ref, k_ref, v_ref, *, d_m):
    kv = _dot(m_ref[...].astype(BF16), w_ref[...])
    k_ref[...] = kv[:, 0:d_m]
    v_ref[...] = kv[:, d_m:2 * d_m]


def _mem_call(mem2d, wmem, l, *, tr):
    rows, d_model = mem2d.shape
    d_m = wmem.shape[2] // 2
    return pl.pallas_call(
        functools.partial(_mem_kernel, d_m=d_m), grid=(rows // tr,),
        in_specs=[pl.BlockSpec((tr, d_model), lambda i: (i, 0)), _layer_spec(wmem, l)],
        out_specs=[pl.BlockSpec((tr, d_m), lambda i: (i, 0)), pl.BlockSpec((tr, d_m), lambda i: (i, 0))],
        out_shape=[jax.ShapeDtypeStruct((rows, d_m), F32), jax.ShapeDtypeStruct((rows, d_m), F32)],
        name="mem", compiler_params=_params(1),
    )(mem2d, wmem)


def _block_diag(w):
    h, n, _ = w.shape
    eye = jnp.eye(h, dtype=w.dtype)
    return (eye[:, None, :, None] * w[:, :, None, :]).reshape(h * n, h * n)


def _gate_blocks(w_r, w_i):
    per = MXU_TILE // w_r.shape[1]
    blocks = []
    for k in range(w_r.shape[0] // per):
        blocks.append(jnp.concatenate([_block_diag(w_r[k * per:(k + 1) * per]),
                                       _block_diag(w_i[k * per:(k + 1) * per])], axis=1))
    return jnp.stack(blocks)


def _w_in_kernel(wt_ref, o_ref, *, n_head, o_tail):
    o_ref[0, :, 0:n_head] = jnp.transpose(wt_ref[0, 0:n_head, :]).astype(BF16)
    o_ref[0, :, n_head:] = jnp.transpose(wt_ref[0, o_tail:, :]).astype(BF16)


def _w_in_call(w_in, n_head, o_tail, *, tk):
    depth, d_model, d_in = w_in.shape
    n_out = n_head + d_in - o_tail
    return pl.pallas_call(
        functools.partial(_w_in_kernel, n_head=n_head, o_tail=o_tail), grid=(depth, d_model // tk),
        in_specs=[pl.BlockSpec((1, d_in, tk), lambda l, i: (l, 0, i))],
        out_specs=pl.BlockSpec((1, tk, n_out), lambda l, i: (l, i, 0)),
        out_shape=jax.ShapeDtypeStruct((depth, d_model, n_out), BF16),
        name="w_in", compiler_params=_params(2),
    )(jnp.swapaxes(w_in, 1, 2))


def _prep_weights(sp_lens, w_in, gmlp_ln_g, gmlp_ln_b, gmlp_ws, gmlp_bs, lru_conv_w,
                  lru_conv_b, lru_w_r, lru_b_r, lru_w_i, lru_b_i, lru_lambda, mla_q_norm, mla_w_uq,
                  mla_kv_norm, mla_w_ukv, mem_w_k, mem_w_v, w_br, w_out, ln_g, ln_b):
    depth, d_model, _ = w_in.shape
    d_a = gmlp_ln_g.shape[1]
    d_b = lru_lambda.shape[1]
    q_lora = mla_q_norm.shape[1]
    kv_lora = mla_kv_norm.shape[1]
    d_c = H_C * D_V
    d_m = H_M * DH_M
    o_cq = 3 * d_a + 2 * d_b
    o_kr = o_cq + q_lora + kv_lora
    o_cg = o_kr + D_ROPE
    o_mq = o_cg + d_c
    o_g = o_mq + d_m
    n_head = o_kr + LANES
    assert o_kr % LANES == 0 and (o_g - o_mq) % LANES == 0 and D_ROPE <= LANES and n_head <= o_mq
    cols = {"a_u": (0, d_a), "a_v": (d_a, 2 * d_a), "a_g": (2 * d_a, 3 * d_a),
            "b_x": (3 * d_a, 3 * d_a + d_b), "b_g": (3 * d_a + d_b, o_cq), "c_all": (o_cq, n_head),
            "m_q": (n_head, n_head + d_m)}
    for k in range(N_BRANCH):
        cols[f"g{k}"] = (n_head + d_m + k * d_model, n_head + d_m + (k + 1) * d_model)
    w1 = _w_in_call(w_in, n_head, o_mq, tk=LANES)
    wcgt = jnp.swapaxes(w_in[:, :, o_cg:o_mq], 1, 2).astype(BF16)

    wq = mla_w_uq.reshape(depth, q_lora, H_C, D_NOPE + D_ROPE) * _EXP2_SCALE
    wq = jnp.concatenate([wq, jnp.zeros((depth, q_lora, H_C, LANES - D_NOPE - D_ROPE), wq.dtype)], axis=3)
    wuqt = jnp.swapaxes(wq.reshape(depth, q_lora, H_C * LANES), 1, 2).astype(BF16)

    wkv = mla_w_ukv.reshape(depth, kv_lora, H_C, D_NOPE + D_V)
    wk = wkv[..., :D_NOPE]
    wv = wkv[..., D_NOPE:]
    wkc = jnp.concatenate([wk, jnp.zeros((depth, kv_lora, H_C, LANES - D_NOPE), wk.dtype)], axis=3)
    wkc = wkc.reshape(depth, kv_lora, H_C * LANES).astype(BF16)
    lane = jnp.arange(H_C * LANES) % LANES
    wkr = ((lane[None, :] - D_NOPE) == jnp.arange(D_ROPE)[:, None]).astype(BF16)
    wkr = jnp.broadcast_to(wkr, (depth,) + wkr.shape)
    wv = jnp.concatenate([wv, jnp.zeros((depth, kv_lora, H_C, V_ROWS - D_V), wv.dtype)], axis=3)
    wvt = jnp.swapaxes(wv.reshape(depth, kv_lora, H_C * V_ROWS), 1, 2).astype(BF16)

    pw = dict(
        dims=(d_a, d_b, d_c, d_m), q_lora=q_lora, kv_lora=kv_lora, cols=cols,
        w1=w1, wcgt=wcgt,
        gln_g=gmlp_ln_g, gln_b=gmlp_ln_b,
        conv_w=lru_conv_w, conv_b=lru_conv_b,
        wgate=jnp.stack([_gate_blocks(lru_w_r[l], lru_w_i[l]) for l in range(depth)]).astype(BF16),
        b_r=lru_b_r, b_i=lru_b_i, lam=lru_lambda,
        q_norm=mla_q_norm, wuqt=wuqt, kv_norm=mla_kv_norm,
        wkc=wkc, wkr=wkr, wvt=wvt,
        wuk=jnp.transpose(wk, (0, 2, 1, 3)).astype(BF16),
        wuvt=jnp.transpose(wkv[..., D_NOPE:], (0, 2, 3, 1)).astype(BF16),
        wbr=w_br.astype(BF16), wout=w_out.astype(BF16), ln_g=ln_g, ln_b=ln_b,
        wmem=jnp.concatenate([mem_w_k, mem_w_v], axis=2).astype(BF16),
    )
    for sp in sp_lens:
        pw["wsp%d" % sp] = jnp.tril(gmlp_ws[:, :, :sp, :sp]).astype(BF16)
        pw["bsp%d" % sp] = jnp.repeat(jnp.swapaxes(gmlp_bs[:, :, :sp], 1, 2), d_a // G_A, axis=2)
    return pw


def _rope_tables(pos):
    half = D_ROPE // 2
    freq = ROPE_BASE ** (-jnp.arange(half, dtype=F32) / half)
    ang = pos.astype(F32)[:, None] * freq[None, :]
    cos, sin = jnp.cos(ang), jnp.sin(ang)
    return (jnp.transpose(cos), jnp.transpose(sin),
            jnp.concatenate([cos, cos], axis=1), jnp.concatenate([-sin, sin], axis=1))


def _pad_conv_state(conv):
    return jnp.pad(conv, ((0, 0), (SUBLANES - (CONV_W - 1), 0), (0, 0)))


def _trunk_layer(x, pw, l, sp, rope, conv0, h0, mk, mv, past_ckv, past_kr, *, tm, alpha):
    bsz, seq, d_model = x.shape
    first_chunk = past_ckv is None
    if first_chunk:
        x_rows, seq_len, tq = x, tm, N_BUF * ATTN_TK
    else:
        x_rows, seq_len, tq = x.reshape(1, bsz * seq, d_model), seq, bsz * seq
        cos_t, sin_t, ck_t, sk_t = rope
        rope = (jnp.tile(cos_t, (1, bsz)), jnp.tile(sin_t, (1, bsz)), jnp.tile(ck_t, (bsz, 1)), jnp.tile(sk_t, (bsz, 1)))
    outs = _branch_call(x_rows, pw, l, sp, _pad_conv_state(conv0), h0[:, None, :], mk, mv, rope,
                        tm=tm, seq_len=seq_len, tq=tq, first_chunk=first_chunk)
    partial, g2, cgst, qt, ckv_new, kr_new, conv_pad, h_new = outs[:8]
    if first_chunk:
        kcat, vt = outs[8:]
        v_rows = None
        oct = _attn_causal_call(qt, kcat, vt, cgst)
    else:
        oct = _attn_cached_call(qt, past_ckv, past_kr, ckv_new, kr_new, cgst, pw, l, n_seq=bsz)
        ckv_new, kr_new, v_rows = (o.reshape(bsz, seq, -1) for o in (ckv_new, kr_new, outs[8]))
    x_new = _merge_call(x_rows, partial, g2, oct, pw, l, tr=MERGE_TILE if first_chunk else tm, alpha=alpha)
    return (x_new.reshape(bsz, seq, d_model), v_rows, conv_pad[:, SUBLANES - (CONV_W - 1):], h_new[:, 0],
            ckv_new, kr_new)


def kernel(x_prompt, x_sample, mem_prompt, cache_mla_ckv, cache_mla_krope, cache_mem_k, cache_mem_v,
           state_lru_h, state_lru_conv, w_in, gmlp_ln_g, gmlp_ln_b, gmlp_ws, gmlp_bs,
           lru_conv_w, lru_conv_b, lru_w_r, lru_b_r, lru_w_i, lru_b_i, lru_lambda,
           mla_q_norm, mla_w_uq, mla_kv_norm, mla_w_ukv, mem_w_k, mem_w_v, w_br, w_out, ln_g, ln_b):
    bp, tp, d_model = x_prompt.shape
    bs, ts, _ = x_sample.shape
    depth = w_in.shape[0]
    past_len = cache_mla_ckv.shape[2]
    n_mem = mem_prompt.shape[1]
    d_b = lru_lambda.shape[1]
    alpha = (2.0 * depth) ** 0.25
    sp_p, sp_s = min(tp, A_CHUNK), min(ts, A_CHUNK)
    pw = _prep_weights(sorted({sp_p, sp_s}), w_in, gmlp_ln_g, gmlp_ln_b, gmlp_ws, gmlp_bs, lru_conv_w,
                       lru_conv_b, lru_w_r, lru_b_r, lru_w_i, lru_b_i, lru_lambda, mla_q_norm, mla_w_uq,
                       mla_kv_norm, mla_w_ukv, mem_w_k, mem_w_v, w_br, w_out, ln_g, ln_b)
    rope_p = _rope_tables(jnp.arange(tp))
    rope_s = _rope_tables(past_len + jnp.arange(ts))
    tm_p = min(tp, ROW_TILE)
    assert tp % (N_BUF * ATTN_TK) == 0 and tm_p % ATTN_TK == 0 and tp % MERGE_TILE == 0 and ts % SUBLANES == 0
    zero_conv = jnp.zeros((bp, CONV_W - 1, d_b), F32)
    zero_h = jnp.zeros((bp, d_b), F32)

    xp, xs = x_prompt, x_sample
    acc = [[] for _ in range(11)]
    for l in range(depth):
        mk, mv = _mem_call(mem_prompt.reshape(bp * n_mem, d_model), pw["wmem"], l, tr=n_mem)
        mk = mk.reshape(bp, n_mem, -1)
        mv = mv.reshape(bp, n_mem, -1)
        xp, _, conv_n, h_n, ckv_n, kr_n = _trunk_layer(
            xp, pw, l, sp_p, rope_p, zero_conv, zero_h, mk, mv, None, None, tm=tm_p, alpha=alpha)
        for k, val in zip(range(6), (ckv_n, kr_n, mk.reshape(bp, n_mem, H_M, DH_M),
                                     mv.reshape(bp, n_mem, H_M, DH_M), h_n, conv_n)):
            acc[k].append(val)
        xs, v_n, conv_n, h_n, ckv_n, kr_n = _trunk_layer(
            xs, pw, l, sp_s, rope_s, state_lru_conv[l], state_lru_h[l],
            cache_mem_k[l].reshape(bs, n_mem, -1), cache_mem_v[l].reshape(bs, n_mem, -1),
            cache_mla_ckv, cache_mla_krope, tm=bs * ts, alpha=alpha)
        for k, val in zip(range(6, 11), (ckv_n, kr_n, h_n, conv_n, v_n)):
            acc[k].append(val)
    return (xp, xs) + tuple(jnp.stack(a) for a in acc)
```

```python
import functools
import math

import jax
import jax.numpy as jnp
from jax import lax
from jax.experimental import pallas as pl
from jax.experimental.pallas import tpu as pltpu

CHUNK = 64
G_A = 4
A_CHUNK = 128
H_B = 8
CONV_W = 4
LRU_C = 8.0
H_C = 8
D_NOPE = 64
D_ROPE = 32
D_V = 64
ROPE_BASE = 10000.0
H_M = 4
DH_M = 64
N_BRANCH = 4
EPS = 1e-6

LANES = 128
SUBLANES = 8
VMEM_LIMIT = 56 * 1024 * 1024
ROW_TILE = 512
ATTN_TK = 256
MXU_TILE = 256
MERGE_TILE = 1024

F32 = jnp.float32
BF16 = jnp.bfloat16


def _dot(a, b):
    return jnp.dot(a, b, preferred_element_type=F32)


def _dot_nt(a, b):
    return lax.dot_general(a, b, (((1,), (1,)), ((), ())), preferred_element_type=F32)


def _dot_tn(a, b):
    return lax.dot_general(a, b, (((0,), (0,)), ((), ())), preferred_element_type=F32)


def _sigmoid(x):
    return 1.0 / (1.0 + jnp.exp(-x))


def _silu(x):
    return x * _sigmoid(x)


def _gelu(x):
    return jax.nn.gelu(x)


def _expm1_nonpos(x):
    u = jnp.exp(x)
    near = (u - 1.0) * x / jnp.log(jnp.where(u == 1.0, 2.0, jnp.maximum(u, 0.5)))
    return jnp.where(u == 1.0, x, jnp.where(u > 0.5, near, u - 1.0))


def _layer_norm(x, g, b):
    mu = jnp.mean(x, -1, keepdims=True)
    var = jnp.mean(jnp.square(x - mu), -1, keepdims=True)
    return (x - mu) * lax.rsqrt(var + EPS) * g + b


def _rms_norm(x, g):
    return x * lax.rsqrt(jnp.mean(jnp.square(x), -1, keepdims=True) + EPS) * g


def _const_spec(shape):
    nd = len(shape)
    return pl.BlockSpec(shape, lambda *_: (0,) * nd, pipeline_mode=pl.Buffered(1))


def _layer_spec(arr, l):
    if arr.ndim == 2:
        return _const_spec(arr.shape)
    nd = arr.ndim - 1
    return pl.BlockSpec((None,) + arr.shape[1:], lambda *_: (l,) + (0,) * nd, pipeline_mode=pl.Buffered(1))


def _params(n_axes):
    return pltpu.CompilerParams(dimension_semantics=("arbitrary",) * n_axes,
                                vmem_limit_bytes=VMEM_LIMIT)


V_ROWS = 80


def _keys_values(ckv, kr, wkc_ref, wkr_ref, wvt_ref):
    c = ckv.astype(BF16)
    kcat = (_dot(c, wkc_ref[...]) + _dot(kr.astype(BF16), wkr_ref[...])).astype(BF16)
    v_t = _dot_nt(wvt_ref[...], c)
    row = lax.broadcasted_iota(jnp.int32, (v_t.shape[0], 1), 0)
    return kcat, jnp.where(row % V_ROWS == D_V, 1.0, v_t).astype(BF16)


def _branch_kernel(x_ref, w1_ref, wcgt_ref, glng_ref, glnb_ref, wsp_ref, bsp_ref,
                   convw_ref, convb_ref, wgate_ref, br_ref, bi_ref, lam_ref, conv0_ref, h0_ref, qnorm_ref, wuqt_ref,
                   kvnorm_ref, cosq_ref, sinq_ref, ck_ref, sk_ref, mk_ref, mv_ref, wbr_ref,
                   *rest, layer, first_chunk, tm, seq_len, sp_len, d_a, d_b, d_c, d_m, q_lora, kv_lora, cols):
    if first_chunk:
        (wkc_ref, wkr_ref, wvt_ref, partial_ref, g2_ref, cgst_ref, qt_ref, ckv_ref, kr_ref, conv_ref, h_ref,
         kcat_ref, vt_ref, merged_scr) = rest
    else:
        partial_ref, g2_ref, cgst_ref, qt_ref, ckv_ref, kr_ref, conv_ref, h_ref, v_ref, merged_scr = rest
    t = pl.program_id(1)
    xb = x_ref[0].astype(BF16)

    def vec(ref):
        return ref[layer:layer + 1, :]

    def zin(name):
        lo, hi = cols[name]
        return _dot(xb, w1_ref[:, lo:hi])

    z_u, z_v, z_ag, z_g0 = zin("a_u"), zin("a_v"), zin("a_g"), zin("g0")
    u = _gelu(z_u)
    v = _layer_norm(_gelu(z_v), vec(glng_ref), vec(glnb_ref))
    if not first_chunk:
        v_ref[0] = v
    vb = v.astype(BF16)
    n_groups = d_a // LANES
    row_blocks = []
    for c in range(tm // sp_len):
        col_blocks = [_dot(wsp_ref[g], vb[c * sp_len:(c + 1) * sp_len, g * LANES:(g + 1) * LANES])
                      for g in range(n_groups)]
        row_blocks.append(jnp.concatenate(col_blocks, axis=1) + bsp_ref[...])
    s = row_blocks[0] if len(row_blocks) == 1 else jnp.concatenate(row_blocks, axis=0)
    oa = (u * s) * _silu(z_ag)
    merged_scr[...] = _sigmoid(z_g0) * _dot(oa.astype(BF16), wbr_ref[0:d_a, :])

    @pl.when(t == 0)
    def _():
        conv_ref[...] = conv0_ref[...]
        h_ref[...] = h0_ref[...]

    bx = zin("b_x")
    n_seq = tm // seq_len
    span = SUBLANES + seq_len
    stacked = jnp.concatenate(
        [piece for q in range(n_seq) for piece in (conv_ref[q], bx[q * seq_len:(q + 1) * seq_len])], axis=0)
    xc = vec(convb_ref)
    for k in range(CONV_W):
        shift = CONV_W - 1 - k
        rolled = stacked if shift == 0 else pltpu.roll(stacked, shift, axis=0)
        sh = [rolled[q * span + SUBLANES:(q + 1) * span] for q in range(n_seq)]
        xc = xc + (sh[0] if n_seq == 1 else jnp.concatenate(sh, axis=0)) * convw_ref[k:k + 1, :]
    for q in range(n_seq):
        conv_ref[q] = bx[(q + 1) * seq_len - SUBLANES:(q + 1) * seq_len]

    xcb = xc.astype(BF16)
    n_blk = d_b // MXU_TILE
    ri = [_dot(xcb[:, k * MXU_TILE:(k + 1) * MXU_TILE], wgate_ref[k]) for k in range(n_blk)]
    z_bg, z_g1, z_g2 = zin("b_g"), zin("g1"), zin("g2")
    z_mq, z_g3, z_c = zin("m_q"), zin("g3"), zin("c_all")
    cg_t = _dot_nt(wcgt_ref[...], xb)
    r = _sigmoid(jnp.concatenate([p[:, 0:MXU_TILE] for p in ri], axis=1) + vec(br_ref))
    i_gate = _sigmoid(jnp.concatenate([p[:, MXU_TILE:2 * MXU_TILE] for p in ri], axis=1) + vec(bi_ref))
    neg_lam = -vec(lam_ref)
    softplus = jnp.maximum(neg_lam, 0.0) + jnp.log1p(jnp.exp(-jnp.abs(neg_lam)))
    log_a = (-LRU_C * r) * softplus
    a = jnp.exp(log_a)
    bval = jnp.sqrt(-_expm1_nonpos(2.0 * log_a)) * (i_gate * xc)
    in_group = lax.broadcasted_iota(jnp.int32, (tm, 1), 0) % SUBLANES
    d = 1
    while d < SUBLANES:
        keep = in_group >= d
        a_sh = pltpu.roll(a, d, axis=0)
        b_sh = pltpu.roll(bval, d, axis=0)
        bval = jnp.where(keep, a * b_sh + bval, bval)
        a = jnp.where(keep, a * a_sh, a)
        d *= 2
    groups_per_seq = seq_len // SUBLANES
    h_groups = []
    for g in range(tm // SUBLANES):
        q = g // groups_per_seq
        if g % groups_per_seq == 0:
            carry = h_ref[q]
        lo = g * SUBLANES
        h_g = a[lo:lo + SUBLANES] * carry + bval[lo:lo + SUBLANES]
        h_groups.append(h_g)
        carry = h_g[SUBLANES - 1:SUBLANES]
        if (g + 1) % groups_per_seq == 0:
            h_ref[q] = carry
    h = jnp.concatenate(h_groups, axis=0)
    ob = h * _silu(z_bg)
    merged_scr[...] += _sigmoid(z_g1) * _dot(ob.astype(BF16), wbr_ref[d_a:d_a + d_b, :])
    g2_ref[0] = _sigmoid(z_g2).astype(BF16)

    mq = z_mq.astype(BF16)
    lane = lax.broadcasted_iota(jnp.int32, (1, LANES), 1)
    om_rows = []
    for q in range(n_seq):
        mkb = mk_ref[q].astype(BF16)
        mvb = mv_ref[q].astype(BF16)
        slabs = []
        for p in range(d_m // LANES):
            mq_p = mq[q * seq_len:(q + 1) * seq_len, p * LANES:(p + 1) * LANES]
            mk_p = mkb[:, p * LANES:(p + 1) * LANES]
            mv_p = mvb[:, p * LANES:(p + 1) * LANES]
            acc = None
            for half in range(LANES // DH_M):
                sel = (lane >= half * DH_M) & (lane < (half + 1) * DH_M)
                sc = _dot_nt(mq_p, jnp.where(sel, mk_p, jnp.zeros_like(mk_p))) * (DH_M ** -0.5)
                e = jnp.exp(sc - jnp.max(sc, -1, keepdims=True))
                prob = (e / jnp.sum(e, -1, keepdims=True)).astype(BF16)
                o = _dot(prob, jnp.where(sel, mv_p, jnp.zeros_like(mv_p)))
                acc = o if acc is None else acc + o
            slabs.append(acc)
        om_rows.append(jnp.concatenate(slabs, axis=1))
    om = om_rows[0] if n_seq == 1 else jnp.concatenate(om_rows, axis=0)
    m_lo = d_a + d_b + d_c
    partial_ref[0] = (merged_scr[...]
                      + _sigmoid(z_g3) * _dot(om.astype(BF16), wbr_ref[m_lo:m_lo + d_m, :])).astype(BF16)
    cgst_ref[0] = _silu(cg_t)

    ckv = _rms_norm(z_c[:, q_lora:q_lora + kv_lora], vec(kvnorm_ref))
    half = D_ROPE // 2
    z_kr = z_c[:, q_lora + kv_lora:q_lora + kv_lora + LANES]
    kr_swapped = jnp.where(lane < half, pltpu.roll(z_kr, LANES - half, axis=1), pltpu.roll(z_kr, half, axis=1))
    kr = z_kr[:, 0:D_ROPE] * ck_ref[...] + kr_swapped[:, 0:D_ROPE] * sk_ref[...]
    ckv_ref[0] = ckv
    kr_ref[0] = kr
    if first_chunk:
        kcat_ref[0], vt_all = _keys_values(ckv, kr, wkc_ref, wkr_ref, wvt_ref)
        for j in range(tm // ATTN_TK):
            vt_ref[0, j] = vt_all[:, j * ATTN_TK:(j + 1) * ATTN_TK]
    cqn =_rms_norm(z_c[:, 0:q_lora], vec(qnorm_ref)).astype(BF16)
    q_t = _dot_nt(wuqt_ref[...], cqn)
    cos_t, sin_t = cosq_ref[...], sinq_ref[...]
    for hd in range(H_C):
        lo = hd * LANES
        x1 = q_t[lo + D_NOPE:lo + D_NOPE + half]
        x2 = q_t[lo + D_NOPE + half:lo + D_NOPE + D_ROPE]
        qt_ref[0, 0, lo:lo + D_NOPE, :] = q_t[lo:lo + D_NOPE].astype(BF16)
        qt_ref[0, 0, lo + D_NOPE:lo + D_NOPE + half, :] = (x1 * cos_t - x2 * sin_t).astype(BF16)
        qt_ref[0, 0, lo + D_NOPE + half:lo + D_NOPE + D_ROPE, :] = (x1 * sin_t + x2 * cos_t).astype(BF16)
        qt_ref[0, 0, lo + D_NOPE + D_ROPE:lo + LANES, :] = jnp.zeros((LANES - D_NOPE - D_ROPE, tm), BF16)


def _branch_call(x, pw, l, sp, conv0_pad, h0, mk, mv, rope, *, tm, seq_len, tq, first_chunk):
    bsz, seq, d_model = x.shape
    d_a, d_b, d_c, d_m = pw["dims"]
    q_lora, kv_lora = pw["q_lora"], pw["kv_lora"]
    n_mem = mk.shape[1]
    cos_t, sin_t, ck_t, sk_t = rope
    grid = (bsz, seq // tm)

    def row_spec(width):
        return pl.BlockSpec((1, tm, width), lambda b, t: (b, t, 0))

    def col_spec(height):
        return pl.BlockSpec((1, height, tm), lambda b, t: (b, 0, t))

    def tab_spec(width):
        return pl.BlockSpec((tm, width), lambda b, t: (t, 0))

    n_seq = tm // seq_len

    def batch_spec(rows, width):
        return pl.BlockSpec((n_seq, rows, width), lambda b, t: (b, 0, 0))

    names = ["w1", "wcgt", "gln_g", "gln_b", "wsp%d" % sp, "bsp%d" % sp, "conv_w", "conv_b",
             "wgate", "b_r", "b_i", "lam"]
    names2 = ["q_norm", "wuqt", "kv_norm"]
    in_specs = ([row_spec(d_model)] + [_layer_spec(pw[n], l) for n in names]
                + [batch_spec(SUBLANES, d_b), batch_spec(1, d_b)] + [_layer_spec(pw[n], l) for n in names2]
                + [pl.BlockSpec((D_ROPE // 2, tm), lambda b, t: (0, t)), pl.BlockSpec((D_ROPE // 2, tm), lambda b, t: (0, t)),
                   tab_spec(D_ROPE), tab_spec(D_ROPE), batch_spec(n_mem, d_m), batch_spec(n_mem, d_m),
                   _layer_spec(pw["wbr"], l)])
    operands = ([x] + [pw[n] for n in names] + [conv0_pad, h0] + [pw[n] for n in names2]
                + [cos_t, sin_t, ck_t, sk_t, mk, mv, pw["wbr"]])
    out_shape = [
        jax.ShapeDtypeStruct((bsz, seq, d_model), BF16),
        jax.ShapeDtypeStruct((bsz, seq, d_model), BF16),
        jax.ShapeDtypeStruct((bsz, d_c, seq), F32),
        jax.ShapeDtypeStruct((bsz, seq // tq, H_C * LANES, tq), BF16),
        jax.ShapeDtypeStruct((bsz, seq, kv_lora), F32),
        jax.ShapeDtypeStruct((bsz, seq, D_ROPE), F32),
        jax.ShapeDtypeStruct((bsz * n_seq, SUBLANES, d_b), F32),
        jax.ShapeDtypeStruct((bsz * n_seq, 1, d_b), F32),
    ]
    out_specs = [
        row_spec(d_model), row_spec(d_model), col_spec(d_c),
        pl.BlockSpec((1, 1, H_C * LANES, tm), lambda b, t: (b, t // (tq // tm), 0, t % (tq // tm))),
        row_spec(kv_lora), row_spec(D_ROPE), batch_spec(SUBLANES, d_b), batch_spec(1, d_b),
    ]
    if first_chunk:
        for name in ("wkc", "wkr", "wvt"):
            in_specs.append(_layer_spec(pw[name], l))
            operands.append(pw[name])
        out_shape += [jax.ShapeDtypeStruct((bsz, seq, H_C * LANES), BF16),
                      jax.ShapeDtypeStruct((bsz, seq // ATTN_TK, H_C * V_ROWS, ATTN_TK), BF16)]
        out_specs += [row_spec(H_C * LANES),
                      pl.BlockSpec((1, tm // ATTN_TK, H_C * V_ROWS, ATTN_TK), lambda b, t: (b, t, 0, 0))]
    else:
        out_shape.append(jax.ShapeDtypeStruct((bsz, seq, d_a), F32))
        out_specs.append(row_spec(d_a))
    kern = functools.partial(
        _branch_kernel, layer=l, first_chunk=first_chunk, tm=tm, seq_len=seq_len, sp_len=sp, d_a=d_a, d_b=d_b, d_c=d_c, d_m=d_m,
        q_lora=q_lora, kv_lora=kv_lora, cols=pw["cols"])
    return pl.pallas_call(
        kern, grid=grid, in_specs=in_specs, out_specs=out_specs, out_shape=out_shape,
        scratch_shapes=[pltpu.VMEM((tm, d_model), F32)],
        name="branch", compiler_params=_params(2),
    )(*operands)


HEADS_PER_STEP = 4
_EXP2_SCALE = (D_NOPE + D_ROPE) ** -0.5 * math.log2(math.e)


def _scores(k_tile, q_tile, out_refs):
    res = [_dot(k_tile[:, h * LANES:(h + 1) * LANES], q_tile[h * LANES:(h + 1) * LANES, :])
           for h in range(k_tile.shape[1] // LANES)]
    for h, r in enumerate(res):
        out_refs[h] = r


def _softmax_tile(s_t, m_old):
    m_new = jnp.maximum(m_old, jnp.max(s_t, axis=0, keepdims=True))
    return jnp.exp2(s_t - m_new).astype(BF16), m_new, jnp.exp2(m_old - m_new)


def _attn_output(acc, cgs_t):
    return (acc[0:D_V] / acc[D_V:D_V + 1] * cgs_t).astype(BF16)


N_BUF = 4


def _attn_causal_kernel(qt_ref, k_ref, vt_ref, cgst_ref, bias_ref, o_ref, *scratch, tq, tk):
    s_buf, e_buf, acc_scr = scratch[0:N_BUF], scratch[N_BUF:2 * N_BUF], scratch[2 * N_BUF]
    i = pl.program_id(2)
    nh = HEADS_PER_STEP
    q_cur = qt_ref.at[0, i]
    q_nxt = qt_ref.at[0, jnp.minimum(i + 1, pl.num_programs(2) - 1)]

    def k_tile(j):
        return k_ref[0, pl.ds(pl.multiple_of(j * tk, tk), tk), :]

    def pv(j, e_in, c0):
        vt = vt_ref[0, j]
        return [_dot(vt[h * V_ROWS:(h + 1) * V_ROWS, :], e_in[h, :, c0:tq]) for h in range(nh)]

    def acc_update(pvs, alphas, c0):
        for h in range(nh):
            acc_scr[h, :, c0:tq] = acc_scr[h, :, c0:tq] * alphas[h][:, c0:tq] + pvs[h]

    def softmax(slot, carry, c0, add_bias):
        new = []
        for h in range(nh):
            m_old, a_prev, _ = carry[h]
            s_t = s_buf[slot][h, :, c0:tq]
            if add_bias:
                lead = s_t[:, 0:tk] + bias_ref[...]
                s_t = lead if c0 + tk == tq else jnp.concatenate([lead, s_t[:, tk:]], axis=1)
            e, m_new, alpha = _softmax_tile(s_t, m_old[:, c0:tq])
            e_buf[slot][h, :, c0:tq] = e
            if c0:
                m_new = jnp.concatenate([m_old[:, 0:c0], m_new], axis=1)
                alpha = jnp.concatenate([jnp.ones((1, c0), F32), alpha], axis=1)
            new.append((m_new, alpha, a_prev))
        return tuple(new)

    def stage(n, slot, carry):
        pvs = pv(jnp.maximum(n - 2, 0), e_buf[(slot + 2) % N_BUF], 0)
        _scores(k_tile(n + 3), q_cur, s_buf[(slot + 3) % N_BUF])
        new = softmax(slot, carry, 0, False)
        acc_update(pvs, [c[2] for c in carry], 0)
        return new

    for slot in (N_BUF - 2, N_BUF - 1):
        e_buf[slot][...] = jnp.zeros(e_buf[slot].shape, BF16)
    acc_scr[...] = jnp.zeros(acc_scr.shape, F32)

    @pl.when(i == 0)
    def _():
        for slot in range(N_BUF - 1):
            kt = k_tile(slot)
            for h in range(nh):
                s_buf[slot][h, :, slot * tk:tq] = _dot(kt[:, h * LANES:(h + 1) * LANES],
                                                       q_cur[h * LANES:(h + 1) * LANES, slot * tk:tq])

    ones = jnp.ones((1, tq), F32)
    carry = tuple((jnp.full((1, tq), -jnp.inf, F32), ones, ones) for _ in range(nh))

    def body(t, carry):
        for slot in range(N_BUF):
            carry = stage(N_BUF * t + slot, slot, carry)
        return carry

    carry = lax.fori_loop(0, i, body, carry)

    j0 = N_BUF * i
    last = N_BUF - 1
    for d in range(N_BUF):
        c_prev = max(d - 2, 0) * tk
        pvs = pv(jnp.maximum(j0 + d - 2, 0), e_buf[(d + 2) % N_BUF], c_prev)
        if d == 0:
            k_last = k_tile(j0 + last)
            for h in range(nh):
                s_buf[last][h, :, tq - tk:tq] = _dot(
                    k_last[:, h * LANES:(h + 1) * LANES],
                    q_cur[h * LANES:(h + 1) * LANES, tq - tk:tq]) + bias_ref[...]
        else:
            _scores(k_tile(d - 1), q_nxt, s_buf[d - 1])
        alphas = [c[2] for c in carry]
        carry = softmax(d, carry, d * tk, d < last)
        acc_update(pvs, alphas, c_prev)
    acc_update(pv(j0 + last - 1, e_buf[last - 1], (last - 1) * tk), [c[2] for c in carry], (last - 1) * tk)
    acc_update(pv(j0 + last, e_buf[last], last * tk), [c[1] for c in carry], last * tk)
    for h in range(nh):
        o_ref[0, h * D_V:(h + 1) * D_V, :] = _attn_output(acc_scr[h], cgst_ref[0, h * D_V:(h + 1) * D_V, :])


def _attn_latent_kernel(qt_ref, pckv_ref, pkr_ref, ckv_ref, kr_ref, cgst_ref, wuk_ref, wuvt_ref, place_ref, place_seq_ref,
                        o_ref, *, n_seq):
    tq = place_ref.shape[1]
    out = None
    for q in range(n_seq):
        cols = slice(q * tq, (q + 1) * tq)
        c = jnp.concatenate([pckv_ref[q], ckv_ref[0, cols, :]], axis=0).astype(BF16)
        kr = jnp.concatenate([pkr_ref[q], kr_ref[0, cols, :]], axis=0).astype(BF16)
        q_lat, q_rope = None, None
        for h in range(H_C):
            lo = h * LANES
            q_h = _dot(wuk_ref[h], qt_ref[0, 0, lo:lo + D_NOPE, cols]).astype(BF16)
            a_h = _dot(q_h, place_ref[h])
            r_h = _dot(qt_ref[0, 0, lo + D_NOPE:lo + D_NOPE + D_ROPE, cols], place_ref[h])
            q_lat = a_h if q_lat is None else q_lat + a_h
            q_rope = r_h if q_rope is None else q_rope + r_h
        s = _dot(c, q_lat.astype(BF16)) + _dot(kr, q_rope.astype(BF16))
        e = jnp.exp2(s - jnp.max(s, axis=0, keepdims=True)).astype(BF16)
        denom = jnp.sum(e.astype(F32), axis=0, keepdims=True)
        lat = _dot_tn(c, e).astype(BF16)
        o_heads = []
        for h in range(H_C):
            o_h = _dot(wuvt_ref[h], lat[:, h * tq:(h + 1) * tq])
            o_heads.append(o_h / denom[:, h * tq:(h + 1) * tq] * cgst_ref[0, h * D_V:(h + 1) * D_V, cols])
        o_q = _dot(jnp.concatenate(o_heads, axis=0).astype(BF16), place_seq_ref[q])
        out = o_q if out is None else out + o_q
    o_ref[0] = out.astype(BF16)


def _attn_causal_call(qt, kcat, vt, cgst):
    bsz, n_qt, _, tq = qt.shape
    seq = n_qt * tq
    t_k = kcat.shape[1]
    nkt, _, tk = vt.shape[1:]
    nh = HEADS_PER_STEP
    assert tq == N_BUF * tk and seq == t_k and tk % CHUNK == 0
    chunk = jnp.arange(tk) // CHUNK
    bias = jnp.where(chunk[:, None] <= chunk[None, :], 0.0, -1e30).astype(F32)
    return pl.pallas_call(
        functools.partial(_attn_causal_kernel, tq=tq, tk=tk), grid=(bsz, H_C // nh, n_qt),
        in_specs=[pl.BlockSpec((1, n_qt, nh * LANES, tq), lambda b, p, i: (b, 0, p, 0)),
                  pl.BlockSpec((1, t_k, nh * LANES), lambda b, p, i: (b, 0, p)),
                  pl.BlockSpec((1, nkt, nh * V_ROWS, tk), lambda b, p, i: (b, 0, p, 0)),
                  pl.BlockSpec((1, nh * D_V, tq), lambda b, p, i: (b, p, i)),
                  _const_spec((tk, tk))],
        out_specs=pl.BlockSpec((1, nh * D_V, tq), lambda b, p, i: (b, p, i)),
        out_shape=jax.ShapeDtypeStruct((bsz, H_C * D_V, seq), BF16),
        scratch_shapes=([pltpu.VMEM((nh, tk, tq), F32)] * N_BUF + [pltpu.VMEM((nh, tk, tq), BF16)] * N_BUF
                        + [pltpu.VMEM((nh, V_ROWS, tq), F32)]),
        name="attn", compiler_params=_params(3),
    )(qt, kcat, vt, cgst, bias)


def _attn_cached_call(qt, past_ckv, past_kr, ckv_new, kr_new, cgst, pw, l, *, n_seq):
    n_cols = qt.shape[-1]
    tq = n_cols // n_seq
    past_len, kv_lora = past_ckv.shape[2:]

    def placement(n):
        p = jnp.eye(n, dtype=BF16)[:, None, :, None] * jnp.eye(tq, dtype=BF16)[None, :, None, :]
        return p.reshape(n, tq, n * tq)

    place, place_seq = placement(H_C), placement(n_seq)
    return pl.pallas_call(
        functools.partial(_attn_latent_kernel, n_seq=n_seq), grid=(1,),
        in_specs=[_const_spec(qt.shape),
                  pl.BlockSpec((None, n_seq, past_len, kv_lora), lambda i: (l, 0, 0, 0)),
                  pl.BlockSpec((None, n_seq, past_len, D_ROPE), lambda i: (l, 0, 0, 0)),
                  _const_spec(ckv_new.shape), _const_spec(kr_new.shape), _const_spec(cgst.shape),
                  _layer_spec(pw["wuk"], l), _layer_spec(pw["wuvt"], l),
                  _const_spec(place.shape), _const_spec(place_seq.shape)],
        out_specs=pl.BlockSpec((1, H_C * D_V, n_cols), lambda i: (0, 0, 0)),
        out_shape=jax.ShapeDtypeStruct((1, H_C * D_V, n_cols), BF16),
        name="attn_cached", compiler_params=_params(1),
    )(qt, past_ckv, past_kr, ckv_new, kr_new, cgst, pw["wuk"], pw["wuvt"], place, place_seq)


def _merge_kernel(x_ref, partial_ref, g2_ref, oct_ref, wbr_ref, wout_ref, lng_ref, lnb_ref, o_ref,
                  *, layer, c_lo, d_c, alpha):
    yc = _dot_tn(oct_ref[0], wbr_ref[c_lo:c_lo + d_c, :])
    merged = partial_ref[0].astype(F32) + g2_ref[0].astype(F32) * yc
    y = _dot(merged.astype(BF16), wout_ref[...])
    o_ref[0] = _layer_norm(alpha * x_ref[0] + y, lng_ref[layer:layer + 1, :], lnb_ref[layer:layer + 1, :])


def _merge_call(x, partial, g2, oct, pw, l, *, tr, alpha):
    bsz, seq, d_model = x.shape
    d_a, d_b, d_c, _ = pw["dims"]
    kern = functools.partial(_merge_kernel, layer=l, c_lo=d_a + d_b, d_c=d_c, alpha=alpha)

    def row_spec(width):
        return pl.BlockSpec((1, tr, width), lambda b, i: (b, i, 0))

    return pl.pallas_call(
        kern, grid=(bsz, seq // tr),
        in_specs=[row_spec(d_model), row_spec(d_model), row_spec(d_model),
                  pl.BlockSpec((1, d_c, tr), lambda b, i: (b, 0, i)),
                  _layer_spec(pw["wbr"], l), _layer_spec(pw["wout"], l),
                  _layer_spec(pw["ln_g"], l), _layer_spec(pw["ln_b"], l)],
        out_specs=row_spec(d_model),
        out_shape=jax.ShapeDtypeStruct((bsz, seq, d_model), F32),
        name="merge", compiler_params=_params(2),
    )(x, partial, g2, oct, pw["wbr"], pw["wout"], pw["ln_g"], pw["ln_b"])


def _mem_kernel(m_ref, w_ref, k_ref, v_ref, *, d_m):
    kv = _dot(m_ref[...].astype(BF16), w_ref[...])
    k_ref[...] = kv[:, 0:d_m]
    v_ref[...] = kv[:, d_m:2 * d_m]


def _mem_call(mem2d, wmem, l, *, tr):
    rows, d_model = mem2d.shape
    d_m = wmem.shape[2] // 2
    return pl.pallas_call(
        functools.partial(_mem_kernel, d_m=d_m), grid=(rows // tr,),
        in_specs=[pl.BlockSpec((tr, d_model), lambda i: (i, 0)), _layer_spec(wmem, l)],
        out_specs=[pl.BlockSpec((tr, d_m), lambda i: (i, 0)), pl.BlockSpec((tr, d_m), lambda i: (i, 0))],
        out_shape=[jax.ShapeDtypeStruct((rows, d_m), F32), jax.ShapeDtypeStruct((rows, d_m), F32)],
        name="mem", compiler_params=_params(1),
    )(mem2d, wmem)


def _block_diag(w):
    h, n, _ = w.shape
    eye = jnp.eye(h, dtype=w.dtype)
    return (eye[:, None, :, None] * w[:, :, None, :]).reshape(h * n, h * n)


def _gate_blocks(w_r, w_i):
    per = MXU_TILE // w_r.shape[1]
    blocks = []
    for k in range(w_r.shape[0] // per):
        blocks.append(jnp.concatenate([_block_diag(w_r[k * per:(k + 1) * per]),
                                       _block_diag(w_i[k * per:(k + 1) * per])], axis=1))
    return jnp.stack(blocks)


def _w_in_kernel(wt_ref, o_ref, *, n_head, o_tail):
    o_ref[0, :, 0:n_head] = jnp.transpose(wt_ref[0, 0:n_head, :]).astype(BF16)
    o_ref[0, :, n_head:] = jnp.transpose(wt_ref[0, o_tail:, :]).astype(BF16)


def _w_in_call(w_in, n_head, o_tail, *, tk):
    depth, d_model, d_in = w_in.shape
    n_out = n_head + d_in - o_tail
    return pl.pallas_call(
        functools.partial(_w_in_kernel, n_head=n_head, o_tail=o_tail), grid=(depth, d_model // tk),
        in_specs=[pl.BlockSpec((1, d_in, tk), lambda l, i: (l, 0, i))],
        out_specs=pl.BlockSpec((1, tk, n_out), lambda l, i: (l, i, 0)),
        out_shape=jax.ShapeDtypeStruct((depth, d_model, n_out), BF16),
        name="w_in", compiler_params=_params(2),
    )(jnp.swapaxes(w_in, 1, 2))


def _prep_weights(sp_lens, w_in, gmlp_ln_g, gmlp_ln_b, gmlp_ws, gmlp_bs, lru_conv_w,
                  lru_conv_b, lru_w_r, lru_b_r, lru_w_i, lru_b_i, lru_lambda, mla_q_norm, mla_w_uq,
                  mla_kv_norm, mla_w_ukv, mem_w_k, mem_w_v, w_br, w_out, ln_g, ln_b):
    depth, d_model, _ = w_in.shape
    d_a = gmlp_ln_g.shape[1]
    d_b = lru_lambda.shape[1]
    q_lora = mla_q_norm.shape[1]
    kv_lora = mla_kv_norm.shape[1]
    d_c = H_C * D_V
    d_m = H_M * DH_M
    o_cq = 3 * d_a + 2 * d_b
    o_kr = o_cq + q_lora + kv_lora
    o_cg = o_kr + D_ROPE
    o_mq = o_cg + d_c
    o_g = o_mq + d_m
    n_head = o_kr + LANES
    assert o_kr % LANES == 0 and (o_g - o_mq) % LANES == 0 and D_ROPE <= LANES and n_head <= o_mq
    cols = {"a_u": (0, d_a), "a_v": (d_a, 2 * d_a), "a_g": (2 * d_a, 3 * d_a),
            "b_x": (3 * d_a, 3 * d_a + d_b), "b_g": (3 * d_a + d_b, o_cq), "c_all": (o_cq, n_head),
            "m_q": (n_head, n_head + d_m)}
    for k in range(N_BRANCH):
        cols[f"g{k}"] = (n_head + d_m + k * d_model, n_head + d_m + (k + 1) * d_model)
    w1 = _w_in_call(w_in, n_head, o_mq, tk=LANES)
    wcgt = jnp.swapaxes(w_in[:, :, o_cg:o_mq], 1, 2).astype(BF16)

    wq = mla_w_uq.reshape(depth, q_lora, H_C, D_NOPE + D_ROPE) * _EXP2_SCALE
    wq = jnp.concatenate([wq, jnp.zeros((depth, q_lora, H_C, LANES - D_NOPE - D_ROPE), wq.dtype)], axis=3)
    wuqt = jnp.swapaxes(wq.reshape(depth, q_lora, H_C * LANES), 1, 2).astype(BF16)

    wkv = mla_w_ukv.reshape(depth, kv_lora, H_C, D_NOPE + D_V)
    wk = wkv[..., :D_NOPE]
    wv = wkv[..., D_NOPE:]
    wkc = jnp.concatenate([wk, jnp.zeros((depth, kv_lora, H_C, LANES - D_NOPE), wk.dtype)], axis=3)
    wkc = wkc.reshape(depth, kv_lora, H_C * LANES).astype(BF16)
    lane = jnp.arange(H_C * LANES) % LANES
    wkr = ((lane[None, :] - D_NOPE) == jnp.arange(D_ROPE)[:, None]).astype(BF16)
    wkr = jnp.broadcast_to(wkr, (depth,) + wkr.shape)
    wv = jnp.concatenate([wv, jnp.zeros((depth, kv_lora, H_C, V_ROWS - D_V), wv.dtype)], axis=3)
    wvt = jnp.swapaxes(wv.reshape(depth, kv_lora, H_C * V_ROWS), 1, 2).astype(BF16)

    pw = dict(
        dims=(d_a, d_b, d_c, d_m), q_lora=q_lora, kv_lora=kv_lora, cols=cols,
        w1=w1, wcgt=wcgt,
        gln_g=gmlp_ln_g, gln_b=gmlp_ln_b,
        conv_w=lru_conv_w, conv_b=lru_conv_b,
        wgate=jnp.stack([_gate_blocks(lru_w_r[l], lru_w_i[l]) for l in range(depth)]).astype(BF16),
        b_r=lru_b_r, b_i=lru_b_i, lam=lru_lambda,
        q_norm=mla_q_norm, wuqt=wuqt, kv_norm=mla_kv_norm,
        wkc=wkc, wkr=wkr, wvt=wvt,
        wuk=jnp.transpose(wk, (0, 2, 1, 3)).astype(BF16),
        wuvt=jnp.transpose(wkv[..., D_NOPE:], (0, 2, 3, 1)).astype(BF16),
        wbr=w_br.astype(BF16), wout=w_out.astype(BF16), ln_g=ln_g, ln_b=ln_b,
        wmem=jnp.concatenate([mem_w_k, mem_w_v], axis=2).astype(BF16),
    )
    for sp in sp_lens:
        pw["wsp%d" % sp] = jnp.tril(gmlp_ws[:, :, :sp, :sp]).astype(BF16)
        pw["bsp%d" % sp] = jnp.repeat(jnp.swapaxes(gmlp_bs[:, :, :sp], 1, 2), d_a // G_A, axis=2)
    return pw


def _rope_tables(pos):
    half = D_ROPE // 2
    freq = ROPE_BASE ** (-jnp.arange(half, dtype=F32) / half)
    ang = pos.astype(F32)[:, None] * freq[None, :]
    cos, sin = jnp.cos(ang), jnp.sin(ang)
    return (jnp.transpose(cos), jnp.transpose(sin),
            jnp.concatenate([cos, cos], axis=1), jnp.concatenate([-sin, sin], axis=1))


def _pad_conv_state(conv):
    return jnp.pad(conv, ((0, 0), (SUBLANES - (CONV_W - 1), 0), (0, 0)))


def _trunk_layer(x, pw, l, sp, rope, conv0, h0, mk, mv, past_ckv, past_kr, *, tm, alpha):
    bsz, seq, d_model = x.shape
    first_chunk = past_ckv is None
    if first_chunk:
        x_rows, seq_len, tq = x, tm, N_BUF * ATTN_TK
    else:
        x_rows, seq_len, tq = x.reshape(1, bsz * seq, d_model), seq, bsz * seq
        cos_t, sin_t, ck_t, sk_t = rope
        rope = (jnp.tile(cos_t, (1, bsz)), jnp.tile(sin_t, (1, bsz)), jnp.tile(ck_t, (bsz, 1)), jnp.tile(sk_t, (bsz, 1)))
    outs = _branch_call(x_rows, pw, l, sp, _pad_conv_state(conv0), h0[:, None, :], mk, mv, rope,
                        tm=tm, seq_len=seq_len, tq=tq, first_chunk=first_chunk)
    partial, g2, cgst, qt, ckv_new, kr_new, conv_pad, h_new = outs[:8]
    if first_chunk:
        kcat, vt = outs[8:]
        v_rows = None
        oct = _attn_causal_call(qt, kcat, vt, cgst)
    else:
        oct = _attn_cached_call(qt, past_ckv, past_kr, ckv_new, kr_new, cgst, pw, l, n_seq=bsz)
        ckv_new, kr_new, v_rows = (o.reshape(bsz, seq, -1) for o in (ckv_new, kr_new, outs[8]))
    x_new = _merge_call(x_rows, partial, g2, oct, pw, l, tr=MERGE_TILE if first_chunk else tm, alpha=alpha)
    return (x_new.reshape(bsz, seq, d_model), v_rows, conv_pad[:, SUBLANES - (CONV_W - 1):], h_new[:, 0],
            ckv_new, kr_new)


def kernel(x_prompt, x_sample, mem_prompt, cache_mla_ckv, cache_mla_krope, cache_mem_k, cache_mem_v,
           state_lru_h, state_lru_conv, w_in, gmlp_ln_g, gmlp_ln_b, gmlp_ws, gmlp_bs,
           lru_conv_w, lru_conv_b, lru_w_r, lru_b_r, lru_w_i, lru_b_i, lru_lambda,
           mla_q_norm, mla_w_uq, mla_kv_norm, mla_w_ukv, mem_w_k, mem_w_v, w_br, w_out, ln_g, ln_b):
    bp, tp, d_model = x_prompt.shape
    bs, ts, _ = x_sample.shape
    depth = w_in.shape[0]
    past_len = cache_mla_ckv.shape[2]
    n_mem = mem_prompt.shape[1]
    d_b = lru_lambda.shape[1]
    alpha = (2.0 * depth) ** 0.25
    sp_p, sp_s = min(tp, A_CHUNK), min(ts, A_CHUNK)
    pw = _prep_weights(sorted({sp_p, sp_s}), w_in, gmlp_ln_g, gmlp_ln_b, gmlp_ws, gmlp_bs, lru_conv_w,
                       lru_conv_b, lru_w_r, lru_b_r, lru_w_i, lru_b_i, lru_lambda, mla_q_norm, mla_w_uq,
                       mla_kv_norm, mla_w_ukv, mem_w_k, mem_w_v, w_br, w_out, ln_g, ln_b)
    rope_p = _rope_tables(jnp.arange(tp))
    rope_s = _rope_tables(past_len + jnp.arange(ts))
    tm_p = min(tp, ROW_TILE)
    assert tp % (N_BUF * ATTN_TK) == 0 and tm_p % ATTN_TK == 0 and tp % MERGE_TILE == 0 and ts % SUBLANES == 0
    zero_conv = jnp.zeros((bp, CONV_W - 1, d_b), F32)
    zero_h = jnp.zeros((bp, d_b), F32)

    xp, xs = x_prompt, x_sample
    acc = [[] for _ in range(11)]
    for l in range(depth):
        mk, mv = _mem_call(mem_prompt.reshape(bp * n_mem, d_model), pw["wmem"], l, tr=n_mem)
        mk = mk.reshape(bp, n_mem, -1)
        mv = mv.reshape(bp, n_mem, -1)
        xp, _, conv_n, h_n, ckv_n, kr_n = _trunk_layer(
            xp, pw, l, sp_p, rope_p, zero_conv, zero_h, mk, mv, None, None, tm=tm_p, alpha=alpha)
        for k, val in zip(range(6), (ckv_n, kr_n, mk.reshape(bp, n_mem, H_M, DH_M),
                                     mv.reshape(bp, n_mem, H_M, DH_M), h_n, conv_n)):
            acc[k].append(val)
        xs, v_n, conv_n, h_n, ckv_n, kr_n = _trunk_layer(
            xs, pw, l, sp_s, rope_s, state_lru_conv[l], state_lru_h[l],
            cache_mem_k[l].reshape(bs, n_mem, -1), cache_mem_v[l].reshape(bs, n_mem, -1),
            cache_mla_ckv, cache_mla_krope, tm=bs * ts, alpha=alpha)
        for k, val in zip(range(6, 11), (ckv_n, kr_n, h_n, conv_n, v_n)):
            acc[k].append(val)
    return (xp, xs) + tuple(jnp.stack(a) for a in acc)
```

```python
import functools
import math

import jax
import jax.numpy as jnp
from jax import lax
from jax.experimental import pallas as pl
from jax.experimental.pallas import tpu as pltpu

CHUNK = 64
G_A = 4
A_CHUNK = 128
H_B = 8
CONV_W = 4
LRU_C = 8.0
H_C = 8
D_NOPE = 64
D_ROPE = 32
D_V = 64
ROPE_BASE = 10000.0
H_M = 4
DH_M = 64
N_BRANCH = 4
EPS = 1e-6

LANES = 128
SUBLANES = 8
VMEM_LIMIT = 56 * 1024 * 1024
ROW_TILE = 512
ATTN_TK = 256
MXU_TILE = 256
MERGE_TILE = 1024

F32 = jnp.float32
BF16 = jnp.bfloat16


def _dot(a, b):
    return jnp.dot(a, b, preferred_element_type=F32)


def _dot_nt(a, b):
    return lax.dot_general(a, b, (((1,), (1,)), ((), ())), preferred_element_type=F32)


def _dot_tn(a, b):
    return lax.dot_general(a, b, (((0,), (0,)), ((), ())), preferred_element_type=F32)


def _sigmoid(x):
    return 1.0 / (1.0 + jnp.exp(-x))


def _silu(x):
    return x * _sigmoid(x)


def _gelu(x):
    return jax.nn.gelu(x)


def _expm1_nonpos(x):
    u = jnp.exp(x)
    near = (u - 1.0) * x / jnp.log(jnp.where(u == 1.0, 2.0, jnp.maximum(u, 0.5)))
    return jnp.where(u == 1.0, x, jnp.where(u > 0.5, near, u - 1.0))


def _layer_norm(x, g, b):
    mu = jnp.mean(x, -1, keepdims=True)
    var = jnp.mean(jnp.square(x - mu), -1, keepdims=True)
    return (x - mu) * lax.rsqrt(var + EPS) * g + b


def _rms_norm(x, g):
    return x * lax.rsqrt(jnp.mean(jnp.square(x), -1, keepdims=True) + EPS) * g


def _const_spec(shape):
    nd = len(shape)
    return pl.BlockSpec(shape, lambda *_: (0,) * nd, pipeline_mode=pl.Buffered(1))


def _layer_spec(arr, l):
    if arr.ndim == 2:
        return _const_spec(arr.shape)
    nd = arr.ndim - 1
    return pl.BlockSpec((None,) + arr.shape[1:], lambda *_: (l,) + (0,) * nd, pipeline_mode=pl.Buffered(1))


def _params(n_axes):
    return pltpu.CompilerParams(dimension_semantics=("arbitrary",) * n_axes,
                                vmem_limit_bytes=VMEM_LIMIT)


V_ROWS = 80


def _keys_values(ckv, kr, wkc_ref, wkr_ref, wvt_ref):
    c = ckv.astype(BF16)
    kcat = (_dot(c, wkc_ref[...]) + _dot(kr.astype(BF16), wkr_ref[...])).astype(BF16)
    v_t = _dot_nt(wvt_ref[...], c)
    row = lax.broadcasted_iota(jnp.int32, (v_t.shape[0], 1), 0)
    return kcat, jnp.where(row % V_ROWS == D_V, 1.0, v_t).astype(BF16)


def _branch_kernel(x_ref, w1_ref, wcgt_ref, glng_ref, glnb_ref, wsp_ref, bsp_ref,
                   convw_ref, convb_ref, wgate_ref, br_ref, bi_ref, lam_ref, conv0_ref, h0_ref, qnorm_ref, wuqt_ref,
                   kvnorm_ref, cosq_ref, sinq_ref, ck_ref, sk_ref, mk_ref, mv_ref, wbr_ref,
                   *rest, layer, first_chunk, tm, seq_len, sp_len, d_a, d_b, d_c, d_m, q_lora, kv_lora, cols):
    if first_chunk:
        (wkc_ref, wkr_ref, wvt_ref, partial_ref, g2_ref, cgst_ref, qt_ref, ckv_ref, kr_ref, conv_ref, h_ref,
         kcat_ref, vt_ref, merged_scr) = rest
    else:
        partial_ref, g2_ref, cgst_ref, qt_ref, ckv_ref, kr_ref, conv_ref, h_ref, v_ref, merged_scr = rest
    t = pl.program_id(1)
    xb = x_ref[0].astype(BF16)

    def vec(ref):
        return ref[layer:layer + 1, :]

    def zin(name):
        lo, hi = cols[name]
        return _dot(xb, w1_ref[:, lo:hi])

    z_u, z_v, z_ag, z_g0 = zin("a_u"), zin("a_v"), zin("a_g"), zin("g0")
    u = _gelu(z_u)
    v = _layer_norm(_gelu(z_v), vec(glng_ref), vec(glnb_ref))
    if not first_chunk:
        v_ref[0] = v
    vb = v.astype(BF16)
    n_groups = d_a // LANES
    row_blocks = []
    for c in range(tm // sp_len):
        col_blocks = [_dot(wsp_ref[g], vb[c * sp_len:(c + 1) * sp_len, g * LANES:(g + 1) * LANES])
                      for g in range(n_groups)]
        row_blocks.append(jnp.concatenate(col_blocks, axis=1) + bsp_ref[...])
    s = row_blocks[0] if len(row_blocks) == 1 else jnp.concatenate(row_blocks, axis=0)
    oa = (u * s) * _silu(z_ag)
    merged_scr[...] = _sigmoid(z_g0) * _dot(oa.astype(BF16), wbr_ref[0:d_a, :])

    @pl.when(t == 0)
    def _():
        conv_ref[...] = conv0_ref[...]
        h_ref[...] = h0_ref[...]

    bx = zin("b_x")
    n_seq = tm // seq_len
    span = SUBLANES + seq_len
    stacked = jnp.concatenate(
        [piece for q in range(n_seq) for piece in (conv_ref[q], bx[q * seq_len:(q + 1) * seq_len])], axis=0)
    xc = vec(convb_ref)
    for k in range(CONV_W):
        shift = CONV_W - 1 - k
        rolled = stacked if shift == 0 else pltpu.roll(stacked, shift, axis=0)
        sh = [rolled[q * span + SUBLANES:(q + 1) * span] for q in range(n_seq)]
        xc = xc + (sh[0] if n_seq == 1 else jnp.concatenate(sh, axis=0)) * convw_ref[k:k + 1, :]
    for q in range(n_seq):
        conv_ref[q] = bx[(q + 1) * seq_len - SUBLANES:(q + 1) * seq_len]

    xcb = xc.astype(BF16)
    n_blk = d_b // MXU_TILE
    ri = [_dot(xcb[:, k * MXU_TILE:(k + 1) * MXU_TILE], wgate_ref[k]) for k in range(n_blk)]
    z_bg, z_g1, z_g2 = zin("b_g"), zin("g1"), zin("g2")
    z_mq, z_g3, z_c = zin("m_q"), zin("g3"), zin("c_all")
    cg_t = _dot_nt(wcgt_ref[...], xb)
    r = _sigmoid(jnp.concatenate([p[:, 0:MXU_TILE] for p in ri], axis=1) + vec(br_ref))
    i_gate = _sigmoid(jnp.concatenate([p[:, MXU_TILE:2 * MXU_TILE] for p in ri], axis=1) + vec(bi_ref))
    neg_lam = -vec(lam_ref)
    softplus = jnp.maximum(neg_lam, 0.0) + jnp.log1p(jnp.exp(-jnp.abs(neg_lam)))
    log_a = (-LRU_C * r) * softplus
    a = jnp.exp(log_a)
    bval = jnp.sqrt(-_expm1_nonpos(2.0 * log_a)) * (i_gate * xc)
    in_group = lax.broadcasted_iota(jnp.int32, (tm, 1), 0) % SUBLANES
    d = 1
    while d < SUBLANES:
        keep = in_group >= d
        a_sh = pltpu.roll(a, d, axis=0)
        b_sh = pltpu.roll(bval, d, axis=0)
        bval = jnp.where(keep, a * b_sh + bval, bval)
        a = jnp.where(keep, a * a_sh, a)
        d *= 2
    groups_per_seq = seq_len // SUBLANES
    h_groups = []
    for g in range(tm // SUBLANES):
        q = g // groups_per_seq
        if g % groups_per_seq == 0:
            carry = h_ref[q]
        lo = g * SUBLANES
        h_g = a[lo:lo + SUBLANES] * carry + bval[lo:lo + SUBLANES]
        h_groups.append(h_g)
        carry = h_g[SUBLANES - 1:SUBLANES]
        if (g + 1) % groups_per_seq == 0:
            h_ref[q] = carry
    h = jnp.concatenate(h_groups, axis=0)
    ob = h * _silu(z_bg)
    merged_scr[...] += _sigmoid(z_g1) * _dot(ob.astype(BF16), wbr_ref[d_a:d_a + d_b, :])
    g2_ref[0] = _sigmoid(z_g2).astype(BF16)

    mq = z_mq.astype(BF16)
    lane = lax.broadcasted_iota(jnp.int32, (1, LANES), 1)
    om_rows = []
    for q in range(n_seq):
        mkb = mk_ref[q].astype(BF16)
        mvb = mv_ref[q].astype(BF16)
        slabs = []
        for p in range(d_m // LANES):
            mq_p = mq[q * seq_len:(q + 1) * seq_len, p * LANES:(p + 1) * LANES]
            mk_p = mkb[:, p * LANES:(p + 1) * LANES]
            mv_p = mvb[:, p * LANES:(p + 1) * LANES]
            acc = None
            for half in range(LANES // DH_M):
                sel = (lane >= half * DH_M) & (lane < (half + 1) * DH_M)
                sc = _dot_nt(mq_p, jnp.where(sel, mk_p, jnp.zeros_like(mk_p))) * (DH_M ** -0.5)
                e = jnp.exp(sc - jnp.max(sc, -1, keepdims=True))
                prob = (e / jnp.sum(e, -1, keepdims=True)).astype(BF16)
                o = _dot(prob, jnp.where(sel, mv_p, jnp.zeros_like(mv_p)))
                acc = o if acc is None else acc + o
            slabs.append(acc)
        om_rows.append(jnp.concatenate(slabs, axis=1))
    om = om_rows[0] if n_seq == 1 else jnp.concatenate(om_rows, axis=0)
    m_lo = d_a + d_b + d_c
    partial_ref[0] = (merged_scr[...]
                      + _sigmoid(z_g3) * _dot(om.astype(BF16), wbr_ref[m_lo:m_lo + d_m, :])).astype(BF16)
    cgst_ref[0] = _silu(cg_t)

    ckv = _rms_norm(z_c[:, q_lora:q_lora + kv_lora], vec(kvnorm_ref))
    half = D_ROPE // 2
    z_kr = z_c[:, q_lora + kv_lora:q_lora + kv_lora + LANES]
    kr_swapped = jnp.where(lane < half, pltpu.roll(z_kr, LANES - half, axis=1), pltpu.roll(z_kr, half, axis=1))
    kr = z_kr[:, 0:D_ROPE] * ck_ref[...] + kr_swapped[:, 0:D_ROPE] * sk_ref[...]
    ckv_ref[0] = ckv
    kr_ref[0] = kr
    if first_chunk:
        kcat_ref[0], vt_all = _keys_values(ckv, kr, wkc_ref, wkr_ref, wvt_ref)
        for j in range(tm // ATTN_TK):
            vt_ref[0, j] = vt_all[:, j * ATTN_TK:(j + 1) * ATTN_TK]
    cqn =_rms_norm(z_c[:, 0:q_lora], vec(qnorm_ref)).astype(BF16)
    q_t = _dot_nt(wuqt_ref[...], cqn)
    cos_t, sin_t = cosq_ref[...], sinq_ref[...]
    for hd in range(H_C):
        lo = hd * LANES
        x1 = q_t[lo + D_NOPE:lo + D_NOPE + half]
        x2 = q_t[lo + D_NOPE + half:lo + D_NOPE + D_ROPE]
        qt_ref[0, 0, lo:lo + D_NOPE, :] = q_t[lo:lo + D_NOPE].astype(BF16)
        qt_ref[0, 0, lo + D_NOPE:lo + D_NOPE + half, :] = (x1 * cos_t - x2 * sin_t).astype(BF16)
        qt_ref[0, 0, lo + D_NOPE + half:lo + D_NOPE + D_ROPE, :] = (x1 * sin_t + x2 * cos_t).astype(BF16)
        qt_ref[0, 0, lo + D_NOPE + D_ROPE:lo + LANES, :] = jnp.zeros((LANES - D_NOPE - D_ROPE, tm), BF16)


def _branch_call(x, pw, l, sp, conv0_pad, h0, mk, mv, rope, *, tm, seq_len, tq, first_chunk):
    bsz, seq, d_model = x.shape
    d_a, d_b, d_c, d_m = pw["dims"]
    q_lora, kv_lora = pw["q_lora"], pw["kv_lora"]
    n_mem = mk.shape[1]
    cos_t, sin_t, ck_t, sk_t = rope
    grid = (bsz, seq // tm)

    def row_spec(width):
        return pl.BlockSpec((1, tm, width), lambda b, t: (b, t, 0))

    def col_spec(height):
        return pl.BlockSpec((1, height, tm), lambda b, t: (b, 0, t))

    def tab_spec(width):
        return pl.BlockSpec((tm, width), lambda b, t: (t, 0))

    n_seq = tm // seq_len

    def batch_spec(rows, width):
        return pl.BlockSpec((n_seq, rows, width), lambda b, t: (b, 0, 0))

    names = ["w1", "wcgt", "gln_g", "gln_b", "wsp%d" % sp, "bsp%d" % sp, "conv_w", "conv_b",
             "wgate", "b_r", "b_i", "lam"]
    names2 = ["q_norm", "wuqt", "kv_norm"]
    in_specs = ([row_spec(d_model)] + [_layer_spec(pw[n], l) for n in names]
                + [batch_spec(SUBLANES, d_b), batch_spec(1, d_b)] + [_layer_spec(pw[n], l) for n in names2]
                + [pl.BlockSpec((D_ROPE // 2, tm), lambda b, t: (0, t)), pl.BlockSpec((D_ROPE // 2, tm), lambda b, t: (0, t)),
                   tab_spec(D_ROPE), tab_spec(D_ROPE), batch_spec(n_mem, d_m), batch_spec(n_mem, d_m),
                   _layer_spec(pw["wbr"], l)])
    operands = ([x] + [pw[n] for n in names] + [conv0_pad, h0] + [pw[n] for n in names2]
                + [cos_t, sin_t, ck_t, sk_t, mk, mv, pw["wbr"]])
    out_shape = [
        jax.ShapeDtypeStruct((bsz, seq, d_model), BF16),
        jax.ShapeDtypeStruct((bsz, seq, d_model), BF16),
        jax.ShapeDtypeStruct((bsz, d_c, seq), F32),
        jax.ShapeDtypeStruct((bsz, seq // tq, H_C * LANES, tq), BF16),
        jax.ShapeDtypeStruct((bsz, seq, kv_lora), F32),
        jax.ShapeDtypeStruct((bsz, seq, D_ROPE), F32),
        jax.ShapeDtypeStruct((bsz * n_seq, SUBLANES, d_b), F32),
        jax.ShapeDtypeStruct((bsz * n_seq, 1, d_b), F32),
    ]
    out_specs = [
        row_spec(d_model), row_spec(d_model), col_spec(d_c),
        pl.BlockSpec((1, 1, H_C * LANES, tm), lambda b, t: (b, t // (tq // tm), 0, t % (tq // tm))),
        row_spec(kv_lora), row_spec(D_ROPE), batch_spec(SUBLANES, d_b), batch_spec(1, d_b),
    ]
    if first_chunk:
        for name in ("wkc", "wkr", "wvt"):
            in_specs.append(_layer_spec(pw[name], l))
            operands.append(pw[name])
        out_shape += [jax.ShapeDtypeStruct((bsz, seq, H_C * LANES), BF16),
                      jax.ShapeDtypeStruct((bsz, seq // ATTN_TK, H_C * V_ROWS, ATTN_TK), BF16)]
        out_specs += [row_spec(H_C * LANES),
                      pl.BlockSpec((1, tm // ATTN_TK, H_C * V_ROWS, ATTN_TK), lambda b, t: (b, t, 0, 0))]
    else:
        out_shape.append(jax.ShapeDtypeStruct((bsz, seq, d_a), F32))
        out_specs.append(row_spec(d_a))
    kern = functools.partial(
        _branch_kernel, layer=l, first_chunk=first_chunk, tm=tm, seq_len=seq_len, sp_len=sp, d_a=d_a, d_b=d_b, d_c=d_c, d_m=d_m,
        q_lora=q_lora, kv_lora=kv_lora, cols=pw["cols"])
    return pl.pallas_call(
        kern, grid=grid, in_specs=in_specs, out_specs=out_specs, out_shape=out_shape,
        scratch_shapes=[pltpu.VMEM((tm, d_model), F32)],
        name="branch", compiler_params=_params(2),
    )(*operands)


HEADS_PER_STEP = 4
_EXP2_SCALE = (D_NOPE + D_ROPE) ** -0.5 * math.log2(math.e)


def _scores(k_tile, q_tile, out_refs):
    res = [_dot(k_tile[:, h * LANES:(h + 1) * LANES], q_tile[h * LANES:(h + 1) * LANES, :])
           for h in range(k_tile.shape[1] // LANES)]
    for h, r in enumerate(res):
        out_refs[h] = r


def _softmax_tile(s_t, m_old):
    m_new = jnp.maximum(m_old, jnp.max(s_t, axis=0, keepdims=True))
    return jnp.exp2(s_t - m_new).astype(BF16), m_new, jnp.exp2(m_old - m_new)


def _attn_output(acc, cgs_t):
    return (acc[0:D_V] / acc[D_V:D_V + 1] * cgs_t).astype(BF16)


N_BUF = 4


def _attn_causal_kernel(qt_ref, k_ref, vt_ref, cgst_ref, bias_ref, o_ref, *scratch, tq, tk):
    s_buf, e_buf, acc_scr = scratch[0:N_BUF], scratch[N_BUF:2 * N_BUF], scratch[2 * N_BUF]
    i = pl.program_id(2)
    nh = HEADS_PER_STEP
    q_cur = qt_ref.at[0, i]
    q_nxt = qt_ref.at[0, jnp.minimum(i + 1, pl.num_programs(2) - 1)]

    def k_tile(j):
        return k_ref[0, pl.ds(pl.multiple_of(j * tk, tk), tk), :]

    def pv(j, e_in, c0):
        vt = vt_ref[0, j]
        return [_dot(vt[h * V_ROWS:(h + 1) * V_ROWS, :], e_in[h, :, c0:tq]) for h in range(nh)]

    def acc_update(pvs, alphas, c0):
        for h in range(nh):
            acc_scr[h, :, c0:tq] = acc_scr[h, :, c0:tq] * alphas[h][:, c0:tq] + pvs[h]

    def softmax(slot, carry, c0, add_bias):
        new = []
        for h in range(nh):
            m_old, a_prev, _ = carry[h]
            s_t = s_buf[slot][h, :, c0:tq]
            if add_bias:
                lead = s_t[:, 0:tk] + bias_ref[...]
                s_t = lead if c0 + tk == tq else jnp.concatenate([lead, s_t[:, tk:]], axis=1)
            e, m_new, alpha = _softmax_tile(s_t, m_old[:, c0:tq])
            e_buf[slot][h, :, c0:tq] = e
            if c0:
                m_new = jnp.concatenate([m_old[:, 0:c0], m_new], axis=1)
                alpha = jnp.concatenate([jnp.ones((1, c0), F32), alpha], axis=1)
            new.append((m_new, alpha, a_prev))
        return tuple(new)

    def stage(n, slot, carry):
        pvs = pv(jnp.maximum(n - 2, 0), e_buf[(slot + 2) % N_BUF], 0)
        _scores(k_tile(n + 3), q_cur, s_buf[(slot + 3) % N_BUF])
        new = softmax(slot, carry, 0, False)
        acc_update(pvs, [c[2] for c in carry], 0)
        return new

    for slot in (N_BUF - 2, N_BUF - 1):
        e_buf[slot][...] = jnp.zeros(e_buf[slot].shape, BF16)
    acc_scr[...] = jnp.zeros(acc_scr.shape, F32)

    @pl.when(i == 0)
    def _():
        for slot in range(N_BUF - 1):
            kt = k_tile(slot)
            for h in range(nh):
                s_buf[slot][h, :, slot * tk:tq] = _dot(kt[:, h * LANES:(h + 1) * LANES],
                                                       q_cur[h * LANES:(h + 1) * LANES, slot * tk:tq])

    ones = jnp.ones((1, tq), F32)
    carry = tuple((jnp.full((1, tq), -jnp.inf, F32), ones, ones) for _ in range(nh))

    def body(t, carry):
        for slot in range(N_BUF):
            carry = stage(N_BUF * t + slot, slot, carry)
        return carry

    carry = lax.fori_loop(0, i, body, carry)

    j0 = N_BUF * i
    last = N_BUF - 1
    for d in range(N_BUF):
        c_prev = max(d - 2, 0) * tk
        pvs = pv(jnp.maximum(j0 + d - 2, 0), e_buf[(d + 2) % N_BUF], c_prev)
        if d == 0:
            k_last = k_tile(j0 + last)
            for h in range(nh):
                s_buf[last][h, :, tq - tk:tq] = _dot(
                    k_last[:, h * LANES:(h + 1) * LANES],
                    q_cur[h * LANES:(h + 1) * LANES, tq - tk:tq]) + bias_ref[...]
        else:
            _scores(k_tile(d - 1), q_nxt, s_buf[d - 1])
        alphas = [c[2] for c in carry]
        carry = softmax(d, carry, d * tk, d < last)
        acc_update(pvs, alphas, c_prev)
    acc_update(pv(j0 + last - 1, e_buf[last - 1], (last - 1) * tk), [c[2] for c in carry], (last - 1) * tk)
    acc_update(pv(j0 + last, e_buf[last], last * tk), [c[1] for c in carry], last * tk)
    for h in range(nh):
        o_ref[0, h * D_V:(h + 1) * D_V, :] = _attn_output(acc_scr[h], cgst_ref[0, h * D_V:(h + 1) * D_V, :])


def _attn_latent_kernel(qt_ref, pckv_ref, pkr_ref, ckv_ref, kr_ref, cgst_ref, wuk_ref, wuvt_ref, place_ref, place_seq_ref,
                        o_ref, *, n_seq):
    tq = place_ref.shape[1]
    out = None
    for q in range(n_seq):
        cols = slice(q * tq, (q + 1) * tq)
        c = jnp.concatenate([pckv_ref[q], ckv_ref[0, cols, :]], axis=0).astype(BF16)
        kr = jnp.concatenate([pkr_ref[q], kr_ref[0, cols, :]], axis=0).astype(BF16)
        q_lat, q_rope = None, None
        for h in range(H_C):
            lo = h * LANES
            q_h = _dot(wuk_ref[h], qt_ref[0, 0, lo:lo + D_NOPE, cols]).astype(BF16)
            a_h = _dot(q_h, place_ref[h])
            r_h = _dot(qt_ref[0, 0, lo + D_NOPE:lo + D_NOPE + D_ROPE, cols], place_ref[h])
            q_lat = a_h if q_lat is None else q_lat + a_h
            q_rope = r_h if q_rope is None else q_rope + r_h
        s = _dot(c, q_lat.astype(BF16)) + _dot(kr, q_rope.astype(BF16))
        e = jnp.exp2(s - jnp.max(s, axis=0, keepdims=True)).astype(BF16)
        denom = jnp.sum(e.astype(F32), axis=0, keepdims=True)
        lat = _dot_tn(c, e).astype(BF16)
        o_heads = []
        for h in range(H_C):
            o_h = _dot(wuvt_ref[h], lat[:, h * tq:(h + 1) * tq])
            o_heads.append(o_h / denom[:, h * tq:(h + 1) * tq] * cgst_ref[0, h * D_V:(h + 1) * D_V, cols])
        o_q = _dot(jnp.concatenate(o_heads, axis=0).astype(BF16), place_seq_ref[q])
        out = o_q if out is None else out + o_q
    o_ref[0] = out.astype(BF16)


def _attn_causal_call(qt, kcat, vt, cgst):
    bsz, n_qt, _, tq = qt.shape
    seq = n_qt * tq
    t_k = kcat.shape[1]
    nkt, _, tk = vt.shape[1:]
    nh = HEADS_PER_STEP
    assert tq == N_BUF * tk and seq == t_k and tk % CHUNK == 0
    chunk = jnp.arange(tk) // CHUNK
    bias = jnp.where(chunk[:, None] <= chunk[None, :], 0.0, -1e30).astype(F32)
    return pl.pallas_call(
        functools.partial(_attn_causal_kernel, tq=tq, tk=tk), grid=(bsz, H_C // nh, n_qt),
        in_specs=[pl.BlockSpec((1, n_qt, nh * LANES, tq), lambda b, p, i: (b, 0, p, 0)),
                  pl.BlockSpec((1, t_k, nh * LANES), lambda b, p, i: (b, 0, p)),
                  pl.BlockSpec((1, nkt, nh * V_ROWS, tk), lambda b, p, i: (b, 0, p, 0)),
                  pl.BlockSpec((1, nh * D_V, tq), lambda b, p, i: (b, p, i)),
                  _const_spec((tk, tk))],
        out_specs=pl.BlockSpec((1, nh * D_V, tq), lambda b, p, i: (b, p, i)),
        out_shape=jax.ShapeDtypeStruct((bsz, H_C * D_V, seq), BF16),
        scratch_shapes=([pltpu.VMEM((nh, tk, tq), F32)] * N_BUF + [pltpu.VMEM((nh, tk, tq), BF16)] * N_BUF
                        + [pltpu.VMEM((nh, V_ROWS, tq), F32)]),
        name="attn", compiler_params=_params(3),
    )(qt, kcat, vt, cgst, bias)


def _attn_cached_call(qt, past_ckv, past_kr, ckv_new, kr_new, cgst, pw, l, *, n_seq):
    n_cols = qt.shape[-1]
    tq = n_cols // n_seq
    past_len, kv_lora = past_ckv.shape[2:]

    def placement(n):
        p = jnp.eye(n, dtype=BF16)[:, None, :, None] * jnp.eye(tq, dtype=BF16)[None, :, None, :]
        return p.reshape(n, tq, n * tq)

    place, place_seq = placement(H_C), placement(n_seq)
    return pl.pallas_call(
        functools.partial(_attn_latent_kernel, n_seq=n_seq), grid=(1,),
        in_specs=[_const_spec(qt.shape),
                  pl.BlockSpec((None, n_seq, past_len, kv_lora), lambda i: (l, 0, 0, 0)),
                  pl.BlockSpec((None, n_seq, past_len, D_ROPE), lambda i: (l, 0, 0, 0)),
                  _const_spec(ckv_new.shape), _const_spec(kr_new.shape), _const_spec(cgst.shape),
                  _layer_spec(pw["wuk"], l), _layer_spec(pw["wuvt"], l),
                  _const_spec(place.shape), _const_spec(place_seq.shape)],
        out_specs=pl.BlockSpec((1, H_C * D_V, n_cols), lambda i: (0, 0, 0)),
        out_shape=jax.ShapeDtypeStruct((1, H_C * D_V, n_cols), BF16),
        name="attn_cached", compiler_params=_params(1),
    )(qt, past_ckv, past_kr, ckv_new, kr_new, cgst, pw["wuk"], pw["wuvt"], place, place_seq)


def _merge_kernel(x_ref, partial_ref, g2_ref, oct_ref, wbr_ref, wout_ref, lng_ref, lnb_ref, o_ref,
                  *, layer, c_lo, d_c, alpha):
    yc = _dot_tn(oct_ref[0], wbr_ref[c_lo:c_lo + d_c, :])
    merged = partial_ref[0].astype(F32) + g2_ref[0].astype(F32) * yc
    y = _dot(merged.astype(BF16), wout_ref[...])
    o_ref[0] = _layer_norm(alpha * x_ref[0] + y, lng_ref[layer:layer + 1, :], lnb_ref[layer:layer + 1, :])


def _merge_call(x, partial, g2, oct, pw, l, *, tr, alpha):
    bsz, seq, d_model = x.shape
    d_a, d_b, d_c, _ = pw["dims"]
    body = functools.partial(_merge_kernel, layer=l, c_lo=d_a + d_b, d_c=d_c, alpha=alpha)

    def row_spec(width, **kw):
        return pl.BlockSpec((1, tr, width), lambda b, i: (b, i, 0), **kw)

    deep = dict(pipeline_mode=pl.Buffered(3))

    def outer(x_hbm, p_hbm, g_hbm, oc_hbm, wbr_ref, wout_ref, lng_ref, lnb_ref, o_hbm):
        def step(x_ref, p_ref, g_ref, oc_ref, o_ref):
            body(x_ref, p_ref, g_ref, oc_ref, wbr_ref, wout_ref, lng_ref, lnb_ref, o_ref)
        pltpu.emit_pipeline(
            step, grid=(bsz, seq // tr),
            in_specs=[row_spec(d_model, **deep), row_spec(d_model, **deep), row_spec(d_model, **deep),
                      pl.BlockSpec((1, d_c, tr), lambda b, i: (b, 0, i), **deep)],
            out_specs=[row_spec(d_model)],
        )(x_hbm, p_hbm, g_hbm, oc_hbm, o_hbm)

    any_spec = pl.BlockSpec(memory_space=pl.ANY)
    vmem = pl.BlockSpec(memory_space=pltpu.VMEM)
    return pl.pallas_call(
        outer,
        in_specs=[any_spec, any_spec, any_spec, any_spec, vmem, vmem, vmem, vmem],
        out_specs=any_spec,
        out_shape=jax.ShapeDtypeStruct((bsz, seq, d_model), F32),
        name="merge", compiler_params=pltpu.CompilerParams(vmem_limit_bytes=VMEM_LIMIT),
    )(x, partial, g2, oct, pw["wbr"][l], pw["wout"][l], pw["ln_g"], pw["ln_b"])


def _mem_kernel(m_ref, w_ref, k_ref, v_ref, *, d_m):
    kv = _dot(m_ref[...].astype(BF16), w_ref[...])
    k_ref[...] = kv[:, 0:d_m]
    v_ref[...] = kv[:, d_m:2 * d_m]


def _mem_call(mem2d, wmem, l, *, tr):
    rows, d_model = mem2d.shape
    d_m = wmem.shape[2] // 2
    return pl.pallas_call(
        functools.partial(_mem_kernel, d_m=d_m), grid=(rows // tr,),
        in_specs=[pl.BlockSpec((tr, d_model), lambda i: (i, 0)), _layer_spec(wmem, l)],
        out_specs=[pl.BlockSpec((tr, d_m), lambda i: (i, 0)), pl.BlockSpec((tr, d_m), lambda i: (i, 0))],
        out_shape=[jax.ShapeDtypeStruct((rows, d_m), F32), jax.ShapeDtypeStruct((rows, d_m), F32)],
        name="mem", compiler_params=_params(1),
    )(mem2d, wmem)


def _block_diag(w):
    h, n, _ = w.shape
    eye = jnp.eye(h, dtype=w.dtype)
    return (eye[:, None, :, None] * w[:, :, None, :]).reshape(h * n, h * n)


def _gate_blocks(w_r, w_i):
    per = MXU_TILE // w_r.shape[1]
    blocks = []
    for k in range(w_r.shape[0] // per):
        blocks.append(jnp.concatenate([_block_diag(w_r[k * per:(k + 1) * per]),
                                       _block_diag(w_i[k * per:(k + 1) * per])], axis=1))
    return jnp.stack(blocks)


def _w_in_kernel(wt_ref, o_ref, *, n_head, o_tail):
    o_ref[0, :, 0:n_head] = jnp.transpose(wt_ref[0, 0:n_head, :]).astype(BF16)
    o_ref[0, :, n_head:] = jnp.transpose(wt_ref[0, o_tail:, :]).astype(BF16)


def _w_in_call(w_in, n_head, o_tail, *, tk):
    depth, d_model, d_in = w_in.shape
    n_out = n_head + d_in - o_tail
    return pl.pallas_call(
        functools.partial(_w_in_kernel, n_head=n_head, o_tail=o_tail), grid=(depth, d_model // tk),
        in_specs=[pl.BlockSpec((1, d_in, tk), lambda l, i: (l, 0, i))],
        out_specs=pl.BlockSpec((1, tk, n_out), lambda l, i: (l, i, 0)),
        out_shape=jax.ShapeDtypeStruct((depth, d_model, n_out), BF16),
        name="w_in", compiler_params=_params(2),
    )(jnp.swapaxes(w_in, 1, 2))


def _prep_weights(sp_lens, w_in, gmlp_ln_g, gmlp_ln_b, gmlp_ws, gmlp_bs, lru_conv_w,
                  lru_conv_b, lru_w_r, lru_b_r, lru_w_i, lru_b_i, lru_lambda, mla_q_norm, mla_w_uq,
                  mla_kv_norm, mla_w_ukv, mem_w_k, mem_w_v, w_br, w_out, ln_g, ln_b):
    depth, d_model, _ = w_in.shape
    d_a = gmlp_ln_g.shape[1]
    d_b = lru_lambda.shape[1]
    q_lora = mla_q_norm.shape[1]
    kv_lora = mla_kv_norm.shape[1]
    d_c = H_C * D_V
    d_m = H_M * DH_M
    o_cq = 3 * d_a + 2 * d_b
    o_kr = o_cq + q_lora + kv_lora
    o_cg = o_kr + D_ROPE
    o_mq = o_cg + d_c
    o_g = o_mq + d_m
    n_head = o_kr + LANES
    assert o_kr % LANES == 0 and (o_g - o_mq) % LANES == 0 and D_ROPE <= LANES and n_head <= o_mq
    cols = {"a_u": (0, d_a), "a_v": (d_a, 2 * d_a), "a_g": (2 * d_a, 3 * d_a),
            "b_x": (3 * d_a, 3 * d_a + d_b), "b_g": (3 * d_a + d_b, o_cq), "c_all": (o_cq, n_head),
            "m_q": (n_head, n_head + d_m)}
    for k in range(N_BRANCH):
        cols[f"g{k}"] = (n_head + d_m + k * d_model, n_head + d_m + (k + 1) * d_model)
    w1 = _w_in_call(w_in, n_head, o_mq, tk=LANES)
    wcgt = jnp.swapaxes(w_in[:, :, o_cg:o_mq], 1, 2).astype(BF16)

    wq = mla_w_uq.reshape(depth, q_lora, H_C, D_NOPE + D_ROPE) * _EXP2_SCALE
    wq = jnp.concatenate([wq, jnp.zeros((depth, q_lora, H_C, LANES - D_NOPE - D_ROPE), wq.dtype)], axis=3)
    wuqt = jnp.swapaxes(wq.reshape(depth, q_lora, H_C * LANES), 1, 2).astype(BF16)

    wkv = mla_w_ukv.reshape(depth, kv_lora, H_C, D_NOPE + D_V)
    wk = wkv[..., :D_NOPE]
    wv = wkv[..., D_NOPE:]
    wkc = jnp.concatenate([wk, jnp.zeros((depth, kv_lora, H_C, LANES - D_NOPE), wk.dtype)], axis=3)
    wkc = wkc.reshape(depth, kv_lora, H_C * LANES).astype(BF16)
    lane = jnp.arange(H_C * LANES) % LANES
    wkr = ((lane[None, :] - D_NOPE) == jnp.arange(D_ROPE)[:, None]).astype(BF16)
    wkr = jnp.broadcast_to(wkr, (depth,) + wkr.shape)
    wv = jnp.concatenate([wv, jnp.zeros((depth, kv_lora, H_C, V_ROWS - D_V), wv.dtype)], axis=3)
    wvt = jnp.swapaxes(wv.reshape(depth, kv_lora, H_C * V_ROWS), 1, 2).astype(BF16)

    pw = dict(
        dims=(d_a, d_b, d_c, d_m), q_lora=q_lora, kv_lora=kv_lora, cols=cols,
        w1=w1, wcgt=wcgt,
        gln_g=gmlp_ln_g, gln_b=gmlp_ln_b,
        conv_w=lru_conv_w, conv_b=lru_conv_b,
        wgate=jnp.stack([_gate_blocks(lru_w_r[l], lru_w_i[l]) for l in range(depth)]).astype(BF16),
        b_r=lru_b_r, b_i=lru_b_i, lam=lru_lambda,
        q_norm=mla_q_norm, wuqt=wuqt, kv_norm=mla_kv_norm,
        wkc=wkc, wkr=wkr, wvt=wvt,
        wuk=jnp.transpose(wk, (0, 2, 1, 3)).astype(BF16),
        wuvt=jnp.transpose(wkv[..., D_NOPE:], (0, 2, 3, 1)).astype(BF16),
        wbr=w_br.astype(BF16), wout=w_out.astype(BF16), ln_g=ln_g, ln_b=ln_b,
        wmem=jnp.concatenate([mem_w_k, mem_w_v], axis=2).astype(BF16),
    )
    for sp in sp_lens:
        pw["wsp%d" % sp] = jnp.tril(gmlp_ws[:, :, :sp, :sp]).astype(BF16)
        pw["bsp%d" % sp] = jnp.repeat(jnp.swapaxes(gmlp_bs[:, :, :sp], 1, 2), d_a // G_A, axis=2)
    return pw


def _rope_tables(pos):
    half = D_ROPE // 2
    freq = ROPE_BASE ** (-jnp.arange(half, dtype=F32) / half)
    ang = pos.astype(F32)[:, None] * freq[None, :]
    cos, sin = jnp.cos(ang), jnp.sin(ang)
    return (jnp.transpose(cos), jnp.transpose(sin),
            jnp.concatenate([cos, cos], axis=1), jnp.concatenate([-sin, sin], axis=1))


def _pad_conv_state(conv):
    return jnp.pad(conv, ((0, 0), (SUBLANES - (CONV_W - 1), 0), (0, 0)))


def _trunk_layer(x, pw, l, sp, rope, conv0, h0, mk, mv, past_ckv, past_kr, *, tm, alpha):
    bsz, seq, d_model = x.shape
    first_chunk = past_ckv is None
    if first_chunk:
        x_rows, seq_len, tq = x, tm, N_BUF * ATTN_TK
    else:
        x_rows, seq_len, tq = x.reshape(1, bsz * seq, d_model), seq, bsz * seq
        cos_t, sin_t, ck_t, sk_t = rope
        rope = (jnp.tile(cos_t, (1, bsz)), jnp.tile(sin_t, (1, bsz)), jnp.tile(ck_t, (bsz, 1)), jnp.tile(sk_t, (bsz, 1)))
    outs = _branch_call(x_rows, pw, l, sp, _pad_conv_state(conv0), h0[:, None, :], mk, mv, rope,
                        tm=tm, seq_len=seq_len, tq=tq, first_chunk=first_chunk)
    partial, g2, cgst, qt, ckv_new, kr_new, conv_pad, h_new = outs[:8]
    if first_chunk:
        kcat, vt = outs[8:]
        v_rows = None
        oct = _attn_causal_call(qt, kcat, vt, cgst)
    else:
        oct = _attn_cached_call(qt, past_ckv, past_kr, ckv_new, kr_new, cgst, pw, l, n_seq=bsz)
        ckv_new, kr_new, v_rows = (o.reshape(bsz, seq, -1) for o in (ckv_new, kr_new, outs[8]))
    x_new = _merge_call(x_rows, partial, g2, oct, pw, l, tr=MERGE_TILE if first_chunk else tm, alpha=alpha)
    return (x_new.reshape(bsz, seq, d_model), v_rows, conv_pad[:, SUBLANES - (CONV_W - 1):], h_new[:, 0],
            ckv_new, kr_new)


def kernel(x_prompt, x_sample, mem_prompt, cache_mla_ckv, cache_mla_krope, cache_mem_k, cache_mem_v,
           state_lru_h, state_lru_conv, w_in, gmlp_ln_g, gmlp_ln_b, gmlp_ws, gmlp_bs,
           lru_conv_w, lru_conv_b, lru_w_r, lru_b_r, lru_w_i, lru_b_i, lru_lambda,
           mla_q_norm, mla_w_uq, mla_kv_norm, mla_w_ukv, mem_w_k, mem_w_v, w_br, w_out, ln_g, ln_b):
    bp, tp, d_model = x_prompt.shape
    bs, ts, _ = x_sample.shape
    depth = w_in.shape[0]
    past_len = cache_mla_ckv.shape[2]
    n_mem = mem_prompt.shape[1]
    d_b = lru_lambda.shape[1]
    alpha = (2.0 * depth) ** 0.25
    sp_p, sp_s = min(tp, A_CHUNK), min(ts, A_CHUNK)
    pw = _prep_weights(sorted({sp_p, sp_s}), w_in, gmlp_ln_g, gmlp_ln_b, gmlp_ws, gmlp_bs, lru_conv_w,
                       lru_conv_b, lru_w_r, lru_b_r, lru_w_i, lru_b_i, lru_lambda, mla_q_norm, mla_w_uq,
                       mla_kv_norm, mla_w_ukv, mem_w_k, mem_w_v, w_br, w_out, ln_g, ln_b)
    rope_p = _rope_tables(jnp.arange(tp))
    rope_s = _rope_tables(past_len + jnp.arange(ts))
    tm_p = min(tp, ROW_TILE)
    assert tp % (N_BUF * ATTN_TK) == 0 and tm_p % ATTN_TK == 0 and tp % MERGE_TILE == 0 and ts % SUBLANES == 0
    zero_conv = jnp.zeros((bp, CONV_W - 1, d_b), F32)
    zero_h = jnp.zeros((bp, d_b), F32)

    xp, xs = x_prompt, x_sample
    acc = [[] for _ in range(11)]
    for l in range(depth):
        mk, mv = _mem_call(mem_prompt.reshape(bp * n_mem, d_model), pw["wmem"], l, tr=n_mem)
        mk = mk.reshape(bp, n_mem, -1)
        mv = mv.reshape(bp, n_mem, -1)
        xp, _, conv_n, h_n, ckv_n, kr_n = _trunk_layer(
            xp, pw, l, sp_p, rope_p, zero_conv, zero_h, mk, mv, None, None, tm=tm_p, alpha=alpha)
        for k, val in zip(range(6), (ckv_n, kr_n, mk.reshape(bp, n_mem, H_M, DH_M),
                                     mv.reshape(bp, n_mem, H_M, DH_M), h_n, conv_n)):
            acc[k].append(val)
        xs, v_n, conv_n, h_n, ckv_n, kr_n = _trunk_layer(
            xs, pw, l, sp_s, rope_s, state_lru_conv[l], state_lru_h[l],
            cache_mem_k[l].reshape(bs, n_mem, -1), cache_mem_v[l].reshape(bs, n_mem, -1),
            cache_mla_ckv, cache_mla_krope, tm=bs * ts, alpha=alpha)
        for k, val in zip(range(6, 11), (ckv_n, kr_n, h_n, conv_n, v_n)):
            acc[k].append(val)
    return (xp, xs) + tuple(jnp.stack(a) for a in acc)
```
